```python
import jax, jax.numpy as jnp
from jax import lax
import numpy as np

D_MODEL = 1024
BATCH = 16
SEQ = 2048
DEPTH = 1
DEC_BATCH = 32
DEC_SEQ = 4
PAST_LEN = 16384
PAGE_SIZE = 128

EPS = 1e-6
H_A = 8
DH_A = 64
W_A = H_A * DH_A
Q_BLOCK = 128
FORGET_BIAS = 3.0
H_B = 4
DK_B = 64
DV_B = 128
WK_B = H_B * DK_B
WV_B = H_B * DV_B
GK_RANK = 16
GATE_NORM = 16.0
GLA_CHUNK = 16
IN_SIZES = (W_A, W_A, W_A, H_A, WK_B, WK_B, WV_B, WV_B, GK_RANK, D_MODEL, D_MODEL)
IN_COLS = sum(IN_SIZES)
N_EXPERTS = 256
TOP_K = 8
N_GROUPS = 8
TOPK_GROUPS = 4
D_EXPERT = 256
D_SHARED = 256
ROUTE_SCALE = 2.5
MOE_BLOCK = 128

kernel_name = "fox_gla_gated_merge_moe_adaln_step"


def rmsnorm(x, g):
    xf = x.astype(jnp.float32)
    y = xf * lax.rsqrt(jnp.mean(xf * xf, axis=-1, keepdims=True) + EPS)
    return (y * g.astype(jnp.float32)).astype(x.dtype)


def adaln(c, w_ada, b_ada):
    ada = jax.nn.silu(c) @ w_ada + b_ada
    return jnp.split(ada[:, None, :], 6, axis=-1)


def mixer_inputs(h, w_in, b_f, w_gk2, b_gk):
    B, T, _ = h.shape
    z = h @ w_in
    offs = [int(o) for o in np.cumsum(IN_SIZES)[:-1]]
    q_a, k_a, v_a, f_a, q_b, k_b, v_b, r_b, gk_lr, gate_a, gate_b = jnp.split(z, offs, axis=-1)
    q_a = q_a.reshape(B, T, H_A, DH_A)
    k_a = k_a.reshape(B, T, H_A, DH_A)
    v_a = v_a.reshape(B, T, H_A, DH_A)
    logf = jax.nn.log_sigmoid((f_a + b_f).astype(jnp.float32))
    q_b = q_b.reshape(B, T, H_B, DK_B)
    k_b = k_b.reshape(B, T, H_B, DK_B)
    v_b = v_b.reshape(B, T, H_B, DV_B)
    gk = jax.nn.log_sigmoid((gk_lr @ w_gk2 + b_gk).astype(jnp.float32)) / GATE_NORM
    gk = gk.reshape(B, T, H_B, DK_B)
    return q_a, k_a, v_a, logf, q_b, k_b, v_b, gk, r_b, gate_a, gate_b


def fox_prompt(q, k, v, logf):
    B, T, H, Dh = q.shape
    scale = Dh ** -0.5
    F = jnp.cumsum(logf, axis=1).transpose(0, 2, 1)
    nq = T // Q_BLOCK
    qb = jnp.moveaxis(q.reshape(B, nq, Q_BLOCK, H, Dh), 1, 0)
    Fq = jnp.moveaxis(F.reshape(B, H, nq, Q_BLOCK), 2, 0)
    pos_k = jnp.arange(T)

    def one_block(args):
        qi, Fi, i = args
        s = jnp.einsum('bqhd,bkhd->bhqk', qi, k).astype(jnp.float32) * scale
        s = s + Fi[..., :, None] - F[:, :, None, :]
        pos_q = i * Q_BLOCK + jnp.arange(Q_BLOCK)
        mask = pos_k[None, :] <= pos_q[:, None]
        p = jax.nn.softmax(jnp.where(mask, s, -jnp.inf), axis=-1)
        return jnp.einsum('bhqk,bkhd->bqhd', p.astype(v.dtype), v)

    out = lax.map(one_block, (qb, Fq, jnp.arange(nq)))
    return jnp.moveaxis(out, 0, 1).reshape(B, T, H, Dh)


def fox_sample(q, k, v, logf, k_past, v_past, logf_past):
    P = k_past.shape[1]
    scale = DH_A ** -0.5
    Fn = jnp.cumsum(logf, axis=1).transpose(0, 2, 1)
    lfp = logf_past.astype(jnp.float32)
    suf = (lax.cumsum(lfp, axis=1, reverse=True) - lfp).transpose(0, 2, 1)
    s_past = jnp.einsum('bthd,bshd->bhts', q, k_past).astype(jnp.float32) * scale
    s_past = s_past + Fn[..., :, None] + suf[:, :, None, :]
    s_new = jnp.einsum('bthd,bshd->bhts', q, k).astype(jnp.float32) * scale
    s_new = s_new + Fn[..., :, None] - Fn[..., None, :]
    T = q.shape[1]
    causal = jnp.tril(jnp.ones((T, T), bool))
    s_new = jnp.where(causal, s_new, -jnp.inf)
    p = jax.nn.softmax(jnp.concatenate([s_past, s_new], axis=-1), axis=-1).astype(v.dtype)
    return (jnp.einsum('bhts,bshd->bthd', p[..., :P], v_past)
            + jnp.einsum('bhts,bshd->bthd', p[..., P:], v))


def gla_chunked(q, k, v, g, s0):
    B, T, H, _ = q.shape
    n = -(-T // GLA_CHUNK)
    pad = n * GLA_CHUNK - T

    def blocks(a):
        a = jnp.pad(a.astype(jnp.float32), ((0, 0), (0, pad), (0, 0), (0, 0)))
        return a.reshape(B, n, GLA_CHUNK, H, a.shape[-1])

    q, k, v, g = blocks(q) * DK_B ** -0.5, blocks(k), blocks(v), blocks(g)
    b = jnp.cumsum(g, axis=2)
    b_last = b[:, :, -1]
    q_in = q * jnp.exp(b)
    k_in = k * jnp.exp(-b)
    k_out = k * jnp.exp(b_last[:, :, None] - b)
    causal = jnp.tril(jnp.ones((GLA_CHUNK, GLA_CHUNK), bool))
    att = jnp.where(causal, jnp.einsum('bnthk,bnshk->bnhts', q_in, k_in), 0.0)
    o_intra = jnp.einsum('bnhts,bnshv->bnthv', att, v)

    def step(S, xs):
        q_c, k_c, v_c, d_c = xs
        o_c = jnp.einsum('bthk,bhkv->bthv', q_c, S)
        S = S * d_c[..., None] + jnp.einsum('bshk,bshv->bhkv', k_c, v_c)
        return S, o_c

    xs = tuple(jnp.moveaxis(a, 1, 0) for a in (q_in, k_out, v, jnp.exp(b_last)))
    s_fin, o_inter = lax.scan(step, s0.astype(jnp.float32), xs)
    o = (o_intra + jnp.moveaxis(o_inter, 0, 1)).reshape(B, n * GLA_CHUNK, H, DV_B)[:, :T]
    return o, s_fin


def mixer_output(y_a, y_b, r_b, gate_a, gate_b, gla_norm_g, w_pa, w_pb, w_o):
    B, T = y_a.shape[:2]
    dt = r_b.dtype
    yb = rmsnorm(y_b, gla_norm_g).astype(dt).reshape(B, T, WV_B) * jax.nn.silu(r_b)
    br_a = y_a.reshape(B, T, W_A) @ w_pa
    br_b = yb @ w_pb
    return (jax.nn.sigmoid(gate_a) * br_a + jax.nn.sigmoid(gate_b) * br_b) @ w_o


def swiglu(x, wg, wu, wd):
    return (jax.nn.silu(x @ wg) * (x @ wu)) @ wd


def route(h, router_w, router_bias):
    N = h.shape[0]
    s = jax.nn.sigmoid((h @ router_w).astype(jnp.float32))
    sb = s + router_bias.astype(jnp.float32)
    grp = sb.reshape(N, N_GROUPS, N_EXPERTS // N_GROUPS)
    gscore = lax.top_k(grp, 2)[0].sum(-1)
    _, gidx = lax.top_k(gscore, TOPK_GROUPS)
    gmask = jax.nn.one_hot(gidx, N_GROUPS, dtype=jnp.float32).sum(1) > 0
    emask = jnp.repeat(gmask, N_EXPERTS // N_GROUPS, axis=1)
    _, idx = lax.top_k(jnp.where(emask, sb, -jnp.inf), TOP_K)
    w = jnp.take_along_axis(s, idx, axis=1)
    w = w / jnp.sum(w, axis=-1, keepdims=True) * ROUTE_SCALE
    return idx, w


def moe_ffn(h, router_w, router_bias, w_gate, w_up, w_down, ws_gate, ws_up, ws_down):
    N = h.shape[0]
    idx, w = route(h, router_w, router_bias)
    nk = N * TOP_K
    flat_e = idx.reshape(nk)
    order = jnp.argsort(flat_e)
    e_sorted = flat_e[order]
    tok_sorted = (order // TOP_K).astype(jnp.int32)
    w_sorted = w.reshape(nk)[order]
    counts = jnp.bincount(flat_e, length=N_EXPERTS)
    padded = (counts + MOE_BLOCK - 1) // MOE_BLOCK * MOE_BLOCK
    pad_end = jnp.cumsum(padded)
    pad_start = pad_end - padded
    start = jnp.cumsum(counts) - counts
    dest = pad_start[e_sorted] + jnp.arange(nk) - start[e_sorted]
    n_blocks = -(-nk // MOE_BLOCK) + N_EXPERTS
    buf_tok = jnp.zeros((n_blocks * MOE_BLOCK,), jnp.int32).at[dest].set(tok_sorted)
    block_expert = jnp.minimum(
        jnp.searchsorted(pad_end, jnp.arange(n_blocks) * MOE_BLOCK, side='right'), N_EXPERTS - 1)

    def run_block(args):
        toks, e = args
        return swiglu(h[toks], w_gate[e], w_up[e], w_down[e])

    yb = lax.map(run_block, (buf_tok.reshape(n_blocks, MOE_BLOCK), block_expert)).reshape(-1, h.shape[-1])
    routed = jax.ops.segment_sum(yb[dest] * w_sorted[:, None].astype(h.dtype), tok_sorted, num_segments=N)
    return routed + swiglu(h, ws_gate, ws_up, ws_down)


def setup_inputs(seed: int = 0) -> dict:
    key = jax.random.key(seed)
    ks = iter(jax.random.split(key, 40))

    def nrm(shape, scale):
        return jax.random.normal(next(ks), shape, jnp.float32) * scale

    n_pages = PAST_LEN // PAGE_SIZE
    n_used = DEC_BATCH * n_pages
    n_phys = n_used + n_used // 4
    perm = jax.random.permutation(next(ks), n_phys)
    page_table = perm[:n_used].reshape(DEC_BATCH, n_pages).astype(jnp.int32)
    return {
        "x_prompt": nrm((BATCH, SEQ, D_MODEL), 1.0),
        "x_sample": nrm((DEC_BATCH, DEC_SEQ, D_MODEL), 1.0),
        "cache_k": nrm((DEPTH, n_phys, PAGE_SIZE, H_A, DH_A), 1.0),
        "cache_v": nrm((DEPTH, n_phys, PAGE_SIZE, H_A, DH_A), 1.0),
        "cache_logf": jax.nn.log_sigmoid(FORGET_BIAS + nrm((DEPTH, n_phys, PAGE_SIZE, H_A), 1.0)),
        "state_gla": nrm((DEPTH, DEC_BATCH, H_B, DK_B, DV_B), 0.1),
        "page_table": page_table,
        "c_prompt": nrm((BATCH, D_MODEL), 1.0),
        "c_sample": nrm((DEC_BATCH, D_MODEL), 1.0),
        "w_ada": nrm((DEPTH, D_MODEL, 6 * D_MODEL), 0.5 * D_MODEL ** -0.5),
        "b_ada": nrm((DEPTH, 6 * D_MODEL), 0.02),
        "norm1_g": 1.0 + nrm((DEPTH, D_MODEL), 0.02),
        "w_in": nrm((DEPTH, D_MODEL, IN_COLS), D_MODEL ** -0.5),
        "b_f": FORGET_BIAS + nrm((DEPTH, H_A), 0.1),
        "w_gk2": nrm((DEPTH, GK_RANK, WK_B), GK_RANK ** -0.5),
        "b_gk": nrm((DEPTH, WK_B), 0.1),
        "gla_norm_g": 1.0 + nrm((DEPTH, DV_B), 0.02),
        "w_pa": nrm((DEPTH, W_A, D_MODEL), W_A ** -0.5),
        "w_pb": nrm((DEPTH, WV_B, D_MODEL), WV_B ** -0.5),
        "w_o": nrm((DEPTH, D_MODEL, D_MODEL), D_MODEL ** -0.5),
        "norm2_g": 1.0 + nrm((DEPTH, D_MODEL), 0.02),
        "router_w": nrm((DEPTH, D_MODEL, N_EXPERTS), D_MODEL ** -0.5),
        "router_bias": nrm((DEPTH, N_EXPERTS), 0.01),
        "w_gate": nrm((DEPTH, N_EXPERTS, D_MODEL, D_EXPERT), D_MODEL ** -0.5),
        "w_up": nrm((DEPTH, N_EXPERTS, D_MODEL, D_EXPERT), D_MODEL ** -0.5),
        "w_down": nrm((DEPTH, N_EXPERTS, D_EXPERT, D_MODEL), D_EXPERT ** -0.5),
        "ws_gate": nrm((DEPTH, D_MODEL, D_SHARED), D_MODEL ** -0.5),
        "ws_up": nrm((DEPTH, D_MODEL, D_SHARED), D_MODEL ** -0.5),
        "ws_down": nrm((DEPTH, D_SHARED, D_MODEL), D_SHARED ** -0.5),
        "final_g": 1.0 + nrm((D_MODEL,), 0.02),
    }


def reference(x_prompt, x_sample, cache_k, cache_v, cache_logf, state_gla, page_table,
              c_prompt, c_sample, w_ada, b_ada, norm1_g, w_in, b_f, w_gk2, b_gk, gla_norm_g,
              w_pa, w_pb, w_o, norm2_g, router_w, router_bias, w_gate, w_up, w_down,
              ws_gate, ws_up, ws_down, final_g):
    x_p, x_s = x_prompt, x_sample
    Bp = x_p.shape[0]
    Bd, Td = x_s.shape[:2]
    kp_l, vp_l, fp_l, sp_l, ks_l, vs_l, fs_l, ss_l = [], [], [], [], [], [], [], []
    for l in range(DEPTH):
        sh1p, sc1p, g1p, sh2p, sc2p, g2p = adaln(c_prompt, w_ada[l], b_ada[l])
        sh1s, sc1s, g1s, sh2s, sc2s, g2s = adaln(c_sample, w_ada[l], b_ada[l])

        h = rmsnorm(x_p, norm1_g[l]) * (1.0 + sc1p) + sh1p
        q_a, k_a, v_a, lf, q_b, k_b, v_b, gk, r_b, ga, gb = mixer_inputs(h, w_in[l], b_f[l], w_gk2[l], b_gk[l])
        y_a = fox_prompt(q_a, k_a, v_a, lf)
        y_b, s_p = gla_chunked(q_b, k_b, v_b, gk, jnp.zeros((Bp, H_B, DK_B, DV_B), jnp.float32))
        x_p = x_p + g1p * mixer_output(y_a, y_b, r_b, ga, gb, gla_norm_g[l], w_pa[l], w_pb[l], w_o[l])
        kp_l.append(k_a); vp_l.append(v_a); fp_l.append(lf); sp_l.append(s_p)

        h = rmsnorm(x_s, norm1_g[l]) * (1.0 + sc1s) + sh1s
        q_a, k_a, v_a, lf, q_b, k_b, v_b, gk, r_b, ga, gb = mixer_inputs(h, w_in[l], b_f[l], w_gk2[l], b_gk[l])
        k_past = cache_k[l, page_table].reshape(Bd, -1, H_A, DH_A)
        v_past = cache_v[l, page_table].reshape(Bd, -1, H_A, DH_A)
        lf_past = cache_logf[l, page_table].reshape(Bd, -1, H_A)
        y_a = fox_sample(q_a, k_a, v_a, lf, k_past, v_past, lf_past)
        y_b, s_s = gla_chunked(q_b, k_b, v_b, gk, state_gla[l])
        x_s = x_s + g1s * mixer_output(y_a, y_b, r_b, ga, gb, gla_norm_g[l], w_pa[l], w_pb[l], w_o[l])
        ks_l.append(k_a); vs_l.append(v_a); fs_l.append(lf); ss_l.append(s_s)

        h = rmsnorm(x_p, norm2_g[l]) * (1.0 + sc2p) + sh2p
        y = moe_ffn(h.reshape(-1, D_MODEL), router_w[l], router_bias[l], w_gate[l], w_up[l], w_down[l],
                    ws_gate[l], ws_up[l], ws_down[l])
        x_p = x_p + g2p * y.reshape(x_p.shape)
        h = rmsnorm(x_s, norm2_g[l]) * (1.0 + sc2s) + sh2s
        y = moe_ffn(h.reshape(-1, D_MODEL), router_w[l], router_bias[l], w_gate[l], w_up[l], w_down[l],
                    ws_gate[l], ws_up[l], ws_down[l])
        x_s = x_s + g2s * y.reshape(x_s.shape)

    y_prompt = rmsnorm(x_p, final_g)
    y_sample = rmsnorm(x_s, final_g)
    return (y_prompt, y_sample,
            jnp.stack(kp_l), jnp.stack(vp_l), jnp.stack(fp_l), jnp.stack(sp_l),
            jnp.stack(ks_l), jnp.stack(vs_l), jnp.stack(fs_l), jnp.stack(ss_l))
```

```python
import functools

import jax
import jax.numpy as jnp
from jax import lax
from jax.experimental import pallas as pl
from jax.experimental.pallas import tpu as pltpu

F32 = jnp.float32
BF16 = jnp.bfloat16
U32 = jnp.uint32
I32 = jnp.int32

EPS = 1e-6
D_MODEL = 1024
H_A, DH_A, W_A = 8, 64, 512
H_B, DK_B, DV_B, WK_B, WV_B = 4, 64, 128, 256, 512
GK_RANK = 16
GATE_NORM = 16.0
N_EXPERTS, TOP_K, N_GROUPS, TOPK_GROUPS = 256, 8, 8, 4
GROUP_SIZE = N_EXPERTS // N_GROUPS
D_EXPERT = 256
ROUTE_SCALE = 2.5
MOE_TILE = 128
LANES = 128
GLA_CHUNK = 64
GLA_SUB = 16
FOX_TILE = 512
DEC_PAGES = 8
NEG = -1e30
VMEM_LIMIT = 56 * 1024 * 1024
HIGHEST = lax.Precision.HIGHEST

NT_DIMS = (((1,), (1,)), ((), ()))
TN_DIMS = (((0,), (0,)), ((), ()))


def _cparams(sem):
    return pltpu.CompilerParams(dimension_semantics=sem, vmem_limit_bytes=VMEM_LIMIT)


def _dot(a, b, **kw):
    return jnp.dot(a, b, preferred_element_type=F32, **kw)


def _dot_nt(a, b, **kw):
    return lax.dot_general(a, b, NT_DIMS, preferred_element_type=F32, **kw)


def _dot_tn(a, b, **kw):
    return lax.dot_general(a, b, TN_DIMS, preferred_element_type=F32, **kw)


def _sigmoid(x):
    return 1.0 / (1.0 + jnp.exp(-x))


def _silu(x):
    return x * _sigmoid(x)


def _log_sigmoid(x):
    return jnp.minimum(x, 0.0) - jnp.log(1.0 + jnp.exp(-jnp.abs(x)))


def _pack_bf16_pair(x):
    c = x.shape[1] // 2
    bits = pltpu.bitcast(x.astype(BF16).astype(F32), U32)
    return (bits[:, c:] & jnp.uint32(0xFFFF0000)) | (bits[:, :c] >> 16)


def _unpack_bf16_pair(p):
    lo = pltpu.bitcast(p << 16, F32).astype(BF16)
    hi = pltpu.bitcast(p & jnp.uint32(0xFFFF0000), F32).astype(BF16)
    return lo, hi


def _ada_kernel(c_ref, w_ref, b_ref, o_ref):
    c = c_ref[...]
    o_ref[...] = _dot(_silu(c).astype(BF16), w_ref[...].astype(BF16)) + b_ref[...]


def _ada(c, w, b):
    n, d = c.shape
    cols = w.shape[1]
    tn = 1536
    return pl.pallas_call(
        _ada_kernel,
        grid=(cols // tn,),
        in_specs=[pl.BlockSpec((n, d), lambda j: (0, 0)),
                  pl.BlockSpec((d, tn), lambda j: (0, j)),
                  pl.BlockSpec((1, tn), lambda j: (0, j))],
        out_specs=pl.BlockSpec((n, tn), lambda j: (0, j)),
        out_shape=jax.ShapeDtypeStruct((n, cols), F32),
        compiler_params=_cparams(("arbitrary",)),
        name="ada",
    )(c, w, b)


def _inproj_kernel(x_ref, mod_ref, g_ref, wqkv_ref, wf_ref, bf_ref, wb_ref, wgk1_ref, wgk2_ref, bgk_ref, wgate_ref,
                   q_ref, k_ref, v_ref, kh_ref, vh_ref, lf_ref, fc_ref, qb_ref, kb_ref, gk_ref, vb_ref, rb_ref,
                   ga_ref, gb_ref, carry_ref, *, seq_tiles):
    i = pl.program_id(0)
    tm = x_ref.shape[0]
    x = x_ref[...]
    y = x * lax.rsqrt(jnp.mean(x * x, axis=-1, keepdims=True) + EPS) * g_ref[...]
    h = (y * (1.0 + mod_ref[1]) + mod_ref[0]).astype(BF16)

    z = _dot(h, wqkv_ref[...])
    q_ref[...] = (z[:, :W_A] * DH_A ** -0.5).astype(BF16)
    k = z[:, W_A:2 * W_A]
    v = z[:, 2 * W_A:]
    k_ref[...] = k
    v_ref[...] = v
    kh_ref[...] = k.astype(BF16)
    vh_ref[...] = v.astype(BF16)

    lf = _log_sigmoid(_dot_nt(wf_ref[...], h) + bf_ref[...])
    lf_ref[...] = lf

    @pl.when(i % seq_tiles == 0)
    def _():
        carry_ref[...] = jnp.zeros_like(carry_ref)

    r = lax.broadcasted_iota(I32, (tm, tm), 0)
    c = lax.broadcasted_iota(I32, (tm, tm), 1)
    upper = jnp.where(r <= c, 1.0, 0.0).astype(F32)
    carry = carry_ref[...]
    fc_ref[...] = _dot(lf, upper, precision=HIGHEST) + jnp.concatenate([carry] * (tm // LANES), axis=1)
    carry_ref[...] = carry + _dot(lf, jnp.ones((tm, LANES), F32), precision=HIGHEST)

    z = _dot(h, wb_ref[...])
    qb_ref[...] = z[:, :WK_B] * DK_B ** -0.5
    kb_ref[...] = z[:, WK_B:2 * WK_B]
    vb_ref[...] = z[:, 2 * WK_B:2 * WK_B + WV_B].astype(BF16)
    rb_ref[...] = _silu(z[:, 2 * WK_B + WV_B:]).astype(BF16)

    lr = _dot(h, wgk1_ref[...]).astype(BF16)
    gk_ref[...] = _log_sigmoid(_dot(lr, wgk2_ref[...]) + bgk_ref[...]) * (1.0 / GATE_NORM)

    z = _dot(h, wgate_ref[...])
    ga_ref[...] = _sigmoid(z[:, :D_MODEL]).astype(BF16)
    gb_ref[...] = _sigmoid(z[:, D_MODEL:]).astype(BF16)


def _const_spec(shape):
    return pl.BlockSpec(shape, lambda i: (0,) * len(shape))


def _inproj(x, mod, rows_per_mod, seq_tiles, tm, g, wts):
    n = x.shape[0]
    mod_rows = mod.shape[2]
    tiles_per_mod = rows_per_mod // tm
    wqkv, wf, bf, wb, wgk1, wgk2, bgk, wgate = wts
    row = lambda w: pl.BlockSpec((tm, w), lambda i: (i, 0))
    lane = lambda: pl.BlockSpec((H_A, tm), lambda i: (0, i))
    outs = [
        (row(W_A), (n, W_A), BF16),
        (row(W_A), (n, W_A), F32),
        (row(W_A), (n, W_A), F32),
        (row(W_A), (n, W_A), BF16),
        (row(W_A), (n, W_A), BF16),
        (lane(), (H_A, n), F32),
        (lane(), (H_A, n), F32),
        (row(WK_B), (n, WK_B), F32),
        (row(WK_B), (n, WK_B), F32),
        (row(WK_B), (n, WK_B), F32),
        (row(WV_B), (n, WV_B), BF16),
        (row(WV_B), (n, WV_B), BF16),
        (row(D_MODEL), (n, D_MODEL), BF16),
        (row(D_MODEL), (n, D_MODEL), BF16),
    ]
    return pl.pallas_call(
        functools.partial(_inproj_kernel, seq_tiles=seq_tiles),
        grid=(n // tm,),
        in_specs=[row(D_MODEL),
                  pl.BlockSpec((6, None, mod_rows, D_MODEL), lambda i: (0, i // tiles_per_mod, 0, 0)),
                  _const_spec(g.shape), _const_spec(wqkv.shape), _const_spec(wf.shape), _const_spec(bf.shape),
                  _const_spec(wb.shape), _const_spec(wgk1.shape), _const_spec(wgk2.shape), _const_spec(bgk.shape),
                  _const_spec(wgate.shape)],
        out_specs=[o[0] for o in outs],
        out_shape=[jax.ShapeDtypeStruct(o[1], o[2]) for o in outs],
        scratch_shapes=[pltpu.VMEM((H_A, LANES), F32)],
        compiler_params=_cparams(("arbitrary",)),
        name="inproj",
    )(x, mod, g, wqkv, wf, bf, wb, wgk1, wgk2, bgk, wgate)


def _fox_kernel(q_ref, k_ref, v_ref, fc_ref, o_ref, qs_ref, m_ref, l_ref, acc_ref):
    qi = pl.program_id(1)
    ki = pl.program_id(2)
    t = q_ref.shape[0]
    pairs = H_A // 2
    lane = lax.broadcasted_iota(I32, (t, LANES), 1)

    @pl.when(ki == 0)
    def _():
        for hp in range(pairs):
            q = q_ref[:, hp * LANES:(hp + 1) * LANES]
            qs_ref[hp, :t, :] = jnp.where(lane < DH_A, q, jnp.zeros_like(q))
            qs_ref[hp, t:, :] = jnp.where(lane >= DH_A, q, jnp.zeros_like(q))
        m_ref[...] = jnp.full_like(m_ref, NEG)
        l_ref[...] = jnp.zeros_like(l_ref)
        acc_ref[...] = jnp.zeros_like(acc_ref)

    def step(diagonal):
        for hp in range(pairs):
            k = k_ref[:, hp * LANES:(hp + 1) * LANES]
            v = v_ref[:, hp * LANES:(hp + 1) * LANES]
            s = _dot_nt(qs_ref[hp], k)
            f0 = jnp.broadcast_to(fc_ref[2 * hp:2 * hp + 1, :], (t, t))
            f1 = jnp.broadcast_to(fc_ref[2 * hp + 1:2 * hp + 2, :], (t, t))
            s = s - jnp.concatenate([f0, f1], axis=0)
            if diagonal:
                row = lax.broadcasted_iota(I32, (2 * t, t), 0)
                col = lax.broadcasted_iota(I32, (2 * t, t), 1)
                row = jnp.where(row >= t, row - t, row)
                s = jnp.where(col <= row, s, NEG)
            m_prev = m_ref[hp]
            m_new = jnp.maximum(m_prev, jnp.max(s, axis=1, keepdims=True))
            alpha = jnp.exp(m_prev - m_new)
            p = jnp.exp(s - m_new)
            l_ref[hp] = alpha * l_ref[hp] + jnp.sum(p, axis=1, keepdims=True)
            acc_ref[hp] = alpha * acc_ref[hp] + _dot(p.astype(BF16), v)
            m_ref[hp] = m_new

    @pl.when(ki < qi)
    def _():
        step(False)

    @pl.when(ki == qi)
    def _():
        step(True)
        for hp in range(pairs):
            o = acc_ref[hp] / l_ref[hp]
            o_ref[:, hp * LANES:(hp + 1) * LANES] = jnp.where(lane < DH_A, o[:t], o[t:]).astype(o_ref.dtype)


def _fox(q, k, v, fc, batch, seq):
    t = FOX_TILE
    nq = seq // t
    kv_map = lambda b, qi, ki: (b * nq + jnp.minimum(ki, qi), 0)
    return pl.pallas_call(
        _fox_kernel,
        grid=(batch, nq, nq),
        in_specs=[pl.BlockSpec((t, W_A), lambda b, qi, ki: (b * nq + qi, 0)),
                  pl.BlockSpec((t, W_A), kv_map),
                  pl.BlockSpec((t, W_A), kv_map),
                  pl.BlockSpec((H_A, t), lambda b, qi, ki: (0, b * nq + jnp.minimum(ki, qi)))],
        out_specs=pl.BlockSpec((t, W_A), lambda b, qi, ki: (b * nq + qi, 0)),
        out_shape=jax.ShapeDtypeStruct((batch * seq, W_A), BF16),
        scratch_shapes=[pltpu.VMEM((H_A // 2, 2 * t, LANES), BF16),
                        pltpu.VMEM((H_A // 2, 2 * t, 1), F32),
                        pltpu.VMEM((H_A // 2, 2 * t, 1), F32),
                        pltpu.VMEM((H_A // 2, 2 * t, LANES), F32)],
        compiler_params=_cparams(("arbitrary", "arbitrary", "arbitrary")),
        name="fox",
    )(q, k, v, fc)


def _lfsuf_kernel(pt_ref, lf_hbm, o_ref, buf_ref, sem):
    b = pl.program_id(0)
    n_pages, w = buf_ref.shape

    def page_copy(j):
        return pltpu.make_async_copy(lf_hbm.at[pl.ds(pt_ref[b, j], 1), :], buf_ref.at[pl.ds(j, 1), :], sem)

    def issue(j, carry):
        page_copy(j).start()
        return carry

    def drain(j, carry):
        page_copy(j).wait()
        return carry

    lax.fori_loop(0, n_pages, issue, 0)
    lax.fori_loop(0, n_pages, drain, 0)

    x = buf_ref[...]
    lane = lax.broadcasted_iota(I32, x.shape, 1)
    s = x
    shift = H_A
    while shift < w:
        s = s + jnp.where(lane + shift < w, pltpu.roll(s, w - shift, axis=1), 0.0)
        shift *= 2
    tot = jnp.where(lane < H_A, s, 0.0)
    shift = H_A
    while shift < w:
        tot = tot + pltpu.roll(tot, shift, axis=1)
        shift *= 2
    r = lax.broadcasted_iota(I32, (n_pages, n_pages), 0)
    c = lax.broadcasted_iota(I32, (n_pages, n_pages), 1)
    later = jnp.where(c > r, 1.0, 0.0).astype(F32)
    o_ref[...] = (s - x) + _dot(later, tot, precision=HIGHEST)


def _lfsuf(page_table, lf_flat):
    nb, n_pages = page_table.shape
    w = lf_flat.shape[1]
    return pl.pallas_call(
        _lfsuf_kernel,
        grid_spec=pltpu.PrefetchScalarGridSpec(
            num_scalar_prefetch=1,
            grid=(nb,),
            in_specs=[pl.BlockSpec(memory_space=pl.ANY)],
            out_specs=pl.BlockSpec((None, n_pages, w), lambda b, pt: (b, 0, 0)),
            scratch_shapes=[pltpu.VMEM((n_pages, w), F32), pltpu.SemaphoreType.DMA(())]),
        out_shape=jax.ShapeDtypeStruct((nb, n_pages, w), F32),
        compiler_params=_cparams(("arbitrary",)),
        name="lfsuf",
    )(page_table, lf_flat)


def _foxdec_kernel(pt_ref, q_ref, kn_ref, vn_ref, lfn_ref, suf_ref, ck_hbm, cv_hbm, o_ref,
                   kbuf, vbuf, sem, m_ref, l_ref, acc_ref):
    b = pl.program_id(0)
    n_pages = suf_ref.shape[0]
    page_rows = kbuf.shape[1] // DEC_PAGES
    n_chunks = n_pages // DEC_PAGES
    rows = q_ref.shape[0]
    flat = DEC_PAGES * page_rows * H_A

    def page_copies(c, slot, j):
        pg = pt_ref[b, c * DEC_PAGES + j]
        dst = pl.ds(j * page_rows, page_rows)
        return (pltpu.make_async_copy(ck_hbm.at[pg], kbuf.at[slot, dst], sem.at[0, slot]),
                pltpu.make_async_copy(cv_hbm.at[pg], vbuf.at[slot, dst], sem.at[1, slot]))

    def fetch(c, slot):
        for j in range(DEC_PAGES):
            for cp in page_copies(c, slot, j):
                cp.start()

    def wait(c, slot):
        for j in range(DEC_PAGES):
            for cp in page_copies(c, slot, j):
                cp.wait()

    q = q_ref[...].astype(BF16)
    m_ref[...] = jnp.full_like(m_ref, NEG)
    l_ref[...] = jnp.zeros_like(l_ref)
    acc_ref[...] = jnp.zeros_like(acc_ref)

    def attend(kf, vf, bias, ok):
        s = _dot_nt(q, kf) + bias
        s = jnp.where(ok, s, NEG)
        m_prev = m_ref[...]
        m_new = jnp.maximum(m_prev, jnp.max(s, axis=1, keepdims=True))
        alpha = jnp.exp(m_prev - m_new)
        p = jnp.where(ok, jnp.exp(s - m_new), 0.0)
        l_ref[...] = alpha * l_ref[...] + jnp.sum(p, axis=1, keepdims=True)
        acc_ref[...] = alpha * acc_ref[...] + _dot(p.astype(BF16), vf)
        m_ref[...] = m_new

    row = lax.broadcasted_iota(I32, (rows, flat), 0)
    col = lax.broadcasted_iota(I32, (rows, flat), 1)
    same_head = (col % H_A) == (row % H_A)

    fetch(0, 0)

    def chunk(c, carry):
        slot = c % 2

        @pl.when(c + 1 < n_chunks)
        def _():
            fetch(c + 1, 1 - slot)

        wait(c, slot)
        kf = kbuf[slot].reshape(flat, DH_A).astype(BF16)
        vf = vbuf[slot].reshape(flat, DH_A).astype(BF16)
        bias = suf_ref[pl.ds(pl.multiple_of(c * DEC_PAGES, DEC_PAGES), DEC_PAGES), :]
        bias = jnp.concatenate([bias[j:j + 1, :] for j in range(DEC_PAGES)], axis=1)
        attend(kf, vf, bias, same_head)
        return carry

    lax.fori_loop(0, n_chunks, chunk, 0)

    lfn = jnp.broadcast_to(lfn_ref[...], (8, rows))
    rr = lax.broadcasted_iota(I32, (rows, rows), 0)
    cc = lax.broadcasted_iota(I32, (rows, rows), 1)
    incl = jnp.where(((rr % H_A) == (cc % H_A)) & (rr <= cc), 1.0, 0.0).astype(F32)
    fn = _dot(lfn, incl, precision=HIGHEST)[0:1, :]
    ok = ((cc % H_A) == (rr % H_A)) & (cc // H_A <= rr // H_A)
    attend(kn_ref[...].astype(BF16), vn_ref[...].astype(BF16), -fn, ok)
    o_ref[...] = acc_ref[...] / l_ref[...]


def _foxdec(page_table, q, kn, vn, lfn, suf, cache_k, cache_v):
    nb, rows, _ = q.shape
    n_pages = page_table.shape[1]
    page_rows = cache_k.shape[1]
    per_seq = lambda shape: pl.BlockSpec((None,) + shape, lambda b, pt: (b,) + (0,) * len(shape))
    return pl.pallas_call(
        _foxdec_kernel,
        grid_spec=pltpu.PrefetchScalarGridSpec(
            num_scalar_prefetch=1,
            grid=(nb,),
            in_specs=[per_seq((rows, DH_A)), per_seq((rows, DH_A)), per_seq((rows, DH_A)), per_seq((1, rows)),
                      per_seq((n_pages, page_rows * H_A)),
                      pl.BlockSpec(memory_space=pl.ANY), pl.BlockSpec(memory_space=pl.ANY)],
            out_specs=per_seq((rows, DH_A)),
            scratch_shapes=[pltpu.VMEM((2, DEC_PAGES * page_rows, H_A, DH_A), F32),
                            pltpu.VMEM((2, DEC_PAGES * page_rows, H_A, DH_A), F32),
                            pltpu.SemaphoreType.DMA((2, 2)),
                            pltpu.VMEM((rows, 1), F32), pltpu.VMEM((rows, 1), F32), pltpu.VMEM((rows, DH_A), F32)]),
        out_shape=jax.ShapeDtypeStruct((nb, rows, DH_A), F32),
        compiler_params=_cparams(("arbitrary",)),
        name="foxdec",
    )(page_table, q, kn, vn, lfn, suf, cache_k, cache_v)


def _gla_kernel(q_ref, k_ref, g_ref, v_ref, r_ref, s0_ref, gn_ref, y_ref, sfin_ref, state_ref):
    tb = pl.program_id(1)
    n_tb = pl.num_programs(1)
    c = GLA_CHUNK
    n_chunks = q_ref.shape[0] // c
    n_sub = c // GLA_SUB

    row_k = lax.broadcasted_iota(I32, (WK_B, WV_B), 0) // DK_B
    col_v = lax.broadcasted_iota(I32, (WK_B, WV_B), 1) // DV_B
    state_diag = row_k == col_v

    @pl.when(tb == 0)
    def _():
        state_ref[...] = jnp.zeros_like(state_ref)
        for h in range(H_B):
            state_ref[h * DK_B:(h + 1) * DK_B, h * DV_B:(h + 1) * DV_B] = s0_ref[h]

    r = lax.broadcasted_iota(I32, (c, c), 0)
    s = lax.broadcasted_iota(I32, (c, c), 1)
    same_sub = (r // GLA_SUB) == (s // GLA_SUB)
    cum_ops = jnp.concatenate([
        jnp.where(same_sub & (s <= r), 1.0, 0.0),
        jnp.where(s <= r, 1.0, 0.0),
        jnp.where(same_sub, 1.0, 0.0),
    ], axis=0).astype(F32)

    t_i = lax.broadcasted_iota(I32, (c, H_B * c), 0)
    s_i = lax.broadcasted_iota(I32, (c, H_B * c), 1) % c
    dsub = t_i // GLA_SUB - s_i // GLA_SUB
    intra = (dsub == 0) & (s_i <= t_i)
    head_k = lax.broadcasted_iota(I32, (c, WK_B), 1) // DK_B
    head_v = lax.broadcasted_iota(I32, (c, WV_B), 1) // DV_B

    def stack_heads(x, head_of_lane):
        zero = jnp.zeros_like(x)
        return jnp.concatenate([jnp.where(head_of_lane == h, x, zero) for h in range(H_B)], axis=0)

    def shift_rows(x, n):
        return jnp.concatenate([x[:n], x[:c - n]], axis=0)

    gn = gn_ref[...]
    for ci in range(n_chunks):
        rows = slice(ci * c, (ci + 1) * c)
        q = q_ref[rows, :]
        k = k_ref[rows, :]
        g = g_ref[rows, :]
        v = v_ref[rows, :]

        cums = _dot(cum_ops, g, precision=HIGHEST)
        bl, bc, tot = cums[:c], cums[c:2 * c], cums[2 * c:]
        pre = bc - bl
        q_loc = q * jnp.exp(bl)
        k_loc = k * jnp.exp(-bl)
        k_end = k * jnp.exp(tot - bl)
        q_far = [q_loc] + [q_loc * jnp.exp(pre - shift_rows(pre, GLA_SUB * d)) for d in range(1, n_sub - 1)]

        a0 = _dot_nt(q_loc.astype(BF16), stack_heads(k_loc, head_k).astype(BF16))
        af = _dot_nt(jnp.concatenate(q_far, axis=0).astype(BF16), stack_heads(k_end, head_k).astype(BF16))
        att = jnp.where(intra, a0, 0.0)
        for d in range(1, n_sub):
            att = jnp.where(dsub == d, af[(d - 1) * c:d * c], att)

        q_in = q * jnp.exp(bc)
        o = _dot(att.astype(BF16), stack_heads(v, head_v)) + _dot(q_in.astype(BF16), state_ref[...].astype(BF16))

        k_out = k * jnp.exp(bc[c - 1:c, :] - bc)
        kv = _dot_tn(k_out.astype(BF16), v)
        decay = jnp.exp(_dot_tn(g, jnp.ones((c, LANES), F32), precision=HIGHEST))
        state_ref[...] = (state_ref[...] * jnp.concatenate([decay] * (WV_B // LANES), axis=1)
                          + jnp.where(state_diag, kv, 0.0))

        for h in range(H_B):
            lanes = slice(h * DV_B, (h + 1) * DV_B)
            oh = o[:, lanes]
            yh = oh * lax.rsqrt(jnp.mean(oh * oh, axis=-1, keepdims=True) + EPS) * gn
            y_ref[rows, lanes] = (yh * r_ref[rows, lanes].astype(F32)).astype(y_ref.dtype)

    @pl.when(tb == n_tb - 1)
    def _():
        for h in range(H_B):
            sfin_ref[h] = state_ref[h * DK_B:(h + 1) * DK_B, h * DV_B:(h + 1) * DV_B]


def _gla(q, k, g, v, r, s0, gn, batch, seq, tb_rows):
    n_tb = seq // tb_rows
    row = lambda w: pl.BlockSpec((tb_rows, w), lambda b, t: (b * n_tb + t, 0))
    state = pl.BlockSpec((None, H_B, DK_B, DV_B), lambda b, t: (b, 0, 0, 0))
    return pl.pallas_call(
        _gla_kernel,
        grid=(batch, n_tb),
        in_specs=[row(WK_B), row(WK_B), row(WK_B), row(WV_B), row(WV_B), state,
                  pl.BlockSpec((1, DV_B), lambda b, t: (0, 0))],
        out_specs=[row(WV_B), state],
        out_shape=[jax.ShapeDtypeStruct((batch * seq, WV_B), BF16),
                   jax.ShapeDtypeStruct((batch, H_B, DK_B, DV_B), F32)],
        scratch_shapes=[pltpu.VMEM((WK_B, WV_B), F32)],
        compiler_params=_cparams(("arbitrary", "arbitrary")),
        name="gla",
    )(q, k, g, v, r, s0, gn)


def _mixout_kernel(*refs, n_tail):
    (x_ref, ya_ref, yb_ref, ga_ref, gb_ref, mod_ref, g2_ref, wpa_ref, wpb_ref, wo_ref, rw_ref) = refs[:11]
    tail_refs = refs[11:11 + n_tail]
    x1_ref, hp_ref, lg_ref = refs[11 + n_tail:]
    i = pl.program_id(0)
    n_own = pl.num_programs(0) - (1 if n_tail else 0)

    @pl.when(i < n_own)
    def _():
        br_a = _dot(ya_ref[...], wpa_ref[...])
        br_b = _dot(yb_ref[...], wpb_ref[...])
        merged = ga_ref[...].astype(F32) * br_a + gb_ref[...].astype(F32) * br_b
        out = _dot(merged.astype(BF16), wo_ref[...])
        x1 = x_ref[...] + mod_ref[2] * out
        x1_ref[...] = x1
        y = x1 * lax.rsqrt(jnp.mean(x1 * x1, axis=-1, keepdims=True) + EPS) * g2_ref[...]
        h = y * (1.0 + mod_ref[4]) + mod_ref[3]
        hp_ref[...] = _pack_bf16_pair(h)
        lg_ref[...] = _dot_nt(rw_ref[...], h, precision=HIGHEST)

    if n_tail:
        @pl.when(i == n_own)
        def _():
            for src, dst in zip(tail_refs, (x1_ref, hp_ref, lg_ref)):
                dst[...] = src[...]


def _mixout(x, ya, yb, ga, gb, mod, rows_per_mod, g2, wpa, wpb, wo, rw, tail):
    n = x.shape[0]
    tm = MOE_TILE
    mod_rows = mod.shape[2]
    tiles_per_mod = rows_per_mod // tm
    n_own = n // tm
    n_total = n + (tm if tail else 0)
    own = lambda i: jnp.minimum(i, n_own - 1)
    row = lambda w: pl.BlockSpec((tm, w), lambda i: (own(i), 0))
    return pl.pallas_call(
        functools.partial(_mixout_kernel, n_tail=len(tail)),
        grid=(n_total // tm,),
        in_specs=[row(D_MODEL), row(W_A), row(WV_B), row(D_MODEL), row(D_MODEL),
                  pl.BlockSpec((6, None, mod_rows, D_MODEL), lambda i: (0, own(i) // tiles_per_mod, 0, 0)),
                  _const_spec(g2.shape), _const_spec(wpa.shape), _const_spec(wpb.shape), _const_spec(wo.shape),
                  _const_spec(rw.shape)] + [_const_spec(t.shape) for t in tail],
        out_specs=[pl.BlockSpec((tm, D_MODEL), lambda i: (i, 0)),
                   pl.BlockSpec((tm, D_MODEL // 2), lambda i: (i, 0)),
                   pl.BlockSpec((N_EXPERTS, tm), lambda i: (0, i))],
        out_shape=[jax.ShapeDtypeStruct((n_total, D_MODEL), F32),
                   jax.ShapeDtypeStruct((n_total, D_MODEL // 2), U32),
                   jax.ShapeDtypeStruct((N_EXPERTS, n_total), F32)],
        compiler_params=_cparams(("arbitrary",)),
        name="mixout",
    )(x, ya, yb, ga, gb, mod, g2, wpa, wpb, wo, rw, *tail)


def _route_kernel(lg_ref, bias_ref, idx_ref, w_ref, rank_ref, cnt_ref, carry_ref):
    i = pl.program_id(0)
    tm = lg_ref.shape[1]

    @pl.when(i == 0)
    def _():
        carry_ref[...] = jnp.zeros_like(carry_ref)

    s = _sigmoid(lg_ref[...])
    sb = s + bias_ref[...][:, 0:1]
    ninf = -jnp.inf

    e_in_group = lax.broadcasted_iota(I32, (GROUP_SIZE, tm), 0)
    gscores = []
    for g in range(N_GROUPS):
        blk = sb[g * GROUP_SIZE:(g + 1) * GROUP_SIZE, :]
        m1 = jnp.max(blk, axis=0, keepdims=True)
        first = jnp.min(jnp.where(blk == m1, e_in_group, GROUP_SIZE), axis=0, keepdims=True)
        m2 = jnp.max(jnp.where(e_in_group == first, ninf, blk), axis=0, keepdims=True)
        gscores.append(m1 + m2)
    gs = jnp.concatenate(gscores, axis=0)

    g_iota = lax.broadcasted_iota(I32, (N_GROUPS, tm), 0)
    chosen = jnp.zeros((N_GROUPS, tm), F32)
    for _ in range(TOPK_GROUPS):
        m = jnp.max(gs, axis=0, keepdims=True)
        first = jnp.min(jnp.where(gs == m, g_iota, N_GROUPS), axis=0, keepdims=True)
        pick = g_iota == first
        chosen = jnp.where(pick, 1.0, chosen)
        gs = jnp.where(pick, ninf, gs)
    allowed = jnp.concatenate(
        [jnp.broadcast_to(chosen[g:g + 1, :], (GROUP_SIZE, tm)) for g in range(N_GROUPS)], axis=0) > 0.5
    cand = jnp.where(allowed, sb, ninf)

    e_iota = lax.broadcasted_iota(I32, (N_EXPERTS, tm), 0)
    onehot = jnp.zeros((N_EXPERTS, tm), F32)
    idxs, ws = [], []
    for _ in range(TOP_K):
        m = jnp.max(cand, axis=0, keepdims=True)
        first = jnp.min(jnp.where(cand == m, e_iota, N_EXPERTS), axis=0, keepdims=True)
        pick = e_iota == first
        idxs.append(first)
        ws.append(jnp.sum(jnp.where(pick, s, 0.0), axis=0, keepdims=True))
        onehot = jnp.where(pick, 1.0, onehot)
        cand = jnp.where(pick, ninf, cand)
    w = jnp.concatenate(ws, axis=0)
    w_ref[...] = w / jnp.sum(w, axis=0, keepdims=True) * ROUTE_SCALE
    idx_ref[...] = jnp.concatenate(idxs, axis=0)

    r = lax.broadcasted_iota(I32, (tm, tm), 0)
    c = lax.broadcasted_iota(I32, (tm, tm), 1)
    earlier = jnp.where(r < c, 1.0, 0.0).astype(BF16)
    carry = carry_ref[...]
    before = _dot(onehot.astype(BF16), earlier) + jnp.concatenate([carry] * (tm // LANES), axis=1)
    rank_ref[...] = jnp.concatenate(
        [jnp.sum(jnp.where(e_iota == ix, before, 0.0), axis=0, keepdims=True) for ix in idxs], axis=0).astype(I32)
    carry = carry + _dot(onehot.astype(BF16), jnp.ones((tm, LANES), BF16))
    carry_ref[...] = carry
    cnt_ref[...] = carry


def _route(logits_t, bias):
    n = logits_t.shape[1]
    tm = MOE_TILE
    tok = lambda: pl.BlockSpec((TOP_K, tm), lambda i: (0, i))
    return pl.pallas_call(
        _route_kernel,
        grid=(n // tm,),
        in_specs=[pl.BlockSpec((N_EXPERTS, tm), lambda i: (0, i)), _const_spec(bias.shape)],
        out_specs=[tok(), tok(), tok(), _const_spec((N_EXPERTS, LANES))],
        out_shape=[jax.ShapeDtypeStruct((TOP_K, n), I32), jax.ShapeDtypeStruct((TOP_K, n), F32),
                   jax.ShapeDtypeStruct((TOP_K, n), I32), jax.ShapeDtypeStruct((N_EXPERTS, LANES), F32)],
        scratch_shapes=[pltpu.VMEM((N_EXPERTS, LANES), F32)],
        compiler_params=_cparams(("arbitrary",)),
        name="route",
    )(logits_t, bias)


def _row_wait(hbm_ref, n_rows, sem):
    pltpu.make_async_copy(hbm_ref.at[pl.ds(0, n_rows), :], hbm_ref.at[pl.ds(0, n_rows), :], sem).wait()


def _dispatch_kernel(dest_hbm, h_ref, zeros_hbm, xs_hbm, idx_smem, sem):
    del zeros_hbm
    i = pl.program_id(0)
    tm = h_ref.shape[0]
    cp = pltpu.make_async_copy(dest_hbm.at[i], idx_smem, sem.at[0])
    cp.start()
    cp.wait()

    def body(r, carry):
        for kk in range(TOP_K):
            pltpu.make_async_copy(h_ref.at[pl.ds(r, 1), :], xs_hbm.at[pl.ds(idx_smem[kk, r], 1), :], sem.at[1]).start()
        return carry

    lax.fori_loop(0, tm, body, 0)
    _row_wait(xs_hbm, tm * TOP_K, sem.at[1])


def _dispatch(dest_tiles, hp, zeros):
    n, w = hp.shape
    tm = MOE_TILE
    return pl.pallas_call(
        _dispatch_kernel,
        grid=(n // tm,),
        in_specs=[pl.BlockSpec(memory_space=pl.ANY),
                  pl.BlockSpec((tm, w), lambda i: (i, 0)),
                  pl.BlockSpec(memory_space=pl.ANY)],
        out_specs=pl.BlockSpec(memory_space=pl.ANY),
        out_shape=jax.ShapeDtypeStruct(zeros.shape, zeros.dtype),
        scratch_shapes=[pltpu.SMEM((TOP_K, tm), I32), pltpu.SemaphoreType.DMA((2,))],
        input_output_aliases={2: 0},
        compiler_params=_cparams(("arbitrary",)),
        name="dispatch",
    )(dest_tiles, hp, zeros)


def _experts_kernel(te_ref, nv_ref, xs_ref, wg_ref, wu_ref, wd_ref, ys_ref):
    i = pl.program_id(0)

    @pl.when(i < nv_ref[0])
    def _():
        half = D_MODEL // 2
        lo, hi = _unpack_bf16_pair(xs_ref[...])
        wg = wg_ref[...].astype(BF16)
        wu = wu_ref[...].astype(BF16)
        gate = _dot(lo, wg[:half]) + _dot(hi, wg[half:])
        up = _dot(lo, wu[:half]) + _dot(hi, wu[half:])
        act = (_silu(gate) * up).astype(BF16)
        ys_ref[...] = _pack_bf16_pair(_dot(act, wd_ref[...].astype(BF16)))

    @pl.when(i >= nv_ref[0])
    def _():
        ys_ref[...] = jnp.zeros_like(ys_ref)


def _experts(tile_expert, n_valid, xs, wg, wu, wd):
    n_rows, w = xs.shape
    tm = MOE_TILE
    tile = lambda i, te, nv: (jnp.minimum(i, nv[0] - 1), 0)
    return pl.pallas_call(
        _experts_kernel,
        grid_spec=pltpu.PrefetchScalarGridSpec(
            num_scalar_prefetch=2,
            grid=(n_rows // tm,),
            in_specs=[pl.BlockSpec((tm, w), tile),
                      pl.BlockSpec((None, D_MODEL, D_EXPERT), lambda i, te, nv: (te[i], 0, 0)),
                      pl.BlockSpec((None, D_MODEL, D_EXPERT), lambda i, te, nv: (te[i], 0, 0)),
                      pl.BlockSpec((None, D_EXPERT, D_MODEL), lambda i, te, nv: (te[i], 0, 0))],
            out_specs=pl.BlockSpec((tm, w), lambda i, te, nv: (i, 0))),
        out_shape=jax.ShapeDtypeStruct((n_rows, w), U32),
        compiler_params=_cparams(("arbitrary",)),
        name="experts",
    )(tile_expert, n_valid, xs, wg, wu, wd)


def _combine_kernel(dest_hbm, ys_hbm, x1_ref, hp_ref, w_ref, mod_ref, fg_ref, sg_ref, su_ref, sd_ref, o_ref,
                    idx_smem, ybuf, sem, *, tile_offset):
    i = pl.program_id(0)
    tm = x1_ref.shape[0]
    half = D_MODEL // 2
    cp = pltpu.make_async_copy(dest_hbm.at[i + tile_offset], idx_smem, sem.at[0])
    cp.start()
    cp.wait()

    def body(r, carry):
        for kk in range(TOP_K):
            pltpu.make_async_copy(ys_hbm.at[pl.ds(idx_smem[kk, r], 1), :], ybuf.at[kk, pl.ds(r, 1), :], sem.at[1]).start()
        return carry

    lax.fori_loop(0, tm, body, 0)

    lo, hi = _unpack_bf16_pair(hp_ref[...])
    sg = sg_ref[...]
    su = su_ref[...]
    gate = _dot(lo, sg[:half]) + _dot(hi, sg[half:])
    up = _dot(lo, su[:half]) + _dot(hi, su[half:])
    y = _dot((_silu(gate) * up).astype(BF16), sd_ref[...])

    _row_wait(ys_hbm, tm * TOP_K, sem.at[1])
    w = w_ref[...]
    for kk in range(TOP_K):
        ylo, yhi = _unpack_bf16_pair(ybuf[kk])
        y = y + w[:, kk:kk + 1] * jnp.concatenate([ylo.astype(F32), yhi.astype(F32)], axis=1)
    x2 = x1_ref[...] + mod_ref[5] * y
    o_ref[...] = x2 * lax.rsqrt(jnp.mean(x2 * x2, axis=-1, keepdims=True) + EPS) * fg_ref[...]


def _combine(dest_tiles, ys, x1, hp, w, mod, rows_per_mod, fg, sg, su, sd, n, row_offset):
    tm = MOE_TILE
    mod_rows = mod.shape[2]
    tiles_per_mod = rows_per_mod // tm
    off = row_offset // tm
    return pl.pallas_call(
        functools.partial(_combine_kernel, tile_offset=off),
        grid=(n // tm,),
        in_specs=[pl.BlockSpec(memory_space=pl.ANY), pl.BlockSpec(memory_space=pl.ANY),
                  pl.BlockSpec((tm, D_MODEL), lambda i: (i + off, 0)),
                  pl.BlockSpec((tm, D_MODEL // 2), lambda i: (i + off, 0)),
                  pl.BlockSpec((tm, TOP_K), lambda i: (i + off, 0)),
                  pl.BlockSpec((6, None, mod_rows, D_MODEL), lambda i: (0, i // tiles_per_mod, 0, 0)),
                  _const_spec(fg.shape), _const_spec(sg.shape), _const_spec(su.shape), _const_spec(sd.shape)],
        out_specs=pl.BlockSpec((tm, D_MODEL), lambda i: (i, 0)),
        out_shape=jax.ShapeDtypeStruct((n, D_MODEL), F32),
        scratch_shapes=[pltpu.SMEM((TOP_K, tm), I32), pltpu.VMEM((TOP_K, tm, D_MODEL // 2), U32),
                        pltpu.SemaphoreType.DMA((2,))],
        compiler_params=_cparams(("arbitrary",)),
        name="combine",
    )(dest_tiles, ys, x1, hp, w, mod, fg, sg, su, sd)


def _split_w_in(w_in):
    sizes = (W_A, W_A, W_A, H_A, WK_B, WK_B, WV_B, WV_B, GK_RANK, D_MODEL, D_MODEL)
    segs, o = [], 0
    for s in sizes:
        segs.append(w_in[:, o:o + s])
        o += s
    return segs


def kernel(x_prompt, x_sample, cache_k, cache_v, cache_logf, state_gla, page_table, c_prompt, c_sample, w_ada, b_ada,
           norm1_g, w_in, b_f, w_gk2, b_gk, gla_norm_g, w_pa, w_pb, w_o, norm2_g, router_w, router_bias, w_gate, w_up,
           w_down, ws_gate, ws_up, ws_down, final_g):
    assert w_ada.shape[0] == 1, "single layer"
    bp, tp, d = x_prompt.shape
    bs, ts, _ = x_sample.shape
    n_p, n_s = bp * tp, bs * ts
    n_tot = n_p + n_s
    assert n_s == MOE_TILE and n_p % MOE_TILE == 0 and tp % FOX_TILE == 0

    q_a, k_a, v_a, f_a, q_b, k_b, v_b, r_b, gk1, gate_a, gate_b = _split_w_in(w_in[0])
    wts = (jnp.concatenate([q_a, k_a, v_a], axis=1).astype(BF16),
           f_a.T.astype(BF16),
           b_f[0].reshape(H_A, 1),
           jnp.concatenate([q_b, k_b, v_b, r_b], axis=1).astype(BF16),
           jnp.pad(gk1, ((0, 0), (0, LANES - GK_RANK))).astype(BF16),
           jnp.pad(w_gk2[0], ((0, LANES - GK_RANK), (0, 0))).astype(BF16),
           b_gk[0].reshape(1, WK_B),
           jnp.concatenate([gate_a, gate_b], axis=1).astype(BF16))
    g1 = norm1_g[0].reshape(1, d)
    g2 = norm2_g[0].reshape(1, d)
    gn = gla_norm_g[0].reshape(1, DV_B)
    fg = final_g.reshape(1, d)
    wpa, wpb, wo = w_pa[0].astype(BF16), w_pb[0].astype(BF16), w_o[0].astype(BF16)
    rw_t = router_w[0].T
    rbias = jnp.broadcast_to(router_bias[0].reshape(N_EXPERTS, 1), (N_EXPERTS, LANES))
    sg, su, sd = ws_gate[0].astype(BF16), ws_up[0].astype(BF16), ws_down[0].astype(BF16)

    ada = _ada(jnp.concatenate([c_prompt, c_sample], axis=0), w_ada[0], b_ada[0].reshape(1, -1))
    ada = ada.reshape(bp + bs, 6, d)
    mod_p = ada[:bp].transpose(1, 0, 2)[:, :, None, :]
    mod_s = jnp.repeat(ada[bp:], ts, axis=0).transpose(1, 0, 2)[:, None, :, :]

    xp = x_prompt.reshape(n_p, d)
    tm_p = 256
    (q, k, v, kh, vh, lf_t, fc_t, qb, kb, gk, vb, rb, ga_p, gb_p) = _inproj(xp, mod_p, tp, tp // tm_p, tm_p, g1, wts)
    ya_p = _fox(q, kh, vh, fc_t, bp, tp)
    yb_p, s_p = _gla(qb, kb, gk, vb, rb, jnp.zeros((bp, H_B, DK_B, DV_B), F32), gn, bp, tp, 256)
    k_prompt = k.reshape(1, bp, tp, H_A, DH_A)
    v_prompt = v.reshape(1, bp, tp, H_A, DH_A)
    logf_prompt = lf_t.T.reshape(1, bp, tp, H_A)

    xs_in = x_sample.reshape(n_s, d)
    (q, k, v, _, _, lf_t, _, qb, kb, gk, vb, rb, ga, gb) = _inproj(xs_in, mod_s, n_s, 1, n_s, g1, wts)
    n_phys, page_rows = cache_k.shape[1], cache_k.shape[2]
    suf = _lfsuf(page_table, cache_logf[0].reshape(n_phys, page_rows * H_A))
    rows = ts * H_A
    ya = _foxdec(page_table,
                 q.astype(F32).reshape(bs, rows, DH_A), k.reshape(bs, rows, DH_A), v.reshape(bs, rows, DH_A),
                 lf_t.T.reshape(bs, 1, rows), suf, cache_k[0], cache_v[0])
    ya = ya.reshape(n_s, W_A).astype(BF16)
    pad = lambda a: jnp.pad(a.reshape(bs, ts, -1), ((0, 0), (0, GLA_CHUNK - ts), (0, 0))).reshape(bs * GLA_CHUNK, -1)
    yb, s_s = _gla(pad(qb), pad(kb), pad(gk), pad(vb), pad(rb), state_gla[0], gn, bs, GLA_CHUNK, GLA_CHUNK)
    yb = yb.reshape(bs, GLA_CHUNK, WV_B)[:, :ts].reshape(n_s, WV_B)
    k_sample = k.reshape(1, bs, ts, H_A, DH_A)
    v_sample = v.reshape(1, bs, ts, H_A, DH_A)
    logf_sample = lf_t.T.reshape(1, bs, ts, H_A)
    tail = _mixout(xs_in, ya, yb, ga, gb, mod_s, n_s, g2, wpa, wpb, wo, rw_t, ())

    x1, hp, logits_t = _mixout(xp, ya_p, yb_p, ga_p, gb_p, mod_p, tp, g2, wpa, wpb, wo, rw_t, tuple(tail))
    idx_t, w_t, rank_t, counts = _route(logits_t, rbias)
    counts = counts[:, 0].astype(I32)
    padded = (counts + MOE_TILE - 1) // MOE_TILE * MOE_TILE
    pad_end = jnp.cumsum(padded)
    pad_start = pad_end - padded
    dest = pad_start[idx_t] + rank_t
    n_tiles = n_tot * TOP_K // MOE_TILE + N_EXPERTS
    tile_expert = jnp.minimum(
        jnp.searchsorted(pad_end, jnp.arange(n_tiles, dtype=I32) * MOE_TILE, side='right'), N_EXPERTS - 1).astype(I32)
    n_valid = (pad_end[-1:] // MOE_TILE).astype(I32)
    dest_tiles = dest.reshape(TOP_K, n_tot // MOE_TILE, MOE_TILE).transpose(1, 0, 2)

    xs_sorted = _dispatch(dest_tiles, hp, jnp.zeros((n_tiles * MOE_TILE, d // 2), U32))
    ys_sorted = _experts(tile_expert, n_valid, xs_sorted, w_gate[0], w_up[0], w_down[0])
    w_tok = w_t.T
    y_prompt = _combine(dest_tiles, ys_sorted, x1, hp, w_tok, mod_p, tp, fg, sg, su, sd, n_p, 0)
    y_sample = _combine(dest_tiles, ys_sorted, x1, hp, w_tok, mod_s, n_s, fg, sg, su, sd, n_s, n_p)

    return (y_prompt.reshape(bp, tp, d), y_sample.reshape(bs, ts, d),
            k_prompt, v_prompt, logf_prompt, s_p[None],
            k_sample, v_sample, logf_sample, s_s[None])
```

```python
import functools

import jax
import jax.numpy as jnp
from jax import lax
from jax.experimental import pallas as pl
from jax.experimental.pallas import tpu as pltpu

F32 = jnp.float32
BF16 = jnp.bfloat16
U32 = jnp.uint32
I32 = jnp.int32

EPS = 1e-6
D_MODEL = 1024
H_A, DH_A, W_A = 8, 64, 512
H_B, DK_B, DV_B, WK_B, WV_B = 4, 64, 128, 256, 512
GK_RANK = 16
GATE_NORM = 16.0
N_EXPERTS, TOP_K, N_GROUPS, TOPK_GROUPS = 256, 8, 8, 4
GROUP_SIZE = N_EXPERTS // N_GROUPS
D_EXPERT = 256
ROUTE_SCALE = 2.5
MOE_TILE = 128
EXPERT_TILE = 256
LANES = 128
GLA_CHUNK = 64
GLA_SUB = 16
FOX_TILE = 512
DEC_PAGES = 8
NEG = -1e30
VMEM_LIMIT = 56 * 1024 * 1024
HIGHEST = lax.Precision.HIGHEST

NT_DIMS = (((1,), (1,)), ((), ()))
TN_DIMS = (((0,), (0,)), ((), ()))


def _cparams(sem):
    return pltpu.CompilerParams(dimension_semantics=sem, vmem_limit_bytes=VMEM_LIMIT)


def _dot(a, b, **kw):
    return jnp.dot(a, b, preferred_element_type=F32, **kw)


def _dot_nt(a, b, **kw):
    return lax.dot_general(a, b, NT_DIMS, preferred_element_type=F32, **kw)


def _dot_tn(a, b, **kw):
    return lax.dot_general(a, b, TN_DIMS, preferred_element_type=F32, **kw)


def _sigmoid(x):
    return 1.0 / (1.0 + jnp.exp(-x))


def _silu(x):
    return x * _sigmoid(x)


def _log_sigmoid(x):
    return jnp.minimum(x, 0.0) - jnp.log(1.0 + jnp.exp(-jnp.abs(x)))


def _pack_bf16_pair(x):
    c = x.shape[1] // 2
    bits = pltpu.bitcast(x.astype(BF16).astype(F32), U32)
    return (bits[:, c:] & jnp.uint32(0xFFFF0000)) | (bits[:, :c] >> 16)


def _unpack_bf16_pair(p):
    lo = pltpu.bitcast(p << 16, F32).astype(BF16)
    hi = pltpu.bitcast(p & jnp.uint32(0xFFFF0000), F32).astype(BF16)
    return lo, hi


def _ada_kernel(c_ref, w_ref, b_ref, o_ref):
    c = c_ref[...]
    o_ref[...] = _dot(_silu(c).astype(BF16), w_ref[...].astype(BF16)) + b_ref[...]


def _ada(c, w, b):
    n, d = c.shape
    cols = w.shape[1]
    tn = 1536
    return pl.pallas_call(
        _ada_kernel,
        grid=(cols // tn,),
        in_specs=[pl.BlockSpec((n, d), lambda j: (0, 0)),
                  pl.BlockSpec((d, tn), lambda j: (0, j)),
                  pl.BlockSpec((1, tn), lambda j: (0, j))],
        out_specs=pl.BlockSpec((n, tn), lambda j: (0, j)),
        out_shape=jax.ShapeDtypeStruct((n, cols), F32),
        compiler_params=_cparams(("arbitrary",)),
        name="ada",
    )(c, w, b)


def _inproj_kernel(x_ref, mod_ref, g_ref, wqkv_ref, wf_ref, bf_ref, wb_ref, wgk1_ref, wgk2_ref, bgk_ref, wgate_ref,
                   q_ref, k_ref, v_ref, kh_ref, vh_ref, lf_ref, fc_ref, qb_ref, kb_ref, gk_ref, vb_ref, rb_ref,
                   ga_ref, gb_ref, carry_ref, *, seq_tiles):
    i = pl.program_id(0)
    tm = x_ref.shape[0]
    x = x_ref[...]
    y = x * lax.rsqrt(jnp.mean(x * x, axis=-1, keepdims=True) + EPS) * g_ref[...]
    h = (y * (1.0 + mod_ref[1]) + mod_ref[0]).astype(BF16)

    z = _dot(h, wqkv_ref[...])
    q_ref[...] = (z[:, :W_A] * DH_A ** -0.5).astype(BF16)
    k = z[:, W_A:2 * W_A]
    v = z[:, 2 * W_A:]
    k_ref[...] = k
    v_ref[...] = v
    kh_ref[...] = k.astype(BF16)
    vh_ref[...] = v.astype(BF16)

    lf = _log_sigmoid(_dot_nt(wf_ref[...], h) + bf_ref[...])
    lf_ref[...] = lf

    @pl.when(i % seq_tiles == 0)
    def _():
        carry_ref[...] = jnp.zeros_like(carry_ref)

    r = lax.broadcasted_iota(I32, (tm, tm), 0)
    c = lax.broadcasted_iota(I32, (tm, tm), 1)
    upper = jnp.where(r <= c, 1.0, 0.0).astype(F32)
    carry = carry_ref[...]
    fc_ref[...] = _dot(lf, upper, precision=HIGHEST) + jnp.concatenate([carry] * (tm // LANES), axis=1)
    carry_ref[...] = carry + _dot(lf, jnp.ones((tm, LANES), F32), precision=HIGHEST)

    z = _dot(h, wb_ref[...])
    qb_ref[...] = z[:, :WK_B] * DK_B ** -0.5
    kb_ref[...] = z[:, WK_B:2 * WK_B]
    vb_ref[...] = z[:, 2 * WK_B:2 * WK_B + WV_B].astype(BF16)
    rb_ref[...] = _silu(z[:, 2 * WK_B + WV_B:]).astype(BF16)

    lr = _dot(h, wgk1_ref[...]).astype(BF16)
    gk_ref[...] = _log_sigmoid(_dot(lr, wgk2_ref[...]) + bgk_ref[...]) * (1.0 / GATE_NORM)

    z = _dot(h, wgate_ref[...])
    ga_ref[...] = _sigmoid(z[:, :D_MODEL]).astype(BF16)
    gb_ref[...] = _sigmoid(z[:, D_MODEL:]).astype(BF16)


def _const_spec(shape):
    return pl.BlockSpec(shape, lambda i: (0,) * len(shape))


def _inproj(x, mod, rows_per_mod, seq_tiles, tm, g, wts):
    n = x.shape[0]
    mod_rows = mod.shape[2]
    tiles_per_mod = rows_per_mod // tm
    wqkv, wf, bf, wb, wgk1, wgk2, bgk, wgate = wts
    row = lambda w: pl.BlockSpec((tm, w), lambda i: (i, 0))
    lane = lambda: pl.BlockSpec((H_A, tm), lambda i: (0, i))
    outs = [
        (row(W_A), (n, W_A), BF16),
        (row(W_A), (n, W_A), F32),
        (row(W_A), (n, W_A), F32),
        (row(W_A), (n, W_A), BF16),
        (row(W_A), (n, W_A), BF16),
        (lane(), (H_A, n), F32),
        (lane(), (H_A, n), F32),
        (row(WK_B), (n, WK_B), F32),
        (row(WK_B), (n, WK_B), F32),
        (row(WK_B), (n, WK_B), F32),
        (row(WV_B), (n, WV_B), BF16),
        (row(WV_B), (n, WV_B), BF16),
        (row(D_MODEL), (n, D_MODEL), BF16),
        (row(D_MODEL), (n, D_MODEL), BF16),
    ]
    return pl.pallas_call(
        functools.partial(_inproj_kernel, seq_tiles=seq_tiles),
        grid=(n // tm,),
        in_specs=[row(D_MODEL),
                  pl.BlockSpec((6, None, mod_rows, D_MODEL), lambda i: (0, i // tiles_per_mod, 0, 0)),
                  _const_spec(g.shape), _const_spec(wqkv.shape), _const_spec(wf.shape), _const_spec(bf.shape),
                  _const_spec(wb.shape), _const_spec(wgk1.shape), _const_spec(wgk2.shape), _const_spec(bgk.shape),
                  _const_spec(wgate.shape)],
        out_specs=[o[0] for o in outs],
        out_shape=[jax.ShapeDtypeStruct(o[1], o[2]) for o in outs],
        scratch_shapes=[pltpu.VMEM((H_A, LANES), F32)],
        compiler_params=_cparams(("arbitrary",)),
        name="inproj",
    )(x, mod, g, wqkv, wf, bf, wb, wgk1, wgk2, bgk, wgate)


def _fox_kernel(q_ref, k_ref, v_ref, fc_ref, o_ref, qs_ref, m_ref, l_ref, acc_ref):
    qi = pl.program_id(1)
    ki = pl.program_id(2)
    t = q_ref.shape[0]
    pairs = H_A // 2
    lane = lax.broadcasted_iota(I32, (t, LANES), 1)

    @pl.when(ki == 0)
    def _():
        for hp in range(pairs):
            q = q_ref[:, hp * LANES:(hp + 1) * LANES]
            qs_ref[hp, :t, :] = jnp.where(lane < DH_A, q, jnp.zeros_like(q))
            qs_ref[hp, t:, :] = jnp.where(lane >= DH_A, q, jnp.zeros_like(q))
        m_ref[...] = jnp.full_like(m_ref, NEG)
        l_ref[...] = jnp.zeros_like(l_ref)
        acc_ref[...] = jnp.zeros_like(acc_ref)

    def step(diagonal):
        for hp in range(pairs):
            k = k_ref[:, hp * LANES:(hp + 1) * LANES]
            v = v_ref[:, hp * LANES:(hp + 1) * LANES]
            s = _dot_nt(qs_ref[hp], k)
            f0 = jnp.broadcast_to(fc_ref[2 * hp:2 * hp + 1, :], (t, t))
            f1 = jnp.broadcast_to(fc_ref[2 * hp + 1:2 * hp + 2, :], (t, t))
            s = s - jnp.concatenate([f0, f1], axis=0)
            if diagonal:
                row = lax.broadcasted_iota(I32, (2 * t, t), 0)
                col = lax.broadcasted_iota(I32, (2 * t, t), 1)
                row = jnp.where(row >= t, row - t, row)
                s = jnp.where(col <= row, s, NEG)
            m_prev = m_ref[hp]
            m_new = jnp.maximum(m_prev, jnp.max(s, axis=1, keepdims=True))
            alpha = jnp.exp(m_prev - m_new)
            p = jnp.exp(s - m_new)
            l_ref[hp] = alpha * l_ref[hp] + jnp.sum(p, axis=1, keepdims=True)
            acc_ref[hp] = alpha * acc_ref[hp] + _dot(p.astype(BF16), v)
            m_ref[hp] = m_new

    @pl.when(ki < qi)
    def _():
        step(False)

    @pl.when(ki == qi)
    def _():
        step(True)
        for hp in range(pairs):
            o = acc_ref[hp] / l_ref[hp]
            o_ref[:, hp * LANES:(hp + 1) * LANES] = jnp.where(lane < DH_A, o[:t], o[t:]).astype(o_ref.dtype)


def _fox(q, k, v, fc, batch, seq):
    t = FOX_TILE
    nq = seq // t
    kv_map = lambda b, qi, ki: (b * nq + jnp.minimum(ki, qi), 0)
    return pl.pallas_call(
        _fox_kernel,
        grid=(batch, nq, nq),
        in_specs=[pl.BlockSpec((t, W_A), lambda b, qi, ki: (b * nq + qi, 0)),
                  pl.BlockSpec((t, W_A), kv_map),
                  pl.BlockSpec((t, W_A), kv_map),
                  pl.BlockSpec((H_A, t), lambda b, qi, ki: (0, b * nq + jnp.minimum(ki, qi)))],
        out_specs=pl.BlockSpec((t, W_A), lambda b, qi, ki: (b * nq + qi, 0)),
        out_shape=jax.ShapeDtypeStruct((batch * seq, W_A), BF16),
        scratch_shapes=[pltpu.VMEM((H_A // 2, 2 * t, LANES), BF16),
                        pltpu.VMEM((H_A // 2, 2 * t, 1), F32),
                        pltpu.VMEM((H_A // 2, 2 * t, 1), F32),
                        pltpu.VMEM((H_A // 2, 2 * t, LANES), F32)],
        compiler_params=_cparams(("arbitrary", "arbitrary", "arbitrary")),
        name="fox",
    )(q, k, v, fc)


def _lfsuf_kernel(pt_ref, lf_hbm, o_ref, buf_ref, sem):
    b = pl.program_id(0)
    n_pages, _, w = buf_ref.shape

    def page_copy(j):
        return pltpu.make_async_copy(lf_hbm.at[0, pt_ref[b, j]], buf_ref.at[j], sem)

    def issue(j, carry):
        page_copy(j).start()
        return carry

    def drain(j, carry):
        page_copy(j).wait()
        return carry

    lax.fori_loop(0, n_pages, issue, 0)
    lax.fori_loop(0, n_pages, drain, 0)

    x = buf_ref[...]
    lane = lax.broadcasted_iota(I32, x.shape, 2)
    s = x
    shift = 1
    while shift < w:
        s = s + jnp.where(lane + shift < w, pltpu.roll(s, w - shift, axis=2), 0.0)
        shift *= 2
    within = s - x
    after = jnp.zeros((H_A, w), F32)
    for j in range(n_pages - 1, -1, -1):
        o_ref[j] = within[j] + after
        after = after + jnp.broadcast_to(s[j][:, 0:1], (H_A, w))


def _lfsuf(page_table, lf_pages):
    nb, n_pages = page_table.shape
    w = lf_pages.shape[3]
    return pl.pallas_call(
        _lfsuf_kernel,
        grid_spec=pltpu.PrefetchScalarGridSpec(
            num_scalar_prefetch=1,
            grid=(nb,),
            in_specs=[pl.BlockSpec(memory_space=pl.ANY)],
            out_specs=pl.BlockSpec((None, n_pages, H_A, w), lambda b, pt: (b, 0, 0, 0)),
            scratch_shapes=[pltpu.VMEM((n_pages, H_A, w), F32), pltpu.SemaphoreType.DMA(())]),
        out_shape=jax.ShapeDtypeStruct((nb, n_pages, H_A, w), F32),
        compiler_params=_cparams(("arbitrary",)),
        name="lfsuf",
    )(page_table, lf_pages)


def _foxdec_kernel(pt_ref, q_ref, kn_ref, vn_ref, lfn_ref, suf_ref, ck_hbm, cv_hbm, o_ref,
                   kbuf, vbuf, sem, m_ref, l_ref, acc_ref):
    b = pl.program_id(0)
    n_pages = suf_ref.shape[0]
    page_rows = kbuf.shape[-1]
    n_chunks = n_pages // DEC_PAGES
    tq = q_ref.shape[1]

    def page_copies(c, slot, j):
        pg = pt_ref[b, c * DEC_PAGES + j]
        return (pltpu.make_async_copy(ck_hbm.at[0, pg], kbuf.at[slot, j], sem.at[0, slot]),
                pltpu.make_async_copy(cv_hbm.at[0, pg], vbuf.at[slot, j], sem.at[1, slot]))

    def fetch(c, slot):
        for j in range(DEC_PAGES):
            for cp in page_copies(c, slot, j):
                cp.start()

    def wait(c, slot):
        for j in range(DEC_PAGES):
            for cp in page_copies(c, slot, j):
                cp.wait()

    q = q_ref[...].astype(BF16)
    m_ref[...] = jnp.full_like(m_ref, NEG)
    l_ref[...] = jnp.zeros_like(l_ref)
    acc_ref[...] = jnp.zeros_like(acc_ref)

    def attend(kt, vt, bias, ok):
        s = jnp.einsum('htd,hds->hts', q, kt, preferred_element_type=F32) + bias
        if ok is not None:
            s = jnp.where(ok, s, NEG)
        m_prev = m_ref[...]
        m_new = jnp.maximum(m_prev, jnp.max(s, axis=2, keepdims=True))
        alpha = jnp.exp(m_prev - m_new)
        p = jnp.exp(s - m_new)
        l_ref[...] = alpha * l_ref[...] + jnp.sum(p, axis=2, keepdims=True)
        acc_ref[...] = alpha * acc_ref[...] + jnp.einsum('hts,hds->htd', p.astype(BF16), vt,
                                                         preferred_element_type=F32)
        m_ref[...] = m_new

    fetch(0, 0)

    def chunk(c, carry):
        slot = c % 2

        @pl.when(c + 1 < n_chunks)
        def _():
            fetch(c + 1, 1 - slot)

        wait(c, slot)
        kt = jnp.concatenate([kbuf[slot, j] for j in range(DEC_PAGES)], axis=2).astype(BF16)
        vt = jnp.concatenate([vbuf[slot, j] for j in range(DEC_PAGES)], axis=2).astype(BF16)
        suf = suf_ref[pl.ds(c * DEC_PAGES, DEC_PAGES)]
        bias = jnp.concatenate([suf[j] for j in range(DEC_PAGES)], axis=1)
        attend(kt, vt, bias[:, None, :], None)
        return carry

    lax.fori_loop(0, n_chunks, chunk, 0)

    r = lax.broadcasted_iota(I32, (page_rows, page_rows), 0)
    c = lax.broadcasted_iota(I32, (page_rows, page_rows), 1)
    fn = _dot(lfn_ref[...], jnp.where(r <= c, 1.0, 0.0).astype(F32), precision=HIGHEST)
    qi = lax.broadcasted_iota(I32, (H_A, tq, page_rows), 1)
    ki = lax.broadcasted_iota(I32, (H_A, tq, page_rows), 2)
    attend(kn_ref[...].astype(BF16), vn_ref[...].astype(BF16), -fn[:, None, :], ki <= qi)
    o_ref[...] = acc_ref[...] / l_ref[...]


def _foxdec(page_table, q, kn, vn, lfn, suf, cache_kt, cache_vt):
    nb, _, tq, _ = q.shape
    n_pages = page_table.shape[1]
    page_rows = cache_kt.shape[4]
    per_seq = lambda shape: pl.BlockSpec((None,) + shape, lambda b, pt: (b,) + (0,) * len(shape))
    page = (H_A, DH_A, page_rows)
    return pl.pallas_call(
        _foxdec_kernel,
        grid_spec=pltpu.PrefetchScalarGridSpec(
            num_scalar_prefetch=1,
            grid=(nb,),
            in_specs=[per_seq((H_A, tq, DH_A)), per_seq(page), per_seq(page), per_seq((H_A, page_rows)),
                      per_seq((n_pages, H_A, page_rows)),
                      pl.BlockSpec(memory_space=pl.ANY), pl.BlockSpec(memory_space=pl.ANY)],
            out_specs=per_seq((H_A, tq, DH_A)),
            scratch_shapes=[pltpu.VMEM((2, DEC_PAGES) + page, F32),
                            pltpu.VMEM((2, DEC_PAGES) + page, F32),
                            pltpu.SemaphoreType.DMA((2, 2)),
                            pltpu.VMEM((H_A, tq, 1), F32), pltpu.VMEM((H_A, tq, 1), F32),
                            pltpu.VMEM((H_A, tq, DH_A), F32)]),
        out_shape=jax.ShapeDtypeStruct((nb, H_A, tq, DH_A), F32),
        compiler_params=_cparams(("arbitrary",)),
        name="foxdec",
    )(page_table, q, kn, vn, lfn, suf, cache_kt, cache_vt)


def _gla_kernel(q_ref, k_ref, g_ref, v_ref, r_ref, s0_ref, gn_ref, y_ref, sfin_ref, state_ref):
    tb = pl.program_id(1)
    n_tb = pl.num_programs(1)
    c = GLA_CHUNK
    n_chunks = q_ref.shape[0] // c
    n_sub = c // GLA_SUB

    row_k = lax.broadcasted_iota(I32, (WK_B, WV_B), 0) // DK_B
    col_v = lax.broadcasted_iota(I32, (WK_B, WV_B), 1) // DV_B
    state_diag = row_k == col_v

    @pl.when(tb == 0)
    def _():
        state_ref[...] = jnp.zeros_like(state_ref)
        for h in range(H_B):
            state_ref[h * DK_B:(h + 1) * DK_B, h * DV_B:(h + 1) * DV_B] = s0_ref[h]

    r = lax.broadcasted_iota(I32, (c, c), 0)
    s = lax.broadcasted_iota(I32, (c, c), 1)
    same_sub = (r // GLA_SUB) == (s // GLA_SUB)
    cum_ops = jnp.concatenate([
        jnp.where(same_sub & (s <= r), 1.0, 0.0),
        jnp.where(s <= r, 1.0, 0.0),
        jnp.where(same_sub, 1.0, 0.0),
    ], axis=0).astype(F32)

    t_i = lax.broadcasted_iota(I32, (c, H_B * c), 0)
    s_i = lax.broadcasted_iota(I32, (c, H_B * c), 1) % c
    dsub = t_i // GLA_SUB - s_i // GLA_SUB
    intra = (dsub == 0) & (s_i <= t_i)
    head_k = lax.broadcasted_iota(I32, (c, WK_B), 1) // DK_B
    head_v = lax.broadcasted_iota(I32, (c, WV_B), 1) // DV_B

    def stack_heads(x, head_of_lane):
        zero = jnp.zeros_like(x)
        return jnp.concatenate([jnp.where(head_of_lane == h, x, zero) for h in range(H_B)], axis=0)

    def shift_rows(x, n):
        return jnp.concatenate([x[:n], x[:c - n]], axis=0)

    gn = gn_ref[...]
    for ci in range(n_chunks):
        rows = slice(ci * c, (ci + 1) * c)
        q = q_ref[rows, :]
        k = k_ref[rows, :]
        g = g_ref[rows, :]
        v = v_ref[rows, :]

        cums = _dot(cum_ops, g, precision=HIGHEST)
        bl, bc, tot = cums[:c], cums[c:2 * c], cums[2 * c:]
        pre = bc - bl
        q_loc = q * jnp.exp(bl)
        k_loc = k * jnp.exp(-bl)
        k_end = k * jnp.exp(tot - bl)
        q_far = [q_loc] + [q_loc * jnp.exp(pre - shift_rows(pre, GLA_SUB * d)) for d in range(1, n_sub - 1)]

        a0 = _dot_nt(q_loc.astype(BF16), stack_heads(k_loc, head_k).astype(BF16))
        af = _dot_nt(jnp.concatenate(q_far, axis=0).astype(BF16), stack_heads(k_end, head_k).astype(BF16))
        att = jnp.where(intra, a0, 0.0)
        for d in range(1, n_sub):
            att = jnp.where(dsub == d, af[(d - 1) * c:d * c], att)

        q_in = q * jnp.exp(bc)
        o = _dot(att.astype(BF16), stack_heads(v, head_v)) + _dot(q_in.astype(BF16), state_ref[...].astype(BF16))

        k_out = k * jnp.exp(bc[c - 1:c, :] - bc)
        kv = _dot_tn(k_out.astype(BF16), v)
        decay = jnp.exp(_dot_tn(g, jnp.ones((c, LANES), F32), precision=HIGHEST))
        state_ref[...] = (state_ref[...] * jnp.concatenate([decay] * (WV_B // LANES), axis=1)
                          + jnp.where(state_diag, kv, 0.0))

        for h in range(H_B):
            lanes = slice(h * DV_B, (h + 1) * DV_B)
            oh = o[:, lanes]
            yh = oh * lax.rsqrt(jnp.mean(oh * oh, axis=-1, keepdims=True) + EPS) * gn
            y_ref[rows, lanes] = (yh * r_ref[rows, lanes].astype(F32)).astype(y_ref.dtype)

    @pl.when(tb == n_tb - 1)
    def _():
        for h in range(H_B):
            sfin_ref[h] = state_ref[h * DK_B:(h + 1) * DK_B, h * DV_B:(h + 1) * DV_B]


def _gla(q, k, g, v, r, s0, gn, batch, seq, tb_rows):
    n_tb = seq // tb_rows
    row = lambda w: pl.BlockSpec((tb_rows, w), lambda b, t: (b * n_tb + t, 0))
    state = pl.BlockSpec((None, H_B, DK_B, DV_B), lambda b, t: (b, 0, 0, 0))
    return pl.pallas_call(
        _gla_kernel,
        grid=(batch, n_tb),
        in_specs=[row(WK_B), row(WK_B), row(WK_B), row(WV_B), row(WV_B), state,
                  pl.BlockSpec((1, DV_B), lambda b, t: (0, 0))],
        out_specs=[row(WV_B), state],
        out_shape=[jax.ShapeDtypeStruct((batch * seq, WV_B), BF16),
                   jax.ShapeDtypeStruct((batch, H_B, DK_B, DV_B), F32)],
        scratch_shapes=[pltpu.VMEM((WK_B, WV_B), F32)],
        compiler_params=_cparams(("arbitrary", "arbitrary")),
        name="gla",
    )(q, k, g, v, r, s0, gn)


def _mixout_kernel(*refs, n_tail):
    (x_ref, ya_ref, yb_ref, ga_ref, gb_ref, mod_ref, g2_ref, wpa_ref, wpb_ref, wo_ref, rw_ref) = refs[:11]
    tail_refs = refs[11:11 + n_tail]
    x1_ref, hp_ref, lg_ref = refs[11 + n_tail:]
    i = pl.program_id(0)
    n_own = pl.num_programs(0) - (1 if n_tail else 0)

    @pl.when(i < n_own)
    def _():
        br_a = _dot(ya_ref[...], wpa_ref[...])
        br_b = _dot(yb_ref[...], wpb_ref[...])
        merged = ga_ref[...].astype(F32) * br_a + gb_ref[...].astype(F32) * br_b
        out = _dot(merged.astype(BF16), wo_ref[...])
        x1 = x_ref[...] + mod_ref[2] * out
        x1_ref[...] = x1
        y = x1 * lax.rsqrt(jnp.mean(x1 * x1, axis=-1, keepdims=True) + EPS) * g2_ref[...]
        h = y * (1.0 + mod_ref[4]) + mod_ref[3]
        hp_ref[...] = _pack_bf16_pair(h)
        lg_ref[...] = _dot_nt(rw_ref[...], h, precision=HIGHEST)

    if n_tail:
        @pl.when(i == n_own)
        def _():
            for src, dst in zip(tail_refs, (x1_ref, hp_ref, lg_ref)):
                dst[...] = src[...]


def _mixout(x, ya, yb, ga, gb, mod, rows_per_mod, g2, wpa, wpb, wo, rw, tail):
    n = x.shape[0]
    tm = MOE_TILE
    mod_rows = mod.shape[2]
    tiles_per_mod = rows_per_mod // tm
    n_own = n // tm
    n_total = n + (tm if tail else 0)
    own = lambda i: jnp.minimum(i, n_own - 1)
    row = lambda w: pl.BlockSpec((tm, w), lambda i: (own(i), 0))
    return pl.pallas_call(
        functools.partial(_mixout_kernel, n_tail=len(tail)),
        grid=(n_total // tm,),
        in_specs=[row(D_MODEL), row(W_A), row(WV_B), row(D_MODEL), row(D_MODEL),
                  pl.BlockSpec((6, None, mod_rows, D_MODEL), lambda i: (0, own(i) // tiles_per_mod, 0, 0)),
                  _const_spec(g2.shape), _const_spec(wpa.shape), _const_spec(wpb.shape), _const_spec(wo.shape),
                  _const_spec(rw.shape)] + [_const_spec(t.shape) for t in tail],
        out_specs=[pl.BlockSpec((tm, D_MODEL), lambda i: (i, 0)),
                   pl.BlockSpec((tm, D_MODEL // 2), lambda i: (i, 0)),
                   pl.BlockSpec((N_EXPERTS, tm), lambda i: (0, i))],
        out_shape=[jax.ShapeDtypeStruct((n_total, D_MODEL), F32),
                   jax.ShapeDtypeStruct((n_total, D_MODEL // 2), U32),
                   jax.ShapeDtypeStruct((N_EXPERTS, n_total), F32)],
        compiler_params=_cparams(("arbitrary",)),
        name="mixout",
    )(x, ya, yb, ga, gb, mod, g2, wpa, wpb, wo, rw, *tail)


def _route_kernel(lg_ref, bias_ref, idx_ref, w_ref, rank_ref, cnt_ref, carry_ref):
    i = pl.program_id(0)
    tm = lg_ref.shape[1]

    @pl.when(i == 0)
    def _():
        carry_ref[...] = jnp.zeros_like(carry_ref)

    s = _sigmoid(lg_ref[...])
    sb = s + bias_ref[...][:, 0:1]
    ninf = -jnp.inf

    e_in_group = lax.broadcasted_iota(I32, (GROUP_SIZE, tm), 0)
    gscores = []
    for g in range(N_GROUPS):
        blk = sb[g * GROUP_SIZE:(g + 1) * GROUP_SIZE, :]
        m1 = jnp.max(blk, axis=0, keepdims=True)
        first = jnp.min(jnp.where(blk == m1, e_in_group, GROUP_SIZE), axis=0, keepdims=True)
        m2 = jnp.max(jnp.where(e_in_group == first, ninf, blk), axis=0, keepdims=True)
        gscores.append(m1 + m2)
    gs = jnp.concatenate(gscores, axis=0)

    g_iota = lax.broadcasted_iota(I32, (N_GROUPS, tm), 0)
    chosen = jnp.zeros((N_GROUPS, tm), F32)
    for _ in range(TOPK_GROUPS):
        m = jnp.max(gs, axis=0, keepdims=True)
        first = jnp.min(jnp.where(gs == m, g_iota, N_GROUPS), axis=0, keepdims=True)
        pick = g_iota == first
        chosen = jnp.where(pick, 1.0, chosen)
        gs = jnp.where(pick, ninf, gs)
    allowed = jnp.concatenate(
        [jnp.broadcast_to(chosen[g:g + 1, :], (GROUP_SIZE, tm)) for g in range(N_GROUPS)], axis=0) > 0.5
    cand = jnp.where(allowed, sb, ninf)

    e_iota = lax.broadcasted_iota(I32, (N_EXPERTS, tm), 0)
    onehot = jnp.zeros((N_EXPERTS, tm), F32)
    idxs, ws = [], []
    for _ in range(TOP_K):
        m = jnp.max(cand, axis=0, keepdims=True)
        first = jnp.min(jnp.where(cand == m, e_iota, N_EXPERTS), axis=0, keepdims=True)
        pick = e_iota == first
        idxs.append(first)
        ws.append(jnp.sum(jnp.where(pick, s, 0.0), axis=0, keepdims=True))
        onehot = jnp.where(pick, 1.0, onehot)
        cand = jnp.where(pick, ninf, cand)
    w = jnp.concatenate(ws, axis=0)
    w_ref[...] = w / jnp.sum(w, axis=0, keepdims=True) * ROUTE_SCALE
    idx_ref[...] = jnp.concatenate(idxs, axis=0)

    r = lax.broadcasted_iota(I32, (tm, tm), 0)
    c = lax.broadcasted_iota(I32, (tm, tm), 1)
    earlier = jnp.where(r < c, 1.0, 0.0).astype(BF16)
    carry = carry_ref[...]
    before = _dot(onehot.astype(BF16), earlier) + jnp.concatenate([carry] * (tm // LANES), axis=1)
    rank_ref[...] = jnp.concatenate(
        [jnp.sum(jnp.where(e_iota == ix, before, 0.0), axis=0, keepdims=True) for ix in idxs], axis=0).astype(I32)
    carry = carry + _dot(onehot.astype(BF16), jnp.ones((tm, LANES), BF16))
    carry_ref[...] = carry
    cnt_ref[...] = carry


def _route(logits_t, bias):
    n = logits_t.shape[1]
    tm = MOE_TILE
    tok = lambda: pl.BlockSpec((TOP_K, tm), lambda i: (0, i))
    return pl.pallas_call(
        _route_kernel,
        grid=(n // tm,),
        in_specs=[pl.BlockSpec((N_EXPERTS, tm), lambda i: (0, i)), _const_spec(bias.shape)],
        out_specs=[tok(), tok(), tok(), _const_spec((N_EXPERTS, LANES))],
        out_shape=[jax.ShapeDtypeStruct((TOP_K, n), I32), jax.ShapeDtypeStruct((TOP_K, n), F32),
                   jax.ShapeDtypeStruct((TOP_K, n), I32), jax.ShapeDtypeStruct((N_EXPERTS, LANES), F32)],
        scratch_shapes=[pltpu.VMEM((N_EXPERTS, LANES), F32)],
        compiler_params=_cparams(("arbitrary",)),
        name="route",
    )(logits_t, bias)


def _dest_kernel(idx_ref, rank_ref, cnt_ref, dest_ref, start_ref):
    i = pl.program_id(0)
    tm = idx_ref.shape[1]

    @pl.when(i == 0)
    def _():
        cnt = cnt_ref[...]
        padded = jnp.floor((cnt + (EXPERT_TILE - 1)) * (1.0 / EXPERT_TILE)) * EXPERT_TILE
        r = lax.broadcasted_iota(I32, (N_EXPERTS, N_EXPERTS), 0)
        c = lax.broadcasted_iota(I32, (N_EXPERTS, N_EXPERTS), 1)
        start_ref[...] = _dot(jnp.where(c < r, 1.0, 0.0).astype(F32), padded, precision=HIGHEST)

    start = jnp.concatenate([start_ref[...]] * (tm // LANES), axis=1)
    e_iota = lax.broadcasted_iota(I32, (N_EXPERTS, tm), 0)
    idx = idx_ref[...]
    first = jnp.concatenate(
        [jnp.sum(jnp.where(e_iota == idx[kk:kk + 1, :], start, 0.0), axis=0, keepdims=True) for kk in range(TOP_K)],
        axis=0)
    dest_ref[...] = first.astype(I32) + rank_ref[...]


def _dest(idx_t, rank_t, counts):
    n = idx_t.shape[1]
    tm = MOE_TILE
    tok = lambda: pl.BlockSpec((TOP_K, tm), lambda i: (0, i))
    return pl.pallas_call(
        _dest_kernel,
        grid=(n // tm,),
        in_specs=[tok(), tok(), _const_spec(counts.shape)],
        out_specs=pl.BlockSpec((None, TOP_K, tm), lambda i: (i, 0, 0)),
        out_shape=jax.ShapeDtypeStruct((n // tm, TOP_K, tm), I32),
        scratch_shapes=[pltpu.VMEM((N_EXPERTS, LANES), F32)],
        compiler_params=_cparams(("arbitrary",)),
        name="dest",
    )(idx_t, rank_t, counts)


def _row_wait(hbm_ref, n_rows, sem):
    pltpu.make_async_copy(hbm_ref.at[pl.ds(0, n_rows), :], hbm_ref.at[pl.ds(0, n_rows), :], sem).wait()


def _dispatch_kernel(dest_hbm, h_ref, zeros_hbm, xs_hbm, idx_smem, sem):
    del zeros_hbm
    i = pl.program_id(0)
    tm = h_ref.shape[0]
    cp = pltpu.make_async_copy(dest_hbm.at[i], idx_smem, sem.at[0])
    cp.start()
    cp.wait()

    def body(r, carry):
        for kk in range(TOP_K):
            pltpu.make_async_copy(h_ref.at[pl.ds(r, 1), :], xs_hbm.at[pl.ds(idx_smem[kk, r], 1), :], sem.at[1]).start()
        return carry

    lax.fori_loop(0, tm, body, 0)
    _row_wait(xs_hbm, tm * TOP_K, sem.at[1])


def _dispatch(dest_tiles, hp, zeros):
    n, w = hp.shape
    tm = MOE_TILE
    return pl.pallas_call(
        _dispatch_kernel,
        grid=(n // tm,),
        in_specs=[pl.BlockSpec(memory_space=pl.ANY),
                  pl.BlockSpec((tm, w), lambda i: (i, 0)),
                  pl.BlockSpec(memory_space=pl.ANY)],
        out_specs=pl.BlockSpec(memory_space=pl.ANY),
        out_shape=jax.ShapeDtypeStruct(zeros.shape, zeros.dtype),
        scratch_shapes=[pltpu.SMEM((TOP_K, tm), I32), pltpu.SemaphoreType.DMA((2,))],
        input_output_aliases={2: 0},
        compiler_params=_cparams(("arbitrary",)),
        name="dispatch",
    )(dest_tiles, hp, zeros)


def _experts_kernel(te_ref, nv_ref, xs_ref, wg_ref, wu_ref, wd_ref, ys_ref, wgb_ref, wub_ref, wdb_ref):
    i = pl.program_id(0)

    @pl.when(i < nv_ref[0])
    def _():
        @pl.when((i == 0) | (te_ref[i] != te_ref[jnp.maximum(i - 1, 0)]))
        def _():
            wgb_ref[...] = wg_ref[...].astype(BF16)
            wub_ref[...] = wu_ref[...].astype(BF16)
            wdb_ref[...] = wd_ref[...].astype(BF16)

        half = D_MODEL // 2
        lo, hi = _unpack_bf16_pair(xs_ref[...])
        gate = _dot(lo, wgb_ref[:half, :]) + _dot(hi, wgb_ref[half:, :])
        up = _dot(lo, wub_ref[:half, :]) + _dot(hi, wub_ref[half:, :])
        act = (_silu(gate) * up).astype(BF16)
        ys_ref[...] = _pack_bf16_pair(_dot(act, wdb_ref[...]))

    @pl.when(i >= nv_ref[0])
    def _():
        ys_ref[...] = jnp.zeros_like(ys_ref)


def _experts(tile_expert, n_valid, xs, wg, wu, wd):
    n_rows, w = xs.shape
    tm = EXPERT_TILE
    tile = lambda i, te, nv: (jnp.minimum(i, nv[0] - 1), 0)
    return pl.pallas_call(
        _experts_kernel,
        grid_spec=pltpu.PrefetchScalarGridSpec(
            num_scalar_prefetch=2,
            grid=(n_rows // tm,),
            in_specs=[pl.BlockSpec((tm, w), tile),
                      pl.BlockSpec((None, D_MODEL, D_EXPERT), lambda i, te, nv: (te[i], 0, 0)),
                      pl.BlockSpec((None, D_MODEL, D_EXPERT), lambda i, te, nv: (te[i], 0, 0)),
                      pl.BlockSpec((None, D_EXPERT, D_MODEL), lambda i, te, nv: (te[i], 0, 0))],
            out_specs=pl.BlockSpec((tm, w), lambda i, te, nv: (i, 0)),
            scratch_shapes=[pltpu.VMEM((D_MODEL, D_EXPERT), BF16), pltpu.VMEM((D_MODEL, D_EXPERT), BF16),
                            pltpu.VMEM((D_EXPERT, D_MODEL), BF16)]),
        out_shape=jax.ShapeDtypeStruct((n_rows, w), U32),
        compiler_params=_cparams(("arbitrary",)),
        name="experts",
    )(tile_expert, n_valid, xs, wg, wu, wd)


def _combine_kernel(dest_hbm, ys_hbm, x1_ref, hp_ref, w_ref, mod_ref, fg_ref, sg_ref, su_ref, sd_ref, o_ref,
                    idx_smem, ybuf, sem, *, tile_offset):
    i = pl.program_id(0)
    tm = x1_ref.shape[0]
    half = D_MODEL // 2
    cp = pltpu.make_async_copy(dest_hbm.at[i + tile_offset], idx_smem, sem.at[0])
    cp.start()
    cp.wait()

    def body(r, carry):
        for kk in range(TOP_K):
            pltpu.make_async_copy(ys_hbm.at[pl.ds(idx_smem[kk, r], 1), :], ybuf.at[kk, pl.ds(r, 1), :], sem.at[1]).start()
        return carry

    lax.fori_loop(0, tm, body, 0)

    lo, hi = _unpack_bf16_pair(hp_ref[...])
    sg = sg_ref[...]
    su = su_ref[...]
    gate = _dot(lo, sg[:half]) + _dot(hi, sg[half:])
    up = _dot(lo, su[:half]) + _dot(hi, su[half:])
    y = _dot((_silu(gate) * up).astype(BF16), sd_ref[...])

    _row_wait(ys_hbm, tm * TOP_K, sem.at[1])
    w = w_ref[...]
    for kk in range(TOP_K):
        ylo, yhi = _unpack_bf16_pair(ybuf[kk])
        y = y + w[:, kk:kk + 1] * jnp.concatenate([ylo.astype(F32), yhi.astype(F32)], axis=1)
    x2 = x1_ref[...] + mod_ref[5] * y
    o_ref[...] = x2 * lax.rsqrt(jnp.mean(x2 * x2, axis=-1, keepdims=True) + EPS) * fg_ref[...]


def _combine(dest_tiles, ys, x1, hp, w, mod, rows_per_mod, fg, sg, su, sd, n, row_offset):
    tm = MOE_TILE
    mod_rows = mod.shape[2]
    tiles_per_mod = rows_per_mod // tm
    off = row_offset // tm
    return pl.pallas_call(
        functools.partial(_combine_kernel, tile_offset=off),
        grid=(n // tm,),
        in_specs=[pl.BlockSpec(memory_space=pl.ANY), pl.BlockSpec(memory_space=pl.ANY),
                  pl.BlockSpec((tm, D_MODEL), lambda i: (i + off, 0)),
                  pl.BlockSpec((tm, D_MODEL // 2), lambda i: (i + off, 0)),
                  pl.BlockSpec((tm, TOP_K), lambda i: (i + off, 0)),
                  pl.BlockSpec((6, None, mod_rows, D_MODEL), lambda i: (0, i // tiles_per_mod, 0, 0)),
                  _const_spec(fg.shape), _const_spec(sg.shape), _const_spec(su.shape), _const_spec(sd.shape)],
        out_specs=pl.BlockSpec((tm, D_MODEL), lambda i: (i, 0)),
        out_shape=jax.ShapeDtypeStruct((n, D_MODEL), F32),
        scratch_shapes=[pltpu.SMEM((TOP_K, tm), I32), pltpu.VMEM((TOP_K, tm, D_MODEL // 2), U32),
                        pltpu.SemaphoreType.DMA((2,))],
        compiler_params=_cparams(("arbitrary",)),
        name="combine",
    )(dest_tiles, ys, x1, hp, w, mod, fg, sg, su, sd)


def _split_w_in(w_in):
    sizes = (W_A, W_A, W_A, H_A, WK_B, WK_B, WV_B, WV_B, GK_RANK, D_MODEL, D_MODEL)
    segs, o = [], 0
    for s in sizes:
        segs.append(w_in[:, o:o + s])
        o += s
    return segs


def kernel(x_prompt, x_sample, cache_k, cache_v, cache_logf, state_gla, page_table, c_prompt, c_sample, w_ada, b_ada,
           norm1_g, w_in, b_f, w_gk2, b_gk, gla_norm_g, w_pa, w_pb, w_o, norm2_g, router_w, router_bias, w_gate, w_up,
           w_down, ws_gate, ws_up, ws_down, final_g):
    assert w_ada.shape[0] == 1, "single layer"
    bp, tp, d = x_prompt.shape
    bs, ts, _ = x_sample.shape
    n_p, n_s = bp * tp, bs * ts
    n_tot = n_p + n_s
    assert n_s == MOE_TILE and n_p % MOE_TILE == 0 and tp % FOX_TILE == 0

    q_a, k_a, v_a, f_a, q_b, k_b, v_b, r_b, gk1, gate_a, gate_b = _split_w_in(w_in[0])
    wts = (jnp.concatenate([q_a, k_a, v_a], axis=1).astype(BF16),
           f_a.T.astype(BF16),
           b_f[0].reshape(H_A, 1),
           jnp.concatenate([q_b, k_b, v_b, r_b], axis=1).astype(BF16),
           jnp.pad(gk1, ((0, 0), (0, LANES - GK_RANK))).astype(BF16),
           jnp.pad(w_gk2[0], ((0, LANES - GK_RANK), (0, 0))).astype(BF16),
           b_gk[0].reshape(1, WK_B),
           jnp.concatenate([gate_a, gate_b], axis=1).astype(BF16))
    g1 = norm1_g[0].reshape(1, d)
    g2 = norm2_g[0].reshape(1, d)
    gn = gla_norm_g[0].reshape(1, DV_B)
    fg = final_g.reshape(1, d)
    wpa, wpb, wo = w_pa[0].astype(BF16), w_pb[0].astype(BF16), w_o[0].astype(BF16)
    rw_t = router_w[0].T
    rbias = jnp.broadcast_to(router_bias[0].reshape(N_EXPERTS, 1), (N_EXPERTS, LANES))
    sg, su, sd = ws_gate[0].astype(BF16), ws_up[0].astype(BF16), ws_down[0].astype(BF16)

    ada = _ada(jnp.concatenate([c_prompt, c_sample], axis=0), w_ada[0], b_ada[0].reshape(1, -1))
    ada = ada.reshape(bp + bs, 6, d)
    mod_p = ada[:bp].transpose(1, 0, 2)[:, :, None, :]
    mod_s = jnp.repeat(ada[bp:], ts, axis=0).transpose(1, 0, 2)[:, None, :, :]

    xp = x_prompt.reshape(n_p, d)
    tm_p = 256
    (q, k, v, kh, vh, lf_t, fc_t, qb, kb, gk, vb, rb, ga_p, gb_p) = _inproj(xp, mod_p, tp, tp // tm_p, tm_p, g1, wts)
    ya_p = _fox(q, kh, vh, fc_t, bp, tp)
    yb_p, s_p = _gla(qb, kb, gk, vb, rb, jnp.zeros((bp, H_B, DK_B, DV_B), F32), gn, bp, tp, 256)
    k_prompt = k.reshape(1, bp, tp, H_A, DH_A)
    v_prompt = v.reshape(1, bp, tp, H_A, DH_A)
    logf_prompt = lf_t.T.reshape(1, bp, tp, H_A)

    xs_in = x_sample.reshape(n_s, d)
    (q, k, v, _, _, lf_t, _, qb, kb, gk, vb, rb, ga, gb) = _inproj(xs_in, mod_s, n_s, 1, n_s, g1, wts)
    page_rows = cache_k.shape[2]
    tq = 8
    cache_kt = jnp.transpose(cache_k, (0, 1, 3, 4, 2))
    cache_vt = jnp.transpose(cache_v, (0, 1, 3, 4, 2))
    suf = _lfsuf(page_table, jnp.transpose(cache_logf, (0, 1, 3, 2)))

    def new_kv(a):
        a = a.reshape(bs, ts, H_A, DH_A).transpose(0, 2, 3, 1)
        return jnp.pad(a, ((0, 0), (0, 0), (0, 0), (0, page_rows - ts)))

    q_dec = jnp.pad(q.astype(F32).reshape(bs, ts, H_A, DH_A).transpose(0, 2, 1, 3),
                    ((0, 0), (0, 0), (0, tq - ts), (0, 0)))
    lfn = jnp.pad(lf_t.reshape(H_A, bs, ts).transpose(1, 0, 2), ((0, 0), (0, 0), (0, page_rows - ts)))
    ya = _foxdec(page_table, q_dec, new_kv(k), new_kv(v), lfn, suf, cache_kt, cache_vt)
    ya = ya[:, :, :ts].transpose(0, 2, 1, 3).reshape(n_s, W_A).astype(BF16)
    pad = lambda a: jnp.pad(a.reshape(bs, ts, -1), ((0, 0), (0, GLA_CHUNK - ts), (0, 0))).reshape(bs * GLA_CHUNK, -1)
    yb, s_s = _gla(pad(qb), pad(kb), pad(gk), pad(vb), pad(rb), state_gla[0], gn, bs, GLA_CHUNK, GLA_CHUNK)
    yb = yb.reshape(bs, GLA_CHUNK, WV_B)[:, :ts].reshape(n_s, WV_B)
    k_sample = k.reshape(1, bs, ts, H_A, DH_A)
    v_sample = v.reshape(1, bs, ts, H_A, DH_A)
    logf_sample = lf_t.T.reshape(1, bs, ts, H_A)
    tail = _mixout(xs_in, ya, yb, ga, gb, mod_s, n_s, g2, wpa, wpb, wo, rw_t, ())

    x1, hp, logits_t = _mixout(xp, ya_p, yb_p, ga_p, gb_p, mod_p, tp, g2, wpa, wpb, wo, rw_t, tuple(tail))
    idx_t, w_t, rank_t, counts = _route(logits_t, rbias)
    dest_tiles = _dest(idx_t, rank_t, counts)
    padded = (counts[:, 0].astype(I32) + EXPERT_TILE - 1) // EXPERT_TILE * EXPERT_TILE
    pad_end = jnp.cumsum(padded)
    n_tiles = -(-n_tot * TOP_K // EXPERT_TILE) + N_EXPERTS
    tile_first_row = jnp.arange(n_tiles, dtype=I32) * EXPERT_TILE
    tile_expert = jnp.minimum(jnp.sum((pad_end[None, :] <= tile_first_row[:, None]).astype(I32), axis=1),
                              N_EXPERTS - 1)
    n_valid = (pad_end[-1:] // EXPERT_TILE).astype(I32)

    xs_sorted = _dispatch(dest_tiles, hp, jnp.zeros((n_tiles * EXPERT_TILE, d // 2), U32))
    ys_sorted = _experts(tile_expert, n_valid, xs_sorted, w_gate[0], w_up[0], w_down[0])
    w_tok = w_t.T
    y_prompt = _combine(dest_tiles, ys_sorted, x1, hp, w_tok, mod_p, tp, fg, sg, su, sd, n_p, 0)
    y_sample = _combine(dest_tiles, ys_sorted, x1, hp, w_tok, mod_s, n_s, fg, sg, su, sd, n_s, n_p)

    return (y_prompt.reshape(bp, tp, d), y_sample.reshape(bs, ts, d),
            k_prompt, v_prompt, logf_prompt, s_p[None],
            k_sample, v_sample, logf_sample, s_s[None])
```

```python
import functools

import jax
import jax.numpy as jnp
from jax import lax
from jax.experimental import pallas as pl
from jax.experimental.pallas import tpu as pltpu

F32 = jnp.float32
BF16 = jnp.bfloat16
U32 = jnp.uint32
I32 = jnp.int32

EPS = 1e-6
D_MODEL = 1024
H_A, DH_A, W_A = 8, 64, 512
H_B, DK_B, DV_B, WK_B, WV_B = 4, 64, 128, 256, 512
GK_RANK = 16
GATE_NORM = 16.0
N_EXPERTS, TOP_K, N_GROUPS, TOPK_GROUPS = 256, 8, 8, 4
GROUP_SIZE = N_EXPERTS // N_GROUPS
D_EXPERT = 256
ROUTE_SCALE = 2.5
MOE_TILE = 128
EXPERT_TILE = 256
LANES = 128
GLA_CHUNK = 64
GLA_SUB = 16
FOX_TILE = 512
FOX_SUB = 256
DEC_PAGES = 8
NEG = -1e30
VMEM_LIMIT = 56 * 1024 * 1024
HIGHEST = lax.Precision.HIGHEST

NT_DIMS = (((1,), (1,)), ((), ()))
TN_DIMS = (((0,), (0,)), ((), ()))


def _cparams(sem):
    return pltpu.CompilerParams(dimension_semantics=sem, vmem_limit_bytes=VMEM_LIMIT)


def _dot(a, b, **kw):
    return jnp.dot(a, b, preferred_element_type=F32, **kw)


def _dot_nt(a, b, **kw):
    return lax.dot_general(a, b, NT_DIMS, preferred_element_type=F32, **kw)


def _dot_tn(a, b, **kw):
    return lax.dot_general(a, b, TN_DIMS, preferred_element_type=F32, **kw)


def _sigmoid(x):
    return 1.0 / (1.0 + jnp.exp(-x))


def _silu(x):
    return x * _sigmoid(x)


def _log_sigmoid(x):
    return jnp.minimum(x, 0.0) - jnp.log(1.0 + jnp.exp(-jnp.abs(x)))


def _pack_bf16_pair(x):
    c = x.shape[1] // 2
    bits = pltpu.bitcast(x.astype(BF16).astype(F32), U32)
    return (bits[:, c:] & jnp.uint32(0xFFFF0000)) | (bits[:, :c] >> 16)


def _unpack_bf16_pair(p):
    lo = pltpu.bitcast(p << 16, F32).astype(BF16)
    hi = pltpu.bitcast(p & jnp.uint32(0xFFFF0000), F32).astype(BF16)
    return lo, hi


def _ada_kernel(c_ref, w_ref, b_ref, o_ref):
    c = c_ref[...]
    o_ref[...] = _dot(_silu(c).astype(BF16), w_ref[...].astype(BF16)) + b_ref[...]


def _ada(c, w, b):
    n, d = c.shape
    cols = w.shape[1]
    tn = 1536
    return pl.pallas_call(
        _ada_kernel,
        grid=(cols // tn,),
        in_specs=[pl.BlockSpec((n, d), lambda j: (0, 0)),
                  pl.BlockSpec((d, tn), lambda j: (0, j)),
                  pl.BlockSpec((1, tn), lambda j: (0, j))],
        out_specs=pl.BlockSpec((n, tn), lambda j: (0, j)),
        out_shape=jax.ShapeDtypeStruct((n, cols), F32),
        compiler_params=_cparams(("arbitrary",)),
        name="ada",
    )(c, w, b)


def _inproj_kernel(x_ref, mod_ref, g_ref, wqkv_ref, wf_ref, bf_ref, wvt_ref, wb_ref, wgk1_ref, wgk2_ref, bgk_ref,
                   wgate_ref,
                   q_ref, k_ref, v_ref, kh_ref, vt_ref, lf_ref, fc_ref, qb_ref, kb_ref, gk_ref, vb_ref, rb_ref,
                   ga_ref, gb_ref, carry_ref, *, seq_tiles):
    i = pl.program_id(0)
    tm = x_ref.shape[0]
    x = x_ref[...]
    y = x * lax.rsqrt(jnp.mean(x * x, axis=-1, keepdims=True) + EPS) * g_ref[...]
    h = (y * (1.0 + mod_ref[1]) + mod_ref[0]).astype(BF16)

    z = _dot(h, wqkv_ref[...])
    q_ref[...] = (z[:, :W_A] * DH_A ** -0.5).astype(BF16)
    k = z[:, W_A:2 * W_A]
    k_ref[...] = k
    v_ref[...] = z[:, 2 * W_A:]
    kh_ref[...] = k.astype(BF16)
    vt_ref[...] = _dot_nt(wvt_ref[...], h).astype(BF16)

    lf = _log_sigmoid(_dot(h, wf_ref[...]) + bf_ref[...])
    lf_ref[...] = lf[:, :H_A]

    @pl.when(i % seq_tiles == 0)
    def _():
        carry_ref[...] = jnp.zeros_like(carry_ref)

    r = lax.broadcasted_iota(I32, (tm, tm), 0)
    c = lax.broadcasted_iota(I32, (tm, tm), 1)
    lower = jnp.where(c <= r, 1.0, 0.0).astype(F32)
    carry = carry_ref[...]
    fc = _dot(lower, lf, precision=HIGHEST) + carry[0:1, :]
    fc_ref[...] = fc[:, :H_A]
    carry_ref[...] = carry + _dot(jnp.ones((8, tm), F32), lf, precision=HIGHEST)

    z = _dot(h, wb_ref[...])
    qb_ref[...] = z[:, :WK_B] * DK_B ** -0.5
    kb_ref[...] = z[:, WK_B:2 * WK_B]
    vb_ref[...] = z[:, 2 * WK_B:2 * WK_B + WV_B].astype(BF16)
    rb_ref[...] = _silu(z[:, 2 * WK_B + WV_B:]).astype(BF16)

    lr = _dot(h, wgk1_ref[...]).astype(BF16)
    gk_ref[...] = _log_sigmoid(_dot(lr, wgk2_ref[...]) + bgk_ref[...]) * (1.0 / GATE_NORM)

    z = _dot(h, wgate_ref[...])
    ga_ref[...] = _sigmoid(z[:, :D_MODEL]).astype(BF16)
    gb_ref[...] = _sigmoid(z[:, D_MODEL:]).astype(BF16)


def _const_spec(shape):
    return pl.BlockSpec(shape, lambda i: (0,) * len(shape))


def _inproj(x, mod, rows_per_mod, seq_tiles, tm, g, wts):
    n = x.shape[0]
    mod_rows = mod.shape[2]
    tiles_per_mod = rows_per_mod // tm
    row = lambda w: pl.BlockSpec((tm, w), lambda i: (i, 0))
    outs = [
        (row(W_A), (n, W_A), BF16),
        (row(W_A), (n, W_A), F32),
        (row(W_A), (n, W_A), F32),
        (row(W_A), (n, W_A), BF16),
        (pl.BlockSpec((W_A, tm), lambda i: (0, i)), (W_A, n), BF16),
        (row(H_A), (n, H_A), F32),
        (row(H_A), (n, H_A), F32),
        (row(WK_B), (n, WK_B), F32),
        (row(WK_B), (n, WK_B), F32),
        (row(WK_B), (n, WK_B), F32),
        (row(WV_B), (n, WV_B), BF16),
        (row(WV_B), (n, WV_B), BF16),
        (row(D_MODEL), (n, D_MODEL), BF16),
        (row(D_MODEL), (n, D_MODEL), BF16),
    ]
    return pl.pallas_call(
        functools.partial(_inproj_kernel, seq_tiles=seq_tiles),
        grid=(n // tm,),
        in_specs=[row(D_MODEL),
                  pl.BlockSpec((6, None, mod_rows, D_MODEL), lambda i: (0, i // tiles_per_mod, 0, 0)),
                  _const_spec(g.shape)] + [_const_spec(w.shape) for w in wts],
        out_specs=[o[0] for o in outs],
        out_shape=[jax.ShapeDtypeStruct(o[1], o[2]) for o in outs],
        scratch_shapes=[pltpu.VMEM((8, LANES), F32)],
        compiler_params=_cparams(("arbitrary",)),
        name="inproj",
    )(x, mod, g, *wts)


def _fox_kernel(q_ref, k_ref, vt_ref, fc_ref, o_ref, qs_ref, m_ref, l_ref, acc_ref):
    qi = pl.program_id(1)
    ki = pl.program_id(2)
    t = q_ref.shape[0]
    pairs = H_A // 2
    sub = FOX_SUB

    @pl.when(ki == 0)
    def _():
        lane = lax.broadcasted_iota(I32, (t, LANES), 1)
        for hp in range(pairs):
            q = q_ref[:, hp * LANES:(hp + 1) * LANES]
            qs_ref[hp, :t, :] = jnp.where(lane < DH_A, q, jnp.zeros_like(q))
            qs_ref[hp, t:, :] = jnp.where(lane >= DH_A, q, jnp.zeros_like(q))
        m_ref[...] = jnp.full_like(m_ref, NEG)
        l_ref[...] = jnp.zeros_like(l_ref)
        acc_ref[...] = jnp.zeros_like(acc_ref)

    def step(diagonal):
        for hp in range(pairs):
            k = k_ref[:, hp * LANES:(hp + 1) * LANES]
            vt = vt_ref[hp * LANES:(hp + 1) * LANES, :]
            for h2 in range(2):
                head = 2 * hp + h2
                fb = jnp.broadcast_to(fc_ref[:, head:head + 1], (t, sub))
                for sb in range(t // sub):
                    cols = slice(h2 * t + sb * sub, h2 * t + (sb + 1) * sub)
                    s = _dot_nt(k, qs_ref[hp, cols, :]) - fb
                    if diagonal:
                        key = lax.broadcasted_iota(I32, (t, sub), 0)
                        qry = lax.broadcasted_iota(I32, (t, sub), 1) + sb * sub
                        s = jnp.where(key <= qry, s, NEG)
                    m_prev = m_ref[hp, :, cols]
                    m_new = jnp.maximum(m_prev, jnp.max(s, axis=0, keepdims=True))
                    alpha = jnp.exp(m_prev - m_new)
                    p = jnp.exp(s - m_new)
                    l_ref[hp, :, cols] = alpha * l_ref[hp, :, cols] + jnp.sum(p, axis=0, keepdims=True)
                    acc_ref[hp, :, cols] = alpha * acc_ref[hp, :, cols] + _dot(vt, p.astype(BF16))
                    m_ref[hp, :, cols] = m_new

    @pl.when(ki < qi)
    def _():
        step(False)

    @pl.when(ki == qi)
    def _():
        step(True)
        for hp in range(pairs):
            o = acc_ref[hp] / l_ref[hp]
            o = jnp.concatenate([o[:DH_A, :t], o[DH_A:, t:]], axis=0)
            o_ref[:, hp * LANES:(hp + 1) * LANES] = o.T.astype(o_ref.dtype)


def _fox(q, k, vt, fc, batch, seq):
    t = FOX_TILE
    nq = seq // t
    kv_map = lambda b, qi, ki: (b * nq + jnp.minimum(ki, qi), 0)
    return pl.pallas_call(
        _fox_kernel,
        grid=(batch, nq, nq),
        in_specs=[pl.BlockSpec((t, W_A), lambda b, qi, ki: (b * nq + qi, 0)),
                  pl.BlockSpec((t, W_A), kv_map),
                  pl.BlockSpec((W_A, t), lambda b, qi, ki: (0, b * nq + jnp.minimum(ki, qi))),
                  pl.BlockSpec((t, H_A), kv_map)],
        out_specs=pl.BlockSpec((t, W_A), lambda b, qi, ki: (b * nq + qi, 0)),
        out_shape=jax.ShapeDtypeStruct((batch * seq, W_A), BF16),
        scratch_shapes=[pltpu.VMEM((H_A // 2, 2 * t, LANES), BF16),
                        pltpu.VMEM((H_A // 2, 1, 2 * t), F32),
                        pltpu.VMEM((H_A // 2, 1, 2 * t), F32),
                        pltpu.VMEM((H_A // 2, LANES, 2 * t), F32)],
        compiler_params=_cparams(("arbitrary", "arbitrary", "arbitrary")),
        name="fox",
    )(q, k, vt, fc)


def _lfsuf_kernel(pt_ref, lf_hbm, o_ref, buf_ref, sem):
    b = pl.program_id(0)
    n_pages, _, w = buf_ref.shape

    def page_copy(j):
        return pltpu.make_async_copy(lf_hbm.at[0, pt_ref[b, j]], buf_ref.at[j], sem)

    def issue(j, carry):
        page_copy(j).start()
        return carry

    def drain(j, carry):
        page_copy(j).wait()
        return carry

    lax.fori_loop(0, n_pages, issue, 0)
    lax.fori_loop(0, n_pages, drain, 0)

    x = buf_ref[...]
    lane = lax.broadcasted_iota(I32, x.shape, 2)
    s = x
    shift = 1
    while shift < w:
        s = s + jnp.where(lane + shift < w, pltpu.roll(s, w - shift, axis=2), 0.0)
        shift *= 2
    within = s - x
    after = jnp.zeros((H_A, w), F32)
    for j in range(n_pages - 1, -1, -1):
        o_ref[j] = within[j] + after
        after = after + jnp.broadcast_to(s[j][:, 0:1], (H_A, w))


def _lfsuf(page_table, lf_pages):
    nb, n_pages = page_table.shape
    w = lf_pages.shape[3]
    return pl.pallas_call(
        _lfsuf_kernel,
        grid_spec=pltpu.PrefetchScalarGridSpec(
            num_scalar_prefetch=1,
            grid=(nb,),
            in_specs=[pl.BlockSpec(memory_space=pl.ANY)],
            out_specs=pl.BlockSpec((None, n_pages, H_A, w), lambda b, pt: (b, 0, 0, 0)),
            scratch_shapes=[pltpu.VMEM((n_pages, H_A, w), F32), pltpu.SemaphoreType.DMA(())]),
        out_shape=jax.ShapeDtypeStruct((nb, n_pages, H_A, w), F32),
        compiler_params=_cparams(("arbitrary",)),
        name="lfsuf",
    )(page_table, lf_pages)


def _foxdec_kernel(pt_ref, q_ref, kn_ref, vn_ref, lfn_ref, suf_ref, ck_hbm, cv_hbm, o_ref,
                   kbuf, vbuf, sem, m_ref, l_ref, acc_ref):
    b = pl.program_id(0)
    n_pages = suf_ref.shape[0]
    page_rows = kbuf.shape[-1]
    n_chunks = n_pages // DEC_PAGES
    tq = q_ref.shape[1]

    def page_copies(c, slot, j):
        pg = pt_ref[b, c * DEC_PAGES + j]
        return (pltpu.make_async_copy(ck_hbm.at[0, pg], kbuf.at[slot, j], sem.at[0, slot]),
                pltpu.make_async_copy(cv_hbm.at[0, pg], vbuf.at[slot, j], sem.at[1, slot]))

    def fetch(c, slot):
        for j in range(DEC_PAGES):
            for cp in page_copies(c, slot, j):
                cp.start()

    def wait(c, slot):
        for j in range(DEC_PAGES):
            for cp in page_copies(c, slot, j):
                cp.wait()

    q = q_ref[...].astype(BF16)
    m_ref[...] = jnp.full_like(m_ref, NEG)
    l_ref[...] = jnp.zeros_like(l_ref)
    acc_ref[...] = jnp.zeros_like(acc_ref)

    def attend(kt, vt, bias, ok):
        s = jnp.einsum('htd,hds->hts', q, kt, preferred_element_type=F32) + bias
        if ok is not None:
            s = jnp.where(ok, s, NEG)
        m_prev = m_ref[...]
        m_new = jnp.maximum(m_prev, jnp.max(s, axis=2, keepdims=True))
        alpha = jnp.exp(m_prev - m_new)
        p = jnp.exp(s - m_new)
        l_ref[...] = alpha * l_ref[...] + jnp.sum(p, axis=2, keepdims=True)
        acc_ref[...] = alpha * acc_ref[...] + jnp.einsum('hts,hds->htd', p.astype(BF16), vt,
                                                         preferred_element_type=F32)
        m_ref[...] = m_new

    fetch(0, 0)

    def chunk(c, carry):
        slot = c % 2

        @pl.when(c + 1 < n_chunks)
        def _():
            fetch(c + 1, 1 - slot)

        wait(c, slot)
        kt = jnp.concatenate([kbuf[slot, j] for j in range(DEC_PAGES)], axis=2).astype(BF16)
        vt = jnp.concatenate([vbuf[slot, j] for j in range(DEC_PAGES)], axis=2).astype(BF16)
        suf = suf_ref[pl.ds(c * DEC_PAGES, DEC_PAGES)]
        bias = jnp.concatenate([suf[j] for j in range(DEC_PAGES)], axis=1)
        attend(kt, vt, bias[:, None, :], None)
        return carry

    lax.fori_loop(0, n_chunks, chunk, 0)

    r = lax.broadcasted_iota(I32, (page_rows, page_rows), 0)
    c = lax.broadcasted_iota(I32, (page_rows, page_rows), 1)
    fn = _dot(lfn_ref[...], jnp.where(r <= c, 1.0, 0.0).astype(F32), precision=HIGHEST)
    qi = lax.broadcasted_iota(I32, (H_A, tq, page_rows), 1)
    ki = lax.broadcasted_iota(I32, (H_A, tq, page_rows), 2)
    attend(kn_ref[...].astype(BF16), vn_ref[...].astype(BF16), -fn[:, None, :], ki <= qi)
    o_ref[...] = acc_ref[...] / l_ref[...]


def _foxdec(page_table, q, kn, vn, lfn, suf, cache_kt, cache_vt):
    nb, _, tq, _ = q.shape
    n_pages = page_table.shape[1]
    page_rows = cache_kt.shape[4]
    per_seq = lambda shape: pl.BlockSpec((None,) + shape, lambda b, pt: (b,) + (0,) * len(shape))
    page = (H_A, DH_A, page_rows)
    return pl.pallas_call(
        _foxdec_kernel,
        grid_spec=pltpu.PrefetchScalarGridSpec(
            num_scalar_prefetch=1,
            grid=(nb,),
            in_specs=[per_seq((H_A, tq, DH_A)), per_seq(page), per_seq(page), per_seq((H_A, page_rows)),
                      per_seq((n_pages, H_A, page_rows)),
                      pl.BlockSpec(memory_space=pl.ANY), pl.BlockSpec(memory_space=pl.ANY)],
            out_specs=per_seq((H_A, tq, DH_A)),
            scratch_shapes=[pltpu.VMEM((2, DEC_PAGES) + page, F32),
                            pltpu.VMEM((2, DEC_PAGES) + page, F32),
                            pltpu.SemaphoreType.DMA((2, 2)),
                            pltpu.VMEM((H_A, tq, 1), F32), pltpu.VMEM((H_A, tq, 1), F32),
                            pltpu.VMEM((H_A, tq, DH_A), F32)]),
        out_shape=jax.ShapeDtypeStruct((nb, H_A, tq, DH_A), F32),
        compiler_params=_cparams(("arbitrary",)),
        name="foxdec",
    )(page_table, q, kn, vn, lfn, suf, cache_kt, cache_vt)


def _gla_kernel(q_ref, k_ref, g_ref, v_ref, r_ref, s0_ref, gn_ref, y_ref, sfin_ref, state_ref):
    tb = pl.program_id(1)
    n_tb = pl.num_programs(1)
    c = GLA_CHUNK
    n_chunks = q_ref.shape[0] // c
    n_sub = c // GLA_SUB

    row_k = lax.broadcasted_iota(I32, (WK_B, WV_B), 0) // DK_B
    col_v = lax.broadcasted_iota(I32, (WK_B, WV_B), 1) // DV_B
    state_diag = row_k == col_v

    @pl.when(tb == 0)
    def _():
        state_ref[...] = jnp.zeros_like(state_ref)
        for h in range(H_B):
            state_ref[h * DK_B:(h + 1) * DK_B, h * DV_B:(h + 1) * DV_B] = s0_ref[h]

    r = lax.broadcasted_iota(I32, (c, c), 0)
    s = lax.broadcasted_iota(I32, (c, c), 1)
    same_sub = (r // GLA_SUB) == (s // GLA_SUB)
    cum_ops = jnp.concatenate([
        jnp.where(same_sub & (s <= r), 1.0, 0.0),
        jnp.where(s <= r, 1.0, 0.0),
        jnp.where(same_sub, 1.0, 0.0),
    ], axis=0).astype(F32)

    t_i = lax.broadcasted_iota(I32, (c, H_B * c), 0)
    s_i = lax.broadcasted_iota(I32, (c, H_B * c), 1) % c
    dsub = t_i // GLA_SUB - s_i // GLA_SUB
    intra = (dsub == 0) & (s_i <= t_i)
    head_k = lax.broadcasted_iota(I32, (c, WK_B), 1) // DK_B
    head_v = lax.broadcasted_iota(I32, (c, WV_B), 1) // DV_B

    def stack_heads(x, head_of_lane):
        zero = jnp.zeros_like(x)
        return jnp.concatenate([jnp.where(head_of_lane == h, x, zero) for h in range(H_B)], axis=0)

    def shift_rows(x, n):
        return jnp.concatenate([x[:n], x[:c - n]], axis=0)

    gn = gn_ref[...]
    for ci in range(n_chunks):
        rows = slice(ci * c, (ci + 1) * c)
        q = q_ref[rows, :]
        k = k_ref[rows, :]
        g = g_ref[rows, :]
        v = v_ref[rows, :]

        cums = _dot(cum_ops, g, precision=HIGHEST)
        bl, bc, tot = cums[:c], cums[c:2 * c], cums[2 * c:]
        pre = bc - bl
        q_loc = q * jnp.exp(bl)
        k_loc = k * jnp.exp(-bl)
        k_end = k * jnp.exp(tot - bl)
        q_far = [q_loc] + [q_loc * jnp.exp(pre - shift_rows(pre, GLA_SUB * d)) for d in range(1, n_sub - 1)]

        a0 = _dot_nt(q_loc.astype(BF16), stack_heads(k_loc, head_k).astype(BF16))
        af = _dot_nt(jnp.concatenate(q_far, axis=0).astype(BF16), stack_heads(k_end, head_k).astype(BF16))
        att = jnp.where(intra, a0, 0.0)
        for d in range(1, n_sub):
            att = jnp.where(dsub == d, af[(d - 1) * c:d * c], att)

        q_in = q * jnp.exp(bc)
        o = _dot(att.astype(BF16), stack_heads(v, head_v)) + _dot(q_in.astype(BF16), state_ref[...].astype(BF16))

        k_out = k * jnp.exp(bc[c - 1:c, :] - bc)
        kv = _dot_tn(k_out.astype(BF16), v)
        decay = jnp.exp(_dot_tn(g, jnp.ones((c, LANES), F32), precision=HIGHEST))
        state_ref[...] = (state_ref[...] * jnp.concatenate([decay] * (WV_B // LANES), axis=1)
                          + jnp.where(state_diag, kv, 0.0))

        for h in range(H_B):
            lanes = slice(h * DV_B, (h + 1) * DV_B)
            oh = o[:, lanes]
            yh = oh * lax.rsqrt(jnp.mean(oh * oh, axis=-1, keepdims=True) + EPS) * gn
            y_ref[rows, lanes] = (yh * r_ref[rows, lanes].astype(F32)).astype(y_ref.dtype)

    @pl.when(tb == n_tb - 1)
    def _():
        for h in range(H_B):
            sfin_ref[h] = state_ref[h * DK_B:(h + 1) * DK_B, h * DV_B:(h + 1) * DV_B]


def _gla(q, k, g, v, r, s0, gn, batch, seq, tb_rows):
    n_tb = seq // tb_rows
    row = lambda w: pl.BlockSpec((tb_rows, w), lambda b, t: (b * n_tb + t, 0))
    state = pl.BlockSpec((None, H_B, DK_B, DV_B), lambda b, t: (b, 0, 0, 0))
    return pl.pallas_call(
        _gla_kernel,
        grid=(batch, n_tb),
        in_specs=[row(WK_B), row(WK_B), row(WK_B), row(WV_B), row(WV_B), state,
                  pl.BlockSpec((1, DV_B), lambda b, t: (0, 0))],
        out_specs=[row(WV_B), state],
        out_shape=[jax.ShapeDtypeStruct((batch * seq, WV_B), BF16),
                   jax.ShapeDtypeStruct((batch, H_B, DK_B, DV_B), F32)],
        scratch_shapes=[pltpu.VMEM((WK_B, WV_B), F32)],
        compiler_params=_cparams(("arbitrary", "arbitrary")),
        name="gla",
    )(q, k, g, v, r, s0, gn)


def _mixout_kernel(*refs, n_tail):
    (x_ref, ya_ref, yb_ref, ga_ref, gb_ref, mod_ref, g2_ref, wpa_ref, wpb_ref, wo_ref, rw_ref) = refs[:11]
    tail_refs = refs[11:11 + n_tail]
    x1_ref, hp_ref, lg_ref = refs[11 + n_tail:]
    i = pl.program_id(0)
    n_own = pl.num_programs(0) - (1 if n_tail else 0)

    @pl.when(i < n_own)
    def _():
        br_a = _dot(ya_ref[...], wpa_ref[...])
        br_b = _dot(yb_ref[...], wpb_ref[...])
        merged = ga_ref[...].astype(F32) * br_a + gb_ref[...].astype(F32) * br_b
        out = _dot(merged.astype(BF16), wo_ref[...])
        x1 = x_ref[...] + mod_ref[2] * out
        x1_ref[...] = x1
        y = x1 * lax.rsqrt(jnp.mean(x1 * x1, axis=-1, keepdims=True) + EPS) * g2_ref[...]
        h = y * (1.0 + mod_ref[4]) + mod_ref[3]
        hp_ref[...] = _pack_bf16_pair(h)
        lg_ref[...] = _dot_nt(rw_ref[...], h, precision=HIGHEST)

    if n_tail:
        @pl.when(i == n_own)
        def _():
            for src, dst in zip(tail_refs, (x1_ref, hp_ref, lg_ref)):
                dst[...] = src[...]


def _mixout(x, ya, yb, ga, gb, mod, rows_per_mod, g2, wpa, wpb, wo, rw, tail):
    n = x.shape[0]
    tm = MOE_TILE
    mod_rows = mod.shape[2]
    tiles_per_mod = rows_per_mod // tm
    n_own = n // tm
    n_total = n + (tm if tail else 0)
    own = lambda i: jnp.minimum(i, n_own - 1)
    row = lambda w: pl.BlockSpec((tm, w), lambda i: (own(i), 0))
    return pl.pallas_call(
        functools.partial(_mixout_kernel, n_tail=len(tail)),
        grid=(n_total // tm,),
        in_specs=[row(D_MODEL), row(W_A), row(WV_B), row(D_MODEL), row(D_MODEL),
                  pl.BlockSpec((6, None, mod_rows, D_MODEL), lambda i: (0, own(i) // tiles_per_mod, 0, 0)),
                  _const_spec(g2.shape), _const_spec(wpa.shape), _const_spec(wpb.shape), _const_spec(wo.shape),
                  _const_spec(rw.shape)] + [_const_spec(t.shape) for t in tail],
        out_specs=[pl.BlockSpec((tm, D_MODEL), lambda i: (i, 0)),
                   pl.BlockSpec((tm, D_MODEL // 2), lambda i: (i, 0)),
                   pl.BlockSpec((N_EXPERTS, tm), lambda i: (0, i))],
        out_shape=[jax.ShapeDtypeStruct((n_total, D_MODEL), F32),
                   jax.ShapeDtypeStruct((n_total, D_MODEL // 2), U32),
                   jax.ShapeDtypeStruct((N_EXPERTS, n_total), F32)],
        compiler_params=_cparams(("arbitrary",)),
        name="mixout",
    )(x, ya, yb, ga, gb, mod, g2, wpa, wpb, wo, rw, *tail)


def _route_kernel(lg_ref, bias_ref, idx_ref, w_ref, rank_ref, cnt_ref, carry_ref):
    i = pl.program_id(0)
    tm = lg_ref.shape[1]

    @pl.when(i == 0)
    def _():
        carry_ref[...] = jnp.zeros_like(carry_ref)

    s = _sigmoid(lg_ref[...])
    sb = s + bias_ref[...][:, 0:1]
    ninf = -jnp.inf

    e_in_group = lax.broadcasted_iota(I32, (GROUP_SIZE, tm), 0)
    gscores = []
    for g in range(N_GROUPS):
        blk = sb[g * GROUP_SIZE:(g + 1) * GROUP_SIZE, :]
        m1 = jnp.max(blk, axis=0, keepdims=True)
        first = jnp.min(jnp.where(blk == m1, e_in_group, GROUP_SIZE), axis=0, keepdims=True)
        m2 = jnp.max(jnp.where(e_in_group == first, ninf, blk), axis=0, keepdims=True)
        gscores.append(m1 + m2)
    gs = jnp.concatenate(gscores, axis=0)

    g_iota = lax.broadcasted_iota(I32, (N_GROUPS, tm), 0)
    chosen = jnp.zeros((N_GROUPS, tm), F32)
    for _ in range(TOPK_GROUPS):
        m = jnp.max(gs, axis=0, keepdims=True)
        first = jnp.min(jnp.where(gs == m, g_iota, N_GROUPS), axis=0, keepdims=True)
        pick = g_iota == first
        chosen = jnp.where(pick, 1.0, chosen)
        gs = jnp.where(pick, ninf, gs)
    allowed = jnp.concatenate(
        [jnp.broadcast_to(chosen[g:g + 1, :], (GROUP_SIZE, tm)) for g in range(N_GROUPS)], axis=0) > 0.5
    cand = jnp.where(allowed, sb, ninf)

    e_iota = lax.broadcasted_iota(I32, (N_EXPERTS, tm), 0)
    onehot = jnp.zeros((N_EXPERTS, tm), F32)
    idxs, ws = [], []
    for _ in range(TOP_K):
        m = jnp.max(cand, axis=0, keepdims=True)
        first = jnp.min(jnp.where(cand == m, e_iota, N_EXPERTS), axis=0, keepdims=True)
        pick = e_iota == first
        idxs.append(first)
        ws.append(jnp.sum(jnp.where(pick, s, 0.0), axis=0, keepdims=True))
        onehot = jnp.where(pick, 1.0, onehot)
        cand = jnp.where(pick, ninf, cand)
    w = jnp.concatenate(ws, axis=0)
    w_ref[...] = w / jnp.sum(w, axis=0, keepdims=True) * ROUTE_SCALE
    idx_ref[...] = jnp.concatenate(idxs, axis=0)

    r = lax.broadcasted_iota(I32, (tm, tm), 0)
    c = lax.broadcasted_iota(I32, (tm, tm), 1)
    earlier = jnp.where(r < c, 1.0, 0.0).astype(BF16)
    carry = carry_ref[...]
    before = _dot(onehot.astype(BF16), earlier) + jnp.concatenate([carry] * (tm // LANES), axis=1)
    rank_ref[...] = jnp.concatenate(
        [jnp.sum(jnp.where(e_iota == ix, before, 0.0), axis=0, keepdims=True) for ix in idxs], axis=0).astype(I32)
    carry = carry + _dot(onehot.astype(BF16), jnp.ones((tm, LANES), BF16))
    carry_ref[...] = carry
    cnt_ref[...] = carry


def _route(logits_t, bias):
    n = logits_t.shape[1]
    tm = MOE_TILE
    tok = lambda: pl.BlockSpec((TOP_K, tm), lambda i: (0, i))
    return pl.pallas_call(
        _route_kernel,
        grid=(n // tm,),
        in_specs=[pl.BlockSpec((N_EXPERTS, tm), lambda i: (0, i)), _const_spec(bias.shape)],
        out_specs=[tok(), tok(), tok(), _const_spec((N_EXPERTS, LANES))],
        out_shape=[jax.ShapeDtypeStruct((TOP_K, n), I32), jax.ShapeDtypeStruct((TOP_K, n), F32),
                   jax.ShapeDtypeStruct((TOP_K, n), I32), jax.ShapeDtypeStruct((N_EXPERTS, LANES), F32)],
        scratch_shapes=[pltpu.VMEM((N_EXPERTS, LANES), F32)],
        compiler_params=_cparams(("arbitrary",)),
        name="route",
    )(logits_t, bias)


def _dest_kernel(idx_ref, rank_ref, cnt_ref, dest_ref, start_ref):
    i = pl.program_id(0)
    tm = idx_ref.shape[1]

    @pl.when(i == 0)
    def _():
        cnt = cnt_ref[...]
        padded = jnp.floor((cnt + (EXPERT_TILE - 1)) * (1.0 / EXPERT_TILE)) * EXPERT_TILE
        r = lax.broadcasted_iota(I32, (N_EXPERTS, N_EXPERTS), 0)
        c = lax.broadcasted_iota(I32, (N_EXPERTS, N_EXPERTS), 1)
        start_ref[...] = _dot(jnp.where(c < r, 1.0, 0.0).astype(F32), padded, precision=HIGHEST)

    start = jnp.concatenate([start_ref[...]] * (tm // LANES), axis=1)
    e_iota = lax.broadcasted_iota(I32, (N_EXPERTS, tm), 0)
    idx = idx_ref[...]
    first = jnp.concatenate(
        [jnp.sum(jnp.where(e_iota == idx[kk:kk + 1, :], start, 0.0), axis=0, keepdims=True) for kk in range(TOP_K)],
        axis=0)
    dest_ref[...] = first.astype(I32) + rank_ref[...]


def _dest(idx_t, rank_t, counts):
    n = idx_t.shape[1]
    tm = MOE_TILE
    tok = lambda: pl.BlockSpec((TOP_K, tm), lambda i: (0, i))
    return pl.pallas_call(
        _dest_kernel,
        grid=(n // tm,),
        in_specs=[tok(), tok(), _const_spec(counts.shape)],
        out_specs=pl.BlockSpec((None, TOP_K, tm), lambda i: (i, 0, 0)),
        out_shape=jax.ShapeDtypeStruct((n // tm, TOP_K, tm), I32),
        scratch_shapes=[pltpu.VMEM((N_EXPERTS, LANES), F32)],
        compiler_params=_cparams(("arbitrary",)),
        name="dest",
    )(idx_t, rank_t, counts)


def _row_wait(hbm_ref, n_rows, sem):
    pltpu.make_async_copy(hbm_ref.at[pl.ds(0, n_rows), :], hbm_ref.at[pl.ds(0, n_rows), :], sem).wait()


def _dispatch_kernel(dest_hbm, h_ref, zeros_hbm, xs_hbm, idx_smem, sem):
    del zeros_hbm
    i = pl.program_id(0)
    tm = h_ref.shape[0]
    cp = pltpu.make_async_copy(dest_hbm.at[i], idx_smem, sem.at[0])
    cp.start()
    cp.wait()

    def body(r, carry):
        for kk in range(TOP_K):
            pltpu.make_async_copy(h_ref.at[pl.ds(r, 1), :], xs_hbm.at[pl.ds(idx_smem[kk, r], 1), :], sem.at[1]).start()
        return carry

    lax.fori_loop(0, tm, body, 0)
    _row_wait(xs_hbm, tm * TOP_K, sem.at[1])


def _dispatch(dest_tiles, hp, zeros):
    n, w = hp.shape
    tm = MOE_TILE
    return pl.pallas_call(
        _dispatch_kernel,
        grid=(n // tm,),
        in_specs=[pl.BlockSpec(memory_space=pl.ANY),
                  pl.BlockSpec((tm, w), lambda i: (i, 0)),
                  pl.BlockSpec(memory_space=pl.ANY)],
        out_specs=pl.BlockSpec(memory_space=pl.ANY),
        out_shape=jax.ShapeDtypeStruct(zeros.shape, zeros.dtype),
        scratch_shapes=[pltpu.SMEM((TOP_K, tm), I32), pltpu.SemaphoreType.DMA((2,))],
        input_output_aliases={2: 0},
        compiler_params=_cparams(("arbitrary",)),
        name="dispatch",
    )(dest_tiles, hp, zeros)


def _experts_kernel(te_ref, nv_ref, xs_ref, wg_ref, wu_ref, wd_ref, ys_ref, wgb_ref, wub_ref, wdb_ref):
    i = pl.program_id(0)

    @pl.when(i < nv_ref[0])
    def _():
        @pl.when((i == 0) | (te_ref[i] != te_ref[jnp.maximum(i - 1, 0)]))
        def _():
            wgb_ref[...] = wg_ref[...].astype(BF16)
            wub_ref[...] = wu_ref[...].astype(BF16)
            wdb_ref[...] = wd_ref[...].astype(BF16)

        half = D_MODEL // 2
        lo, hi = _unpack_bf16_pair(xs_ref[...])
        gate = _dot(lo, wgb_ref[:half, :]) + _dot(hi, wgb_ref[half:, :])
        up = _dot(lo, wub_ref[:half, :]) + _dot(hi, wub_ref[half:, :])
        act = (_silu(gate) * up).astype(BF16)
        ys_ref[...] = _pack_bf16_pair(_dot(act, wdb_ref[...]))

    @pl.when(i >= nv_ref[0])
    def _():
        ys_ref[...] = jnp.zeros_like(ys_ref)


def _experts(tile_expert, n_valid, xs, wg, wu, wd):
    n_rows, w = xs.shape
    tm = EXPERT_TILE
    tile = lambda i, te, nv: (jnp.minimum(i, nv[0] - 1), 0)
    return pl.pallas_call(
        _experts_kernel,
        grid_spec=pltpu.PrefetchScalarGridSpec(
            num_scalar_prefetch=2,
            grid=(n_rows // tm,),
            in_specs=[pl.BlockSpec((tm, w), tile),
                      pl.BlockSpec((None, D_MODEL, D_EXPERT), lambda i, te, nv: (te[i], 0, 0)),
                      pl.BlockSpec((None, D_MODEL, D_EXPERT), lambda i, te, nv: (te[i], 0, 0)),
                      pl.BlockSpec((None, D_EXPERT, D_MODEL), lambda i, te, nv: (te[i], 0, 0))],
            out_specs=pl.BlockSpec((tm, w), lambda i, te, nv: (i, 0)),
            scratch_shapes=[pltpu.VMEM((D_MODEL, D_EXPERT), BF16), pltpu.VMEM((D_MODEL, D_EXPERT), BF16),
                            pltpu.VMEM((D_EXPERT, D_MODEL), BF16)]),
        out_shape=jax.ShapeDtypeStruct((n_rows, w), U32),
        compiler_params=_cparams(("arbitrary",)),
        name="experts",
    )(tile_expert, n_valid, xs, wg, wu, wd)


def _combine_kernel(dest_hbm, ys_hbm, x1_ref, hp_ref, w_ref, mod_ref, fg_ref, sg_ref, su_ref, sd_ref, o_ref,
                    idx_smem, ybuf, sem, *, tile_offset):
    i = pl.program_id(0)
    tm = x1_ref.shape[0]
    half = D_MODEL // 2
    cp = pltpu.make_async_copy(dest_hbm.at[i + tile_offset], idx_smem, sem.at[0])
    cp.start()
    cp.wait()

    def body(r, carry):
        for kk in range(TOP_K):
            pltpu.make_async_copy(ys_hbm.at[pl.ds(idx_smem[kk, r], 1), :], ybuf.at[kk, pl.ds(r, 1), :], sem.at[1]).start()
        return carry

    lax.fori_loop(0, tm, body, 0)

    lo, hi = _unpack_bf16_pair(hp_ref[...])
    sg = sg_ref[...]
    su = su_ref[...]
    gate = _dot(lo, sg[:half]) + _dot(hi, sg[half:])
    up = _dot(lo, su[:half]) + _dot(hi, su[half:])
    y = _dot((_silu(gate) * up).astype(BF16), sd_ref[...])

    _row_wait(ys_hbm, tm * TOP_K, sem.at[1])
    w = w_ref[...]
    for kk in range(TOP_K):
        ylo, yhi = _unpack_bf16_pair(ybuf[kk])
        y = y + w[:, kk:kk + 1] * jnp.concatenate([ylo.astype(F32), yhi.astype(F32)], axis=1)
    x2 = x1_ref[...] + mod_ref[5] * y
    o_ref[...] = x2 * lax.rsqrt(jnp.mean(x2 * x2, axis=-1, keepdims=True) + EPS) * fg_ref[...]


def _combine(dest_tiles, ys, x1, hp, w, mod, rows_per_mod, fg, sg, su, sd, n, row_offset):
    tm = MOE_TILE
    mod_rows = mod.shape[2]
    tiles_per_mod = rows_per_mod // tm
    off = row_offset // tm
    return pl.pallas_call(
        functools.partial(_combine_kernel, tile_offset=off),
        grid=(n // tm,),
        in_specs=[pl.BlockSpec(memory_space=pl.ANY), pl.BlockSpec(memory_space=pl.ANY),
                  pl.BlockSpec((tm, D_MODEL), lambda i: (i + off, 0)),
                  pl.BlockSpec((tm, D_MODEL // 2), lambda i: (i + off, 0)),
                  pl.BlockSpec((tm, TOP_K), lambda i: (i + off, 0)),
                  pl.BlockSpec((6, None, mod_rows, D_MODEL), lambda i: (0, i // tiles_per_mod, 0, 0)),
                  _const_spec(fg.shape), _const_spec(sg.shape), _const_spec(su.shape), _const_spec(sd.shape)],
        out_specs=pl.BlockSpec((tm, D_MODEL), lambda i: (i, 0)),
        out_shape=jax.ShapeDtypeStruct((n, D_MODEL), F32),
        scratch_shapes=[pltpu.SMEM((TOP_K, tm), I32), pltpu.VMEM((TOP_K, tm, D_MODEL // 2), U32),
                        pltpu.SemaphoreType.DMA((2,))],
        compiler_params=_cparams(("arbitrary",)),
        name="combine",
    )(dest_tiles, ys, x1, hp, w, mod, fg, sg, su, sd)


def _split_w_in(w_in):
    sizes = (W_A, W_A, W_A, H_A, WK_B, WK_B, WV_B, WV_B, GK_RANK, D_MODEL, D_MODEL)
    segs, o = [], 0
    for s in sizes:
        segs.append(w_in[:, o:o + s])
        o += s
    return segs


def kernel(x_prompt, x_sample, cache_k, cache_v, cache_logf, state_gla, page_table, c_prompt, c_sample, w_ada, b_ada,
           norm1_g, w_in, b_f, w_gk2, b_gk, gla_norm_g, w_pa, w_pb, w_o, norm2_g, router_w, router_bias, w_gate, w_up,
           w_down, ws_gate, ws_up, ws_down, final_g):
    assert w_ada.shape[0] == 1, "single layer"
    bp, tp, d = x_prompt.shape
    bs, ts, _ = x_sample.shape
    n_p, n_s = bp * tp, bs * ts
    n_tot = n_p + n_s
    assert n_s == MOE_TILE and n_p % MOE_TILE == 0 and tp % FOX_TILE == 0

    q_a, k_a, v_a, f_a, q_b, k_b, v_b, r_b, gk1, gate_a, gate_b = _split_w_in(w_in[0])
    wts = (jnp.concatenate([q_a, k_a, v_a], axis=1).astype(BF16),
           jnp.pad(f_a, ((0, 0), (0, LANES - H_A))).astype(BF16),
           jnp.pad(b_f[0].reshape(1, H_A), ((0, 0), (0, LANES - H_A))),
           v_a.T.astype(BF16),
           jnp.concatenate([q_b, k_b, v_b, r_b], axis=1).astype(BF16),
           jnp.pad(gk1, ((0, 0), (0, LANES - GK_RANK))).astype(BF16),
           jnp.pad(w_gk2[0], ((0, LANES - GK_RANK), (0, 0))).astype(BF16),
           b_gk[0].reshape(1, WK_B),
           jnp.concatenate([gate_a, gate_b], axis=1).astype(BF16))
    g1 = norm1_g[0].reshape(1, d)
    g2 = norm2_g[0].reshape(1, d)
    gn = gla_norm_g[0].reshape(1, DV_B)
    fg = final_g.reshape(1, d)
    wpa, wpb, wo = w_pa[0].astype(BF16), w_pb[0].astype(BF16), w_o[0].astype(BF16)
    rw_t = router_w[0].T
    rbias = jnp.broadcast_to(router_bias[0].reshape(N_EXPERTS, 1), (N_EXPERTS, LANES))
    sg, su, sd = ws_gate[0].astype(BF16), ws_up[0].astype(BF16), ws_down[0].astype(BF16)

    ada = _ada(jnp.concatenate([c_prompt, c_sample], axis=0), w_ada[0], b_ada[0].reshape(1, -1))
    ada = ada.reshape(bp + bs, 6, d)
    mod_p = ada[:bp].transpose(1, 0, 2)[:, :, None, :]
    mod_s = jnp.repeat(ada[bp:], ts, axis=0).transpose(1, 0, 2)[:, None, :, :]

    xp = x_prompt.reshape(n_p, d)
    tm_p = 256
    (q, k, v, kh, vt, lf, fc, qb, kb, gk, vb, rb, ga_p, gb_p) = _inproj(xp, mod_p, tp, tp // tm_p, tm_p, g1, wts)
    ya_p = _fox(q, kh, vt, fc, bp, tp)
    yb_p, s_p = _gla(qb, kb, gk, vb, rb, jnp.zeros((bp, H_B, DK_B, DV_B), F32), gn, bp, tp, 256)
    k_prompt = k.reshape(1, bp, tp, H_A, DH_A)
    v_prompt = v.reshape(1, bp, tp, H_A, DH_A)
    logf_prompt = lf.reshape(1, bp, tp, H_A)

    xs_in = x_sample.reshape(n_s, d)
    (q, k, v, _, _, lf, _, qb, kb, gk, vb, rb, ga, gb) = _inproj(xs_in, mod_s, n_s, 1, n_s, g1, wts)
    page_rows = cache_k.shape[2]
    tq = 8
    cache_kt = jnp.transpose(cache_k, (0, 1, 3, 4, 2))
    cache_vt = jnp.transpose(cache_v, (0, 1, 3, 4, 2))
    suf = _lfsuf(page_table, jnp.transpose(cache_logf, (0, 1, 3, 2)))

    def new_kv(a):
        a = a.reshape(bs, ts, H_A, DH_A).transpose(0, 2, 3, 1)
        return jnp.pad(a, ((0, 0), (0, 0), (0, 0), (0, page_rows - ts)))

    q_dec = jnp.pad(q.astype(F32).reshape(bs, ts, H_A, DH_A).transpose(0, 2, 1, 3),
                    ((0, 0), (0, 0), (0, tq - ts), (0, 0)))
    lfn = jnp.pad(lf.reshape(bs, ts, H_A).transpose(0, 2, 1), ((0, 0), (0, 0), (0, page_rows - ts)))
    ya = _foxdec(page_table, q_dec, new_kv(k), new_kv(v), lfn, suf, cache_kt, cache_vt)
    ya = ya[:, :, :ts].transpose(0, 2, 1, 3).reshape(n_s, W_A).astype(BF16)
    pad = lambda a: jnp.pad(a.reshape(bs, ts, -1), ((0, 0), (0, GLA_CHUNK - ts), (0, 0))).reshape(bs * GLA_CHUNK, -1)
    yb, s_s = _gla(pad(qb), pad(kb), pad(gk), pad(vb), pad(rb), state_gla[0], gn, bs, GLA_CHUNK, GLA_CHUNK)
    yb = yb.reshape(bs, GLA_CHUNK, WV_B)[:, :ts].reshape(n_s, WV_B)
    k_sample = k.reshape(1, bs, ts, H_A, DH_A)
    v_sample = v.reshape(1, bs, ts, H_A, DH_A)
    logf_sample = lf.reshape(1, bs, ts, H_A)
    tail = _mixout(xs_in, ya, yb, ga, gb, mod_s, n_s, g2, wpa, wpb, wo, rw_t, ())

    x1, hp, logits_t = _mixout(xp, ya_p, yb_p, ga_p, gb_p, mod_p, tp, g2, wpa, wpb, wo, rw_t, tuple(tail))
    idx_t, w_t, rank_t, counts = _route(logits_t, rbias)
    dest_tiles = _dest(idx_t, rank_t, counts)
    padded = (counts[:, 0].astype(I32) + EXPERT_TILE - 1) // EXPERT_TILE * EXPERT_TILE
    pad_end = jnp.cumsum(padded)
    n_tiles = -(-n_tot * TOP_K // EXPERT_TILE) + N_EXPERTS
    tile_first_row = jnp.arange(n_tiles, dtype=I32) * EXPERT_TILE
    tile_expert = jnp.minimum(jnp.sum((pad_end[None, :] <= tile_first_row[:, None]).astype(I32), axis=1),
                              N_EXPERTS - 1)
    n_valid = (pad_end[-1:] // EXPERT_TILE).astype(I32)

    xs_sorted = _dispatch(dest_tiles, hp, jnp.zeros((n_tiles * EXPERT_TILE, d // 2), U32))
    ys_sorted = _experts(tile_expert, n_valid, xs_sorted, w_gate[0], w_up[0], w_down[0])
    w_tok = w_t.T
    y_prompt = _combine(dest_tiles, ys_sorted, x1, hp, w_tok, mod_p, tp, fg, sg, su, sd, n_p, 0)
    y_sample = _combine(dest_tiles, ys_sorted, x1, hp, w_tok, mod_s, n_s, fg, sg, su, sd, n_s, n_p)

    return (y_prompt.reshape(bp, tp, d), y_sample.reshape(bs, ts, d),
            k_prompt, v_prompt, logf_prompt, s_p[None],
            k_sample, v_sample, logf_sample, s_s[None])
```

```python
import functools

import jax
import jax.numpy as jnp
from jax import lax
from jax.experimental import pallas as pl
from jax.experimental.pallas import tpu as pltpu
from jax.experimental.pallas import tpu_sc as plsc

F32 = jnp.float32
BF16 = jnp.bfloat16
U32 = jnp.uint32
I32 = jnp.int32

EPS = 1e-6
D_MODEL = 1024
H_A, DH_A, W_A = 8, 64, 512
H_B, DK_B, DV_B, WK_B, WV_B = 4, 64, 128, 256, 512
GK_RANK = 16
GATE_NORM = 16.0
N_EXPERTS, TOP_K, N_GROUPS, TOPK_GROUPS = 256, 8, 8, 4
GROUP_SIZE = N_EXPERTS // N_GROUPS
D_EXPERT = 256
ROUTE_SCALE = 2.5
MOE_TILE = 128
EXPERT_TILE = 256
LANES = 128
SC_CORES = 2
SC_WORKERS = 32
GLA_CHUNK = 64
GLA_SUB = 16
FOX_TILE = 512
FOX_SUB = 256
DEC_PAGES = 8
NEG = -1e30
VMEM_LIMIT = 56 * 1024 * 1024
HIGHEST = lax.Precision.HIGHEST

NT_DIMS = (((1,), (1,)), ((), ()))
TN_DIMS = (((0,), (0,)), ((), ()))


def _cparams(sem):
    return pltpu.CompilerParams(dimension_semantics=sem, vmem_limit_bytes=VMEM_LIMIT)


def _dot(a, b, **kw):
    return jnp.dot(a, b, preferred_element_type=F32, **kw)


def _dot_nt(a, b, **kw):
    return lax.dot_general(a, b, NT_DIMS, preferred_element_type=F32, **kw)


def _dot_tn(a, b, **kw):
    return lax.dot_general(a, b, TN_DIMS, preferred_element_type=F32, **kw)


def _sigmoid(x):
    return 1.0 / (1.0 + jnp.exp(-x))


def _silu(x):
    return x * _sigmoid(x)


def _log_sigmoid(x):
    return jnp.minimum(x, 0.0) - jnp.log(1.0 + jnp.exp(-jnp.abs(x)))


def _pack_bf16_pair(x):
    c = x.shape[1] // 2
    bits = pltpu.bitcast(x.astype(BF16).astype(F32), U32)
    return (bits[:, c:] & jnp.uint32(0xFFFF0000)) | (bits[:, :c] >> 16)


def _unpack_bf16_pair(p):
    lo = pltpu.bitcast(p << 16, F32).astype(BF16)
    hi = pltpu.bitcast(p & jnp.uint32(0xFFFF0000), F32).astype(BF16)
    return lo, hi


def _ada_kernel(c_ref, w_ref, b_ref, o_ref):
    c = c_ref[...]
    o_ref[...] = _dot(_silu(c).astype(BF16), w_ref[...].astype(BF16)) + b_ref[...]


def _ada(c, w, b):
    n, d = c.shape
    cols = w.shape[1]
    tn = 1536
    return pl.pallas_call(
        _ada_kernel,
        grid=(cols // tn,),
        in_specs=[pl.BlockSpec((n, d), lambda j: (0, 0)),
                  pl.BlockSpec((d, tn), lambda j: (0, j)),
                  pl.BlockSpec((1, tn), lambda j: (0, j))],
        out_specs=pl.BlockSpec((n, tn), lambda j: (0, j)),
        out_shape=jax.ShapeDtypeStruct((n, cols), F32),
        compiler_params=_cparams(("arbitrary",)),
        name="ada",
    )(c, w, b)


def _inproj_kernel(x_ref, mod_ref, g_ref, wqkv_ref, wf_ref, bf_ref, wvt_ref, wb_ref, wgk1_ref, wgk2_ref, bgk_ref,
                   wgate_ref,
                   q_ref, k_ref, v_ref, kh_ref, vt_ref, lf_ref, fc_ref, qb_ref, kb_ref, gk_ref, vb_ref, rb_ref,
                   ga_ref, gb_ref, carry_ref, *, seq_tiles):
    i = pl.program_id(0)
    tm = x_ref.shape[0]
    x = x_ref[...]
    y = x * lax.rsqrt(jnp.mean(x * x, axis=-1, keepdims=True) + EPS) * g_ref[...]
    h = (y * (1.0 + mod_ref[1]) + mod_ref[0]).astype(BF16)

    z = _dot(h, wqkv_ref[...])
    q_ref[...] = (z[:, :W_A] * DH_A ** -0.5).astype(BF16)
    k = z[:, W_A:2 * W_A]
    k_ref[...] = k
    v_ref[...] = z[:, 2 * W_A:]
    kh_ref[...] = k.astype(BF16)
    vt_ref[...] = _dot_nt(wvt_ref[...], h).astype(BF16)

    lf = _log_sigmoid(_dot(h, wf_ref[...]) + bf_ref[...])
    lf_ref[...] = lf[:, :H_A]

    @pl.when(i % seq_tiles == 0)
    def _():
        carry_ref[...] = jnp.zeros_like(carry_ref)

    r = lax.broadcasted_iota(I32, (tm, tm), 0)
    c = lax.broadcasted_iota(I32, (tm, tm), 1)
    lower = jnp.where(c <= r, 1.0, 0.0).astype(F32)
    carry = carry_ref[...]
    fc = _dot(lower, lf, precision=HIGHEST) + carry[0:1, :]
    fc_ref[...] = fc[:, :H_A]
    carry_ref[...] = carry + _dot(jnp.ones((8, tm), F32), lf, precision=HIGHEST)

    z = _dot(h, wb_ref[...])
    qb_ref[...] = z[:, :WK_B] * DK_B ** -0.5
    kb_ref[...] = z[:, WK_B:2 * WK_B]
    vb_ref[...] = z[:, 2 * WK_B:2 * WK_B + WV_B].astype(BF16)
    rb_ref[...] = _silu(z[:, 2 * WK_B + WV_B:]).astype(BF16)

    lr = _dot(h, wgk1_ref[...]).astype(BF16)
    gk_ref[...] = _log_sigmoid(_dot(lr, wgk2_ref[...]) + bgk_ref[...]) * (1.0 / GATE_NORM)

    z = _dot(h, wgate_ref[...])
    ga_ref[...] = _sigmoid(z[:, :D_MODEL]).astype(BF16)
    gb_ref[...] = _sigmoid(z[:, D_MODEL:]).astype(BF16)


def _const_spec(shape):
    return pl.BlockSpec(shape, lambda i: (0,) * len(shape))


def _inproj(x, mod, rows_per_mod, seq_tiles, tm, g, wts):
    n = x.shape[0]
    mod_rows = mod.shape[2]
    tiles_per_mod = rows_per_mod // tm
    row = lambda w: pl.BlockSpec((tm, w), lambda i: (i, 0))
    outs = [
        (row(W_A), (n, W_A), BF16),
        (row(W_A), (n, W_A), F32),
        (row(W_A), (n, W_A), F32),
        (row(W_A), (n, W_A), BF16),
        (pl.BlockSpec((W_A, tm), lambda i: (0, i)), (W_A, n), BF16),
        (row(H_A), (n, H_A), F32),
        (row(H_A), (n, H_A), F32),
        (row(WK_B), (n, WK_B), F32),
        (row(WK_B), (n, WK_B), F32),
        (row(WK_B), (n, WK_B), F32),
        (row(WV_B), (n, WV_B), BF16),
        (row(WV_B), (n, WV_B), BF16),
        (row(D_MODEL), (n, D_MODEL), BF16),
        (row(D_MODEL), (n, D_MODEL), BF16),
    ]
    return pl.pallas_call(
        functools.partial(_inproj_kernel, seq_tiles=seq_tiles),
        grid=(n // tm,),
        in_specs=[row(D_MODEL),
                  pl.BlockSpec((6, None, mod_rows, D_MODEL), lambda i: (0, i // tiles_per_mod, 0, 0)),
                  _const_spec(g.shape)] + [_const_spec(w.shape) for w in wts],
        out_specs=[o[0] for o in outs],
        out_shape=[jax.ShapeDtypeStruct(o[1], o[2]) for o in outs],
        scratch_shapes=[pltpu.VMEM((8, LANES), F32)],
        compiler_params=_cparams(("arbitrary",)),
        name="inproj",
    )(x, mod, g, *wts)


def _fox_kernel(q_ref, k_ref, vt_ref, fc_ref, o_ref, qs_ref, m_ref, l_ref, acc_ref):
    qi = pl.program_id(1)
    ki = pl.program_id(2)
    t = q_ref.shape[0]
    pairs = H_A // 2
    sub = FOX_SUB

    @pl.when(ki == 0)
    def _():
        lane = lax.broadcasted_iota(I32, (t, LANES), 1)
        for hp in range(pairs):
            q = q_ref[:, hp * LANES:(hp + 1) * LANES]
            qs_ref[hp, :t, :] = jnp.where(lane < DH_A, q, jnp.zeros_like(q))
            qs_ref[hp, t:, :] = jnp.where(lane >= DH_A, q, jnp.zeros_like(q))
        m_ref[...] = jnp.full_like(m_ref, NEG)
        l_ref[...] = jnp.zeros_like(l_ref)
        acc_ref[...] = jnp.zeros_like(acc_ref)

    def step(diagonal):
        for hp in range(pairs):
            k = k_ref[:, hp * LANES:(hp + 1) * LANES]
            vt = vt_ref[hp * LANES:(hp + 1) * LANES, :]
            for h2 in range(2):
                head = 2 * hp + h2
                fb = jnp.broadcast_to(fc_ref[:, head:head + 1], (t, sub))
                for sb in range(t // sub):
                    cols = slice(h2 * t + sb * sub, h2 * t + (sb + 1) * sub)
                    s = _dot_nt(k, qs_ref[hp, cols, :]) - fb
                    if diagonal:
                        key = lax.broadcasted_iota(I32, (t, sub), 0)
                        qry = lax.broadcasted_iota(I32, (t, sub), 1) + sb * sub
                        s = jnp.where(key <= qry, s, NEG)
                    m_prev = m_ref[hp, :, cols]
                    m_new = jnp.maximum(m_prev, jnp.max(s, axis=0, keepdims=True))
                    alpha = jnp.exp(m_prev - m_new)
                    p = jnp.exp(s - m_new)
                    l_ref[hp, :, cols] = alpha * l_ref[hp, :, cols] + jnp.sum(p, axis=0, keepdims=True)
                    acc_ref[hp, :, cols] = alpha * acc_ref[hp, :, cols] + _dot(vt, p.astype(BF16))
                    m_ref[hp, :, cols] = m_new

    @pl.when(ki < qi)
    def _():
        step(False)

    @pl.when(ki == qi)
    def _():
        step(True)
        for hp in range(pairs):
            o = acc_ref[hp] / l_ref[hp]
            o = jnp.concatenate([o[:DH_A, :t], o[DH_A:, t:]], axis=0)
            o_ref[:, hp * LANES:(hp + 1) * LANES] = o.T.astype(o_ref.dtype)


def _fox(q, k, vt, fc, batch, seq):
    t = FOX_TILE
    nq = seq // t
    kv_map = lambda b, qi, ki: (b * nq + jnp.minimum(ki, qi), 0)
    return pl.pallas_call(
        _fox_kernel,
        grid=(batch, nq, nq),
        in_specs=[pl.BlockSpec((t, W_A), lambda b, qi, ki: (b * nq + qi, 0)),
                  pl.BlockSpec((t, W_A), kv_map),
                  pl.BlockSpec((W_A, t), lambda b, qi, ki: (0, b * nq + jnp.minimum(ki, qi))),
                  pl.BlockSpec((t, H_A), kv_map)],
        out_specs=pl.BlockSpec((t, W_A), lambda b, qi, ki: (b * nq + qi, 0)),
        out_shape=jax.ShapeDtypeStruct((batch * seq, W_A), BF16),
        scratch_shapes=[pltpu.VMEM((H_A // 2, 2 * t, LANES), BF16),
                        pltpu.VMEM((H_A // 2, 1, 2 * t), F32),
                        pltpu.VMEM((H_A // 2, 1, 2 * t), F32),
                        pltpu.VMEM((H_A // 2, LANES, 2 * t), F32)],
        compiler_params=_cparams(("arbitrary", "arbitrary", "arbitrary")),
        name="fox",
    )(q, k, vt, fc)


def _lfsuf_kernel(pt_ref, lf_hbm, o_ref, buf_ref, sem):
    b = pl.program_id(0)
    n_pages, _, w = buf_ref.shape

    def page_copy(j):
        return pltpu.make_async_copy(lf_hbm.at[0, pt_ref[b, j]], buf_ref.at[j], sem)

    def issue(j, carry):
        page_copy(j).start()
        return carry

    def drain(j, carry):
        page_copy(j).wait()
        return carry

    lax.fori_loop(0, n_pages, issue, 0)
    lax.fori_loop(0, n_pages, drain, 0)

    x = buf_ref[...]
    lane = lax.broadcasted_iota(I32, x.shape, 2)
    s = x
    shift = 1
    while shift < w:
        s = s + jnp.where(lane + shift < w, pltpu.roll(s, w - shift, axis=2), 0.0)
        shift *= 2
    within = s - x
    after = jnp.zeros((H_A, w), F32)
    for j in range(n_pages - 1, -1, -1):
        o_ref[j] = within[j] + after
        after = after + jnp.broadcast_to(s[j][:, 0:1], (H_A, w))


def _lfsuf(page_table, lf_pages):
    nb, n_pages = page_table.shape
    w = lf_pages.shape[3]
    return pl.pallas_call(
        _lfsuf_kernel,
        grid_spec=pltpu.PrefetchScalarGridSpec(
            num_scalar_prefetch=1,
            grid=(nb,),
            in_specs=[pl.BlockSpec(memory_space=pl.ANY)],
            out_specs=pl.BlockSpec((None, n_pages, H_A, w), lambda b, pt: (b, 0, 0, 0)),
            scratch_shapes=[pltpu.VMEM((n_pages, H_A, w), F32), pltpu.SemaphoreType.DMA(())]),
        out_shape=jax.ShapeDtypeStruct((nb, n_pages, H_A, w), F32),
        compiler_params=_cparams(("arbitrary",)),
        name="lfsuf",
    )(page_table, lf_pages)


def _foxdec_kernel(pt_ref, q_ref, kn_ref, vn_ref, lfn_ref, suf_ref, ck_hbm, cv_hbm, o_ref,
                   kbuf, vbuf, sem, m_ref, l_ref, acc_ref):
    b = pl.program_id(0)
    n_pages = suf_ref.shape[0]
    page_rows = kbuf.shape[-1]
    n_chunks = n_pages // DEC_PAGES
    tq = q_ref.shape[1]

    def page_copies(c, slot, j):
        pg = pt_ref[b, c * DEC_PAGES + j]
        return (pltpu.make_async_copy(ck_hbm.at[0, pg], kbuf.at[slot, j], sem.at[0, slot]),
                pltpu.make_async_copy(cv_hbm.at[0, pg], vbuf.at[slot, j], sem.at[1, slot]))

    def fetch(c, slot):
        for j in range(DEC_PAGES):
            for cp in page_copies(c, slot, j):
                cp.start()

    def wait(c, slot):
        for j in range(DEC_PAGES):
            for cp in page_copies(c, slot, j):
                cp.wait()

    q = q_ref[...].astype(BF16)
    m_ref[...] = jnp.full_like(m_ref, NEG)
    l_ref[...] = jnp.zeros_like(l_ref)
    acc_ref[...] = jnp.zeros_like(acc_ref)

    def attend(kt, vt, bias, ok):
        s = jnp.einsum('htd,hds->hts', q, kt, preferred_element_type=F32) + bias
        if ok is not None:
            s = jnp.where(ok, s, NEG)
        m_prev = m_ref[...]
        m_new = jnp.maximum(m_prev, jnp.max(s, axis=2, keepdims=True))
        alpha = jnp.exp(m_prev - m_new)
        p = jnp.exp(s - m_new)
        l_ref[...] = alpha * l_ref[...] + jnp.sum(p, axis=2, keepdims=True)
        acc_ref[...] = alpha * acc_ref[...] + jnp.einsum('hts,hds->htd', p.astype(BF16), vt,
                                                         preferred_element_type=F32)
        m_ref[...] = m_new

    fetch(0, 0)

    def chunk(c, carry):
        slot = c % 2

        @pl.when(c + 1 < n_chunks)
        def _():
            fetch(c + 1, 1 - slot)

        wait(c, slot)
        kt = jnp.concatenate([kbuf[slot, j] for j in range(DEC_PAGES)], axis=2).astype(BF16)
        vt = jnp.concatenate([vbuf[slot, j] for j in range(DEC_PAGES)], axis=2).astype(BF16)
        suf = suf_ref[pl.ds(c * DEC_PAGES, DEC_PAGES)]
        bias = jnp.concatenate([suf[j] for j in range(DEC_PAGES)], axis=1)
        attend(kt, vt, bias[:, None, :], None)
        return carry

    lax.fori_loop(0, n_chunks, chunk, 0)

    r = lax.broadcasted_iota(I32, (page_rows, page_rows), 0)
    c = lax.broadcasted_iota(I32, (page_rows, page_rows), 1)
    fn = _dot(lfn_ref[...], jnp.where(r <= c, 1.0, 0.0).astype(F32), precision=HIGHEST)
    qi = lax.broadcasted_iota(I32, (H_A, tq, page_rows), 1)
    ki = lax.broadcasted_iota(I32, (H_A, tq, page_rows), 2)
    attend(kn_ref[...].astype(BF16), vn_ref[...].astype(BF16), -fn[:, None, :], ki <= qi)
    o_ref[...] = acc_ref[...] / l_ref[...]


def _foxdec(page_table, q, kn, vn, lfn, suf, cache_kt, cache_vt):
    nb, _, tq, _ = q.shape
    n_pages = page_table.shape[1]
    page_rows = cache_kt.shape[4]
    per_seq = lambda shape: pl.BlockSpec((None,) + shape, lambda b, pt: (b,) + (0,) * len(shape))
    page = (H_A, DH_A, page_rows)
    return pl.pallas_call(
        _foxdec_kernel,
        grid_spec=pltpu.PrefetchScalarGridSpec(
            num_scalar_prefetch=1,
            grid=(nb,),
            in_specs=[per_seq((H_A, tq, DH_A)), per_seq(page), per_seq(page), per_seq((H_A, page_rows)),
                      per_seq((n_pages, H_A, page_rows)),
                      pl.BlockSpec(memory_space=pl.ANY), pl.BlockSpec(memory_space=pl.ANY)],
            out_specs=per_seq((H_A, tq, DH_A)),
            scratch_shapes=[pltpu.VMEM((2, DEC_PAGES) + page, F32),
                            pltpu.VMEM((2, DEC_PAGES) + page, F32),
                            pltpu.SemaphoreType.DMA((2, 2)),
                            pltpu.VMEM((H_A, tq, 1), F32), pltpu.VMEM((H_A, tq, 1), F32),
                            pltpu.VMEM((H_A, tq, DH_A), F32)]),
        out_shape=jax.ShapeDtypeStruct((nb, H_A, tq, DH_A), F32),
        compiler_params=_cparams(("arbitrary",)),
        name="foxdec",
    )(page_table, q, kn, vn, lfn, suf, cache_kt, cache_vt)


def _gla_kernel(q_ref, k_ref, g_ref, v_ref, r_ref, s0_ref, gn_ref, y_ref, sfin_ref, state_ref):
    tb = pl.program_id(1)
    n_tb = pl.num_programs(1)
    c = GLA_CHUNK
    n_chunks = q_ref.shape[0] // c
    n_sub = c // GLA_SUB

    row_k = lax.broadcasted_iota(I32, (WK_B, WV_B), 0) // DK_B
    col_v = lax.broadcasted_iota(I32, (WK_B, WV_B), 1) // DV_B
    state_diag = row_k == col_v

    @pl.when(tb == 0)
    def _():
        state_ref[...] = jnp.zeros_like(state_ref)
        for h in range(H_B):
            state_ref[h * DK_B:(h + 1) * DK_B, h * DV_B:(h + 1) * DV_B] = s0_ref[h]

    r = lax.broadcasted_iota(I32, (c, c), 0)
    s = lax.broadcasted_iota(I32, (c, c), 1)
    same_sub = (r // GLA_SUB) == (s // GLA_SUB)
    cum_ops = jnp.concatenate([
        jnp.where(same_sub & (s <= r), 1.0, 0.0),
        jnp.where(s <= r, 1.0, 0.0),
        jnp.where(same_sub, 1.0, 0.0),
    ], axis=0).astype(F32)

    t_i = lax.broadcasted_iota(I32, (c, H_B * c), 0)
    s_i = lax.broadcasted_iota(I32, (c, H_B * c), 1) % c
    dsub = t_i // GLA_SUB - s_i // GLA_SUB
    intra = (dsub == 0) & (s_i <= t_i)
    head_k = lax.broadcasted_iota(I32, (c, WK_B), 1) // DK_B
    head_v = lax.broadcasted_iota(I32, (c, WV_B), 1) // DV_B

    def stack_heads(x, head_of_lane):
        zero = jnp.zeros_like(x)
        return jnp.concatenate([jnp.where(head_of_lane == h, x, zero) for h in range(H_B)], axis=0)

    def shift_rows(x, n):
        return jnp.concatenate([x[:n], x[:c - n]], axis=0)

    gn = gn_ref[...]
    for ci in range(n_chunks):
        rows = slice(ci * c, (ci + 1) * c)
        q = q_ref[rows, :]
        k = k_ref[rows, :]
        g = g_ref[rows, :]
        v = v_ref[rows, :]

        cums = _dot(cum_ops, g, precision=HIGHEST)
        bl, bc, tot = cums[:c], cums[c:2 * c], cums[2 * c:]
        pre = bc - bl
        q_loc = q * jnp.exp(bl)
        k_loc = k * jnp.exp(-bl)
        k_end = k * jnp.exp(tot - bl)
        q_far = [q_loc] + [q_loc * jnp.exp(pre - shift_rows(pre, GLA_SUB * d)) for d in range(1, n_sub - 1)]

        a0 = _dot_nt(q_loc.astype(BF16), stack_heads(k_loc, head_k).astype(BF16))
        af = _dot_nt(jnp.concatenate(q_far, axis=0).astype(BF16), stack_heads(k_end, head_k).astype(BF16))
        att = jnp.where(intra, a0, 0.0)
        for d in range(1, n_sub):
            att = jnp.where(dsub == d, af[(d - 1) * c:d * c], att)

        q_in = q * jnp.exp(bc)
        o = _dot(att.astype(BF16), stack_heads(v, head_v)) + _dot(q_in.astype(BF16), state_ref[...].astype(BF16))

        k_out = k * jnp.exp(bc[c - 1:c, :] - bc)
        kv = _dot_tn(k_out.astype(BF16), v)
        decay = jnp.exp(_dot_tn(g, jnp.ones((c, LANES), F32), precision=HIGHEST))
        state_ref[...] = (state_ref[...] * jnp.concatenate([decay] * (WV_B // LANES), axis=1)
                          + jnp.where(state_diag, kv, 0.0))

        for h in range(H_B):
            lanes = slice(h * DV_B, (h + 1) * DV_B)
            oh = o[:, lanes]
            yh = oh * lax.rsqrt(jnp.mean(oh * oh, axis=-1, keepdims=True) + EPS) * gn
            y_ref[rows, lanes] = (yh * r_ref[rows, lanes].astype(F32)).astype(y_ref.dtype)

    @pl.when(tb == n_tb - 1)
    def _():
        for h in range(H_B):
            sfin_ref[h] = state_ref[h * DK_B:(h + 1) * DK_B, h * DV_B:(h + 1) * DV_B]


def _gla(q, k, g, v, r, s0, gn, batch, seq, tb_rows):
    n_tb = seq // tb_rows
    row = lambda w: pl.BlockSpec((tb_rows, w), lambda b, t: (b * n_tb + t, 0))
    state = pl.BlockSpec((None, H_B, DK_B, DV_B), lambda b, t: (b, 0, 0, 0))
    return pl.pallas_call(
        _gla_kernel,
        grid=(batch, n_tb),
        in_specs=[row(WK_B), row(WK_B), row(WK_B), row(WV_B), row(WV_B), state,
                  pl.BlockSpec((1, DV_B), lambda b, t: (0, 0))],
        out_specs=[row(WV_B), state],
        out_shape=[jax.ShapeDtypeStruct((batch * seq, WV_B), BF16),
                   jax.ShapeDtypeStruct((batch, H_B, DK_B, DV_B), F32)],
        scratch_shapes=[pltpu.VMEM((WK_B, WV_B), F32)],
        compiler_params=_cparams(("arbitrary", "arbitrary")),
        name="gla",
    )(q, k, g, v, r, s0, gn)


def _mixout_kernel(*refs, n_tail):
    (x_ref, ya_ref, yb_ref, ga_ref, gb_ref, mod_ref, g2_ref, wpa_ref, wpb_ref, wo_ref, rw_ref) = refs[:11]
    tail_refs = refs[11:11 + n_tail]
    x1_ref, hp_ref, lg_ref = refs[11 + n_tail:]
    i = pl.program_id(0)
    n_own = pl.num_programs(0) - (1 if n_tail else 0)

    @pl.when(i < n_own)
    def _():
        br_a = _dot(ya_ref[...], wpa_ref[...])
        br_b = _dot(yb_ref[...], wpb_ref[...])
        merged = ga_ref[...].astype(F32) * br_a + gb_ref[...].astype(F32) * br_b
        out = _dot(merged.astype(BF16), wo_ref[...])
        x1 = x_ref[...] + mod_ref[2] * out
        x1_ref[...] = x1
        y = x1 * lax.rsqrt(jnp.mean(x1 * x1, axis=-1, keepdims=True) + EPS) * g2_ref[...]
        h = y * (1.0 + mod_ref[4]) + mod_ref[3]
        hp_ref[...] = _pack_bf16_pair(h)
        lg_ref[...] = _dot_nt(rw_ref[...], h, precision=HIGHEST)

    if n_tail:
        @pl.when(i == n_own)
        def _():
            for src, dst in zip(tail_refs, (x1_ref, hp_ref, lg_ref)):
                dst[...] = src[...]


def _mixout(x, ya, yb, ga, gb, mod, rows_per_mod, g2, wpa, wpb, wo, rw, tail):
    n = x.shape[0]
    tm = MOE_TILE
    mod_rows = mod.shape[2]
    tiles_per_mod = rows_per_mod // tm
    n_own = n // tm
    n_total = n + (tm if tail else 0)
    own = lambda i: jnp.minimum(i, n_own - 1)
    row = lambda w: pl.BlockSpec((tm, w), lambda i: (own(i), 0))
    return pl.pallas_call(
        functools.partial(_mixout_kernel, n_tail=len(tail)),
        grid=(n_total // tm,),
        in_specs=[row(D_MODEL), row(W_A), row(WV_B), row(D_MODEL), row(D_MODEL),
                  pl.BlockSpec((6, None, mod_rows, D_MODEL), lambda i: (0, own(i) // tiles_per_mod, 0, 0)),
                  _const_spec(g2.shape), _const_spec(wpa.shape), _const_spec(wpb.shape), _const_spec(wo.shape),
                  _const_spec(rw.shape)] + [_const_spec(t.shape) for t in tail],
        out_specs=[pl.BlockSpec((tm, D_MODEL), lambda i: (i, 0)),
                   pl.BlockSpec((tm, D_MODEL // 2), lambda i: (i, 0)),
                   pl.BlockSpec((N_EXPERTS, tm), lambda i: (0, i))],
        out_shape=[jax.ShapeDtypeStruct((n_total, D_MODEL), F32),
                   jax.ShapeDtypeStruct((n_total, D_MODEL // 2), U32),
                   jax.ShapeDtypeStruct((N_EXPERTS, n_total), F32)],
        compiler_params=_cparams(("arbitrary",)),
        name="mixout",
    )(x, ya, yb, ga, gb, mod, g2, wpa, wpb, wo, rw, *tail)


def _route_kernel(lg_ref, bias_ref, idx_ref, w_ref, rank_ref, cnt_ref, carry_ref):
    i = pl.program_id(0)
    tm = lg_ref.shape[1]

    @pl.when(i == 0)
    def _():
        carry_ref[...] = jnp.zeros_like(carry_ref)

    s = _sigmoid(lg_ref[...])
    sb = s + bias_ref[...][:, 0:1]
    ninf = -jnp.inf

    e_in_group = lax.broadcasted_iota(I32, (GROUP_SIZE, tm), 0)
    gscores = []
    for g in range(N_GROUPS):
        blk = sb[g * GROUP_SIZE:(g + 1) * GROUP_SIZE, :]
        m1 = jnp.max(blk, axis=0, keepdims=True)
        first = jnp.min(jnp.where(blk == m1, e_in_group, GROUP_SIZE), axis=0, keepdims=True)
        m2 = jnp.max(jnp.where(e_in_group == first, ninf, blk), axis=0, keepdims=True)
        gscores.append(m1 + m2)
    gs = jnp.concatenate(gscores, axis=0)

    g_iota = lax.broadcasted_iota(I32, (N_GROUPS, tm), 0)
    chosen = jnp.zeros((N_GROUPS, tm), F32)
    for _ in range(TOPK_GROUPS):
        m = jnp.max(gs, axis=0, keepdims=True)
        first = jnp.min(jnp.where(gs == m, g_iota, N_GROUPS), axis=0, keepdims=True)
        pick = g_iota == first
        chosen = jnp.where(pick, 1.0, chosen)
        gs = jnp.where(pick, ninf, gs)
    allowed = jnp.concatenate(
        [jnp.broadcast_to(chosen[g:g + 1, :], (GROUP_SIZE, tm)) for g in range(N_GROUPS)], axis=0) > 0.5
    cand = jnp.where(allowed, sb, ninf)

    e_iota = lax.broadcasted_iota(I32, (N_EXPERTS, tm), 0)
    onehot = jnp.zeros((N_EXPERTS, tm), F32)
    idxs, ws = [], []
    for _ in range(TOP_K):
        m = jnp.max(cand, axis=0, keepdims=True)
        first = jnp.min(jnp.where(cand == m, e_iota, N_EXPERTS), axis=0, keepdims=True)
        pick = e_iota == first
        idxs.append(first)
        ws.append(jnp.sum(jnp.where(pick, s, 0.0), axis=0, keepdims=True))
        onehot = jnp.where(pick, 1.0, onehot)
        cand = jnp.where(pick, ninf, cand)
    w = jnp.concatenate(ws, axis=0)
    w_ref[...] = w / jnp.sum(w, axis=0, keepdims=True) * ROUTE_SCALE
    idx_ref[...] = jnp.concatenate(idxs, axis=0)

    r = lax.broadcasted_iota(I32, (tm, tm), 0)
    c = lax.broadcasted_iota(I32, (tm, tm), 1)
    earlier = jnp.where(r < c, 1.0, 0.0).astype(BF16)
    carry = carry_ref[...]
    before = _dot(onehot.astype(BF16), earlier) + jnp.concatenate([carry] * (tm // LANES), axis=1)
    rank_ref[...] = jnp.concatenate(
        [jnp.sum(jnp.where(e_iota == ix, before, 0.0), axis=0, keepdims=True) for ix in idxs], axis=0).astype(I32)
    carry = carry + _dot(onehot.astype(BF16), jnp.ones((tm, LANES), BF16))
    carry_ref[...] = carry
    cnt_ref[...] = carry


def _route(logits_t, bias):
    n = logits_t.shape[1]
    tm = MOE_TILE
    tok = lambda: pl.BlockSpec((TOP_K, tm), lambda i: (0, i))
    return pl.pallas_call(
        _route_kernel,
        grid=(n // tm,),
        in_specs=[pl.BlockSpec((N_EXPERTS, tm), lambda i: (0, i)), _const_spec(bias.shape)],
        out_specs=[tok(), tok(), tok(), _const_spec((N_EXPERTS, LANES))],
        out_shape=[jax.ShapeDtypeStruct((TOP_K, n), I32), jax.ShapeDtypeStruct((TOP_K, n), F32),
                   jax.ShapeDtypeStruct((TOP_K, n), I32), jax.ShapeDtypeStruct((N_EXPERTS, LANES), F32)],
        scratch_shapes=[pltpu.VMEM((N_EXPERTS, LANES), F32)],
        compiler_params=_cparams(("arbitrary",)),
        name="route",
    )(logits_t, bias)


def _dest_kernel(idx_ref, rank_ref, cnt_ref, dest_ref, start_ref):
    i = pl.program_id(0)
    tm = idx_ref.shape[1]

    @pl.when(i == 0)
    def _():
        cnt = cnt_ref[...]
        padded = jnp.floor((cnt + (EXPERT_TILE - 1)) * (1.0 / EXPERT_TILE)) * EXPERT_TILE
        r = lax.broadcasted_iota(I32, (N_EXPERTS, N_EXPERTS), 0)
        c = lax.broadcasted_iota(I32, (N_EXPERTS, N_EXPERTS), 1)
        start_ref[...] = _dot(jnp.where(c < r, 1.0, 0.0).astype(F32), padded, precision=HIGHEST)

    start = jnp.concatenate([start_ref[...]] * (tm // LANES), axis=1)
    e_iota = lax.broadcasted_iota(I32, (N_EXPERTS, tm), 0)
    idx = idx_ref[...]
    first = jnp.concatenate(
        [jnp.sum(jnp.where(e_iota == idx[kk:kk + 1, :], start, 0.0), axis=0, keepdims=True) for kk in range(TOP_K)],
        axis=0)
    dest_ref[...] = first.astype(I32) + rank_ref[...]


def _dest(idx_t, rank_t, counts):
    n = idx_t.shape[1]
    tm = MOE_TILE
    tok = lambda: pl.BlockSpec((TOP_K, tm), lambda i: (0, i))
    return pl.pallas_call(
        _dest_kernel,
        grid=(n // tm,),
        in_specs=[tok(), tok(), _const_spec(counts.shape)],
        out_specs=pl.BlockSpec((None, TOP_K, tm), lambda i: (i, 0, 0)),
        out_shape=jax.ShapeDtypeStruct((n // tm, TOP_K, tm), I32),
        scratch_shapes=[pltpu.VMEM((N_EXPERTS, LANES), F32)],
        compiler_params=_cparams(("arbitrary",)),
        name="dest",
    )(idx_t, rank_t, counts)


def _sc_worker():
    return lax.axis_index("s") * SC_CORES + lax.axis_index("c")


def _sc_mesh():
    return plsc.VectorSubcoreMesh(core_axis_name="c", subcore_axis_name="s")


def _dispatch(dest_tiles, hp, n_rows_out):
    n_tok_tiles = dest_tiles.shape[0]
    w = hp.shape[1]
    iters = -(-n_tok_tiles // SC_WORKERS)

    def body(hp_hbm, dest_hbm, xs_hbm, idx_v, rows_v, sem):
        wid = _sc_worker()

        @pl.loop(0, iters)
        def _(j):
            tile = wid + SC_WORKERS * j

            @pl.when(tile < n_tok_tiles)
            def _():
                pltpu.sync_copy(dest_hbm.at[tile], idx_v)
                pltpu.sync_copy(hp_hbm.at[pl.ds(tile * MOE_TILE, MOE_TILE)], rows_v)
                copies = [pltpu.async_copy(rows_v, xs_hbm.at[idx_v.at[kk]], sem) for kk in range(TOP_K)]
                for cp in copies:
                    cp.wait()

    return pl.kernel(
        body,
        out_type=jax.ShapeDtypeStruct((n_rows_out, w), U32),
        mesh=_sc_mesh(),
        scratch_types=[pltpu.VMEM((TOP_K, MOE_TILE), I32), pltpu.VMEM((MOE_TILE, w), U32), pltpu.SemaphoreType.DMA],
        name="dispatch",
    )(hp, dest_tiles)


def _gather_back(ys, dest_tiles):
    n_tok_tiles = dest_tiles.shape[0]
    w = ys.shape[1]
    n_lists = n_tok_tiles * TOP_K
    iters = -(-n_lists // SC_WORKERS)

    def body(ys_hbm, dest_hbm, out_hbm, idx_v, rows_v, sem):
        wid = _sc_worker()

        @pl.loop(0, iters)
        def _(j):
            lst = wid + SC_WORKERS * j

            @pl.when(lst < n_lists)
            def _():
                pltpu.sync_copy(dest_hbm.at[pl.ds(lst * MOE_TILE, MOE_TILE)], idx_v)
                pltpu.async_copy(ys_hbm.at[idx_v], rows_v, sem).wait()
                pltpu.sync_copy(rows_v, out_hbm.at[pl.ds(lst * MOE_TILE, MOE_TILE)])

    out = pl.kernel(
        body,
        out_type=jax.ShapeDtypeStruct((n_lists * MOE_TILE, w), U32),
        mesh=_sc_mesh(),
        scratch_types=[pltpu.VMEM((MOE_TILE,), I32), pltpu.VMEM((MOE_TILE, w), U32), pltpu.SemaphoreType.DMA],
        name="gather_back",
    )(ys, dest_tiles.reshape(-1))
    return out.reshape(n_tok_tiles, TOP_K, MOE_TILE, w)


def _experts_kernel(te_ref, nv_ref, rows_ref, xs_ref, wg_ref, wu_ref, wd_ref, ys_ref, wgb_ref, wub_ref, wdb_ref):
    i = pl.program_id(0)

    @pl.when(i < nv_ref[0])
    def _():
        @pl.when((i == 0) | (te_ref[i] != te_ref[jnp.maximum(i - 1, 0)]))
        def _():
            wgb_ref[...] = wg_ref[...].astype(BF16)
            wub_ref[...] = wu_ref[...].astype(BF16)
            wdb_ref[...] = wd_ref[...].astype(BF16)

        half = D_MODEL // 2
        xs = xs_ref[...]
        filled = lax.broadcasted_iota(I32, xs.shape, 0) < rows_ref[i]
        lo, hi = _unpack_bf16_pair(jnp.where(filled, xs, jnp.zeros_like(xs)))
        gate = _dot(lo, wgb_ref[:half, :]) + _dot(hi, wgb_ref[half:, :])
        up = _dot(lo, wub_ref[:half, :]) + _dot(hi, wub_ref[half:, :])
        act = (_silu(gate) * up).astype(BF16)
        ys_ref[...] = _pack_bf16_pair(_dot(act, wdb_ref[...]))

    @pl.when(i >= nv_ref[0])
    def _():
        ys_ref[...] = jnp.zeros_like(ys_ref)


def _experts(tile_expert, n_valid, tile_rows, xs, wg, wu, wd):
    n_rows, w = xs.shape
    tm = EXPERT_TILE
    tile = lambda i, te, nv, tr: (jnp.minimum(i, nv[0] - 1), 0)
    expert = lambda i, te, nv, tr: (te[i], 0, 0)
    return pl.pallas_call(
        _experts_kernel,
        grid_spec=pltpu.PrefetchScalarGridSpec(
            num_scalar_prefetch=3,
            grid=(n_rows // tm,),
            in_specs=[pl.BlockSpec((tm, w), tile),
                      pl.BlockSpec((None, D_MODEL, D_EXPERT), expert),
                      pl.BlockSpec((None, D_MODEL, D_EXPERT), expert),
                      pl.BlockSpec((None, D_EXPERT, D_MODEL), expert)],
            out_specs=pl.BlockSpec((tm, w), lambda i, te, nv, tr: (i, 0)),
            scratch_shapes=[pltpu.VMEM((D_MODEL, D_EXPERT), BF16), pltpu.VMEM((D_MODEL, D_EXPERT), BF16),
                            pltpu.VMEM((D_EXPERT, D_MODEL), BF16)]),
        out_shape=jax.ShapeDtypeStruct((n_rows, w), U32),
        compiler_params=_cparams(("arbitrary",)),
        name="experts",
    )(tile_expert, n_valid, tile_rows, xs, wg, wu, wd)


def _combine_kernel(yt_ref, x1_ref, hp_ref, w_ref, mod_ref, fg_ref, sg_ref, su_ref, sd_ref, o_ref):
    half = D_MODEL // 2
    lo, hi = _unpack_bf16_pair(hp_ref[...])
    sg = sg_ref[...]
    su = su_ref[...]
    gate = _dot(lo, sg[:half]) + _dot(hi, sg[half:])
    up = _dot(lo, su[:half]) + _dot(hi, su[half:])
    y = _dot((_silu(gate) * up).astype(BF16), sd_ref[...])

    w = w_ref[...]
    for kk in range(TOP_K):
        ylo, yhi = _unpack_bf16_pair(yt_ref[kk])
        y = y + w[:, kk:kk + 1] * jnp.concatenate([ylo.astype(F32), yhi.astype(F32)], axis=1)
    x2 = x1_ref[...] + mod_ref[5] * y
    o_ref[...] = x2 * lax.rsqrt(jnp.mean(x2 * x2, axis=-1, keepdims=True) + EPS) * fg_ref[...]


def _combine(y_tok, x1, hp, w, mod, rows_per_mod, fg, sg, su, sd, n, row_offset):
    tm = MOE_TILE
    mod_rows = mod.shape[2]
    tiles_per_mod = rows_per_mod // tm
    off = row_offset // tm
    return pl.pallas_call(
        _combine_kernel,
        grid=(n // tm,),
        in_specs=[pl.BlockSpec((None, TOP_K, tm, D_MODEL // 2), lambda i: (i + off, 0, 0, 0)),
                  pl.BlockSpec((tm, D_MODEL), lambda i: (i + off, 0)),
                  pl.BlockSpec((tm, D_MODEL // 2), lambda i: (i + off, 0)),
                  pl.BlockSpec((tm, TOP_K), lambda i: (i + off, 0)),
                  pl.BlockSpec((6, None, mod_rows, D_MODEL), lambda i: (0, i // tiles_per_mod, 0, 0)),
                  _const_spec(fg.shape), _const_spec(sg.shape), _const_spec(su.shape), _const_spec(sd.shape)],
        out_specs=pl.BlockSpec((tm, D_MODEL), lambda i: (i, 0)),
        out_shape=jax.ShapeDtypeStruct((n, D_MODEL), F32),
        compiler_params=_cparams(("arbitrary",)),
        name="combine",
    )(y_tok, x1, hp, w, mod, fg, sg, su, sd)


def _split_w_in(w_in):
    sizes = (W_A, W_A, W_A, H_A, WK_B, WK_B, WV_B, WV_B, GK_RANK, D_MODEL, D_MODEL)
    segs, o = [], 0
    for s in sizes:
        segs.append(w_in[:, o:o + s])
        o += s
    return segs


def kernel(x_prompt, x_sample, cache_k, cache_v, cache_logf, state_gla, page_table, c_prompt, c_sample, w_ada, b_ada,
           norm1_g, w_in, b_f, w_gk2, b_gk, gla_norm_g, w_pa, w_pb, w_o, norm2_g, router_w, router_bias, w_gate, w_up,
           w_down, ws_gate, ws_up, ws_down, final_g):
    assert w_ada.shape[0] == 1, "single layer"
    bp, tp, d = x_prompt.shape
    bs, ts, _ = x_sample.shape
    n_p, n_s = bp * tp, bs * ts
    n_tot = n_p + n_s
    assert n_s == MOE_TILE and n_p % MOE_TILE == 0 and tp % FOX_TILE == 0

    q_a, k_a, v_a, f_a, q_b, k_b, v_b, r_b, gk1, gate_a, gate_b = _split_w_in(w_in[0])
    wts = (jnp.concatenate([q_a, k_a, v_a], axis=1).astype(BF16),
           jnp.pad(f_a, ((0, 0), (0, LANES - H_A))).astype(BF16),
           jnp.pad(b_f[0].reshape(1, H_A), ((0, 0), (0, LANES - H_A))),
           v_a.T.astype(BF16),
           jnp.concatenate([q_b, k_b, v_b, r_b], axis=1).astype(BF16),
           jnp.pad(gk1, ((0, 0), (0, LANES - GK_RANK))).astype(BF16),
           jnp.pad(w_gk2[0], ((0, LANES - GK_RANK), (0, 0))).astype(BF16),
           b_gk[0].reshape(1, WK_B),
           jnp.concatenate([gate_a, gate_b], axis=1).astype(BF16))
    g1 = norm1_g[0].reshape(1, d)
    g2 = norm2_g[0].reshape(1, d)
    gn = gla_norm_g[0].reshape(1, DV_B)
    fg = final_g.reshape(1, d)
    wpa, wpb, wo = w_pa[0].astype(BF16), w_pb[0].astype(BF16), w_o[0].astype(BF16)
    rw_t = router_w[0].T
    rbias = jnp.broadcast_to(router_bias[0].reshape(N_EXPERTS, 1), (N_EXPERTS, LANES))
    sg, su, sd = ws_gate[0].astype(BF16), ws_up[0].astype(BF16), ws_down[0].astype(BF16)

    ada = _ada(jnp.concatenate([c_prompt, c_sample], axis=0), w_ada[0], b_ada[0].reshape(1, -1))
    ada = ada.reshape(bp + bs, 6, d)
    mod_p = ada[:bp].transpose(1, 0, 2)[:, :, None, :]
    mod_s = jnp.repeat(ada[bp:], ts, axis=0).transpose(1, 0, 2)[:, None, :, :]

    xp = x_prompt.reshape(n_p, d)
    tm_p = 256
    (q, k, v, kh, vt, lf, fc, qb, kb, gk, vb, rb, ga_p, gb_p) = _inproj(xp, mod_p, tp, tp // tm_p, tm_p, g1, wts)
    ya_p = _fox(q, kh, vt, fc, bp, tp)
    yb_p, s_p = _gla(qb, kb, gk, vb, rb, jnp.zeros((bp, H_B, DK_B, DV_B), F32), gn, bp, tp, 256)
    k_prompt = k.reshape(1, bp, tp, H_A, DH_A)
    v_prompt = v.reshape(1, bp, tp, H_A, DH_A)
    logf_prompt = lf.reshape(1, bp, tp, H_A)

    xs_in = x_sample.reshape(n_s, d)
    (q, k, v, _, _, lf, _, qb, kb, gk, vb, rb, ga, gb) = _inproj(xs_in, mod_s, n_s, 1, n_s, g1, wts)
    page_rows = cache_k.shape[2]
    tq = 8
    cache_kt = jnp.transpose(cache_k, (0, 1, 3, 4, 2))
    cache_vt = jnp.transpose(cache_v, (0, 1, 3, 4, 2))
    suf = _lfsuf(page_table, jnp.transpose(cache_logf, (0, 1, 3, 2)))

    def new_kv(a):
        a = a.reshape(bs, ts, H_A, DH_A).transpose(0, 2, 3, 1)
        return jnp.pad(a, ((0, 0), (0, 0), (0, 0), (0, page_rows - ts)))

    q_dec = jnp.pad(q.astype(F32).reshape(bs, ts, H_A, DH_A).transpose(0, 2, 1, 3),
                    ((0, 0), (0, 0), (0, tq - ts), (0, 0)))
    lfn = jnp.pad(lf.reshape(bs, ts, H_A).transpose(0, 2, 1), ((0, 0), (0, 0), (0, page_rows - ts)))
    ya = _foxdec(page_table, q_dec, new_kv(k), new_kv(v), lfn, suf, cache_kt, cache_vt)
    ya = ya[:, :, :ts].transpose(0, 2, 1, 3).reshape(n_s, W_A).astype(BF16)
    pad = lambda a: jnp.pad(a.reshape(bs, ts, -1), ((0, 0), (0, GLA_CHUNK - ts), (0, 0))).reshape(bs * GLA_CHUNK, -1)
    yb, s_s = _gla(pad(qb), pad(kb), pad(gk), pad(vb), pad(rb), state_gla[0], gn, bs, GLA_CHUNK, GLA_CHUNK)
    yb = yb.reshape(bs, GLA_CHUNK, WV_B)[:, :ts].reshape(n_s, WV_B)
    k_sample = k.reshape(1, bs, ts, H_A, DH_A)
    v_sample = v.reshape(1, bs, ts, H_A, DH_A)
    logf_sample = lf.reshape(1, bs, ts, H_A)
    tail = _mixout(xs_in, ya, yb, ga, gb, mod_s, n_s, g2, wpa, wpb, wo, rw_t, ())

    x1, hp, logits_t = _mixout(xp, ya_p, yb_p, ga_p, gb_p, mod_p, tp, g2, wpa, wpb, wo, rw_t, tuple(tail))
    idx_t, w_t, rank_t, counts = _route(logits_t, rbias)
    dest_tiles = _dest(idx_t, rank_t, counts)
    padded = (counts[:, 0].astype(I32) + EXPERT_TILE - 1) // EXPERT_TILE * EXPERT_TILE
    pad_end = jnp.cumsum(padded)
    n_tiles = -(-n_tot * TOP_K // EXPERT_TILE) + N_EXPERTS
    tile_first_row = jnp.arange(n_tiles, dtype=I32) * EXPERT_TILE
    tile_expert = jnp.minimum(jnp.sum((pad_end[None, :] <= tile_first_row[:, None]).astype(I32), axis=1),
                              N_EXPERTS - 1)
    n_valid = (pad_end[-1:] // EXPERT_TILE).astype(I32)
    used_end = pad_end - padded + counts[:, 0].astype(I32)
    tile_rows = jnp.clip(used_end[tile_expert] - tile_first_row, 0, EXPERT_TILE).astype(I32)

    xs_sorted = _dispatch(dest_tiles, hp, n_tiles * EXPERT_TILE)
    ys_sorted = _experts(tile_expert, n_valid, tile_rows, xs_sorted, w_gate[0], w_up[0], w_down[0])
    y_tok = _gather_back(ys_sorted, dest_tiles)
    w_tok = w_t.T
    y_prompt = _combine(y_tok, x1, hp, w_tok, mod_p, tp, fg, sg, su, sd, n_p, 0)
    y_sample = _combine(y_tok, x1, hp, w_tok, mod_s, n_s, fg, sg, su, sd, n_s, n_p)

    return (y_prompt.reshape(bp, tp, d), y_sample.reshape(bs, ts, d),
            k_prompt, v_prompt, logf_prompt, s_p[None],
            k_sample, v_sample, logf_sample, s_s[None])
```

```python
import functools

import jax
import jax.numpy as jnp
from jax import lax
from jax.experimental import pallas as pl
from jax.experimental.pallas import tpu as pltpu
from jax.experimental.pallas import tpu_sc as plsc

F32 = jnp.float32
BF16 = jnp.bfloat16
U32 = jnp.uint32
I32 = jnp.int32

EPS = 1e-6
D_MODEL = 1024
H_A, DH_A, W_A = 8, 64, 512
H_B, DK_B, DV_B, WK_B, WV_B = 4, 64, 128, 256, 512
GK_RANK = 16
GATE_NORM = 16.0
N_EXPERTS, TOP_K, N_GROUPS, TOPK_GROUPS = 256, 8, 8, 4
GROUP_SIZE = N_EXPERTS // N_GROUPS
D_EXPERT = 256
ROUTE_SCALE = 2.5
MOE_TILE = 128
EXPERT_TILE = 256
LANES = 128
SC_CORES = 2
SC_WORKERS = 32
GLA_CHUNK = 64
GLA_SUB = 16
FOX_TILE = 512
FOX_SUB = 256
DEC_PAGES = 8
NEG = -1e30
VMEM_LIMIT = 56 * 1024 * 1024
HIGHEST = lax.Precision.HIGHEST

NT_DIMS = (((1,), (1,)), ((), ()))
TN_DIMS = (((0,), (0,)), ((), ()))


def _cparams(sem):
    return pltpu.CompilerParams(dimension_semantics=sem, vmem_limit_bytes=VMEM_LIMIT)


def _dot(a, b, **kw):
    return jnp.dot(a, b, preferred_element_type=F32, **kw)


def _dot_nt(a, b, **kw):
    return lax.dot_general(a, b, NT_DIMS, preferred_element_type=F32, **kw)


def _dot_tn(a, b, **kw):
    return lax.dot_general(a, b, TN_DIMS, preferred_element_type=F32, **kw)


def _sigmoid(x):
    return 1.0 / (1.0 + jnp.exp(-x))


def _silu(x):
    return x * _sigmoid(x)


def _log_sigmoid(x):
    return jnp.minimum(x, 0.0) - jnp.log(1.0 + jnp.exp(-jnp.abs(x)))


def _pack_bf16_pair(x):
    c = x.shape[1] // 2
    bits = pltpu.bitcast(x.astype(BF16).astype(F32), U32)
    return (bits[:, c:] & jnp.uint32(0xFFFF0000)) | (bits[:, :c] >> 16)


def _unpack_bf16_pair(p):
    lo = pltpu.bitcast(p << 16, F32).astype(BF16)
    hi = pltpu.bitcast(p & jnp.uint32(0xFFFF0000), F32).astype(BF16)
    return lo, hi


def _ada_kernel(c_ref, w_ref, b_ref, o_ref):
    c = c_ref[...]
    o_ref[...] = _dot(_silu(c).astype(BF16), w_ref[...].astype(BF16)) + b_ref[...]


def _ada(c, w, b):
    n, d = c.shape
    cols = w.shape[1]
    tn = 1536
    return pl.pallas_call(
        _ada_kernel,
        grid=(cols // tn,),
        in_specs=[pl.BlockSpec((n, d), lambda j: (0, 0)),
                  pl.BlockSpec((d, tn), lambda j: (0, j)),
                  pl.BlockSpec((1, tn), lambda j: (0, j))],
        out_specs=pl.BlockSpec((n, tn), lambda j: (0, j)),
        out_shape=jax.ShapeDtypeStruct((n, cols), F32),
        compiler_params=_cparams(("arbitrary",)),
        name="ada",
    )(c, w, b)


def _inproj_kernel(*refs, seq_tiles, prompt):
    (x_ref, mod_ref, g_ref, wqk_ref, wv_ref, wkvt_ref, wf_ref, bf_ref, wb_ref, wgk1_ref, wgk2_ref, bgk_ref,
     wgate_ref) = refs[:13]
    if prompt:
        (q_ref, kh_ref, vth_ref, kt_ref, vt_ref, lf_ref, fc_ref, qb_ref, kb_ref, gk_ref, vb_ref, rb_ref, ga_ref,
         gb_ref, carry_ref) = refs[13:]
    else:
        q_ref, k_ref, v_ref, lf_ref, qb_ref, kb_ref, gk_ref, vb_ref, rb_ref, ga_ref, gb_ref = refs[13:]
    i = pl.program_id(0)
    tm = x_ref.shape[0]
    x = x_ref[...]
    y = x * lax.rsqrt(jnp.mean(x * x, axis=-1, keepdims=True) + EPS) * g_ref[...]
    h = (y * (1.0 + mod_ref[1]) + mod_ref[0]).astype(BF16)

    z = _dot(h, wqk_ref[...])
    q_ref[...] = (z[:, :W_A] * DH_A ** -0.5).astype(BF16)
    k = z[:, W_A:]

    lf = _log_sigmoid(_dot(h, wf_ref[...]) + bf_ref[...])
    lf_ref[...] = lf[:, :H_A]

    if prompt:
        kvt = _dot_nt(wkvt_ref[...], h)
        kt_ref[...] = kvt[:W_A]
        vt_ref[...] = kvt[W_A:]
        vth_ref[...] = kvt[W_A:].astype(BF16)
        kh_ref[...] = k.astype(BF16)

        @pl.when(i % seq_tiles == 0)
        def _():
            carry_ref[...] = jnp.zeros_like(carry_ref)

        r = lax.broadcasted_iota(I32, (tm, tm), 0)
        c = lax.broadcasted_iota(I32, (tm, tm), 1)
        lower = jnp.where(c <= r, 1.0, 0.0).astype(F32)
        carry = carry_ref[...]
        fc = _dot(lower, lf, precision=HIGHEST) + carry[0:1, :]
        fc_ref[...] = fc[:, :H_A]
        carry_ref[...] = carry + _dot(jnp.ones((8, tm), F32), lf, precision=HIGHEST)
    else:
        k_ref[...] = k
        v_ref[...] = _dot(h, wv_ref[...])

    z = _dot(h, wb_ref[...])
    qb_ref[...] = z[:, :WK_B] * DK_B ** -0.5
    kb_ref[...] = z[:, WK_B:2 * WK_B]
    vb_ref[...] = z[:, 2 * WK_B:2 * WK_B + WV_B].astype(BF16)
    rb_ref[...] = _silu(z[:, 2 * WK_B + WV_B:]).astype(BF16)

    lr = _dot(h, wgk1_ref[...]).astype(BF16)
    gk_ref[...] = _log_sigmoid(_dot(lr, wgk2_ref[...]) + bgk_ref[...]) * (1.0 / GATE_NORM)

    z = _dot(h, wgate_ref[...])
    ga_ref[...] = _sigmoid(z[:, :D_MODEL]).astype(BF16)
    gb_ref[...] = _sigmoid(z[:, D_MODEL:]).astype(BF16)


def _const_spec(shape):
    return pl.BlockSpec(shape, lambda i: (0,) * len(shape))


def _inproj(x, mod, rows_per_mod, seq_tiles, tm, g, wts, prompt):
    n = x.shape[0]
    mod_rows = mod.shape[2]
    tiles_per_mod = rows_per_mod // tm
    row = lambda w: pl.BlockSpec((tm, w), lambda i: (i, 0))
    if prompt:
        n_seq = n // (seq_tiles * tm)
        seq_t = pl.BlockSpec((None, W_A, tm), lambda i: (i // seq_tiles, 0, i % seq_tiles))
        kv_shape = (n_seq, W_A, seq_tiles * tm)
        outs = [
            (row(W_A), (n, W_A), BF16),
            (row(W_A), (n, W_A), BF16),
            (pl.BlockSpec((W_A, tm), lambda i: (0, i)), (W_A, n), BF16),
            (seq_t, kv_shape, F32),
            (seq_t, kv_shape, F32),
            (row(H_A), (n, H_A), F32),
            (row(H_A), (n, H_A), F32),
        ]
    else:
        outs = [
            (row(W_A), (n, W_A), BF16),
            (row(W_A), (n, W_A), F32),
            (row(W_A), (n, W_A), F32),
            (row(H_A), (n, H_A), F32),
        ]
    outs += [
        (row(WK_B), (n, WK_B), F32),
        (row(WK_B), (n, WK_B), F32),
        (row(WK_B), (n, WK_B), F32),
        (row(WV_B), (n, WV_B), BF16),
        (row(WV_B), (n, WV_B), BF16),
        (row(D_MODEL), (n, D_MODEL), BF16),
        (row(D_MODEL), (n, D_MODEL), BF16),
    ]
    return pl.pallas_call(
        functools.partial(_inproj_kernel, seq_tiles=seq_tiles, prompt=prompt),
        grid=(n // tm,),
        in_specs=[row(D_MODEL),
                  pl.BlockSpec((6, None, mod_rows, D_MODEL), lambda i: (0, i // tiles_per_mod, 0, 0)),
                  _const_spec(g.shape)] + [_const_spec(w.shape) for w in wts],
        out_specs=[o[0] for o in outs],
        out_shape=[jax.ShapeDtypeStruct(o[1], o[2]) for o in outs],
        scratch_shapes=[pltpu.VMEM((8, LANES), F32)] if prompt else [],
        compiler_params=_cparams(("arbitrary",)),
        name="inproj",
    )(x, mod, g, *wts)


def _fox_kernel(q_ref, k_ref, vt_ref, fc_ref, o_ref, qs_ref, m_ref, l_ref, acc_ref):
    qi = pl.program_id(1)
    ki = pl.program_id(2)
    t = q_ref.shape[0]
    pairs = H_A // 2
    sub = FOX_SUB

    @pl.when(ki == 0)
    def _():
        lane = lax.broadcasted_iota(I32, (t, LANES), 1)
        for hp in range(pairs):
            q = q_ref[:, hp * LANES:(hp + 1) * LANES]
            qs_ref[hp, :t, :] = jnp.where(lane < DH_A, q, jnp.zeros_like(q))
            qs_ref[hp, t:, :] = jnp.where(lane >= DH_A, q, jnp.zeros_like(q))
        m_ref[...] = jnp.full_like(m_ref, NEG)
        l_ref[...] = jnp.zeros_like(l_ref)
        acc_ref[...] = jnp.zeros_like(acc_ref)

    def step(diagonal):
        for hp in range(pairs):
            k = k_ref[:, hp * LANES:(hp + 1) * LANES]
            vt = vt_ref[hp * LANES:(hp + 1) * LANES, :]
            for h2 in range(2):
                head = 2 * hp + h2
                fb = jnp.broadcast_to(fc_ref[:, head:head + 1], (t, sub))
                for sb in range(t // sub):
                    cols = slice(h2 * t + sb * sub, h2 * t + (sb + 1) * sub)
                    s = _dot_nt(k, qs_ref[hp, cols, :]) - fb
                    if diagonal:
                        key = lax.broadcasted_iota(I32, (t, sub), 0)
                        qry = lax.broadcasted_iota(I32, (t, sub), 1) + sb * sub
                        s = jnp.where(key <= qry, s, NEG)
                    m_prev = m_ref[hp, :, cols]
                    m_new = jnp.maximum(m_prev, jnp.max(s, axis=0, keepdims=True))
                    alpha = jnp.exp(m_prev - m_new)
                    p = jnp.exp(s - m_new)
                    l_ref[hp, :, cols] = alpha * l_ref[hp, :, cols] + jnp.sum(p, axis=0, keepdims=True)
                    acc_ref[hp, :, cols] = alpha * acc_ref[hp, :, cols] + _dot(vt, p.astype(BF16))
                    m_ref[hp, :, cols] = m_new

    @pl.when(ki < qi)
    def _():
        step(False)

    @pl.when(ki == qi)
    def _():
        step(True)
        for hp in range(pairs):
            o = acc_ref[hp] / l_ref[hp]
            o = jnp.concatenate([o[:DH_A, :t], o[DH_A:, t:]], axis=0)
            o_ref[:, hp * LANES:(hp + 1) * LANES] = o.T.astype(o_ref.dtype)


def _fox(q, k, vt, fc, batch, seq):
    t = FOX_TILE
    nq = seq // t
    kv_map = lambda b, qi, ki: (b * nq + jnp.minimum(ki, qi), 0)
    return pl.pallas_call(
        _fox_kernel,
        grid=(batch, nq, nq),
        in_specs=[pl.BlockSpec((t, W_A), lambda b, qi, ki: (b * nq + qi, 0)),
                  pl.BlockSpec((t, W_A), kv_map),
                  pl.BlockSpec((W_A, t), lambda b, qi, ki: (0, b * nq + jnp.minimum(ki, qi))),
                  pl.BlockSpec((t, H_A), kv_map)],
        out_specs=pl.BlockSpec((t, W_A), lambda b, qi, ki: (b * nq + qi, 0)),
        out_shape=jax.ShapeDtypeStruct((batch * seq, W_A), BF16),
        scratch_shapes=[pltpu.VMEM((H_A // 2, 2 * t, LANES), BF16),
                        pltpu.VMEM((H_A // 2, 1, 2 * t), F32),
                        pltpu.VMEM((H_A // 2, 1, 2 * t), F32),
                        pltpu.VMEM((H_A // 2, LANES, 2 * t), F32)],
        compiler_params=_cparams(("arbitrary", "arbitrary", "arbitrary")),
        name="fox",
    )(q, k, vt, fc)


def _lfsuf_kernel(pt_ref, lf_hbm, o_ref, buf_ref, sem):
    b = pl.program_id(0)
    n_pages, _, w = buf_ref.shape

    def page_copy(j):
        return pltpu.make_async_copy(lf_hbm.at[0, pt_ref[b, j]], buf_ref.at[j], sem)

    def issue(j, carry):
        page_copy(j).start()
        return carry

    def drain(j, carry):
        page_copy(j).wait()
        return carry

    lax.fori_loop(0, n_pages, issue, 0)
    lax.fori_loop(0, n_pages, drain, 0)

    x = buf_ref[...]
    lane = lax.broadcasted_iota(I32, x.shape, 2)
    s = x
    shift = 1
    while shift < w:
        s = s + jnp.where(lane + shift < w, pltpu.roll(s, w - shift, axis=2), 0.0)
        shift *= 2
    within = s - x
    after = jnp.zeros((H_A, w), F32)
    for j in range(n_pages - 1, -1, -1):
        o_ref[j] = within[j] + after
        after = after + jnp.broadcast_to(s[j][:, 0:1], (H_A, w))


def _lfsuf(page_table, lf_pages):
    nb, n_pages = page_table.shape
    w = lf_pages.shape[3]
    return pl.pallas_call(
        _lfsuf_kernel,
        grid_spec=pltpu.PrefetchScalarGridSpec(
            num_scalar_prefetch=1,
            grid=(nb,),
            in_specs=[pl.BlockSpec(memory_space=pl.ANY)],
            out_specs=pl.BlockSpec((None, n_pages, H_A, w), lambda b, pt: (b, 0, 0, 0)),
            scratch_shapes=[pltpu.VMEM((n_pages, H_A, w), F32), pltpu.SemaphoreType.DMA(())]),
        out_shape=jax.ShapeDtypeStruct((nb, n_pages, H_A, w), F32),
        compiler_params=_cparams(("arbitrary",)),
        name="lfsuf",
    )(page_table, lf_pages)


def _foxdec_kernel(pt_ref, q_ref, kn_ref, vn_ref, lfn_ref, suf_ref, ck_hbm, cv_hbm, o_ref,
                   kbuf, vbuf, sem, m_ref, l_ref, acc_ref):
    b = pl.program_id(0)
    n_pages = suf_ref.shape[0]
    page_rows = kbuf.shape[-1]
    n_chunks = n_pages // DEC_PAGES
    tq = q_ref.shape[1]

    def page_copies(c, slot, j):
        pg = pt_ref[b, c * DEC_PAGES + j]
        return (pltpu.make_async_copy(ck_hbm.at[0, pg], kbuf.at[slot, j], sem.at[0, slot]),
                pltpu.make_async_copy(cv_hbm.at[0, pg], vbuf.at[slot, j], sem.at[1, slot]))

    def fetch(c, slot):
        for j in range(DEC_PAGES):
            for cp in page_copies(c, slot, j):
                cp.start()

    def wait(c, slot):
        for j in range(DEC_PAGES):
            for cp in page_copies(c, slot, j):
                cp.wait()

    q = q_ref[...].astype(BF16)
    m_ref[...] = jnp.full_like(m_ref, NEG)
    l_ref[...] = jnp.zeros_like(l_ref)
    acc_ref[...] = jnp.zeros_like(acc_ref)

    def attend(kt, vt, bias, ok):
        s = jnp.einsum('htd,hds->hts', q, kt, preferred_element_type=F32) + bias
        if ok is not None:
            s = jnp.where(ok, s, NEG)
        m_prev = m_ref[...]
        m_new = jnp.maximum(m_prev, jnp.max(s, axis=2, keepdims=True))
        alpha = jnp.exp(m_prev - m_new)
        p = jnp.exp(s - m_new)
        l_ref[...] = alpha * l_ref[...] + jnp.sum(p, axis=2, keepdims=True)
        acc_ref[...] = alpha * acc_ref[...] + jnp.einsum('hts,hds->htd', p.astype(BF16), vt,
                                                         preferred_element_type=F32)
        m_ref[...] = m_new

    fetch(0, 0)

    def chunk(c, carry):
        slot = c % 2

        @pl.when(c + 1 < n_chunks)
        def _():
            fetch(c + 1, 1 - slot)

        wait(c, slot)
        kt = jnp.concatenate([kbuf[slot, j] for j in range(DEC_PAGES)], axis=2).astype(BF16)
        vt = jnp.concatenate([vbuf[slot, j] for j in range(DEC_PAGES)], axis=2).astype(BF16)
        suf = suf_ref[pl.ds(c * DEC_PAGES, DEC_PAGES)]
        bias = jnp.concatenate([suf[j] for j in range(DEC_PAGES)], axis=1)
        attend(kt, vt, bias[:, None, :], None)
        return carry

    lax.fori_loop(0, n_chunks, chunk, 0)

    r = lax.broadcasted_iota(I32, (page_rows, page_rows), 0)
    c = lax.broadcasted_iota(I32, (page_rows, page_rows), 1)
    fn = _dot(lfn_ref[...], jnp.where(r <= c, 1.0, 0.0).astype(F32), precision=HIGHEST)
    qi = lax.broadcasted_iota(I32, (H_A, tq, page_rows), 1)
    ki = lax.broadcasted_iota(I32, (H_A, tq, page_rows), 2)
    attend(kn_ref[...].astype(BF16), vn_ref[...].astype(BF16), -fn[:, None, :], ki <= qi)
    o_ref[...] = acc_ref[...] / l_ref[...]


def _foxdec(page_table, q, kn, vn, lfn, suf, cache_kt, cache_vt):
    nb, _, tq, _ = q.shape
    n_pages = page_table.shape[1]
    page_rows = cache_kt.shape[4]
    per_seq = lambda shape: pl.BlockSpec((None,) + shape, lambda b, pt: (b,) + (0,) * len(shape))
    page = (H_A, DH_A, page_rows)
    return pl.pallas_call(
        _foxdec_kernel,
        grid_spec=pltpu.PrefetchScalarGridSpec(
            num_scalar_prefetch=1,
            grid=(nb,),
            in_specs=[per_seq((H_A, tq, DH_A)), per_seq(page), per_seq(page), per_seq((H_A, page_rows)),
                      per_seq((n_pages, H_A, page_rows)),
                      pl.BlockSpec(memory_space=pl.ANY), pl.BlockSpec(memory_space=pl.ANY)],
            out_specs=per_seq((H_A, tq, DH_A)),
            scratch_shapes=[pltpu.VMEM((2, DEC_PAGES) + page, F32),
                            pltpu.VMEM((2, DEC_PAGES) + page, F32),
                            pltpu.SemaphoreType.DMA((2, 2)),
                            pltpu.VMEM((H_A, tq, 1), F32), pltpu.VMEM((H_A, tq, 1), F32),
                            pltpu.VMEM((H_A, tq, DH_A), F32)]),
        out_shape=jax.ShapeDtypeStruct((nb, H_A, tq, DH_A), F32),
        compiler_params=_cparams(("arbitrary",)),
        name="foxdec",
    )(page_table, q, kn, vn, lfn, suf, cache_kt, cache_vt)


def _gla_kernel(q_ref, k_ref, g_ref, v_ref, r_ref, s0_ref, gn_ref, y_ref, sfin_ref, state_ref):
    tb = pl.program_id(1)
    n_tb = pl.num_programs(1)
    c = GLA_CHUNK
    n_chunks = q_ref.shape[0] // c
    n_sub = c // GLA_SUB

    row_k = lax.broadcasted_iota(I32, (WK_B, WV_B), 0) // DK_B
    col_v = lax.broadcasted_iota(I32, (WK_B, WV_B), 1) // DV_B
    state_diag = row_k == col_v

    @pl.when(tb == 0)
    def _():
        state_ref[...] = jnp.zeros_like(state_ref)
        for h in range(H_B):
            state_ref[h * DK_B:(h + 1) * DK_B, h * DV_B:(h + 1) * DV_B] = s0_ref[h]

    r = lax.broadcasted_iota(I32, (c, c), 0)
    s = lax.broadcasted_iota(I32, (c, c), 1)
    same_sub = (r // GLA_SUB) == (s // GLA_SUB)
    cum_ops = jnp.concatenate([
        jnp.where(same_sub & (s <= r), 1.0, 0.0),
        jnp.where(s <= r, 1.0, 0.0),
        jnp.where(same_sub, 1.0, 0.0),
    ], axis=0).astype(F32)

    t_i = lax.broadcasted_iota(I32, (c, H_B * c), 0)
    s_i = lax.broadcasted_iota(I32, (c, H_B * c), 1) % c
    dsub = t_i // GLA_SUB - s_i // GLA_SUB
    intra = (dsub == 0) & (s_i <= t_i)
    head_k = lax.broadcasted_iota(I32, (c, WK_B), 1) // DK_B
    head_v = lax.broadcasted_iota(I32, (c, WV_B), 1) // DV_B

    def stack_heads(x, head_of_lane):
        zero = jnp.zeros_like(x)
        return jnp.concatenate([jnp.where(head_of_lane == h, x, zero) for h in range(H_B)], axis=0)

    def shift_rows(x, n):
        return jnp.concatenate([x[:n], x[:c - n]], axis=0)

    gn = gn_ref[...]
    for ci in range(n_chunks):
        rows = slice(ci * c, (ci + 1) * c)
        q = q_ref[rows, :]
        k = k_ref[rows, :]
        g = g_ref[rows, :]
        v = v_ref[rows, :]

        cums = _dot(cum_ops, g, precision=HIGHEST)
        bl, bc, tot = cums[:c], cums[c:2 * c], cums[2 * c:]
        pre = bc - bl
        q_loc = q * jnp.exp(bl)
        k_loc = k * jnp.exp(-bl)
        k_end = k * jnp.exp(tot - bl)
        q_far = [q_loc] + [q_loc * jnp.exp(pre - shift_rows(pre, GLA_SUB * d)) for d in range(1, n_sub - 1)]

        a0 = _dot_nt(q_loc.astype(BF16), stack_heads(k_loc, head_k).astype(BF16))
        af = _dot_nt(jnp.concatenate(q_far, axis=0).astype(BF16), stack_heads(k_end, head_k).astype(BF16))
        att = jnp.where(intra, a0, 0.0)
        for d in range(1, n_sub):
            att = jnp.where(dsub == d, af[(d - 1) * c:d * c], att)

        q_in = q * jnp.exp(bc)
        o = _dot(att.astype(BF16), stack_heads(v, head_v)) + _dot(q_in.astype(BF16), state_ref[...].astype(BF16))

        k_out = k * jnp.exp(bc[c - 1:c, :] - bc)
        kv = _dot_tn(k_out.astype(BF16), v)
        decay = jnp.exp(_dot_tn(g, jnp.ones((c, LANES), F32), precision=HIGHEST))
        state_ref[...] = (state_ref[...] * jnp.concatenate([decay] * (WV_B // LANES), axis=1)
                          + jnp.where(state_diag, kv, 0.0))

        for h in range(H_B):
            lanes = slice(h * DV_B, (h + 1) * DV_B)
            oh = o[:, lanes]
            yh = oh * lax.rsqrt(jnp.mean(oh * oh, axis=-1, keepdims=True) + EPS) * gn
            y_ref[rows, lanes] = (yh * r_ref[rows, lanes].astype(F32)).astype(y_ref.dtype)

    @pl.when(tb == n_tb - 1)
    def _():
        for h in range(H_B):
            sfin_ref[h] = state_ref[h * DK_B:(h + 1) * DK_B, h * DV_B:(h + 1) * DV_B]


def _gla(q, k, g, v, r, s0, gn, batch, seq, tb_rows):
    n_tb = seq // tb_rows
    row = lambda w: pl.BlockSpec((tb_rows, w), lambda b, t: (b * n_tb + t, 0))
    state = pl.BlockSpec((None, H_B, DK_B, DV_B), lambda b, t: (b, 0, 0, 0))
    return pl.pallas_call(
        _gla_kernel,
        grid=(batch, n_tb),
        in_specs=[row(WK_B), row(WK_B), row(WK_B), row(WV_B), row(WV_B), state,
                  pl.BlockSpec((1, DV_B), lambda b, t: (0, 0))],
        out_specs=[row(WV_B), state],
        out_shape=[jax.ShapeDtypeStruct((batch * seq, WV_B), BF16),
                   jax.ShapeDtypeStruct((batch, H_B, DK_B, DV_B), F32)],
        scratch_shapes=[pltpu.VMEM((WK_B, WV_B), F32)],
        compiler_params=_cparams(("arbitrary", "arbitrary")),
        name="gla",
    )(q, k, g, v, r, s0, gn)


def _mixout_kernel(*refs, n_tail):
    (x_ref, ya_ref, yb_ref, ga_ref, gb_ref, mod_ref, g2_ref, wpa_ref, wpb_ref, wo_ref, rw_ref) = refs[:11]
    tail_refs = refs[11:11 + n_tail]
    x1_ref, hp_ref, lg_ref = refs[11 + n_tail:]
    i = pl.program_id(0)
    n_own = pl.num_programs(0) - (1 if n_tail else 0)

    @pl.when(i < n_own)
    def _():
        br_a = _dot(ya_ref[...], wpa_ref[...])
        br_b = _dot(yb_ref[...], wpb_ref[...])
        merged = ga_ref[...].astype(F32) * br_a + gb_ref[...].astype(F32) * br_b
        out = _dot(merged.astype(BF16), wo_ref[...])
        x1 = x_ref[...] + mod_ref[2] * out
        x1_ref[...] = x1
        y = x1 * lax.rsqrt(jnp.mean(x1 * x1, axis=-1, keepdims=True) + EPS) * g2_ref[...]
        h = y * (1.0 + mod_ref[4]) + mod_ref[3]
        hp_ref[...] = _pack_bf16_pair(h)
        lg_ref[...] = _dot_nt(rw_ref[...], h, precision=HIGHEST)

    if n_tail:
        @pl.when(i == n_own)
        def _():
            for src, dst in zip(tail_refs, (x1_ref, hp_ref, lg_ref)):
                dst[...] = src[...]


def _mixout(x, ya, yb, ga, gb, mod, rows_per_mod, g2, wpa, wpb, wo, rw, tail):
    n = x.shape[0]
    tm = MOE_TILE
    mod_rows = mod.shape[2]
    tiles_per_mod = rows_per_mod // tm
    n_own = n // tm
    n_total = n + (tm if tail else 0)
    own = lambda i: jnp.minimum(i, n_own - 1)
    row = lambda w: pl.BlockSpec((tm, w), lambda i: (own(i), 0))
    return pl.pallas_call(
        functools.partial(_mixout_kernel, n_tail=len(tail)),
        grid=(n_total // tm,),
        in_specs=[row(D_MODEL), row(W_A), row(WV_B), row(D_MODEL), row(D_MODEL),
                  pl.BlockSpec((6, None, mod_rows, D_MODEL), lambda i: (0, own(i) // tiles_per_mod, 0, 0)),
                  _const_spec(g2.shape), _const_spec(wpa.shape), _const_spec(wpb.shape), _const_spec(wo.shape),
                  _const_spec(rw.shape)] + [_const_spec(t.shape) for t in tail],
        out_specs=[pl.BlockSpec((tm, D_MODEL), lambda i: (i, 0)),
                   pl.BlockSpec((tm, D_MODEL // 2), lambda i: (i, 0)),
                   pl.BlockSpec((N_EXPERTS, tm), lambda i: (0, i))],
        out_shape=[jax.ShapeDtypeStruct((n_total, D_MODEL), F32),
                   jax.ShapeDtypeStruct((n_total, D_MODEL // 2), U32),
                   jax.ShapeDtypeStruct((N_EXPERTS, n_total), F32)],
        compiler_params=_cparams(("arbitrary",)),
        name="mixout",
    )(x, ya, yb, ga, gb, mod, g2, wpa, wpb, wo, rw, *tail)


def _route_kernel(lg_ref, bias_ref, idx_ref, w_ref, rank_ref, cnt_ref, carry_ref):
    i = pl.program_id(0)
    tm = lg_ref.shape[1]

    @pl.when(i == 0)
    def _():
        carry_ref[...] = jnp.zeros_like(carry_ref)

    s = _sigmoid(lg_ref[...])
    sb = s + bias_ref[...][:, 0:1]
    ninf = -jnp.inf

    e_in_group = lax.broadcasted_iota(I32, (GROUP_SIZE, tm), 0)
    gscores = []
    for g in range(N_GROUPS):
        blk = sb[g * GROUP_SIZE:(g + 1) * GROUP_SIZE, :]
        m1 = jnp.max(blk, axis=0, keepdims=True)
        first = jnp.min(jnp.where(blk == m1, e_in_group, GROUP_SIZE), axis=0, keepdims=True)
        m2 = jnp.max(jnp.where(e_in_group == first, ninf, blk), axis=0, keepdims=True)
        gscores.append(m1 + m2)
    gs = jnp.concatenate(gscores, axis=0)

    g_iota = lax.broadcasted_iota(I32, (N_GROUPS, tm), 0)
    chosen = jnp.zeros((N_GROUPS, tm), F32)
    for _ in range(TOPK_GROUPS):
        m = jnp.max(gs, axis=0, keepdims=True)
        first = jnp.min(jnp.where(gs == m, g_iota, N_GROUPS), axis=0, keepdims=True)
        pick = g_iota == first
        chosen = jnp.where(pick, 1.0, chosen)
        gs = jnp.where(pick, ninf, gs)
    allowed = jnp.concatenate(
        [jnp.broadcast_to(chosen[g:g + 1, :], (GROUP_SIZE, tm)) for g in range(N_GROUPS)], axis=0) > 0.5
    cand = jnp.where(allowed, sb, ninf)

    e_iota = lax.broadcasted_iota(I32, (N_EXPERTS, tm), 0)
    onehot = jnp.zeros((N_EXPERTS, tm), F32)
    idxs, ws = [], []
    for _ in range(TOP_K):
        m = jnp.max(cand, axis=0, keepdims=True)
        first = jnp.min(jnp.where(cand == m, e_iota, N_EXPERTS), axis=0, keepdims=True)
        pick = e_iota == first
        idxs.append(first)
        ws.append(jnp.sum(jnp.where(pick, s, 0.0), axis=0, keepdims=True))
        onehot = jnp.where(pick, 1.0, onehot)
        cand = jnp.where(pick, ninf, cand)
    w = jnp.concatenate(ws, axis=0)
    w_ref[...] = w / jnp.sum(w, axis=0, keepdims=True) * ROUTE_SCALE
    idx_ref[...] = jnp.concatenate(idxs, axis=0)

    r = lax.broadcasted_iota(I32, (tm, tm), 0)
    c = lax.broadcasted_iota(I32, (tm, tm), 1)
    earlier = jnp.where(r < c, 1.0, 0.0).astype(BF16)
    carry = carry_ref[...]
    before = _dot(onehot.astype(BF16), earlier) + jnp.concatenate([carry] * (tm // LANES), axis=1)
    rank_ref[...] = jnp.concatenate(
        [jnp.sum(jnp.where(e_iota == ix, before, 0.0), axis=0, keepdims=True) for ix in idxs], axis=0).astype(I32)
    carry = carry + _dot(onehot.astype(BF16), jnp.ones((tm, LANES), BF16))
    carry_ref[...] = carry
    cnt_ref[...] = carry


def _route(logits_t, bias):
    n = logits_t.shape[1]
    tm = MOE_TILE
    tok = lambda: pl.BlockSpec((TOP_K, tm), lambda i: (0, i))
    return pl.pallas_call(
        _route_kernel,
        grid=(n // tm,),
        in_specs=[pl.BlockSpec((N_EXPERTS, tm), lambda i: (0, i)), _const_spec(bias.shape)],
        out_specs=[tok(), tok(), tok(), _const_spec((N_EXPERTS, LANES))],
        out_shape=[jax.ShapeDtypeStruct((TOP_K, n), I32), jax.ShapeDtypeStruct((TOP_K, n), F32),
                   jax.ShapeDtypeStruct((TOP_K, n), I32), jax.ShapeDtypeStruct((N_EXPERTS, LANES), F32)],
        scratch_shapes=[pltpu.VMEM((N_EXPERTS, LANES), F32)],
        compiler_params=_cparams(("arbitrary",)),
        name="route",
    )(logits_t, bias)


def _dest_kernel(idx_ref, rank_ref, cnt_ref, dest_ref, start_ref):
    i = pl.program_id(0)
    tm = idx_ref.shape[1]

    @pl.when(i == 0)
    def _():
        cnt = cnt_ref[...]
        padded = jnp.floor((cnt + (EXPERT_TILE - 1)) * (1.0 / EXPERT_TILE)) * EXPERT_TILE
        r = lax.broadcasted_iota(I32, (N_EXPERTS, N_EXPERTS), 0)
        c = lax.broadcasted_iota(I32, (N_EXPERTS, N_EXPERTS), 1)
        start_ref[...] = _dot(jnp.where(c < r, 1.0, 0.0).astype(F32), padded, precision=HIGHEST)

    start = jnp.concatenate([start_ref[...]] * (tm // LANES), axis=1)
    e_iota = lax.broadcasted_iota(I32, (N_EXPERTS, tm), 0)
    idx = idx_ref[...]
    first = jnp.concatenate(
        [jnp.sum(jnp.where(e_iota == idx[kk:kk + 1, :], start, 0.0), axis=0, keepdims=True) for kk in range(TOP_K)],
        axis=0)
    dest_ref[...] = first.astype(I32) + rank_ref[...]


def _dest(idx_t, rank_t, counts):
    n = idx_t.shape[1]
    tm = MOE_TILE
    tok = lambda: pl.BlockSpec((TOP_K, tm), lambda i: (0, i))
    return pl.pallas_call(
        _dest_kernel,
        grid=(n // tm,),
        in_specs=[tok(), tok(), _const_spec(counts.shape)],
        out_specs=pl.BlockSpec((None, TOP_K, tm), lambda i: (i, 0, 0)),
        out_shape=jax.ShapeDtypeStruct((n // tm, TOP_K, tm), I32),
        scratch_shapes=[pltpu.VMEM((N_EXPERTS, LANES), F32)],
        compiler_params=_cparams(("arbitrary",)),
        name="dest",
    )(idx_t, rank_t, counts)


def _sc_worker():
    return lax.axis_index("s") * SC_CORES + lax.axis_index("c")


def _sc_mesh():
    return plsc.VectorSubcoreMesh(core_axis_name="c", subcore_axis_name="s")


def _dispatch(dest_tiles, hp, n_rows_out):
    n_tok_tiles = dest_tiles.shape[0]
    w = hp.shape[1]
    iters = -(-n_tok_tiles // SC_WORKERS)

    def body(hp_hbm, dest_hbm, xs_hbm, idx_v, rows_v, sem):
        wid = _sc_worker()

        @pl.loop(0, iters)
        def _(j):
            tile = wid + SC_WORKERS * j

            @pl.when(tile < n_tok_tiles)
            def _():
                pltpu.sync_copy(dest_hbm.at[tile], idx_v)
                pltpu.sync_copy(hp_hbm.at[pl.ds(tile * MOE_TILE, MOE_TILE)], rows_v)
                copies = [pltpu.async_copy(rows_v, xs_hbm.at[idx_v.at[kk]], sem) for kk in range(TOP_K)]
                for cp in copies:
                    cp.wait()

    return pl.kernel(
        body,
        out_type=jax.ShapeDtypeStruct((n_rows_out, w), U32),
        mesh=_sc_mesh(),
        scratch_types=[pltpu.VMEM((TOP_K, MOE_TILE), I32), pltpu.VMEM((MOE_TILE, w), U32), pltpu.SemaphoreType.DMA],
        name="dispatch",
    )(hp, dest_tiles)


def _gather_back(ys, dest_tiles):
    n_tok_tiles = dest_tiles.shape[0]
    w = ys.shape[1]
    n_lists = n_tok_tiles * TOP_K
    iters = -(-n_lists // SC_WORKERS)

    def body(ys_hbm, dest_hbm, out_hbm, idx_v, rows_v, sem):
        wid = _sc_worker()

        @pl.loop(0, iters)
        def _(j):
            lst = wid + SC_WORKERS * j

            @pl.when(lst < n_lists)
            def _():
                pltpu.sync_copy(dest_hbm.at[pl.ds(lst * MOE_TILE, MOE_TILE)], idx_v)
                pltpu.async_copy(ys_hbm.at[idx_v], rows_v, sem).wait()
                pltpu.sync_copy(rows_v, out_hbm.at[pl.ds(lst * MOE_TILE, MOE_TILE)])

    out = pl.kernel(
        body,
        out_type=jax.ShapeDtypeStruct((n_lists * MOE_TILE, w), U32),
        mesh=_sc_mesh(),
        scratch_types=[pltpu.VMEM((MOE_TILE,), I32), pltpu.VMEM((MOE_TILE, w), U32), pltpu.SemaphoreType.DMA],
        name="gather_back",
    )(ys, dest_tiles.reshape(-1))
    return out.reshape(n_tok_tiles, TOP_K, MOE_TILE, w)


def _experts_kernel(first_ref, count_ref, rows_ref, xs_hbm, wg_ref, wu_ref, wd_ref, ys_hbm,
                    xbuf, ybuf, wgb_ref, wub_ref, wdb_ref, sem):
    e = pl.program_id(0)
    tm = EXPERT_TILE
    n_e = count_ref[e]
    total = first_ref[N_EXPERTS - 1] + count_ref[N_EXPERTS - 1]

    def x_copy(t, slot):
        return pltpu.make_async_copy(xs_hbm.at[pl.ds(t * tm, tm), :], xbuf.at[slot], sem.at[0, slot])

    def y_copy(t, slot):
        return pltpu.make_async_copy(ybuf.at[slot], ys_hbm.at[pl.ds(t * tm, tm), :], sem.at[1, slot])

    @pl.when((e == 0) & (total > 0))
    def _():
        x_copy(0, 0).start()

    @pl.when(n_e > 0)
    def _():
        wgb_ref[...] = wg_ref[...].astype(BF16)
        wub_ref[...] = wu_ref[...].astype(BF16)
        wdb_ref[...] = wd_ref[...].astype(BF16)
        half = D_MODEL // 2

        def body(j, carry):
            t = first_ref[e] + j
            slot = t % 2
            x_copy(t, slot).wait()

            @pl.when(t + 1 < total)
            def _():
                x_copy(t + 1, 1 - slot).start()

            @pl.when(t >= 2)
            def _():
                y_copy(t - 2, slot).wait()

            xs = xbuf[slot]
            filled = lax.broadcasted_iota(I32, xs.shape, 0) < rows_ref[t]
            lo, hi = _unpack_bf16_pair(jnp.where(filled, xs, jnp.zeros_like(xs)))
            gate = _dot(lo, wgb_ref[:half, :]) + _dot(hi, wgb_ref[half:, :])
            up = _dot(lo, wub_ref[:half, :]) + _dot(hi, wub_ref[half:, :])
            act = (_silu(gate) * up).astype(BF16)
            ybuf[slot] = _pack_bf16_pair(_dot(act, wdb_ref[...]))
            y_copy(t, slot).start()
            return carry

        lax.fori_loop(0, n_e, body, 0)

    @pl.when(e == N_EXPERTS - 1)
    def _():
        for back in (2, 1):
            @pl.when(total >= back)
            def _():
                y_copy(total - back, (total - back) % 2).wait()


def _experts(tile_first, tile_count, tile_rows, xs, wg, wu, wd):
    n_rows, w = xs.shape
    tm = EXPERT_TILE
    expert = lambda e, tf, tc, tr: (e, 0, 0)
    return pl.pallas_call(
        _experts_kernel,
        grid_spec=pltpu.PrefetchScalarGridSpec(
            num_scalar_prefetch=3,
            grid=(N_EXPERTS,),
            in_specs=[pl.BlockSpec(memory_space=pl.ANY),
                      pl.BlockSpec((None, D_MODEL, D_EXPERT), expert),
                      pl.BlockSpec((None, D_MODEL, D_EXPERT), expert),
                      pl.BlockSpec((None, D_EXPERT, D_MODEL), expert)],
            out_specs=pl.BlockSpec(memory_space=pl.ANY),
            scratch_shapes=[pltpu.VMEM((2, tm, w), U32), pltpu.VMEM((2, tm, w), U32),
                            pltpu.VMEM((D_MODEL, D_EXPERT), BF16), pltpu.VMEM((D_MODEL, D_EXPERT), BF16),
                            pltpu.VMEM((D_EXPERT, D_MODEL), BF16),
                            pltpu.SemaphoreType.DMA((2, 2))]),
        out_shape=jax.ShapeDtypeStruct((n_rows, w), U32),
        compiler_params=_cparams(("arbitrary",)),
        name="experts",
    )(tile_first, tile_count, tile_rows, xs, wg, wu, wd)


def _combine_kernel(yt_ref, x1_ref, hp_ref, w_ref, mod_ref, fg_ref, sg_ref, su_ref, sd_ref, o_ref):
    half = D_MODEL // 2
    lo, hi = _unpack_bf16_pair(hp_ref[...])
    sg = sg_ref[...]
    su = su_ref[...]
    gate = _dot(lo, sg[:half]) + _dot(hi, sg[half:])
    up = _dot(lo, su[:half]) + _dot(hi, su[half:])
    y = _dot((_silu(gate) * up).astype(BF16), sd_ref[...])

    w = w_ref[...]
    for kk in range(TOP_K):
        ylo, yhi = _unpack_bf16_pair(yt_ref[kk])
        y = y + w[:, kk:kk + 1] * jnp.concatenate([ylo.astype(F32), yhi.astype(F32)], axis=1)
    x2 = x1_ref[...] + mod_ref[5] * y
    o_ref[...] = x2 * lax.rsqrt(jnp.mean(x2 * x2, axis=-1, keepdims=True) + EPS) * fg_ref[...]


def _combine(y_tok, x1, hp, w, mod, rows_per_mod, fg, sg, su, sd, n, row_offset):
    tm = MOE_TILE
    mod_rows = mod.shape[2]
    tiles_per_mod = rows_per_mod // tm
    off = row_offset // tm
    return pl.pallas_call(
        _combine_kernel,
        grid=(n // tm,),
        in_specs=[pl.BlockSpec((None, TOP_K, tm, D_MODEL // 2), lambda i: (i + off, 0, 0, 0)),
                  pl.BlockSpec((tm, D_MODEL), lambda i: (i + off, 0)),
                  pl.BlockSpec((tm, D_MODEL // 2), lambda i: (i + off, 0)),
                  pl.BlockSpec((tm, TOP_K), lambda i: (i + off, 0)),
                  pl.BlockSpec((6, None, mod_rows, D_MODEL), lambda i: (0, i // tiles_per_mod, 0, 0)),
                  _const_spec(fg.shape), _const_spec(sg.shape), _const_spec(su.shape), _const_spec(sd.shape)],
        out_specs=pl.BlockSpec((tm, D_MODEL), lambda i: (i, 0)),
        out_shape=jax.ShapeDtypeStruct((n, D_MODEL), F32),
        compiler_params=_cparams(("arbitrary",)),
        name="combine",
    )(y_tok, x1, hp, w, mod, fg, sg, su, sd)


def _split_w_in(w_in):
    sizes = (W_A, W_A, W_A, H_A, WK_B, WK_B, WV_B, WV_B, GK_RANK, D_MODEL, D_MODEL)
    segs, o = [], 0
    for s in sizes:
        segs.append(w_in[:, o:o + s])
        o += s
    return segs


def kernel(x_prompt, x_sample, cache_k, cache_v, cache_logf, state_gla, page_table, c_prompt, c_sample, w_ada, b_ada,
           norm1_g, w_in, b_f, w_gk2, b_gk, gla_norm_g, w_pa, w_pb, w_o, norm2_g, router_w, router_bias, w_gate, w_up,
           w_down, ws_gate, ws_up, ws_down, final_g):
    assert w_ada.shape[0] == 1, "single layer"
    bp, tp, d = x_prompt.shape
    bs, ts, _ = x_sample.shape
    n_p, n_s = bp * tp, bs * ts
    n_tot = n_p + n_s
    assert n_s == MOE_TILE and n_p % MOE_TILE == 0 and tp % FOX_TILE == 0

    q_a, k_a, v_a, f_a, q_b, k_b, v_b, r_b, gk1, gate_a, gate_b = _split_w_in(w_in[0])
    wts = (jnp.concatenate([q_a, k_a], axis=1).astype(BF16),
           v_a.astype(BF16),
           jnp.concatenate([k_a, v_a], axis=1).T.astype(BF16),
           jnp.pad(f_a, ((0, 0), (0, LANES - H_A))).astype(BF16),
           jnp.pad(b_f[0].reshape(1, H_A), ((0, 0), (0, LANES - H_A))),
           jnp.concatenate([q_b, k_b, v_b, r_b], axis=1).astype(BF16),
           jnp.pad(gk1, ((0, 0), (0, LANES - GK_RANK))).astype(BF16),
           jnp.pad(w_gk2[0], ((0, LANES - GK_RANK), (0, 0))).astype(BF16),
           b_gk[0].reshape(1, WK_B),
           jnp.concatenate([gate_a, gate_b], axis=1).astype(BF16))
    g1 = norm1_g[0].reshape(1, d)
    g2 = norm2_g[0].reshape(1, d)
    gn = gla_norm_g[0].reshape(1, DV_B)
    fg = final_g.reshape(1, d)
    wpa, wpb, wo = w_pa[0].astype(BF16), w_pb[0].astype(BF16), w_o[0].astype(BF16)
    rw_t = router_w[0].T
    rbias = jnp.broadcast_to(router_bias[0].reshape(N_EXPERTS, 1), (N_EXPERTS, LANES))
    sg, su, sd = ws_gate[0].astype(BF16), ws_up[0].astype(BF16), ws_down[0].astype(BF16)

    ada = _ada(jnp.concatenate([c_prompt, c_sample], axis=0), w_ada[0], b_ada[0].reshape(1, -1))
    ada = ada.reshape(bp + bs, 6, d)
    mod_p = ada[:bp].transpose(1, 0, 2)[:, :, None, :]
    mod_s = jnp.repeat(ada[bp:], ts, axis=0).transpose(1, 0, 2)[:, None, :, :]

    xp = x_prompt.reshape(n_p, d)
    tm_p = 256
    (q, kh, vth, kt, vt, lf, fc, qb, kb, gk, vb, rb, ga_p, gb_p) = _inproj(
        xp, mod_p, tp, tp // tm_p, tm_p, g1, wts, True)
    ya_p = _fox(q, kh, vth, fc, bp, tp)
    yb_p, s_p = _gla(qb, kb, gk, vb, rb, jnp.zeros((bp, H_B, DK_B, DV_B), F32), gn, bp, tp, 256)
    k_prompt = kt.reshape(bp, H_A, DH_A, tp).transpose(0, 3, 1, 2)[None]
    v_prompt = vt.reshape(bp, H_A, DH_A, tp).transpose(0, 3, 1, 2)[None]
    logf_prompt = lf.reshape(1, bp, tp, H_A)

    xs_in = x_sample.reshape(n_s, d)
    (q, k, v, lf, qb, kb, gk, vb, rb, ga, gb) = _inproj(xs_in, mod_s, n_s, 1, n_s, g1, wts, False)
    page_rows = cache_k.shape[2]
    tq = 8
    cache_kt = jnp.transpose(cache_k, (0, 1, 3, 4, 2))
    cache_vt = jnp.transpose(cache_v, (0, 1, 3, 4, 2))
    suf = _lfsuf(page_table, jnp.transpose(cache_logf, (0, 1, 3, 2)))

    def new_kv(a):
        a = a.reshape(bs, ts, H_A, DH_A).transpose(0, 2, 3, 1)
        return jnp.pad(a, ((0, 0), (0, 0), (0, 0), (0, page_rows - ts)))

    q_dec = jnp.pad(q.astype(F32).reshape(bs, ts, H_A, DH_A).transpose(0, 2, 1, 3),
                    ((0, 0), (0, 0), (0, tq - ts), (0, 0)))
    lfn = jnp.pad(lf.reshape(bs, ts, H_A).transpose(0, 2, 1), ((0, 0), (0, 0), (0, page_rows - ts)))
    ya = _foxdec(page_table, q_dec, new_kv(k), new_kv(v), lfn, suf, cache_kt, cache_vt)
    ya = ya[:, :, :ts].transpose(0, 2, 1, 3).reshape(n_s, W_A).astype(BF16)
    pad = lambda a: jnp.pad(a.reshape(bs, ts, -1), ((0, 0), (0, GLA_CHUNK - ts), (0, 0))).reshape(bs * GLA_CHUNK, -1)
    yb, s_s = _gla(pad(qb), pad(kb), pad(gk), pad(vb), pad(rb), state_gla[0], gn, bs, GLA_CHUNK, GLA_CHUNK)
    yb = yb.reshape(bs, GLA_CHUNK, WV_B)[:, :ts].reshape(n_s, WV_B)
    k_sample = k.reshape(1, bs, ts, H_A, DH_A)
    v_sample = v.reshape(1, bs, ts, H_A, DH_A)
    logf_sample = lf.reshape(1, bs, ts, H_A)
    tail = _mixout(xs_in, ya, yb, ga, gb, mod_s, n_s, g2, wpa, wpb, wo, rw_t, ())

    x1, hp, logits_t = _mixout(xp, ya_p, yb_p, ga_p, gb_p, mod_p, tp, g2, wpa, wpb, wo, rw_t, tuple(tail))
    idx_t, w_t, rank_t, counts = _route(logits_t, rbias)
    dest_tiles = _dest(idx_t, rank_t, counts)
    padded = (counts[:, 0].astype(I32) + EXPERT_TILE - 1) // EXPERT_TILE * EXPERT_TILE
    pad_end = jnp.cumsum(padded)
    n_tiles = -(-n_tot * TOP_K // EXPERT_TILE) + N_EXPERTS
    tile_first_row = jnp.arange(n_tiles, dtype=I32) * EXPERT_TILE
    tile_expert = jnp.minimum(jnp.sum((pad_end[None, :] <= tile_first_row[:, None]).astype(I32), axis=1),
                              N_EXPERTS - 1)
    used_end = pad_end - padded + counts[:, 0].astype(I32)
    tile_rows = jnp.clip(used_end[tile_expert] - tile_first_row, 0, EXPERT_TILE).astype(I32)
    tile_first = ((pad_end - padded) // EXPERT_TILE).astype(I32)
    tile_count = (padded // EXPERT_TILE).astype(I32)

    xs_sorted = _dispatch(dest_tiles, hp, n_tiles * EXPERT_TILE)
    ys_sorted = _experts(tile_first, tile_count, tile_rows, xs_sorted, w_gate[0], w_up[0], w_down[0])
    y_tok = _gather_back(ys_sorted, dest_tiles)
    w_tok = w_t.T
    y_prompt = _combine(y_tok, x1, hp, w_tok, mod_p, tp, fg, sg, su, sd, n_p, 0)
    y_sample = _combine(y_tok, x1, hp, w_tok, mod_s, n_s, fg, sg, su, sd, n_s, n_p)

    return (y_prompt.reshape(bp, tp, d), y_sample.reshape(bs, ts, d),
            k_prompt, v_prompt, logf_prompt, s_p[None],
            k_sample, v_sample, logf_sample, s_s[None])
```

```python
import functools

import jax
import jax.numpy as jnp
from jax import lax
from jax.experimental import pallas as pl
from jax.experimental.pallas import tpu as pltpu
from jax.experimental.pallas import tpu_sc as plsc

F32 = jnp.float32
BF16 = jnp.bfloat16
U32 = jnp.uint32
I32 = jnp.int32

EPS = 1e-6
D_MODEL = 1024
H_A, DH_A, W_A = 8, 64, 512
H_B, DK_B, DV_B, WK_B, WV_B = 4, 64, 128, 256, 512
GK_RANK = 16
GATE_NORM = 16.0
N_EXPERTS, TOP_K, N_GROUPS, TOPK_GROUPS = 256, 8, 8, 4
GROUP_SIZE = N_EXPERTS // N_GROUPS
D_EXPERT = 256
ROUTE_SCALE = 2.5
MOE_TILE = 128
EXPERT_TILE = 256
EXPERT_RING = 4
LANES = 128
SC_CORES = 2
SC_WORKERS = 32
GLA_CHUNK = 64
GLA_SUB = 16
FOX_TILE = 512
FOX_SUB = 256
DEC_PAGES = 8
NEG = -1e30
LOG2E = 1.4426950408889634
VMEM_LIMIT = 56 * 1024 * 1024
HIGHEST = lax.Precision.HIGHEST

NT_DIMS = (((1,), (1,)), ((), ()))
TN_DIMS = (((0,), (0,)), ((), ()))


def _cparams(sem):
    return pltpu.CompilerParams(dimension_semantics=sem, vmem_limit_bytes=VMEM_LIMIT)


def _dot(a, b, **kw):
    return jnp.dot(a, b, preferred_element_type=F32, **kw)


def _dot_nt(a, b, **kw):
    return lax.dot_general(a, b, NT_DIMS, preferred_element_type=F32, **kw)


def _dot_tn(a, b, **kw):
    return lax.dot_general(a, b, TN_DIMS, preferred_element_type=F32, **kw)


def _sigmoid(x):
    return 1.0 / (1.0 + jnp.exp(-x))


def _silu(x):
    return x * _sigmoid(x)


def _log_sigmoid(x):
    return jnp.minimum(x, 0.0) - jnp.log(1.0 + jnp.exp(-jnp.abs(x)))


def _pack_bf16_pair(x):
    c = x.shape[1] // 2
    bits = pltpu.bitcast(x.astype(BF16).astype(F32), U32)
    return (bits[:, c:] & jnp.uint32(0xFFFF0000)) | (bits[:, :c] >> 16)


def _unpack_bf16_pair(p):
    lo = pltpu.bitcast(p << 16, F32).astype(BF16)
    hi = pltpu.bitcast(p & jnp.uint32(0xFFFF0000), F32).astype(BF16)
    return lo, hi


def _ada_kernel(c_ref, w_ref, b_ref, o_ref):
    c = c_ref[...]
    o_ref[...] = _dot(_silu(c).astype(BF16), w_ref[...].astype(BF16)) + b_ref[...]


def _ada(c, w, b):
    n, d = c.shape
    cols = w.shape[1]
    tn = 1536
    return pl.pallas_call(
        _ada_kernel,
        grid=(cols // tn,),
        in_specs=[pl.BlockSpec((n, d), lambda j: (0, 0)),
                  pl.BlockSpec((d, tn), lambda j: (0, j)),
                  pl.BlockSpec((1, tn), lambda j: (0, j))],
        out_specs=pl.BlockSpec((n, tn), lambda j: (0, j)),
        out_shape=jax.ShapeDtypeStruct((n, cols), F32),
        compiler_params=_cparams(("arbitrary",)),
        name="ada",
    )(c, w, b)


def _inproj_kernel(*refs, seq_tiles, prompt):
    (x_ref, mod_ref, g_ref, wqk_ref, wv_ref, wkvt_ref, wf_ref, bf_ref, wb_ref, wgk1_ref, wgk2_ref, bgk_ref,
     wgate_ref) = refs[:13]
    if prompt:
        (q_ref, kh_ref, vth_ref, kt_ref, vt_ref, lf_ref, fc_ref, qb_ref, kb_ref, gk_ref, vb_ref, rb_ref, ga_ref,
         gb_ref, carry_ref) = refs[13:]
    else:
        q_ref, k_ref, v_ref, lf_ref, qb_ref, kb_ref, gk_ref, vb_ref, rb_ref, ga_ref, gb_ref = refs[13:]
    i = pl.program_id(0)
    tm = x_ref.shape[0]
    x = x_ref[...]
    y = x * lax.rsqrt(jnp.mean(x * x, axis=-1, keepdims=True) + EPS) * g_ref[...]
    h = (y * (1.0 + mod_ref[1]) + mod_ref[0]).astype(BF16)

    z = _dot(h, wqk_ref[...])
    q_ref[...] = (z[:, :W_A] * (DH_A ** -0.5 * (LOG2E if prompt else 1.0))).astype(BF16)
    k = z[:, W_A:]

    lf = _log_sigmoid(_dot(h, wf_ref[...]) + bf_ref[...])
    lf_ref[...] = lf[:, :H_A]

    if prompt:
        kvt = _dot_nt(wkvt_ref[...], h)
        kt_ref[...] = kvt[:W_A]
        vt_ref[...] = kvt[W_A:]
        vth_ref[...] = kvt[W_A:].astype(BF16)
        kh_ref[...] = k.astype(BF16)

        @pl.when(i % seq_tiles == 0)
        def _():
            carry_ref[...] = jnp.zeros_like(carry_ref)

        r = lax.broadcasted_iota(I32, (tm, tm), 0)
        c = lax.broadcasted_iota(I32, (tm, tm), 1)
        lower = jnp.where(c <= r, 1.0, 0.0).astype(F32)
        carry = carry_ref[...]
        fc = _dot(lower, lf, precision=HIGHEST) + carry[0:1, :]
        fc_ref[...] = fc[:, :H_A] * LOG2E
        carry_ref[...] = carry + _dot(jnp.ones((8, tm), F32), lf, precision=HIGHEST)
    else:
        k_ref[...] = k
        v_ref[...] = _dot(h, wv_ref[...])

    z = _dot(h, wb_ref[...])
    qb_ref[...] = z[:, :WK_B] * DK_B ** -0.5
    kb_ref[...] = z[:, WK_B:2 * WK_B]
    vb_ref[...] = z[:, 2 * WK_B:2 * WK_B + WV_B].astype(BF16)
    rb_ref[...] = _silu(z[:, 2 * WK_B + WV_B:]).astype(BF16)

    lr = _dot(h, wgk1_ref[...]).astype(BF16)
    gk_ref[...] = _log_sigmoid(_dot(lr, wgk2_ref[...]) + bgk_ref[...]) * (1.0 / GATE_NORM)

    z = _dot(h, wgate_ref[...])
    ga_ref[...] = _sigmoid(z[:, :D_MODEL]).astype(BF16)
    gb_ref[...] = _sigmoid(z[:, D_MODEL:]).astype(BF16)


def _const_spec(shape):
    return pl.BlockSpec(shape, lambda i: (0,) * len(shape))


def _inproj(x, mod, rows_per_mod, seq_tiles, tm, g, wts, prompt):
    n = x.shape[0]
    mod_rows = mod.shape[2]
    tiles_per_mod = rows_per_mod // tm
    row = lambda w: pl.BlockSpec((tm, w), lambda i: (i, 0))
    if prompt:
        n_seq = n // (seq_tiles * tm)
        seq_t = pl.BlockSpec((None, W_A, tm), lambda i: (i // seq_tiles, 0, i % seq_tiles))
        kv_shape = (n_seq, W_A, seq_tiles * tm)
        outs = [
            (row(W_A), (n, W_A), BF16),
            (row(W_A), (n, W_A), BF16),
            (pl.BlockSpec((W_A, tm), lambda i: (0, i)), (W_A, n), BF16),
            (seq_t, kv_shape, F32),
            (seq_t, kv_shape, F32),
            (row(H_A), (n, H_A), F32),
            (row(H_A), (n, H_A), F32),
        ]
    else:
        outs = [
            (row(W_A), (n, W_A), BF16),
            (row(W_A), (n, W_A), F32),
            (row(W_A), (n, W_A), F32),
            (row(H_A), (n, H_A), F32),
        ]
    outs += [
        (row(WK_B), (n, WK_B), F32),
        (row(WK_B), (n, WK_B), F32),
        (row(WK_B), (n, WK_B), F32),
        (row(WV_B), (n, WV_B), BF16),
        (row(WV_B), (n, WV_B), BF16),
        (row(D_MODEL), (n, D_MODEL), BF16),
        (row(D_MODEL), (n, D_MODEL), BF16),
    ]
    return pl.pallas_call(
        functools.partial(_inproj_kernel, seq_tiles=seq_tiles, prompt=prompt),
        grid=(n // tm,),
        in_specs=[row(D_MODEL),
                  pl.BlockSpec((6, None, mod_rows, D_MODEL), lambda i: (0, i // tiles_per_mod, 0, 0)),
                  _const_spec(g.shape)] + [_const_spec(w.shape) for w in wts],
        out_specs=[o[0] for o in outs],
        out_shape=[jax.ShapeDtypeStruct(o[1], o[2]) for o in outs],
        scratch_shapes=[pltpu.VMEM((8, LANES), F32)] if prompt else [],
        compiler_params=_cparams(("arbitrary",)),
        name="inproj",
    )(x, mod, g, *wts)


def _fox_kernel(q_ref, k_ref, vt_ref, fc_ref, o_ref, qs_ref, m_ref, l_ref, acc_ref):
    qi = pl.program_id(1)
    ki = pl.program_id(2)
    t = q_ref.shape[0]
    pairs = H_A // 2
    sub = FOX_SUB

    @pl.when(ki == 0)
    def _():
        lane = lax.broadcasted_iota(I32, (t, LANES), 1)
        for hp in range(pairs):
            q = q_ref[:, hp * LANES:(hp + 1) * LANES]
            qs_ref[hp, :t, :] = jnp.where(lane < DH_A, q, jnp.zeros_like(q))
            qs_ref[hp, t:, :] = jnp.where(lane >= DH_A, q, jnp.zeros_like(q))
        m_ref[...] = jnp.full_like(m_ref, NEG)
        l_ref[...] = jnp.zeros_like(l_ref)
        acc_ref[...] = jnp.zeros_like(acc_ref)

    def step(diagonal):
        for hp in range(pairs):
            k = k_ref[:, hp * LANES:(hp + 1) * LANES]
            vt = vt_ref[hp * LANES:(hp + 1) * LANES, :]
            for h2 in range(2):
                head = 2 * hp + h2
                fb = jnp.broadcast_to(fc_ref[:, head:head + 1], (t, sub))
                for sb in range(t // sub):
                    cols = slice(h2 * t + sb * sub, h2 * t + (sb + 1) * sub)
                    s = _dot_nt(k, qs_ref[hp, cols, :]) - fb
                    if diagonal:
                        key = lax.broadcasted_iota(I32, (t, sub), 0)
                        qry = lax.broadcasted_iota(I32, (t, sub), 1) + sb * sub
                        s = jnp.where(key <= qry, s, NEG)
                    m_prev = m_ref[hp, :, cols]
                    m_new = jnp.maximum(m_prev, jnp.max(s, axis=0, keepdims=True))
                    alpha = jnp.exp2(m_prev - m_new)
                    p = jnp.exp2(s - m_new)
                    l_ref[hp, :, cols] = alpha * l_ref[hp, :, cols] + jnp.sum(p, axis=0, keepdims=True)
                    acc_ref[hp, :, cols] = alpha * acc_ref[hp, :, cols] + _dot(vt, p.astype(BF16))
                    m_ref[hp, :, cols] = m_new

    @pl.when(ki < qi)
    def _():
        step(False)

    @pl.when(ki == qi)
    def _():
        step(True)
        for hp in range(pairs):
            o = acc_ref[hp] / l_ref[hp]
            o = jnp.concatenate([o[:DH_A, :t], o[DH_A:, t:]], axis=0)
            o_ref[:, hp * LANES:(hp + 1) * LANES] = o.T.astype(o_ref.dtype)


def _fox(q, k, vt, fc, batch, seq):
    t = FOX_TILE
    nq = seq // t
    kv_map = lambda b, qi, ki: (b * nq + jnp.minimum(ki, qi), 0)
    return pl.pallas_call(
        _fox_kernel,
        grid=(batch, nq, nq),
        in_specs=[pl.BlockSpec((t, W_A), lambda b, qi, ki: (b * nq + qi, 0)),
                  pl.BlockSpec((t, W_A), kv_map),
                  pl.BlockSpec((W_A, t), lambda b, qi, ki: (0, b * nq + jnp.minimum(ki, qi))),
                  pl.BlockSpec((t, H_A), kv_map)],
        out_specs=pl.BlockSpec((t, W_A), lambda b, qi, ki: (b * nq + qi, 0)),
        out_shape=jax.ShapeDtypeStruct((batch * seq, W_A), BF16),
        scratch_shapes=[pltpu.VMEM((H_A // 2, 2 * t, LANES), BF16),
                        pltpu.VMEM((H_A // 2, 1, 2 * t), F32),
                        pltpu.VMEM((H_A // 2, 1, 2 * t), F32),
                        pltpu.VMEM((H_A // 2, LANES, 2 * t), F32)],
        compiler_params=_cparams(("arbitrary", "arbitrary", "arbitrary")),
        name="fox",
    )(q, k, vt, fc)


def _lfsuf_kernel(pt_ref, lf_hbm, o_ref, buf_ref, sem):
    b = pl.program_id(0)
    n_pages, _, w = buf_ref.shape

    def page_copy(j):
        return pltpu.make_async_copy(lf_hbm.at[0, pt_ref[b, j]], buf_ref.at[j], sem)

    def issue(j, carry):
        page_copy(j).start()
        return carry

    def drain(j, carry):
        page_copy(j).wait()
        return carry

    lax.fori_loop(0, n_pages, issue, 0)
    lax.fori_loop(0, n_pages, drain, 0)

    x = buf_ref[...]
    lane = lax.broadcasted_iota(I32, x.shape, 2)
    s = x
    shift = 1
    while shift < w:
        s = s + jnp.where(lane + shift < w, pltpu.roll(s, w - shift, axis=2), 0.0)
        shift *= 2
    within = s - x
    after = jnp.zeros((H_A, w), F32)
    for j in range(n_pages - 1, -1, -1):
        o_ref[j] = within[j] + after
        after = after + jnp.broadcast_to(s[j][:, 0:1], (H_A, w))


def _lfsuf(page_table, lf_pages):
    nb, n_pages = page_table.shape
    w = lf_pages.shape[3]
    return pl.pallas_call(
        _lfsuf_kernel,
        grid_spec=pltpu.PrefetchScalarGridSpec(
            num_scalar_prefetch=1,
            grid=(nb,),
            in_specs=[pl.BlockSpec(memory_space=pl.ANY)],
            out_specs=pl.BlockSpec((None, n_pages, H_A, w), lambda b, pt: (b, 0, 0, 0)),
            scratch_shapes=[pltpu.VMEM((n_pages, H_A, w), F32), pltpu.SemaphoreType.DMA(())]),
        out_shape=jax.ShapeDtypeStruct((nb, n_pages, H_A, w), F32),
        compiler_params=_cparams(("arbitrary",)),
        name="lfsuf",
    )(page_table, lf_pages)


def _foxdec_kernel(pt_ref, q_ref, kn_ref, vn_ref, lfn_ref, suf_ref, ck_hbm, cv_hbm, o_ref,
                   kbuf, vbuf, sem, m_ref, l_ref, acc_ref):
    b = pl.program_id(0)
    n_pages = suf_ref.shape[0]
    page_rows = kbuf.shape[-1]
    n_chunks = n_pages // DEC_PAGES
    tq = q_ref.shape[1]

    def page_copies(c, slot, j):
        pg = pt_ref[b, c * DEC_PAGES + j]
        return (pltpu.make_async_copy(ck_hbm.at[0, pg], kbuf.at[slot, j], sem.at[0, slot]),
                pltpu.make_async_copy(cv_hbm.at[0, pg], vbuf.at[slot, j], sem.at[1, slot]))

    def fetch(c, slot):
        for j in range(DEC_PAGES):
            for cp in page_copies(c, slot, j):
                cp.start()

    def wait(c, slot):
        for j in range(DEC_PAGES):
            for cp in page_copies(c, slot, j):
                cp.wait()

    q = q_ref[...].astype(BF16)
    m_ref[...] = jnp.full_like(m_ref, NEG)
    l_ref[...] = jnp.zeros_like(l_ref)
    acc_ref[...] = jnp.zeros_like(acc_ref)

    def attend(kt, vt, bias, ok):
        s = jnp.einsum('htd,hds->hts', q, kt, preferred_element_type=F32) + bias
        if ok is not None:
            s = jnp.where(ok, s, NEG)
        m_prev = m_ref[...]
        m_new = jnp.maximum(m_prev, jnp.max(s, axis=2, keepdims=True))
        alpha = jnp.exp(m_prev - m_new)
        p = jnp.exp(s - m_new)
        l_ref[...] = alpha * l_ref[...] + jnp.sum(p, axis=2, keepdims=True)
        acc_ref[...] = alpha * acc_ref[...] + jnp.einsum('hts,hds->htd', p.astype(BF16), vt,
                                                         preferred_element_type=F32)
        m_ref[...] = m_new

    fetch(0, 0)

    def chunk(c, carry):
        slot = c % 2

        @pl.when(c + 1 < n_chunks)
        def _():
            fetch(c + 1, 1 - slot)

        wait(c, slot)
        kt = jnp.concatenate([kbuf[slot, j] for j in range(DEC_PAGES)], axis=2).astype(BF16)
        vt = jnp.concatenate([vbuf[slot, j] for j in range(DEC_PAGES)], axis=2).astype(BF16)
        suf = suf_ref[pl.ds(c * DEC_PAGES, DEC_PAGES)]
        bias = jnp.concatenate([suf[j] for j in range(DEC_PAGES)], axis=1)
        attend(kt, vt, bias[:, None, :], None)
        return carry

    lax.fori_loop(0, n_chunks, chunk, 0)

    r = lax.broadcasted_iota(I32, (page_rows, page_rows), 0)
    c = lax.broadcasted_iota(I32, (page_rows, page_rows), 1)
    fn = _dot(lfn_ref[...], jnp.where(r <= c, 1.0, 0.0).astype(F32), precision=HIGHEST)
    qi = lax.broadcasted_iota(I32, (H_A, tq, page_rows), 1)
    ki = lax.broadcasted_iota(I32, (H_A, tq, page_rows), 2)
    attend(kn_ref[...].astype(BF16), vn_ref[...].astype(BF16), -fn[:, None, :], ki <= qi)
    o_ref[...] = acc_ref[...] / l_ref[...]


def _foxdec(page_table, q, kn, vn, lfn, suf, cache_kt, cache_vt):
    nb, _, tq, _ = q.shape
    n_pages = page_table.shape[1]
    page_rows = cache_kt.shape[4]
    per_seq = lambda shape: pl.BlockSpec((None,) + shape, lambda b, pt: (b,) + (0,) * len(shape))
    page = (H_A, DH_A, page_rows)
    return pl.pallas_call(
        _foxdec_kernel,
        grid_spec=pltpu.PrefetchScalarGridSpec(
            num_scalar_prefetch=1,
            grid=(nb,),
            in_specs=[per_seq((H_A, tq, DH_A)), per_seq(page), per_seq(page), per_seq((H_A, page_rows)),
                      per_seq((n_pages, H_A, page_rows)),
                      pl.BlockSpec(memory_space=pl.ANY), pl.BlockSpec(memory_space=pl.ANY)],
            out_specs=per_seq((H_A, tq, DH_A)),
            scratch_shapes=[pltpu.VMEM((2, DEC_PAGES) + page, F32),
                            pltpu.VMEM((2, DEC_PAGES) + page, F32),
                            pltpu.SemaphoreType.DMA((2, 2)),
                            pltpu.VMEM((H_A, tq, 1), F32), pltpu.VMEM((H_A, tq, 1), F32),
                            pltpu.VMEM((H_A, tq, DH_A), F32)]),
        out_shape=jax.ShapeDtypeStruct((nb, H_A, tq, DH_A), F32),
        compiler_params=_cparams(("arbitrary",)),
        name="foxdec",
    )(page_table, q, kn, vn, lfn, suf, cache_kt, cache_vt)


def _gla_kernel(q_ref, k_ref, g_ref, v_ref, r_ref, s0_ref, gn_ref, y_ref, sfin_ref, state_ref):
    tb = pl.program_id(1)
    n_tb = pl.num_programs(1)
    c = GLA_CHUNK
    n_chunks = q_ref.shape[0] // c
    n_sub = c // GLA_SUB

    row_k = lax.broadcasted_iota(I32, (WK_B, WV_B), 0) // DK_B
    col_v = lax.broadcasted_iota(I32, (WK_B, WV_B), 1) // DV_B
    state_diag = row_k == col_v

    @pl.when(tb == 0)
    def _():
        state_ref[...] = jnp.zeros_like(state_ref)
        for h in range(H_B):
            state_ref[h * DK_B:(h + 1) * DK_B, h * DV_B:(h + 1) * DV_B] = s0_ref[h]

    r = lax.broadcasted_iota(I32, (c, c), 0)
    s = lax.broadcasted_iota(I32, (c, c), 1)
    same_sub = (r // GLA_SUB) == (s // GLA_SUB)
    cum_ops = jnp.concatenate([
        jnp.where(same_sub & (s <= r), 1.0, 0.0),
        jnp.where(s <= r, 1.0, 0.0),
        jnp.where(same_sub, 1.0, 0.0),
    ], axis=0).astype(F32)

    t_i = lax.broadcasted_iota(I32, (c, H_B * c), 0)
    s_i = lax.broadcasted_iota(I32, (c, H_B * c), 1) % c
    dsub = t_i // GLA_SUB - s_i // GLA_SUB
    intra = (dsub == 0) & (s_i <= t_i)
    head_k = lax.broadcasted_iota(I32, (c, WK_B), 1) // DK_B
    head_v = lax.broadcasted_iota(I32, (c, WV_B), 1) // DV_B

    def stack_heads(x, head_of_lane):
        zero = jnp.zeros_like(x)
        return jnp.concatenate([jnp.where(head_of_lane == h, x, zero) for h in range(H_B)], axis=0)

    def shift_rows(x, n):
        return jnp.concatenate([x[:n], x[:c - n]], axis=0)

    gn = gn_ref[...]
    for ci in range(n_chunks):
        rows = slice(ci * c, (ci + 1) * c)
        q = q_ref[rows, :]
        k = k_ref[rows, :]
        g = g_ref[rows, :]
        v = v_ref[rows, :]

        cums = _dot(cum_ops, g, precision=HIGHEST)
        bl, bc, tot = cums[:c], cums[c:2 * c], cums[2 * c:]
        pre = bc - bl
        q_loc = q * jnp.exp(bl)
        k_loc = k * jnp.exp(-bl)
        k_end = k * jnp.exp(tot - bl)
        q_far = [q_loc] + [q_loc * jnp.exp(pre - shift_rows(pre, GLA_SUB * d)) for d in range(1, n_sub - 1)]

        a0 = _dot_nt(q_loc.astype(BF16), stack_heads(k_loc, head_k).astype(BF16))
        af = _dot_nt(jnp.concatenate(q_far, axis=0).astype(BF16), stack_heads(k_end, head_k).astype(BF16))
        att = jnp.where(intra, a0, 0.0)
        for d in range(1, n_sub):
            att = jnp.where(dsub == d, af[(d - 1) * c:d * c], att)

        q_in = q * jnp.exp(bc)
        o = _dot(att.astype(BF16), stack_heads(v, head_v)) + _dot(q_in.astype(BF16), state_ref[...].astype(BF16))

        k_out = k * jnp.exp(bc[c - 1:c, :] - bc)
        kv = _dot_tn(k_out.astype(BF16), v)
        decay = jnp.exp(_dot_tn(g, jnp.ones((c, LANES), F32), precision=HIGHEST))
        state_ref[...] = (state_ref[...] * jnp.concatenate([decay] * (WV_B // LANES), axis=1)
                          + jnp.where(state_diag, kv, 0.0))

        for h in range(H_B):
            lanes = slice(h * DV_B, (h + 1) * DV_B)
            oh = o[:, lanes]
            yh = oh * lax.rsqrt(jnp.mean(oh * oh, axis=-1, keepdims=True) + EPS) * gn
            y_ref[rows, lanes] = (yh * r_ref[rows, lanes].astype(F32)).astype(y_ref.dtype)

    @pl.when(tb == n_tb - 1)
    def _():
        for h in range(H_B):
            sfin_ref[h] = state_ref[h * DK_B:(h + 1) * DK_B, h * DV_B:(h + 1) * DV_B]


def _gla(q, k, g, v, r, s0, gn, batch, seq, tb_rows):
    n_tb = seq // tb_rows
    row = lambda w: pl.BlockSpec((tb_rows, w), lambda b, t: (b * n_tb + t, 0))
    state = pl.BlockSpec((None, H_B, DK_B, DV_B), lambda b, t: (b, 0, 0, 0))
    return pl.pallas_call(
        _gla_kernel,
        grid=(batch, n_tb),
        in_specs=[row(WK_B), row(WK_B), row(WK_B), row(WV_B), row(WV_B), state,
                  pl.BlockSpec((1, DV_B), lambda b, t: (0, 0))],
        out_specs=[row(WV_B), state],
        out_shape=[jax.ShapeDtypeStruct((batch * seq, WV_B), BF16),
                   jax.ShapeDtypeStruct((batch, H_B, DK_B, DV_B), F32)],
        scratch_shapes=[pltpu.VMEM((WK_B, WV_B), F32)],
        compiler_params=_cparams(("arbitrary", "arbitrary")),
        name="gla",
    )(q, k, g, v, r, s0, gn)


def _mixout_kernel(*refs, n_tail):
    (x_ref, ya_ref, yb_ref, ga_ref, gb_ref, mod_ref, g2_ref, wpa_ref, wpb_ref, wo_ref, rw_ref) = refs[:11]
    tail_refs = refs[11:11 + n_tail]
    x1_ref, hp_ref, lg_ref = refs[11 + n_tail:]
    i = pl.program_id(0)
    n_own = pl.num_programs(0) - (1 if n_tail else 0)

    @pl.when(i < n_own)
    def _():
        br_a = _dot(ya_ref[...], wpa_ref[...])
        br_b = _dot(yb_ref[...], wpb_ref[...])
        merged = ga_ref[...].astype(F32) * br_a + gb_ref[...].astype(F32) * br_b
        out = _dot(merged.astype(BF16), wo_ref[...])
        x1 = x_ref[...] + mod_ref[2] * out
        x1_ref[...] = x1
        y = x1 * lax.rsqrt(jnp.mean(x1 * x1, axis=-1, keepdims=True) + EPS) * g2_ref[...]
        h = y * (1.0 + mod_ref[4]) + mod_ref[3]
        hp_ref[...] = _pack_bf16_pair(h)
        lg_ref[...] = _dot_nt(rw_ref[...], h, precision=HIGHEST)

    if n_tail:
        @pl.when(i == n_own)
        def _():
            for src, dst in zip(tail_refs, (x1_ref, hp_ref, lg_ref)):
                dst[...] = src[...]


def _mixout(x, ya, yb, ga, gb, mod, rows_per_mod, g2, wpa, wpb, wo, rw, tail):
    n = x.shape[0]
    tm = MOE_TILE
    mod_rows = mod.shape[2]
    tiles_per_mod = rows_per_mod // tm
    n_own = n // tm
    n_total = n + (tm if tail else 0)
    own = lambda i: jnp.minimum(i, n_own - 1)
    row = lambda w: pl.BlockSpec((tm, w), lambda i: (own(i), 0))
    return pl.pallas_call(
        functools.partial(_mixout_kernel, n_tail=len(tail)),
        grid=(n_total // tm,),
        in_specs=[row(D_MODEL), row(W_A), row(WV_B), row(D_MODEL), row(D_MODEL),
                  pl.BlockSpec((6, None, mod_rows, D_MODEL), lambda i: (0, own(i) // tiles_per_mod, 0, 0)),
                  _const_spec(g2.shape), _const_spec(wpa.shape), _const_spec(wpb.shape), _const_spec(wo.shape),
                  _const_spec(rw.shape)] + [_const_spec(t.shape) for t in tail],
        out_specs=[pl.BlockSpec((tm, D_MODEL), lambda i: (i, 0)),
                   pl.BlockSpec((tm, D_MODEL // 2), lambda i: (i, 0)),
                   pl.BlockSpec((N_EXPERTS, tm), lambda i: (0, i))],
        out_shape=[jax.ShapeDtypeStruct((n_total, D_MODEL), F32),
                   jax.ShapeDtypeStruct((n_total, D_MODEL // 2), U32),
                   jax.ShapeDtypeStruct((N_EXPERTS, n_total), F32)],
        compiler_params=_cparams(("arbitrary",)),
        name="mixout",
    )(x, ya, yb, ga, gb, mod, g2, wpa, wpb, wo, rw, *tail)


def _route_kernel(lg_ref, bias_ref, idx_ref, w_ref, rank_ref, cnt_ref, carry_ref):
    i = pl.program_id(0)
    tm = lg_ref.shape[1]

    @pl.when(i == 0)
    def _():
        carry_ref[...] = jnp.zeros_like(carry_ref)

    s = _sigmoid(lg_ref[...])
    sb = s + bias_ref[...][:, 0:1]
    ninf = -jnp.inf

    e_in_group = lax.broadcasted_iota(I32, (GROUP_SIZE, tm), 0)
    gscores = []
    for g in range(N_GROUPS):
        blk = sb[g * GROUP_SIZE:(g + 1) * GROUP_SIZE, :]
        m1 = jnp.max(blk, axis=0, keepdims=True)
        first = jnp.min(jnp.where(blk == m1, e_in_group, GROUP_SIZE), axis=0, keepdims=True)
        m2 = jnp.max(jnp.where(e_in_group == first, ninf, blk), axis=0, keepdims=True)
        gscores.append(m1 + m2)
    gs = jnp.concatenate(gscores, axis=0)

    g_iota = lax.broadcasted_iota(I32, (N_GROUPS, tm), 0)
    chosen = jnp.zeros((N_GROUPS, tm), F32)
    for _ in range(TOPK_GROUPS):
        m = jnp.max(gs, axis=0, keepdims=True)
        first = jnp.min(jnp.where(gs == m, g_iota, N_GROUPS), axis=0, keepdims=True)
        pick = g_iota == first
        chosen = jnp.where(pick, 1.0, chosen)
        gs = jnp.where(pick, ninf, gs)
    allowed = jnp.concatenate(
        [jnp.broadcast_to(chosen[g:g + 1, :], (GROUP_SIZE, tm)) for g in range(N_GROUPS)], axis=0) > 0.5
    cand = jnp.where(allowed, sb, ninf)

    e_iota = lax.broadcasted_iota(I32, (N_EXPERTS, tm), 0)
    onehot = jnp.zeros((N_EXPERTS, tm), F32)
    idxs, ws = [], []
    for _ in range(TOP_K):
        m = jnp.max(cand, axis=0, keepdims=True)
        first = jnp.min(jnp.where(cand == m, e_iota, N_EXPERTS), axis=0, keepdims=True)
        pick = e_iota == first
        idxs.append(first)
        ws.append(jnp.sum(jnp.where(pick, s, 0.0), axis=0, keepdims=True))
        onehot = jnp.where(pick, 1.0, onehot)
        cand = jnp.where(pick, ninf, cand)
    w = jnp.concatenate(ws, axis=0)
    w_ref[...] = w / jnp.sum(w, axis=0, keepdims=True) * ROUTE_SCALE
    idx_ref[...] = jnp.concatenate(idxs, axis=0)

    r = lax.broadcasted_iota(I32, (tm, tm), 0)
    c = lax.broadcasted_iota(I32, (tm, tm), 1)
    earlier = jnp.where(r < c, 1.0, 0.0).astype(BF16)
    carry = carry_ref[...]
    before = _dot(onehot.astype(BF16), earlier) + jnp.concatenate([carry] * (tm // LANES), axis=1)
    rank_ref[...] = jnp.concatenate(
        [jnp.sum(jnp.where(e_iota == ix, before, 0.0), axis=0, keepdims=True) for ix in idxs], axis=0).astype(I32)
    carry = carry + _dot(onehot.astype(BF16), jnp.ones((tm, LANES), BF16))
    carry_ref[...] = carry
    cnt_ref[...] = carry


def _route(logits_t, bias):
    n = logits_t.shape[1]
    tm = MOE_TILE
    tok = lambda: pl.BlockSpec((TOP_K, tm), lambda i: (0, i))
    return pl.pallas_call(
        _route_kernel,
        grid=(n // tm,),
        in_specs=[pl.BlockSpec((N_EXPERTS, tm), lambda i: (0, i)), _const_spec(bias.shape)],
        out_specs=[tok(), tok(), tok(), _const_spec((N_EXPERTS, LANES))],
        out_shape=[jax.ShapeDtypeStruct((TOP_K, n), I32), jax.ShapeDtypeStruct((TOP_K, n), F32),
                   jax.ShapeDtypeStruct((TOP_K, n), I32), jax.ShapeDtypeStruct((N_EXPERTS, LANES), F32)],
        scratch_shapes=[pltpu.VMEM((N_EXPERTS, LANES), F32)],
        compiler_params=_cparams(("arbitrary",)),
        name="route",
    )(logits_t, bias)


def _dest_kernel(idx_ref, rank_ref, cnt_ref, dest_ref, start_ref):
    i = pl.program_id(0)
    tm = idx_ref.shape[1]

    @pl.when(i == 0)
    def _():
        cnt = cnt_ref[...]
        padded = jnp.floor((cnt + (EXPERT_TILE - 1)) * (1.0 / EXPERT_TILE)) * EXPERT_TILE
        r = lax.broadcasted_iota(I32, (N_EXPERTS, N_EXPERTS), 0)
        c = lax.broadcasted_iota(I32, (N_EXPERTS, N_EXPERTS), 1)
        start_ref[...] = _dot(jnp.where(c < r, 1.0, 0.0).astype(F32), padded, precision=HIGHEST)

    start = jnp.concatenate([start_ref[...]] * (tm // LANES), axis=1)
    e_iota = lax.broadcasted_iota(I32, (N_EXPERTS, tm), 0)
    idx = idx_ref[...]
    first = jnp.concatenate(
        [jnp.sum(jnp.where(e_iota == idx[kk:kk + 1, :], start, 0.0), axis=0, keepdims=True) for kk in range(TOP_K)],
        axis=0)
    dest_ref[...] = first.astype(I32) + rank_ref[...]


def _dest(idx_t, rank_t, counts):
    n = idx_t.shape[1]
    tm = MOE_TILE
    tok = lambda: pl.BlockSpec((TOP_K, tm), lambda i: (0, i))
    return pl.pallas_call(
        _dest_kernel,
        grid=(n // tm,),
        in_specs=[tok(), tok(), _const_spec(counts.shape)],
        out_specs=pl.BlockSpec((None, TOP_K, tm), lambda i: (i, 0, 0)),
        out_shape=jax.ShapeDtypeStruct((n // tm, TOP_K, tm), I32),
        scratch_shapes=[pltpu.VMEM((N_EXPERTS, LANES), F32)],
        compiler_params=_cparams(("arbitrary",)),
        name="dest",
    )(idx_t, rank_t, counts)


def _sc_worker():
    return lax.axis_index("s") * SC_CORES + lax.axis_index("c")


def _sc_mesh():
    return plsc.VectorSubcoreMesh(core_axis_name="c", subcore_axis_name="s")


def _dispatch(dest_tiles, hp, n_rows_out):
    n_tok_tiles = dest_tiles.shape[0]
    w = hp.shape[1]
    iters = -(-n_tok_tiles // SC_WORKERS)

    def body(hp_hbm, dest_hbm, xs_hbm, idx_v, rows_v, sem):
        wid = _sc_worker()

        @pl.loop(0, iters)
        def _(j):
            tile = wid + SC_WORKERS * j

            @pl.when(tile < n_tok_tiles)
            def _():
                pltpu.sync_copy(dest_hbm.at[tile], idx_v)
                pltpu.sync_copy(hp_hbm.at[pl.ds(tile * MOE_TILE, MOE_TILE)], rows_v)
                copies = [pltpu.async_copy(rows_v, xs_hbm.at[idx_v.at[kk]], sem) for kk in range(TOP_K)]
                for cp in copies:
                    cp.wait()

    return pl.kernel(
        body,
        out_type=jax.ShapeDtypeStruct((n_rows_out, w), U32),
        mesh=_sc_mesh(),
        scratch_types=[pltpu.VMEM((TOP_K, MOE_TILE), I32), pltpu.VMEM((MOE_TILE, w), U32), pltpu.SemaphoreType.DMA],
        name="dispatch",
    )(hp, dest_tiles)


def _gather_back(ys, dest_tiles):
    n_tok_tiles = dest_tiles.shape[0]
    w = ys.shape[1]
    n_lists = n_tok_tiles * TOP_K
    iters = -(-n_lists // SC_WORKERS)

    def body(ys_hbm, dest_hbm, out_hbm, idx_v, rows_v, sem):
        wid = _sc_worker()

        @pl.loop(0, iters)
        def _(j):
            lst = wid + SC_WORKERS * j

            @pl.when(lst < n_lists)
            def _():
                pltpu.sync_copy(dest_hbm.at[pl.ds(lst * MOE_TILE, MOE_TILE)], idx_v)
                pltpu.async_copy(ys_hbm.at[idx_v], rows_v, sem).wait()
                pltpu.sync_copy(rows_v, out_hbm.at[pl.ds(lst * MOE_TILE, MOE_TILE)])

    out = pl.kernel(
        body,
        out_type=jax.ShapeDtypeStruct((n_lists * MOE_TILE, w), U32),
        mesh=_sc_mesh(),
        scratch_types=[pltpu.VMEM((MOE_TILE,), I32), pltpu.VMEM((MOE_TILE, w), U32), pltpu.SemaphoreType.DMA],
        name="gather_back",
    )(ys, dest_tiles.reshape(-1))
    return out.reshape(n_tok_tiles, TOP_K, MOE_TILE, w)


def _experts_kernel(first_ref, count_ref, rows_ref, xs_hbm, wg_ref, wu_ref, wd_ref, ys_hbm,
                    xbuf, ybuf, wgb_ref, wub_ref, wdb_ref, sem):
    e = pl.program_id(0)
    tm = EXPERT_TILE
    ring = EXPERT_RING
    n_e = count_ref[e]
    total = first_ref[N_EXPERTS - 1] + count_ref[N_EXPERTS - 1]

    def x_copy(t, slot):
        return pltpu.make_async_copy(xs_hbm.at[pl.ds(t * tm, tm), :], xbuf.at[slot], sem.at[0, slot])

    def y_copy(t, slot):
        return pltpu.make_async_copy(ybuf.at[slot], ys_hbm.at[pl.ds(t * tm, tm), :], sem.at[1, slot])

    @pl.when(e == 0)
    def _():
        for t in range(ring - 1):
            @pl.when(t < total)
            def _():
                x_copy(t, t).start()

    @pl.when(n_e > 0)
    def _():
        wgb_ref[...] = wg_ref[...].astype(BF16)
        wub_ref[...] = wu_ref[...].astype(BF16)
        wdb_ref[...] = wd_ref[...].astype(BF16)
        half = D_MODEL // 2

        def body(j, carry):
            t = first_ref[e] + j
            slot = t % ring
            x_copy(t, slot).wait()

            @pl.when(t + ring - 1 < total)
            def _():
                x_copy(t + ring - 1, (t + ring - 1) % ring).start()

            @pl.when(t >= ring)
            def _():
                y_copy(t - ring, slot).wait()

            xs = xbuf[slot]
            filled = lax.broadcasted_iota(I32, xs.shape, 0) < rows_ref[t]
            lo, hi = _unpack_bf16_pair(jnp.where(filled, xs, jnp.zeros_like(xs)))
            gate = _dot(lo, wgb_ref[:half, :]) + _dot(hi, wgb_ref[half:, :])
            up = _dot(lo, wub_ref[:half, :]) + _dot(hi, wub_ref[half:, :])
            act = (_silu(gate) * up).astype(BF16)
            ybuf[slot] = _pack_bf16_pair(_dot(act, wdb_ref[...]))
            y_copy(t, slot).start()
            return carry

        lax.fori_loop(0, n_e, body, 0)

    @pl.when(e == N_EXPERTS - 1)
    def _():
        for back in range(ring, 0, -1):
            @pl.when(total >= back)
            def _():
                y_copy(total - back, (total - back) % ring).wait()


def _experts(tile_first, tile_count, tile_rows, xs, wg, wu, wd):
    n_rows, w = xs.shape
    tm = EXPERT_TILE
    expert = lambda e, tf, tc, tr: (e, 0, 0)
    return pl.pallas_call(
        _experts_kernel,
        grid_spec=pltpu.PrefetchScalarGridSpec(
            num_scalar_prefetch=3,
            grid=(N_EXPERTS,),
            in_specs=[pl.BlockSpec(memory_space=pl.ANY),
                      pl.BlockSpec((None, D_MODEL, D_EXPERT), expert),
                      pl.BlockSpec((None, D_MODEL, D_EXPERT), expert),
                      pl.BlockSpec((None, D_EXPERT, D_MODEL), expert)],
            out_specs=pl.BlockSpec(memory_space=pl.ANY),
            scratch_shapes=[pltpu.VMEM((EXPERT_RING, tm, w), U32), pltpu.VMEM((EXPERT_RING, tm, w), U32),
                            pltpu.VMEM((D_MODEL, D_EXPERT), BF16), pltpu.VMEM((D_MODEL, D_EXPERT), BF16),
                            pltpu.VMEM((D_EXPERT, D_MODEL), BF16),
                            pltpu.SemaphoreType.DMA((2, EXPERT_RING))]),
        out_shape=jax.ShapeDtypeStruct((n_rows, w), U32),
        compiler_params=_cparams(("arbitrary",)),
        name="experts",
    )(tile_first, tile_count, tile_rows, xs, wg, wu, wd)


def _combine_kernel(yt_ref, x1_ref, hp_ref, w_ref, mod_ref, fg_ref, sg_ref, su_ref, sd_ref, o_ref):
    half = D_MODEL // 2
    lo, hi = _unpack_bf16_pair(hp_ref[...])
    sg = sg_ref[...]
    su = su_ref[...]
    gate = _dot(lo, sg[:half]) + _dot(hi, sg[half:])
    up = _dot(lo, su[:half]) + _dot(hi, su[half:])
    y = _dot((_silu(gate) * up).astype(BF16), sd_ref[...])

    w = w_ref[...]
    for kk in range(TOP_K):
        ylo, yhi = _unpack_bf16_pair(yt_ref[kk])
        y = y + w[:, kk:kk + 1] * jnp.concatenate([ylo.astype(F32), yhi.astype(F32)], axis=1)
    x2 = x1_ref[...] + mod_ref[5] * y
    o_ref[...] = x2 * lax.rsqrt(jnp.mean(x2 * x2, axis=-1, keepdims=True) + EPS) * fg_ref[...]


def _combine(y_tok, x1, hp, w, mod, rows_per_mod, fg, sg, su, sd, n, row_offset):
    tm = MOE_TILE
    mod_rows = mod.shape[2]
    tiles_per_mod = rows_per_mod // tm
    off = row_offset // tm
    return pl.pallas_call(
        _combine_kernel,
        grid=(n // tm,),
        in_specs=[pl.BlockSpec((None, TOP_K, tm, D_MODEL // 2), lambda i: (i + off, 0, 0, 0)),
                  pl.BlockSpec((tm, D_MODEL), lambda i: (i + off, 0)),
                  pl.BlockSpec((tm, D_MODEL // 2), lambda i: (i + off, 0)),
                  pl.BlockSpec((tm, TOP_K), lambda i: (i + off, 0)),
                  pl.BlockSpec((6, None, mod_rows, D_MODEL), lambda i: (0, i // tiles_per_mod, 0, 0)),
                  _const_spec(fg.shape), _const_spec(sg.shape), _const_spec(su.shape), _const_spec(sd.shape)],
        out_specs=pl.BlockSpec((tm, D_MODEL), lambda i: (i, 0)),
        out_shape=jax.ShapeDtypeStruct((n, D_MODEL), F32),
        compiler_params=_cparams(("arbitrary",)),
        name="combine",
    )(y_tok, x1, hp, w, mod, fg, sg, su, sd)


def _split_w_in(w_in):
    sizes = (W_A, W_A, W_A, H_A, WK_B, WK_B, WV_B, WV_B, GK_RANK, D_MODEL, D_MODEL)
    segs, o = [], 0
    for s in sizes:
        segs.append(w_in[:, o:o + s])
        o += s
    return segs


def kernel(x_prompt, x_sample, cache_k, cache_v, cache_logf, state_gla, page_table, c_prompt, c_sample, w_ada, b_ada,
           norm1_g, w_in, b_f, w_gk2, b_gk, gla_norm_g, w_pa, w_pb, w_o, norm2_g, router_w, router_bias, w_gate, w_up,
           w_down, ws_gate, ws_up, ws_down, final_g):
    assert w_ada.shape[0] == 1, "single layer"
    bp, tp, d = x_prompt.shape
    bs, ts, _ = x_sample.shape
    n_p, n_s = bp * tp, bs * ts
    n_tot = n_p + n_s
    assert n_s == MOE_TILE and n_p % MOE_TILE == 0 and tp % FOX_TILE == 0

    q_a, k_a, v_a, f_a, q_b, k_b, v_b, r_b, gk1, gate_a, gate_b = _split_w_in(w_in[0])
    wts = (jnp.concatenate([q_a, k_a], axis=1).astype(BF16),
           v_a.astype(BF16),
           jnp.concatenate([k_a, v_a], axis=1).T.astype(BF16),
           jnp.pad(f_a, ((0, 0), (0, LANES - H_A))).astype(BF16),
           jnp.pad(b_f[0].reshape(1, H_A), ((0, 0), (0, LANES - H_A))),
           jnp.concatenate([q_b, k_b, v_b, r_b], axis=1).astype(BF16),
           jnp.pad(gk1, ((0, 0), (0, LANES - GK_RANK))).astype(BF16),
           jnp.pad(w_gk2[0], ((0, LANES - GK_RANK), (0, 0))).astype(BF16),
           b_gk[0].reshape(1, WK_B),
           jnp.concatenate([gate_a, gate_b], axis=1).astype(BF16))
    g1 = norm1_g[0].reshape(1, d)
    g2 = norm2_g[0].reshape(1, d)
    gn = gla_norm_g[0].reshape(1, DV_B)
    fg = final_g.reshape(1, d)
    wpa, wpb, wo = w_pa[0].astype(BF16), w_pb[0].astype(BF16), w_o[0].astype(BF16)
    rw_t = router_w[0].T
    rbias = jnp.broadcast_to(router_bias[0].reshape(N_EXPERTS, 1), (N_EXPERTS, LANES))
    sg, su, sd = ws_gate[0].astype(BF16), ws_up[0].astype(BF16), ws_down[0].astype(BF16)

    ada = _ada(jnp.concatenate([c_prompt, c_sample], axis=0), w_ada[0], b_ada[0].reshape(1, -1))
    ada = ada.reshape(bp + bs, 6, d)
    mod_p = ada[:bp].transpose(1, 0, 2)[:, :, None, :]
    mod_s = jnp.repeat(ada[bp:], ts, axis=0).transpose(1, 0, 2)[:, None, :, :]

    xp = x_prompt.reshape(n_p, d)
    tm_p = 512
    (q, kh, vth, kt, vt, lf, fc, qb, kb, gk, vb, rb, ga_p, gb_p) = _inproj(
        xp, mod_p, tp, tp // tm_p, tm_p, g1, wts, True)
    ya_p = _fox(q, kh, vth, fc, bp, tp)
    yb_p, s_p = _gla(qb, kb, gk, vb, rb, jnp.zeros((bp, H_B, DK_B, DV_B), F32), gn, bp, tp, 256)
    k_prompt = kt.reshape(bp, H_A, DH_A, tp).transpose(0, 3, 1, 2)[None]
    v_prompt = vt.reshape(bp, H_A, DH_A, tp).transpose(0, 3, 1, 2)[None]
    logf_prompt = lf.reshape(1, bp, tp, H_A)

    xs_in = x_sample.reshape(n_s, d)
    (q, k, v, lf, qb, kb, gk, vb, rb, ga, gb) = _inproj(xs_in, mod_s, n_s, 1, n_s, g1, wts, False)
    page_rows = cache_k.shape[2]
    tq = 8
    cache_kt = jnp.transpose(cache_k, (0, 1, 3, 4, 2))
    cache_vt = jnp.transpose(cache_v, (0, 1, 3, 4, 2))
    suf = _lfsuf(page_table, jnp.transpose(cache_logf, (0, 1, 3, 2)))

    def new_kv(a):
        a = a.reshape(bs, ts, H_A, DH_A).transpose(0, 2, 3, 1)
        return jnp.pad(a, ((0, 0), (0, 0), (0, 0), (0, page_rows - ts)))

    q_dec = jnp.pad(q.astype(F32).reshape(bs, ts, H_A, DH_A).transpose(0, 2, 1, 3),
                    ((0, 0), (0, 0), (0, tq - ts), (0, 0)))
    lfn = jnp.pad(lf.reshape(bs, ts, H_A).transpose(0, 2, 1), ((0, 0), (0, 0), (0, page_rows - ts)))
    ya = _foxdec(page_table, q_dec, new_kv(k), new_kv(v), lfn, suf, cache_kt, cache_vt)
    ya = ya[:, :, :ts].transpose(0, 2, 1, 3).reshape(n_s, W_A).astype(BF16)
    pad = lambda a: jnp.pad(a.reshape(bs, ts, -1), ((0, 0), (0, GLA_CHUNK - ts), (0, 0))).reshape(bs * GLA_CHUNK, -1)
    yb, s_s = _gla(pad(qb), pad(kb), pad(gk), pad(vb), pad(rb), state_gla[0], gn, bs, GLA_CHUNK, GLA_CHUNK)
    yb = yb.reshape(bs, GLA_CHUNK, WV_B)[:, :ts].reshape(n_s, WV_B)
    k_sample = k.reshape(1, bs, ts, H_A, DH_A)
    v_sample = v.reshape(1, bs, ts, H_A, DH_A)
    logf_sample = lf.reshape(1, bs, ts, H_A)
    tail = _mixout(xs_in, ya, yb, ga, gb, mod_s, n_s, g2, wpa, wpb, wo, rw_t, ())

    x1, hp, logits_t = _mixout(xp, ya_p, yb_p, ga_p, gb_p, mod_p, tp, g2, wpa, wpb, wo, rw_t, tuple(tail))
    idx_t, w_t, rank_t, counts = _route(logits_t, rbias)
    dest_tiles = _dest(idx_t, rank_t, counts)
    padded = (counts[:, 0].astype(I32) + EXPERT_TILE - 1) // EXPERT_TILE * EXPERT_TILE
    pad_end = jnp.cumsum(padded)
    n_tiles = -(-n_tot * TOP_K // EXPERT_TILE) + N_EXPERTS
    tile_first_row = jnp.arange(n_tiles, dtype=I32) * EXPERT_TILE
    tile_expert = jnp.minimum(jnp.sum((pad_end[None, :] <= tile_first_row[:, None]).astype(I32), axis=1),
                              N_EXPERTS - 1)
    used_end = pad_end - padded + counts[:, 0].astype(I32)
    tile_rows = jnp.clip(used_end[tile_expert] - tile_first_row, 0, EXPERT_TILE).astype(I32)
    tile_first = ((pad_end - padded) // EXPERT_TILE).astype(I32)
    tile_count = (padded // EXPERT_TILE).astype(I32)

    xs_sorted = _dispatch(dest_tiles, hp, n_tiles * EXPERT_TILE)
    ys_sorted = _experts(tile_first, tile_count, tile_rows, xs_sorted, w_gate[0], w_up[0], w_down[0])
    y_tok = _gather_back(ys_sorted, dest_tiles)
    w_tok = w_t.T
    y_prompt = _combine(y_tok, x1, hp, w_tok, mod_p, tp, fg, sg, su, sd, n_p, 0)
    y_sample = _combine(y_tok, x1, hp, w_tok, mod_s, n_s, fg, sg, su, sd, n_s, n_p)

    return (y_prompt.reshape(bp, tp, d), y_sample.reshape(bs, ts, d),
            k_prompt, v_prompt, logf_prompt, s_p[None],
            k_sample, v_sample, logf_sample, s_s[None])
```

```python
import functools

import jax
import jax.numpy as jnp
from jax import lax
from jax.experimental import pallas as pl
from jax.experimental.pallas import tpu as pltpu
from jax.experimental.pallas import tpu_sc as plsc

F32 = jnp.float32
BF16 = jnp.bfloat16
U32 = jnp.uint32
I32 = jnp.int32

EPS = 1e-6
D_MODEL = 1024
H_A, DH_A, W_A = 8, 64, 512
H_B, DK_B, DV_B, WK_B, WV_B = 4, 64, 128, 256, 512
GK_RANK = 16
GATE_NORM = 16.0
N_EXPERTS, TOP_K, N_GROUPS, TOPK_GROUPS = 256, 8, 8, 4
GROUP_SIZE = N_EXPERTS // N_GROUPS
D_EXPERT = 256
ROUTE_SCALE = 2.5
MOE_TILE = 128
EXPERT_TILE = 256
EXPERT_RING = 4
LANES = 128
SC_CORES = 2
SC_WORKERS = 32
GLA_CHUNK = 64
GLA_SUB = 16
FOX_TILE = 512
FOX_SUB = 128
DEC_PAGES = 8
DEC_RING = 3
NEG = -1e30
LOG2E = 1.4426950408889634
VMEM_LIMIT = 56 * 1024 * 1024
HIGHEST = lax.Precision.HIGHEST

NT_DIMS = (((1,), (1,)), ((), ()))
TN_DIMS = (((0,), (0,)), ((), ()))


def _cparams(sem):
    return pltpu.CompilerParams(dimension_semantics=sem, vmem_limit_bytes=VMEM_LIMIT)


def _dot(a, b, **kw):
    return jnp.dot(a, b, preferred_element_type=F32, **kw)


def _dot_nt(a, b, **kw):
    return lax.dot_general(a, b, NT_DIMS, preferred_element_type=F32, **kw)


def _dot_tn(a, b, **kw):
    return lax.dot_general(a, b, TN_DIMS, preferred_element_type=F32, **kw)


def _sigmoid(x):
    return 1.0 / (1.0 + jnp.exp(-x))


def _silu(x):
    return x * _sigmoid(x)


def _log_sigmoid(x):
    return jnp.minimum(x, 0.0) - jnp.log(1.0 + jnp.exp(-jnp.abs(x)))


def _pack_bf16_pair(x):
    c = x.shape[1] // 2
    bits = pltpu.bitcast(x.astype(BF16).astype(F32), U32)
    return (bits[:, c:] & jnp.uint32(0xFFFF0000)) | (bits[:, :c] >> 16)


def _unpack_bf16_pair(p):
    lo = pltpu.bitcast(p << 16, F32).astype(BF16)
    hi = pltpu.bitcast(p & jnp.uint32(0xFFFF0000), F32).astype(BF16)
    return lo, hi


def _ada_kernel(c_ref, w_ref, b_ref, o_ref):
    c = c_ref[...]
    o_ref[...] = _dot(_silu(c).astype(BF16), w_ref[...].astype(BF16)) + b_ref[...]


def _ada(c, w, b):
    n, d = c.shape
    cols = w.shape[1]
    tn = 1536
    return pl.pallas_call(
        _ada_kernel,
        grid=(cols // tn,),
        in_specs=[pl.BlockSpec((n, d), lambda j: (0, 0)),
                  pl.BlockSpec((d, tn), lambda j: (0, j)),
                  pl.BlockSpec((1, tn), lambda j: (0, j))],
        out_specs=pl.BlockSpec((n, tn), lambda j: (0, j)),
        out_shape=jax.ShapeDtypeStruct((n, cols), F32),
        compiler_params=_cparams(("arbitrary",)),
        name="ada",
    )(c, w, b)


def _inproj_kernel(*refs, seq_tiles, prompt):
    (x_ref, mod_ref, g_ref, wqk_ref, wv_ref, wkvt_ref, wf_ref, bf_ref, wb_ref, wgk1_ref, wgk2_ref, bgk_ref,
     wgate_ref) = refs[:13]
    if prompt:
        (q_ref, kh_ref, vth_ref, kt_ref, vt_ref, lf_ref, fc_ref, qb_ref, kb_ref, gk_ref, vb_ref, rb_ref, ga_ref,
         gb_ref, carry_ref) = refs[13:]
    else:
        q_ref, k_ref, v_ref, lf_ref, qb_ref, kb_ref, gk_ref, vb_ref, rb_ref, ga_ref, gb_ref = refs[13:]
    i = pl.program_id(0)
    tm = x_ref.shape[0]
    x = x_ref[...]
    y = x * lax.rsqrt(jnp.mean(x * x, axis=-1, keepdims=True) + EPS) * g_ref[...]
    h = (y * (1.0 + mod_ref[1]) + mod_ref[0]).astype(BF16)

    z = _dot(h, wqk_ref[...])
    q_ref[...] = (z[:, :W_A] * (DH_A ** -0.5 * (LOG2E if prompt else 1.0))).astype(BF16)
    k = z[:, W_A:]

    lf = _log_sigmoid(_dot(h, wf_ref[...]) + bf_ref[...])
    lf_ref[...] = lf[:, :H_A]

    if prompt:
        kvt = _dot_nt(wkvt_ref[...], h)
        kt_ref[...] = kvt[:W_A]
        vt_ref[...] = kvt[W_A:]
        vth_ref[...] = kvt[W_A:].astype(BF16)
        kh_ref[...] = k.astype(BF16)

        @pl.when(i % seq_tiles == 0)
        def _():
            carry_ref[...] = jnp.zeros_like(carry_ref)

        r = lax.broadcasted_iota(I32, (tm, tm), 0)
        c = lax.broadcasted_iota(I32, (tm, tm), 1)
        lower = jnp.where(c <= r, 1.0, 0.0).astype(F32)
        carry = carry_ref[...]
        fc = _dot(lower, lf, precision=HIGHEST) + carry[0:1, :]
        fc_ref[...] = fc[:, :H_A] * LOG2E
        carry_ref[...] = carry + _dot(jnp.ones((8, tm), F32), lf, precision=HIGHEST)
    else:
        k_ref[...] = k
        v_ref[...] = _dot(h, wv_ref[...])

    z = _dot(h, wb_ref[...])
    qb_ref[...] = z[:, :WK_B] * DK_B ** -0.5
    kb_ref[...] = z[:, WK_B:2 * WK_B]
    vb_ref[...] = z[:, 2 * WK_B:2 * WK_B + WV_B].astype(BF16)
    rb_ref[...] = _silu(z[:, 2 * WK_B + WV_B:]).astype(BF16)

    lr = _dot(h, wgk1_ref[...]).astype(BF16)
    gk_ref[...] = _log_sigmoid(_dot(lr, wgk2_ref[...]) + bgk_ref[...]) * (1.0 / GATE_NORM)

    z = _dot(h, wgate_ref[...])
    ga_ref[...] = _sigmoid(z[:, :D_MODEL]).astype(BF16)
    gb_ref[...] = _sigmoid(z[:, D_MODEL:]).astype(BF16)


def _const_spec(shape):
    return pl.BlockSpec(shape, lambda i: (0,) * len(shape))


def _inproj(x, mod, rows_per_mod, seq_tiles, tm, g, wts, prompt):
    n = x.shape[0]
    mod_rows = mod.shape[2]
    tiles_per_mod = rows_per_mod // tm
    row = lambda w: pl.BlockSpec((tm, w), lambda i: (i, 0))
    if prompt:
        n_seq = n // (seq_tiles * tm)
        seq_t = pl.BlockSpec((None, W_A, tm), lambda i: (i // seq_tiles, 0, i % seq_tiles))
        kv_shape = (n_seq, W_A, seq_tiles * tm)
        outs = [
            (row(W_A), (n, W_A), BF16),
            (row(W_A), (n, W_A), BF16),
            (pl.BlockSpec((W_A, tm), lambda i: (0, i)), (W_A, n), BF16),
            (seq_t, kv_shape, F32),
            (seq_t, kv_shape, F32),
            (row(H_A), (n, H_A), F32),
            (row(H_A), (n, H_A), F32),
        ]
    else:
        outs = [
            (row(W_A), (n, W_A), BF16),
            (row(W_A), (n, W_A), F32),
            (row(W_A), (n, W_A), F32),
            (row(H_A), (n, H_A), F32),
        ]
    outs += [
        (row(WK_B), (n, WK_B), F32),
        (row(WK_B), (n, WK_B), F32),
        (row(WK_B), (n, WK_B), F32),
        (row(WV_B), (n, WV_B), BF16),
        (row(WV_B), (n, WV_B), BF16),
        (row(D_MODEL), (n, D_MODEL), BF16),
        (row(D_MODEL), (n, D_MODEL), BF16),
    ]
    return pl.pallas_call(
        functools.partial(_inproj_kernel, seq_tiles=seq_tiles, prompt=prompt),
        grid=(n // tm,),
        in_specs=[row(D_MODEL),
                  pl.BlockSpec((6, None, mod_rows, D_MODEL), lambda i: (0, i // tiles_per_mod, 0, 0)),
                  _const_spec(g.shape)] + [_const_spec(w.shape) for w in wts],
        out_specs=[o[0] for o in outs],
        out_shape=[jax.ShapeDtypeStruct(o[1], o[2]) for o in outs],
        scratch_shapes=[pltpu.VMEM((8, LANES), F32)] if prompt else [],
        compiler_params=_cparams(("arbitrary",)),
        name="inproj",
    )(x, mod, g, *wts)


def _fox_kernel(q_ref, k_ref, vt_ref, fc_ref, o_ref, qs_ref, m_ref, l_ref, acc_ref):
    qi = pl.program_id(1)
    ki = pl.program_id(2)
    t = q_ref.shape[0]
    pairs = H_A // 2
    sub = FOX_SUB

    @pl.when(ki == 0)
    def _():
        lane = lax.broadcasted_iota(I32, (t, LANES), 1)
        for hp in range(pairs):
            q = q_ref[:, hp * LANES:(hp + 1) * LANES]
            qs_ref[hp, :t, :] = jnp.where(lane < DH_A, q, jnp.zeros_like(q))
            qs_ref[hp, t:, :] = jnp.where(lane >= DH_A, q, jnp.zeros_like(q))
        m_ref[...] = jnp.full_like(m_ref, NEG)
        l_ref[...] = jnp.zeros_like(l_ref)
        acc_ref[...] = jnp.zeros_like(acc_ref)

    def step(diagonal):
        for hp in range(pairs):
            k = k_ref[:, hp * LANES:(hp + 1) * LANES]
            vt = vt_ref[hp * LANES:(hp + 1) * LANES, :]
            for h2 in range(2):
                head = 2 * hp + h2
                fb = jnp.broadcast_to(fc_ref[:, head:head + 1], (t, sub))
                for sb in range(t // sub):
                    cols = slice(h2 * t + sb * sub, h2 * t + (sb + 1) * sub)
                    s = _dot_nt(k, qs_ref[hp, cols, :]) - fb
                    if diagonal:
                        key = lax.broadcasted_iota(I32, (t, sub), 0)
                        qry = lax.broadcasted_iota(I32, (t, sub), 1) + sb * sub
                        s = jnp.where(key <= qry, s, NEG)
                    m_prev = m_ref[hp, :, cols]
                    m_new = jnp.maximum(m_prev, jnp.max(s, axis=0, keepdims=True))
                    alpha = jnp.exp2(m_prev - m_new)
                    p = jnp.exp2(s - m_new)
                    l_ref[hp, :, cols] = alpha * l_ref[hp, :, cols] + jnp.sum(p, axis=0, keepdims=True)
                    acc_ref[hp, :, cols] = alpha * acc_ref[hp, :, cols] + _dot(vt, p.astype(BF16))
                    m_ref[hp, :, cols] = m_new

    @pl.when(ki < qi)
    def _():
        step(False)

    @pl.when(ki == qi)
    def _():
        step(True)
        for hp in range(pairs):
            o = acc_ref[hp] / l_ref[hp]
            o = jnp.concatenate([o[:DH_A, :t], o[DH_A:, t:]], axis=0)
            o_ref[:, hp * LANES:(hp + 1) * LANES] = o.T.astype(o_ref.dtype)


def _fox(q, k, vt, fc, batch, seq):
    t = FOX_TILE
    nq = seq // t
    kv_map = lambda b, qi, ki: (b * nq + jnp.minimum(ki, qi), 0)
    return pl.pallas_call(
        _fox_kernel,
        grid=(batch, nq, nq),
        in_specs=[pl.BlockSpec((t, W_A), lambda b, qi, ki: (b * nq + qi, 0)),
                  pl.BlockSpec((t, W_A), kv_map),
                  pl.BlockSpec((W_A, t), lambda b, qi, ki: (0, b * nq + jnp.minimum(ki, qi))),
                  pl.BlockSpec((t, H_A), kv_map)],
        out_specs=pl.BlockSpec((t, W_A), lambda b, qi, ki: (b * nq + qi, 0)),
        out_shape=jax.ShapeDtypeStruct((batch * seq, W_A), BF16),
        scratch_shapes=[pltpu.VMEM((H_A // 2, 2 * t, LANES), BF16),
                        pltpu.VMEM((H_A // 2, 1, 2 * t), F32),
                        pltpu.VMEM((H_A // 2, 1, 2 * t), F32),
                        pltpu.VMEM((H_A // 2, LANES, 2 * t), F32)],
        compiler_params=_cparams(("arbitrary", "arbitrary", "arbitrary")),
        name="fox",
    )(q, k, vt, fc)


def _lfsuf_kernel(pt_ref, lf_hbm, o_ref, buf_ref, sem):
    b = pl.program_id(0)
    n_pages, _, w = buf_ref.shape

    def page_copy(j):
        return pltpu.make_async_copy(lf_hbm.at[0, pt_ref[b, j]], buf_ref.at[j], sem)

    def issue(j, carry):
        page_copy(j).start()
        return carry

    def drain(j, carry):
        page_copy(j).wait()
        return carry

    lax.fori_loop(0, n_pages, issue, 0)
    lax.fori_loop(0, n_pages, drain, 0)

    x = buf_ref[...]
    lane = lax.broadcasted_iota(I32, x.shape, 2)
    s = x
    shift = 1
    while shift < w:
        s = s + jnp.where(lane + shift < w, pltpu.roll(s, w - shift, axis=2), 0.0)
        shift *= 2
    within = s - x
    after = jnp.zeros((H_A, w), F32)
    for j in range(n_pages - 1, -1, -1):
        o_ref[j] = within[j] + after
        after = after + jnp.broadcast_to(s[j][:, 0:1], (H_A, w))


def _lfsuf(page_table, lf_pages):
    nb, n_pages = page_table.shape
    w = lf_pages.shape[3]
    return pl.pallas_call(
        _lfsuf_kernel,
        grid_spec=pltpu.PrefetchScalarGridSpec(
            num_scalar_prefetch=1,
            grid=(nb,),
            in_specs=[pl.BlockSpec(memory_space=pl.ANY)],
            out_specs=pl.BlockSpec((None, n_pages, H_A, w), lambda b, pt: (b, 0, 0, 0)),
            scratch_shapes=[pltpu.VMEM((n_pages, H_A, w), F32), pltpu.SemaphoreType.DMA(())]),
        out_shape=jax.ShapeDtypeStruct((nb, n_pages, H_A, w), F32),
        compiler_params=_cparams(("arbitrary",)),
        name="lfsuf",
    )(page_table, lf_pages)


def _foxdec_kernel(pt_ref, q_ref, kn_ref, vn_ref, lfn_ref, suf_ref, ck_hbm, cv_hbm, o_ref,
                   kbuf, vbuf, sem, m_ref, l_ref, acc_ref):
    b = pl.program_id(0)
    n_pages = suf_ref.shape[0]
    page_rows = kbuf.shape[-1]
    n_chunks = n_pages // DEC_PAGES
    tq = q_ref.shape[1]

    total = pl.num_programs(0) * n_chunks

    def page_copies(g, j):
        slot = g % DEC_RING
        pg = pt_ref[g // n_chunks, (g % n_chunks) * DEC_PAGES + j]
        return (pltpu.make_async_copy(ck_hbm.at[0, pg], kbuf.at[slot, j], sem.at[0, slot]),
                pltpu.make_async_copy(cv_hbm.at[0, pg], vbuf.at[slot, j], sem.at[1, slot]))

    def fetch(g):
        for j in range(DEC_PAGES):
            for cp in page_copies(g, j):
                cp.start()

    def wait(g):
        for j in range(DEC_PAGES):
            for cp in page_copies(g, j):
                cp.wait()

    q = q_ref[...].astype(BF16)
    m_ref[...] = jnp.full_like(m_ref, NEG)
    l_ref[...] = jnp.zeros_like(l_ref)
    acc_ref[...] = jnp.zeros_like(acc_ref)

    def attend(kt, vt, bias, ok):
        s = jnp.einsum('htd,hds->hts', q, kt, preferred_element_type=F32) + bias
        if ok is not None:
            s = jnp.where(ok, s, NEG)
        m_prev = m_ref[...]
        m_new = jnp.maximum(m_prev, jnp.max(s, axis=2, keepdims=True))
        alpha = jnp.exp(m_prev - m_new)
        p = jnp.exp(s - m_new)
        l_ref[...] = alpha * l_ref[...] + jnp.sum(p, axis=2, keepdims=True)
        acc_ref[...] = alpha * acc_ref[...] + jnp.einsum('hts,hds->htd', p.astype(BF16), vt,
                                                         preferred_element_type=F32)
        m_ref[...] = m_new

    @pl.when(b == 0)
    def _():
        for g in range(DEC_RING - 1):
            fetch(g)

    def chunk(c, carry):
        g = b * n_chunks + c
        slot = g % DEC_RING

        @pl.when(g + DEC_RING - 1 < total)
        def _():
            fetch(g + DEC_RING - 1)

        wait(g)
        kt = jnp.concatenate([kbuf[slot, j] for j in range(DEC_PAGES)], axis=2).astype(BF16)
        vt = jnp.concatenate([vbuf[slot, j] for j in range(DEC_PAGES)], axis=2).astype(BF16)
        suf = suf_ref[pl.ds(c * DEC_PAGES, DEC_PAGES)]
        bias = jnp.concatenate([suf[j] for j in range(DEC_PAGES)], axis=1)
        attend(kt, vt, bias[:, None, :], None)
        return carry

    lax.fori_loop(0, n_chunks, chunk, 0)

    r = lax.broadcasted_iota(I32, (page_rows, page_rows), 0)
    c = lax.broadcasted_iota(I32, (page_rows, page_rows), 1)
    fn = _dot(lfn_ref[...], jnp.where(r <= c, 1.0, 0.0).astype(F32), precision=HIGHEST)
    qi = lax.broadcasted_iota(I32, (H_A, tq, page_rows), 1)
    ki = lax.broadcasted_iota(I32, (H_A, tq, page_rows), 2)
    attend(kn_ref[...].astype(BF16), vn_ref[...].astype(BF16), -fn[:, None, :], ki <= qi)
    o_ref[...] = acc_ref[...] / l_ref[...]


def _foxdec(page_table, q, kn, vn, lfn, suf, cache_kt, cache_vt):
    nb, _, tq, _ = q.shape
    n_pages = page_table.shape[1]
    page_rows = cache_kt.shape[4]
    per_seq = lambda shape: pl.BlockSpec((None,) + shape, lambda b, pt: (b,) + (0,) * len(shape))
    page = (H_A, DH_A, page_rows)
    return pl.pallas_call(
        _foxdec_kernel,
        grid_spec=pltpu.PrefetchScalarGridSpec(
            num_scalar_prefetch=1,
            grid=(nb,),
            in_specs=[per_seq((H_A, tq, DH_A)), per_seq(page), per_seq(page), per_seq((H_A, page_rows)),
                      per_seq((n_pages, H_A, page_rows)),
                      pl.BlockSpec(memory_space=pl.ANY), pl.BlockSpec(memory_space=pl.ANY)],
            out_specs=per_seq((H_A, tq, DH_A)),
            scratch_shapes=[pltpu.VMEM((DEC_RING, DEC_PAGES) + page, F32),
                            pltpu.VMEM((DEC_RING, DEC_PAGES) + page, F32),
                            pltpu.SemaphoreType.DMA((2, DEC_RING)),
                            pltpu.VMEM((H_A, tq, 1), F32), pltpu.VMEM((H_A, tq, 1), F32),
                            pltpu.VMEM((H_A, tq, DH_A), F32)]),
        out_shape=jax.ShapeDtypeStruct((nb, H_A, tq, DH_A), F32),
        compiler_params=_cparams(("arbitrary",)),
        name="foxdec",
    )(page_table, q, kn, vn, lfn, suf, cache_kt, cache_vt)


def _gla_kernel(q_ref, k_ref, g_ref, v_ref, r_ref, s0_ref, gn_ref, y_ref, sfin_ref, state_ref):
    tb = pl.program_id(1)
    n_tb = pl.num_programs(1)
    c = GLA_CHUNK
    n_chunks = q_ref.shape[0] // c
    n_sub = c // GLA_SUB

    row_k = lax.broadcasted_iota(I32, (WK_B, WV_B), 0) // DK_B
    col_v = lax.broadcasted_iota(I32, (WK_B, WV_B), 1) // DV_B
    state_diag = row_k == col_v

    @pl.when(tb == 0)
    def _():
        state_ref[...] = jnp.zeros_like(state_ref)
        for h in range(H_B):
            state_ref[h * DK_B:(h + 1) * DK_B, h * DV_B:(h + 1) * DV_B] = s0_ref[h]

    r = lax.broadcasted_iota(I32, (c, c), 0)
    s = lax.broadcasted_iota(I32, (c, c), 1)
    same_sub = (r // GLA_SUB) == (s // GLA_SUB)
    cum_ops = jnp.concatenate([
        jnp.where(same_sub & (s <= r), 1.0, 0.0),
        jnp.where(s <= r, 1.0, 0.0),
        jnp.where(same_sub, 1.0, 0.0),
    ], axis=0).astype(F32)

    t_i = lax.broadcasted_iota(I32, (c, H_B * c), 0)
    s_i = lax.broadcasted_iota(I32, (c, H_B * c), 1) % c
    dsub = t_i // GLA_SUB - s_i // GLA_SUB
    intra = (dsub == 0) & (s_i <= t_i)
    head_k = lax.broadcasted_iota(I32, (c, WK_B), 1) // DK_B
    head_v = lax.broadcasted_iota(I32, (c, WV_B), 1) // DV_B

    def stack_heads(x, head_of_lane):
        zero = jnp.zeros_like(x)
        return jnp.concatenate([jnp.where(head_of_lane == h, x, zero) for h in range(H_B)], axis=0)

    def shift_rows(x, n):
        return jnp.concatenate([x[:n], x[:c - n]], axis=0)

    gn = gn_ref[...]
    for ci in range(n_chunks):
        rows = slice(ci * c, (ci + 1) * c)
        q = q_ref[rows, :]
        k = k_ref[rows, :]
        g = g_ref[rows, :]
        v = v_ref[rows, :]

        cums = _dot(cum_ops, g, precision=HIGHEST)
        bl, bc, tot = cums[:c], cums[c:2 * c], cums[2 * c:]
        pre = bc - bl
        q_loc = q * jnp.exp(bl)
        k_loc = k * jnp.exp(-bl)
        k_end = k * jnp.exp(tot - bl)
        q_far = [q_loc] + [q_loc * jnp.exp(pre - shift_rows(pre, GLA_SUB * d)) for d in range(1, n_sub - 1)]

        a0 = _dot_nt(q_loc.astype(BF16), stack_heads(k_loc, head_k).astype(BF16))
        af = _dot_nt(jnp.concatenate(q_far, axis=0).astype(BF16), stack_heads(k_end, head_k).astype(BF16))
        att = jnp.where(intra, a0, 0.0)
        for d in range(1, n_sub):
            att = jnp.where(dsub == d, af[(d - 1) * c:d * c], att)

        q_in = q * jnp.exp(bc)
        o = _dot(att.astype(BF16), stack_heads(v, head_v)) + _dot(q_in.astype(BF16), state_ref[...].astype(BF16))

        k_out = k * jnp.exp(bc[c - 1:c, :] - bc)
        kv = _dot_tn(k_out.astype(BF16), v)
        decay = jnp.exp(_dot_tn(g, jnp.ones((c, LANES), F32), precision=HIGHEST))
        state_ref[...] = (state_ref[...] * jnp.concatenate([decay] * (WV_B // LANES), axis=1)
                          + jnp.where(state_diag, kv, 0.0))

        for h in range(H_B):
            lanes = slice(h * DV_B, (h + 1) * DV_B)
            oh = o[:, lanes]
            yh = oh * lax.rsqrt(jnp.mean(oh * oh, axis=-1, keepdims=True) + EPS) * gn
            y_ref[rows, lanes] = (yh * r_ref[rows, lanes].astype(F32)).astype(y_ref.dtype)

    @pl.when(tb == n_tb - 1)
    def _():
        for h in range(H_B):
            sfin_ref[h] = state_ref[h * DK_B:(h + 1) * DK_B, h * DV_B:(h + 1) * DV_B]


def _gla(q, k, g, v, r, s0, gn, batch, seq, tb_rows):
    n_tb = seq // tb_rows
    row = lambda w: pl.BlockSpec((tb_rows, w), lambda b, t: (b * n_tb + t, 0))
    state = pl.BlockSpec((None, H_B, DK_B, DV_B), lambda b, t: (b, 0, 0, 0))
    return pl.pallas_call(
        _gla_kernel,
        grid=(batch, n_tb),
        in_specs=[row(WK_B), row(WK_B), row(WK_B), row(WV_B), row(WV_B), state,
                  pl.BlockSpec((1, DV_B), lambda b, t: (0, 0))],
        out_specs=[row(WV_B), state],
        out_shape=[jax.ShapeDtypeStruct((batch * seq, WV_B), BF16),
                   jax.ShapeDtypeStruct((batch, H_B, DK_B, DV_B), F32)],
        scratch_shapes=[pltpu.VMEM((WK_B, WV_B), F32)],
        compiler_params=_cparams(("arbitrary", "arbitrary")),
        name="gla",
    )(q, k, g, v, r, s0, gn)


def _mixout_kernel(*refs, n_tail):
    (x_ref, ya_ref, yb_ref, ga_ref, gb_ref, mod_ref, g2_ref, wpa_ref, wpb_ref, wo_ref, rw_ref) = refs[:11]
    tail_refs = refs[11:11 + n_tail]
    x1_ref, hp_ref, lg_ref = refs[11 + n_tail:]
    i = pl.program_id(0)
    n_own = pl.num_programs(0) - (1 if n_tail else 0)

    @pl.when(i < n_own)
    def _():
        br_a = _dot(ya_ref[...], wpa_ref[...])
        br_b = _dot(yb_ref[...], wpb_ref[...])
        merged = ga_ref[...].astype(F32) * br_a + gb_ref[...].astype(F32) * br_b
        out = _dot(merged.astype(BF16), wo_ref[...])
        x1 = x_ref[...] + mod_ref[2] * out
        x1_ref[...] = x1
        y = x1 * lax.rsqrt(jnp.mean(x1 * x1, axis=-1, keepdims=True) + EPS) * g2_ref[...]
        h = y * (1.0 + mod_ref[4]) + mod_ref[3]
        hp_ref[...] = _pack_bf16_pair(h)
        h_hi = h.astype(BF16)
        h_lo = (h - h_hi.astype(F32)).astype(BF16)
        lg_ref[...] = _dot_nt(rw_ref[0], h_hi) + (_dot_nt(rw_ref[0], h_lo) + _dot_nt(rw_ref[1], h_hi))

    if n_tail:
        @pl.when(i == n_own)
        def _():
            for src, dst in zip(tail_refs, (x1_ref, hp_ref, lg_ref)):
                dst[...] = src[...]


def _mixout(x, ya, yb, ga, gb, mod, rows_per_mod, g2, wpa, wpb, wo, rw, tail):
    n = x.shape[0]
    tm = MOE_TILE
    mod_rows = mod.shape[2]
    tiles_per_mod = rows_per_mod // tm
    n_own = n // tm
    n_total = n + (tm if tail else 0)
    own = lambda i: jnp.minimum(i, n_own - 1)
    row = lambda w: pl.BlockSpec((tm, w), lambda i: (own(i), 0))
    return pl.pallas_call(
        functools.partial(_mixout_kernel, n_tail=len(tail)),
        grid=(n_total // tm,),
        in_specs=[row(D_MODEL), row(W_A), row(WV_B), row(D_MODEL), row(D_MODEL),
                  pl.BlockSpec((6, None, mod_rows, D_MODEL), lambda i: (0, own(i) // tiles_per_mod, 0, 0)),
                  _const_spec(g2.shape), _const_spec(wpa.shape), _const_spec(wpb.shape), _const_spec(wo.shape),
                  _const_spec(rw.shape)] + [_const_spec(t.shape) for t in tail],
        out_specs=[pl.BlockSpec((tm, D_MODEL), lambda i: (i, 0)),
                   pl.BlockSpec((tm, D_MODEL // 2), lambda i: (i, 0)),
                   pl.BlockSpec((N_EXPERTS, tm), lambda i: (0, i))],
        out_shape=[jax.ShapeDtypeStruct((n_total, D_MODEL), F32),
                   jax.ShapeDtypeStruct((n_total, D_MODEL // 2), U32),
                   jax.ShapeDtypeStruct((N_EXPERTS, n_total), F32)],
        compiler_params=_cparams(("arbitrary",)),
        name="mixout",
    )(x, ya, yb, ga, gb, mod, g2, wpa, wpb, wo, rw, *tail)


def _route_kernel(lg_ref, bias_ref, idx_ref, w_ref, rank_ref, cnt_ref, carry_ref):
    i = pl.program_id(0)
    tm = lg_ref.shape[1]

    @pl.when(i == 0)
    def _():
        carry_ref[...] = jnp.zeros_like(carry_ref)

    s = _sigmoid(lg_ref[...])
    sb = s + bias_ref[...][:, 0:1]
    ninf = -jnp.inf

    e_in_group = lax.broadcasted_iota(I32, (GROUP_SIZE, tm), 0)
    gscores = []
    for g in range(N_GROUPS):
        blk = sb[g * GROUP_SIZE:(g + 1) * GROUP_SIZE, :]
        m1 = jnp.max(blk, axis=0, keepdims=True)
        first = jnp.min(jnp.where(blk == m1, e_in_group, GROUP_SIZE), axis=0, keepdims=True)
        m2 = jnp.max(jnp.where(e_in_group == first, ninf, blk), axis=0, keepdims=True)
        gscores.append(m1 + m2)
    gs = jnp.concatenate(gscores, axis=0)

    g_iota = lax.broadcasted_iota(I32, (N_GROUPS, tm), 0)
    chosen = jnp.zeros((N_GROUPS, tm), F32)
    for _ in range(TOPK_GROUPS):
        m = jnp.max(gs, axis=0, keepdims=True)
        first = jnp.min(jnp.where(gs == m, g_iota, N_GROUPS), axis=0, keepdims=True)
        pick = g_iota == first
        chosen = jnp.where(pick, 1.0, chosen)
        gs = jnp.where(pick, ninf, gs)
    allowed = jnp.concatenate(
        [jnp.broadcast_to(chosen[g:g + 1, :], (GROUP_SIZE, tm)) for g in range(N_GROUPS)], axis=0) > 0.5
    cand = jnp.where(allowed, sb, ninf)

    e_iota = lax.broadcasted_iota(I32, (N_EXPERTS, tm), 0)
    onehot = jnp.zeros((N_EXPERTS, tm), F32)
    idxs, ws = [], []
    for _ in range(TOP_K):
        m = jnp.max(cand, axis=0, keepdims=True)
        first = jnp.min(jnp.where(cand == m, e_iota, N_EXPERTS), axis=0, keepdims=True)
        pick = e_iota == first
        idxs.append(first)
        ws.append(jnp.sum(jnp.where(pick, s, 0.0), axis=0, keepdims=True))
        onehot = jnp.where(pick, 1.0, onehot)
        cand = jnp.where(pick, ninf, cand)
    w = jnp.concatenate(ws, axis=0)
    w_ref[...] = w / jnp.sum(w, axis=0, keepdims=True) * ROUTE_SCALE
    idx_ref[...] = jnp.concatenate(idxs, axis=0)

    r = lax.broadcasted_iota(I32, (tm, tm), 0)
    c = lax.broadcasted_iota(I32, (tm, tm), 1)
    earlier = jnp.where(r < c, 1.0, 0.0).astype(BF16)
    carry = carry_ref[...]
    before = _dot(onehot.astype(BF16), earlier) + jnp.concatenate([carry] * (tm // LANES), axis=1)
    rank_ref[...] = jnp.concatenate(
        [jnp.sum(jnp.where(e_iota == ix, before, 0.0), axis=0, keepdims=True) for ix in idxs], axis=0).astype(I32)
    carry = carry + _dot(onehot.astype(BF16), jnp.ones((tm, LANES), BF16))
    carry_ref[...] = carry
    cnt_ref[...] = carry


def _route(logits_t, bias):
    n = logits_t.shape[1]
    tm = MOE_TILE
    tok = lambda: pl.BlockSpec((TOP_K, tm), lambda i: (0, i))
    return pl.pallas_call(
        _route_kernel,
        grid=(n // tm,),
        in_specs=[pl.BlockSpec((N_EXPERTS, tm), lambda i: (0, i)), _const_spec(bias.shape)],
        out_specs=[tok(), tok(), tok(), _const_spec((N_EXPERTS, LANES))],
        out_shape=[jax.ShapeDtypeStruct((TOP_K, n), I32), jax.ShapeDtypeStruct((TOP_K, n), F32),
                   jax.ShapeDtypeStruct((TOP_K, n), I32), jax.ShapeDtypeStruct((N_EXPERTS, LANES), F32)],
        scratch_shapes=[pltpu.VMEM((N_EXPERTS, LANES), F32)],
        compiler_params=_cparams(("arbitrary",)),
        name="route",
    )(logits_t, bias)


def _dest_kernel(idx_ref, rank_ref, cnt_ref, dest_ref, start_ref):
    i = pl.program_id(0)
    tm = idx_ref.shape[1]

    @pl.when(i == 0)
    def _():
        cnt = cnt_ref[...]
        padded = jnp.floor((cnt + (EXPERT_TILE - 1)) * (1.0 / EXPERT_TILE)) * EXPERT_TILE
        r = lax.broadcasted_iota(I32, (N_EXPERTS, N_EXPERTS), 0)
        c = lax.broadcasted_iota(I32, (N_EXPERTS, N_EXPERTS), 1)
        start_ref[...] = _dot(jnp.where(c < r, 1.0, 0.0).astype(F32), padded, precision=HIGHEST)

    start = jnp.concatenate([start_ref[...]] * (tm // LANES), axis=1)
    e_iota = lax.broadcasted_iota(I32, (N_EXPERTS, tm), 0)
    idx = idx_ref[...]
    first = jnp.concatenate(
        [jnp.sum(jnp.where(e_iota == idx[kk:kk + 1, :], start, 0.0), axis=0, keepdims=True) for kk in range(TOP_K)],
        axis=0)
    dest_ref[...] = first.astype(I32) + rank_ref[...]


def _dest(idx_t, rank_t, counts):
    n = idx_t.shape[1]
    tm = MOE_TILE
    tok = lambda: pl.BlockSpec((TOP_K, tm), lambda i: (0, i))
    return pl.pallas_call(
        _dest_kernel,
        grid=(n // tm,),
        in_specs=[tok(), tok(), _const_spec(counts.shape)],
        out_specs=pl.BlockSpec((None, TOP_K, tm), lambda i: (i, 0, 0)),
        out_shape=jax.ShapeDtypeStruct((n // tm, TOP_K, tm), I32),
        scratch_shapes=[pltpu.VMEM((N_EXPERTS, LANES), F32)],
        compiler_params=_cparams(("arbitrary",)),
        name="dest",
    )(idx_t, rank_t, counts)


def _sc_worker():
    return lax.axis_index("s") * SC_CORES + lax.axis_index("c")


def _sc_mesh():
    return plsc.VectorSubcoreMesh(core_axis_name="c", subcore_axis_name="s")


def _dispatch(dest_tiles, hp, n_rows_out):
    n_tok_tiles = dest_tiles.shape[0]
    w = hp.shape[1]
    iters = -(-n_tok_tiles // SC_WORKERS)

    def body(hp_hbm, dest_hbm, xs_hbm, idx_v, rows_v, sem):
        wid = _sc_worker()

        @pl.loop(0, iters)
        def _(j):
            tile = wid + SC_WORKERS * j

            @pl.when(tile < n_tok_tiles)
            def _():
                pltpu.sync_copy(dest_hbm.at[tile], idx_v)
                pltpu.sync_copy(hp_hbm.at[pl.ds(tile * MOE_TILE, MOE_TILE)], rows_v)
                copies = [pltpu.async_copy(rows_v, xs_hbm.at[idx_v.at[kk]], sem) for kk in range(TOP_K)]
                for cp in copies:
                    cp.wait()

    return pl.kernel(
        body,
        out_type=jax.ShapeDtypeStruct((n_rows_out, w), U32),
        mesh=_sc_mesh(),
        scratch_types=[pltpu.VMEM((TOP_K, MOE_TILE), I32), pltpu.VMEM((MOE_TILE, w), U32), pltpu.SemaphoreType.DMA],
        name="dispatch",
    )(hp, dest_tiles)


def _gather_back(ys, dest_tiles):
    n_tok_tiles = dest_tiles.shape[0]
    w = ys.shape[1]
    n_lists = n_tok_tiles * TOP_K
    iters = -(-n_lists // SC_WORKERS)

    def body(ys_hbm, dest_hbm, out_hbm, idx_v, rows_v, sem):
        wid = _sc_worker()

        @pl.loop(0, iters)
        def _(j):
            lst = wid + SC_WORKERS * j

            @pl.when(lst < n_lists)
            def _():
                pltpu.sync_copy(dest_hbm.at[pl.ds(lst * MOE_TILE, MOE_TILE)], idx_v)
                pltpu.async_copy(ys_hbm.at[idx_v], rows_v, sem).wait()
                pltpu.sync_copy(rows_v, out_hbm.at[pl.ds(lst * MOE_TILE, MOE_TILE)])

    out = pl.kernel(
        body,
        out_type=jax.ShapeDtypeStruct((n_lists * MOE_TILE, w), U32),
        mesh=_sc_mesh(),
        scratch_types=[pltpu.VMEM((MOE_TILE,), I32), pltpu.VMEM((MOE_TILE, w), U32), pltpu.SemaphoreType.DMA],
        name="gather_back",
    )(ys, dest_tiles.reshape(-1))
    return out.reshape(n_tok_tiles, TOP_K, MOE_TILE, w)


def _experts_kernel(first_ref, count_ref, rows_ref, xs_hbm, wg_ref, wu_ref, wd_ref, ys_hbm,
                    xbuf, ybuf, wgb_ref, wub_ref, wdb_ref, sem):
    e = pl.program_id(0)
    tm = EXPERT_TILE
    ring = EXPERT_RING
    n_e = count_ref[e]
    total = first_ref[N_EXPERTS - 1] + count_ref[N_EXPERTS - 1]

    def x_copy(t, slot):
        return pltpu.make_async_copy(xs_hbm.at[pl.ds(t * tm, tm), :], xbuf.at[slot], sem.at[0, slot])

    def y_copy(t, slot):
        return pltpu.make_async_copy(ybuf.at[slot], ys_hbm.at[pl.ds(t * tm, tm), :], sem.at[1, slot])

    @pl.when(e == 0)
    def _():
        for t in range(ring - 1):
            @pl.when(t < total)
            def _():
                x_copy(t, t).start()

    @pl.when(n_e > 0)
    def _():
        wgb_ref[...] = wg_ref[...].astype(BF16)
        wub_ref[...] = wu_ref[...].astype(BF16)
        wdb_ref[...] = wd_ref[...].astype(BF16)
        half = D_MODEL // 2

        def body(j, carry):
            t = first_ref[e] + j
            slot = t % ring
            x_copy(t, slot).wait()

            @pl.when(t + ring - 1 < total)
            def _():
                x_copy(t + ring - 1, (t + ring - 1) % ring).start()

            @pl.when(t >= ring)
            def _():
                y_copy(t - ring, slot).wait()

            xs = xbuf[slot]
            filled = lax.broadcasted_iota(I32, xs.shape, 0) < rows_ref[t]
            lo, hi = _unpack_bf16_pair(jnp.where(filled, xs, jnp.zeros_like(xs)))
            gate = _dot(lo, wgb_ref[:half, :]) + _dot(hi, wgb_ref[half:, :])
            up = _dot(lo, wub_ref[:half, :]) + _dot(hi, wub_ref[half:, :])
            act = (_silu(gate) * up).astype(BF16)
            ybuf[slot] = _pack_bf16_pair(_dot(act, wdb_ref[...]))
            y_copy(t, slot).start()
            return carry

        lax.fori_loop(0, n_e, body, 0)

    @pl.when(e == N_EXPERTS - 1)
    def _():
        for back in range(ring, 0, -1):
            @pl.when(total >= back)
            def _():
                y_copy(total - back, (total - back) % ring).wait()


def _experts(tile_first, tile_count, tile_rows, xs, wg, wu, wd):
    n_rows, w = xs.shape
    tm = EXPERT_TILE
    expert = lambda e, tf, tc, tr: (e, 0, 0)
    return pl.pallas_call(
        _experts_kernel,
        grid_spec=pltpu.PrefetchScalarGridSpec(
            num_scalar_prefetch=3,
            grid=(N_EXPERTS,),
            in_specs=[pl.BlockSpec(memory_space=pl.ANY),
                      pl.BlockSpec((None, D_MODEL, D_EXPERT), expert),
                      pl.BlockSpec((None, D_MODEL, D_EXPERT), expert),
                      pl.BlockSpec((None, D_EXPERT, D_MODEL), expert)],
            out_specs=pl.BlockSpec(memory_space=pl.ANY),
            scratch_shapes=[pltpu.VMEM((EXPERT_RING, tm, w), U32), pltpu.VMEM((EXPERT_RING, tm, w), U32),
                            pltpu.VMEM((D_MODEL, D_EXPERT), BF16), pltpu.VMEM((D_MODEL, D_EXPERT), BF16),
                            pltpu.VMEM((D_EXPERT, D_MODEL), BF16),
                            pltpu.SemaphoreType.DMA((2, EXPERT_RING))]),
        out_shape=jax.ShapeDtypeStruct((n_rows, w), U32),
        compiler_params=_cparams(("arbitrary",)),
        name="experts",
    )(tile_first, tile_count, tile_rows, xs, wg, wu, wd)


def _shared_kernel(hp_ref, sg_ref, su_ref, sd_ref, o_ref):
    half = D_MODEL // 2
    lo, hi = _unpack_bf16_pair(hp_ref[...])
    sg = sg_ref[...]
    su = su_ref[...]
    gate = _dot(lo, sg[:half]) + _dot(hi, sg[half:])
    up = _dot(lo, su[:half]) + _dot(hi, su[half:])
    o_ref[...] = _dot((_silu(gate) * up).astype(BF16), sd_ref[...]).astype(o_ref.dtype)


def _shared(hp, sg, su, sd):
    n = hp.shape[0]
    tm = MOE_TILE
    return pl.pallas_call(
        _shared_kernel,
        grid=(n // tm,),
        in_specs=[pl.BlockSpec((tm, D_MODEL // 2), lambda i: (i, 0)),
                  _const_spec(sg.shape), _const_spec(su.shape), _const_spec(sd.shape)],
        out_specs=pl.BlockSpec((tm, D_MODEL), lambda i: (i, 0)),
        out_shape=jax.ShapeDtypeStruct((n, D_MODEL), BF16),
        compiler_params=_cparams(("arbitrary",)),
        name="shared",
    )(hp, sg, su, sd)


def _combine_kernel(yt_ref, x1_ref, ysh_ref, w_ref, mod_ref, fg_ref, o_ref):
    y = ysh_ref[...].astype(F32)
    w = w_ref[...]
    for kk in range(TOP_K):
        ylo, yhi = _unpack_bf16_pair(yt_ref[kk])
        y = y + w[:, kk:kk + 1] * jnp.concatenate([ylo.astype(F32), yhi.astype(F32)], axis=1)
    x2 = x1_ref[...] + mod_ref[5] * y
    o_ref[...] = x2 * lax.rsqrt(jnp.mean(x2 * x2, axis=-1, keepdims=True) + EPS) * fg_ref[...]


def _combine(y_tok, x1, y_shared, w, mod, rows_per_mod, fg, n, row_offset):
    tm = MOE_TILE
    mod_rows = mod.shape[2]
    tiles_per_mod = rows_per_mod // tm
    off = row_offset // tm
    return pl.pallas_call(
        _combine_kernel,
        grid=(n // tm,),
        in_specs=[pl.BlockSpec((None, TOP_K, tm, D_MODEL // 2), lambda i: (i + off, 0, 0, 0)),
                  pl.BlockSpec((tm, D_MODEL), lambda i: (i + off, 0)),
                  pl.BlockSpec((tm, D_MODEL), lambda i: (i + off, 0)),
                  pl.BlockSpec((tm, TOP_K), lambda i: (i + off, 0)),
                  pl.BlockSpec((6, None, mod_rows, D_MODEL), lambda i: (0, i // tiles_per_mod, 0, 0)),
                  _const_spec(fg.shape)],
        out_specs=pl.BlockSpec((tm, D_MODEL), lambda i: (i, 0)),
        out_shape=jax.ShapeDtypeStruct((n, D_MODEL), F32),
        compiler_params=_cparams(("arbitrary",)),
        name="combine",
    )(y_tok, x1, y_shared, w, mod, fg)


def _split_w_in(w_in):
    sizes = (W_A, W_A, W_A, H_A, WK_B, WK_B, WV_B, WV_B, GK_RANK, D_MODEL, D_MODEL)
    segs, o = [], 0
    for s in sizes:
        segs.append(w_in[:, o:o + s])
        o += s
    return segs


def kernel(x_prompt, x_sample, cache_k, cache_v, cache_logf, state_gla, page_table, c_prompt, c_sample, w_ada, b_ada,
           norm1_g, w_in, b_f, w_gk2, b_gk, gla_norm_g, w_pa, w_pb, w_o, norm2_g, router_w, router_bias, w_gate, w_up,
           w_down, ws_gate, ws_up, ws_down, final_g):
    assert w_ada.shape[0] == 1, "single layer"
    bp, tp, d = x_prompt.shape
    bs, ts, _ = x_sample.shape
    n_p, n_s = bp * tp, bs * ts
    n_tot = n_p + n_s
    assert n_s == MOE_TILE and n_p % MOE_TILE == 0 and tp % FOX_TILE == 0

    q_a, k_a, v_a, f_a, q_b, k_b, v_b, r_b, gk1, gate_a, gate_b = _split_w_in(w_in[0])
    wts = (jnp.concatenate([q_a, k_a], axis=1).astype(BF16),
           v_a.astype(BF16),
           jnp.concatenate([k_a, v_a], axis=1).T.astype(BF16),
           jnp.pad(f_a, ((0, 0), (0, LANES - H_A))).astype(BF16),
           jnp.pad(b_f[0].reshape(1, H_A), ((0, 0), (0, LANES - H_A))),
           jnp.concatenate([q_b, k_b, v_b, r_b], axis=1).astype(BF16),
           jnp.pad(gk1, ((0, 0), (0, LANES - GK_RANK))).astype(BF16),
           jnp.pad(w_gk2[0], ((0, LANES - GK_RANK), (0, 0))).astype(BF16),
           b_gk[0].reshape(1, WK_B),
           jnp.concatenate([gate_a, gate_b], axis=1).astype(BF16))
    g1 = norm1_g[0].reshape(1, d)
    g2 = norm2_g[0].reshape(1, d)
    gn = gla_norm_g[0].reshape(1, DV_B)
    fg = final_g.reshape(1, d)
    wpa, wpb, wo = w_pa[0].astype(BF16), w_pb[0].astype(BF16), w_o[0].astype(BF16)
    rw_hi = router_w[0].T.astype(BF16)
    rw_t = jnp.stack([rw_hi, (router_w[0].T - rw_hi.astype(F32)).astype(BF16)])
    rbias = jnp.broadcast_to(router_bias[0].reshape(N_EXPERTS, 1), (N_EXPERTS, LANES))
    sg, su, sd = ws_gate[0].astype(BF16), ws_up[0].astype(BF16), ws_down[0].astype(BF16)

    ada = _ada(jnp.concatenate([c_prompt, c_sample], axis=0), w_ada[0], b_ada[0].reshape(1, -1))
    ada = ada.reshape(bp + bs, 6, d)
    mod_p = ada[:bp].transpose(1, 0, 2)[:, :, None, :]
    mod_s = jnp.repeat(ada[bp:], ts, axis=0).transpose(1, 0, 2)[:, None, :, :]

    xp = x_prompt.reshape(n_p, d)
    tm_p = 512
    (q, kh, vth, kt, vt, lf, fc, qb, kb, gk, vb, rb, ga_p, gb_p) = _inproj(
        xp, mod_p, tp, tp // tm_p, tm_p, g1, wts, True)
    ya_p = _fox(q, kh, vth, fc, bp, tp)
    yb_p, s_p = _gla(qb, kb, gk, vb, rb, jnp.zeros((bp, H_B, DK_B, DV_B), F32), gn, bp, tp, 256)
    k_prompt = kt.reshape(bp, H_A, DH_A, tp).transpose(0, 3, 1, 2)[None]
    v_prompt = vt.reshape(bp, H_A, DH_A, tp).transpose(0, 3, 1, 2)[None]
    logf_prompt = lf.reshape(1, bp, tp, H_A)

    xs_in = x_sample.reshape(n_s, d)
    (q, k, v, lf, qb, kb, gk, vb, rb, ga, gb) = _inproj(xs_in, mod_s, n_s, 1, n_s, g1, wts, False)
    page_rows = cache_k.shape[2]
    tq = 8
    cache_kt = jnp.transpose(cache_k, (0, 1, 3, 4, 2))
    cache_vt = jnp.transpose(cache_v, (0, 1, 3, 4, 2))
    suf = _lfsuf(page_table, jnp.transpose(cache_logf, (0, 1, 3, 2)))

    def new_kv(a):
        a = a.reshape(bs, ts, H_A, DH_A).transpose(0, 2, 3, 1)
        return jnp.pad(a, ((0, 0), (0, 0), (0, 0), (0, page_rows - ts)))

    q_dec = jnp.pad(q.astype(F32).reshape(bs, ts, H_A, DH_A).transpose(0, 2, 1, 3),
                    ((0, 0), (0, 0), (0, tq - ts), (0, 0)))
    lfn = jnp.pad(lf.reshape(bs, ts, H_A).transpose(0, 2, 1), ((0, 0), (0, 0), (0, page_rows - ts)))
    ya = _foxdec(page_table, q_dec, new_kv(k), new_kv(v), lfn, suf, cache_kt, cache_vt)
    ya = ya[:, :, :ts].transpose(0, 2, 1, 3).reshape(n_s, W_A).astype(BF16)
    pad = lambda a: jnp.pad(a.reshape(bs, ts, -1), ((0, 0), (0, GLA_CHUNK - ts), (0, 0))).reshape(bs * GLA_CHUNK, -1)
    yb, s_s = _gla(pad(qb), pad(kb), pad(gk), pad(vb), pad(rb), state_gla[0], gn, bs, GLA_CHUNK, GLA_CHUNK)
    yb = yb.reshape(bs, GLA_CHUNK, WV_B)[:, :ts].reshape(n_s, WV_B)
    k_sample = k.reshape(1, bs, ts, H_A, DH_A)
    v_sample = v.reshape(1, bs, ts, H_A, DH_A)
    logf_sample = lf.reshape(1, bs, ts, H_A)
    tail = _mixout(xs_in, ya, yb, ga, gb, mod_s, n_s, g2, wpa, wpb, wo, rw_t, ())

    x1, hp, logits_t = _mixout(xp, ya_p, yb_p, ga_p, gb_p, mod_p, tp, g2, wpa, wpb, wo, rw_t, tuple(tail))
    idx_t, w_t, rank_t, counts = _route(logits_t, rbias)
    dest_tiles = _dest(idx_t, rank_t, counts)
    padded = (counts[:, 0].astype(I32) + EXPERT_TILE - 1) // EXPERT_TILE * EXPERT_TILE
    pad_end = jnp.cumsum(padded)
    n_tiles = -(-n_tot * TOP_K // EXPERT_TILE) + N_EXPERTS
    tile_first_row = jnp.arange(n_tiles, dtype=I32) * EXPERT_TILE
    tile_expert = jnp.minimum(jnp.sum((pad_end[None, :] <= tile_first_row[:, None]).astype(I32), axis=1),
                              N_EXPERTS - 1)
    used_end = pad_end - padded + counts[:, 0].astype(I32)
    tile_rows = jnp.clip(used_end[tile_expert] - tile_first_row, 0, EXPERT_TILE).astype(I32)
    tile_first = ((pad_end - padded) // EXPERT_TILE).astype(I32)
    tile_count = (padded // EXPERT_TILE).astype(I32)

    xs_sorted = _dispatch(dest_tiles, hp, n_tiles * EXPERT_TILE)
    ys_sorted = _experts(tile_first, tile_count, tile_rows, xs_sorted, w_gate[0], w_up[0], w_down[0])
    y_tok = _gather_back(ys_sorted, dest_tiles)
    w_tok = w_t.T
    y_shared = _shared(hp, sg, su, sd)
    y_prompt = _combine(y_tok, x1, y_shared, w_tok, mod_p, tp, fg, n_p, 0)
    y_sample = _combine(y_tok, x1, y_shared, w_tok, mod_s, n_s, fg, n_s, n_p)

    return (y_prompt.reshape(bp, tp, d), y_sample.reshape(bs, ts, d),
            k_prompt, v_prompt, logf_prompt, s_p[None],
            k_sample, v_sample, logf_sample, s_s[None])
```

```python
import functools

import jax
import jax.numpy as jnp
from jax import lax
from jax.experimental import pallas as pl
from jax.experimental.pallas import tpu as pltpu
from jax.experimental.pallas import tpu_sc as plsc

F32 = jnp.float32
BF16 = jnp.bfloat16
U32 = jnp.uint32
I32 = jnp.int32

EPS = 1e-6
D_MODEL = 1024
H_A, DH_A, W_A = 8, 64, 512
H_B, DK_B, DV_B, WK_B, WV_B = 4, 64, 128, 256, 512
GK_RANK = 16
GATE_NORM = 16.0
N_EXPERTS, TOP_K, N_GROUPS, TOPK_GROUPS = 256, 8, 8, 4
GROUP_SIZE = N_EXPERTS // N_GROUPS
D_EXPERT = 256
ROUTE_SCALE = 2.5
MOE_TILE = 128
EXPERT_TILE = 256
EXPERT_RING = 4
LANES = 128
SC_CORES = 2
SC_WORKERS = 32
GLA_CHUNK = 64
GLA_SUB = 16
FOX_TILE = 512
FOX_SUB = 128
DEC_PAGES = 8
DEC_RING = 3
NEG = -1e30
LOG2E = 1.4426950408889634
VMEM_LIMIT = 56 * 1024 * 1024
HIGHEST = lax.Precision.HIGHEST

NT_DIMS = (((1,), (1,)), ((), ()))
TN_DIMS = (((0,), (0,)), ((), ()))


def _cparams(sem):
    return pltpu.CompilerParams(dimension_semantics=sem, vmem_limit_bytes=VMEM_LIMIT)


def _dot(a, b, **kw):
    return jnp.dot(a, b, preferred_element_type=F32, **kw)


def _dot_nt(a, b, **kw):
    return lax.dot_general(a, b, NT_DIMS, preferred_element_type=F32, **kw)


def _dot_tn(a, b, **kw):
    return lax.dot_general(a, b, TN_DIMS, preferred_element_type=F32, **kw)


def _sigmoid(x):
    return 0.5 * jnp.tanh(0.5 * x) + 0.5


def _silu(x):
    return x * _sigmoid(x)


def _log_sigmoid(x):
    return jnp.minimum(x, 0.0) - jnp.log(1.0 + jnp.exp(-jnp.abs(x)))


def _pack_bf16_pair(x):
    c = x.shape[1] // 2
    bits = pltpu.bitcast(x.astype(BF16).astype(F32), U32)
    return (bits[:, c:] & jnp.uint32(0xFFFF0000)) | (bits[:, :c] >> 16)


def _unpack_bf16_pair(p):
    lo = pltpu.bitcast(p << 16, F32).astype(BF16)
    hi = pltpu.bitcast(p & jnp.uint32(0xFFFF0000), F32).astype(BF16)
    return lo, hi


def _ada_kernel(c_ref, w_ref, b_ref, o_ref):
    c = c_ref[...]
    o_ref[...] = _dot(_silu(c).astype(BF16), w_ref[...].astype(BF16)) + b_ref[...]


def _ada(c, w, b):
    n, d = c.shape
    cols = w.shape[1]
    tn = 1536
    return pl.pallas_call(
        _ada_kernel,
        grid=(cols // tn,),
        in_specs=[pl.BlockSpec((n, d), lambda j: (0, 0)),
                  pl.BlockSpec((d, tn), lambda j: (0, j)),
                  pl.BlockSpec((1, tn), lambda j: (0, j))],
        out_specs=pl.BlockSpec((n, tn), lambda j: (0, j)),
        out_shape=jax.ShapeDtypeStruct((n, cols), F32),
        compiler_params=_cparams(("arbitrary",)),
        name="ada",
    )(c, w, b)


def _inproj_kernel(*refs, seq_tiles, prompt):
    (x_ref, mod_ref, g_ref, wqk_ref, wv_ref, wkvt_ref, wf_ref, bf_ref, wb_ref, wgk1_ref, wgk2_ref, bgk_ref,
     wgate_ref) = refs[:13]
    if prompt:
        (q_ref, kh_ref, vth_ref, kt_ref, vt_ref, lf_ref, fc_ref, qb_ref, kb_ref, gk_ref, vb_ref, rb_ref, ga_ref,
         gb_ref, carry_ref) = refs[13:]
    else:
        q_ref, k_ref, v_ref, lf_ref, qb_ref, kb_ref, gk_ref, vb_ref, rb_ref, ga_ref, gb_ref = refs[13:]
    i = pl.program_id(0)
    tm = x_ref.shape[0]
    x = x_ref[...]
    y = x * lax.rsqrt(jnp.mean(x * x, axis=-1, keepdims=True) + EPS) * g_ref[...]
    h = (y * (1.0 + mod_ref[1]) + mod_ref[0]).astype(BF16)

    z = _dot(h, wqk_ref[...])
    q_ref[...] = (z[:, :W_A] * (DH_A ** -0.5 * (LOG2E if prompt else 1.0))).astype(BF16)
    k = z[:, W_A:]

    lf = _log_sigmoid(_dot(h, wf_ref[...]) + bf_ref[...])
    lf_ref[...] = lf[:, :H_A]

    if prompt:
        kvt = _dot_nt(wkvt_ref[...], h)
        kt_ref[...] = kvt[:W_A]
        vt_ref[...] = kvt[W_A:]
        vth_ref[...] = kvt[W_A:].astype(BF16)
        kh_ref[...] = k.astype(BF16)

        @pl.when(i % seq_tiles == 0)
        def _():
            carry_ref[...] = jnp.zeros_like(carry_ref)

        r = lax.broadcasted_iota(I32, (tm, tm), 0)
        c = lax.broadcasted_iota(I32, (tm, tm), 1)
        lower = jnp.where(c <= r, 1.0, 0.0).astype(F32)
        carry = carry_ref[...]
        fc = _dot(lower, lf, precision=HIGHEST) + carry[0:1, :]
        fc_ref[...] = fc[:, :H_A] * LOG2E
        carry_ref[...] = carry + _dot(jnp.ones((8, tm), F32), lf, precision=HIGHEST)
    else:
        k_ref[...] = k
        v_ref[...] = _dot(h, wv_ref[...])

    z = _dot(h, wb_ref[...])
    qb_ref[...] = z[:, :WK_B] * DK_B ** -0.5
    kb_ref[...] = z[:, WK_B:2 * WK_B]
    vb_ref[...] = z[:, 2 * WK_B:2 * WK_B + WV_B].astype(BF16)
    rb_ref[...] = _silu(z[:, 2 * WK_B + WV_B:]).astype(BF16)

    lr = _dot(h, wgk1_ref[...]).astype(BF16)
    gk_ref[...] = _log_sigmoid(_dot(lr, wgk2_ref[...]) + bgk_ref[...]) * (1.0 / GATE_NORM)

    z = _dot(h, wgate_ref[...])
    ga_ref[...] = _sigmoid(z[:, :D_MODEL]).astype(BF16)
    gb_ref[...] = _sigmoid(z[:, D_MODEL:]).astype(BF16)


def _const_spec(shape):
    return pl.BlockSpec(shape, lambda i: (0,) * len(shape))


def _inproj(x, mod, rows_per_mod, seq_tiles, tm, g, wts, prompt):
    n = x.shape[0]
    mod_rows = mod.shape[2]
    tiles_per_mod = rows_per_mod // tm
    row = lambda w: pl.BlockSpec((tm, w), lambda i: (i, 0))
    if prompt:
        n_seq = n // (seq_tiles * tm)
        seq_t = pl.BlockSpec((None, W_A, tm), lambda i: (i // seq_tiles, 0, i % seq_tiles))
        kv_shape = (n_seq, W_A, seq_tiles * tm)
        outs = [
            (row(W_A), (n, W_A), BF16),
            (row(W_A), (n, W_A), BF16),
            (pl.BlockSpec((W_A, tm), lambda i: (0, i)), (W_A, n), BF16),
            (seq_t, kv_shape, F32),
            (seq_t, kv_shape, F32),
            (row(H_A), (n, H_A), F32),
            (row(H_A), (n, H_A), F32),
        ]
    else:
        outs = [
            (row(W_A), (n, W_A), BF16),
            (row(W_A), (n, W_A), F32),
            (row(W_A), (n, W_A), F32),
            (row(H_A), (n, H_A), F32),
        ]
    outs += [
        (row(WK_B), (n, WK_B), F32),
        (row(WK_B), (n, WK_B), F32),
        (row(WK_B), (n, WK_B), F32),
        (row(WV_B), (n, WV_B), BF16),
        (row(WV_B), (n, WV_B), BF16),
        (row(D_MODEL), (n, D_MODEL), BF16),
        (row(D_MODEL), (n, D_MODEL), BF16),
    ]
    return pl.pallas_call(
        functools.partial(_inproj_kernel, seq_tiles=seq_tiles, prompt=prompt),
        grid=(n // tm,),
        in_specs=[row(D_MODEL),
                  pl.BlockSpec((6, None, mod_rows, D_MODEL), lambda i: (0, i // tiles_per_mod, 0, 0)),
                  _const_spec(g.shape)] + [_const_spec(w.shape) for w in wts],
        out_specs=[o[0] for o in outs],
        out_shape=[jax.ShapeDtypeStruct(o[1], o[2]) for o in outs],
        scratch_shapes=[pltpu.VMEM((8, LANES), F32)] if prompt else [],
        compiler_params=_cparams(("arbitrary",)),
        name="inproj",
    )(x, mod, g, *wts)


def _fox_kernel(q_ref, k_ref, vt_ref, fc_ref, o_ref, qs_ref, m_ref, l_ref, acc_ref):
    qi = pl.program_id(1)
    ki = pl.program_id(2)
    t = q_ref.shape[0]
    pairs = H_A // 2
    sub = FOX_SUB

    @pl.when(ki == 0)
    def _():
        lane = lax.broadcasted_iota(I32, (t, LANES), 1)
        for hp in range(pairs):
            q = q_ref[:, hp * LANES:(hp + 1) * LANES]
            qs_ref[hp, :t, :] = jnp.where(lane < DH_A, q, jnp.zeros_like(q))
            qs_ref[hp, t:, :] = jnp.where(lane >= DH_A, q, jnp.zeros_like(q))
        m_ref[...] = jnp.full_like(m_ref, NEG)
        l_ref[...] = jnp.zeros_like(l_ref)
        acc_ref[...] = jnp.zeros_like(acc_ref)

    def step(diagonal):
        for hp in range(pairs):
            k = k_ref[:, hp * LANES:(hp + 1) * LANES]
            vt = vt_ref[hp * LANES:(hp + 1) * LANES, :]
            for h2 in range(2):
                head = 2 * hp + h2
                fb = jnp.broadcast_to(fc_ref[:, head:head + 1], (t, sub))
                for sb in range(t // sub):
                    cols = slice(h2 * t + sb * sub, h2 * t + (sb + 1) * sub)
                    s = _dot_nt(k, qs_ref[hp, cols, :]) - fb
                    if diagonal:
                        key = lax.broadcasted_iota(I32, (t, sub), 0)
                        qry = lax.broadcasted_iota(I32, (t, sub), 1) + sb * sub
                        s = jnp.where(key <= qry, s, NEG)
                    m_prev = m_ref[hp, :, cols]
                    m_new = jnp.maximum(m_prev, jnp.max(s, axis=0, keepdims=True))
                    alpha = jnp.exp2(m_prev - m_new)
                    p = jnp.exp2(s - m_new)
                    l_ref[hp, :, cols] = alpha * l_ref[hp, :, cols] + jnp.sum(p, axis=0, keepdims=True)
                    acc_ref[hp, :, cols] = alpha * acc_ref[hp, :, cols] + _dot(vt, p.astype(BF16))
                    m_ref[hp, :, cols] = m_new

    @pl.when(ki < qi)
    def _():
        step(False)

    @pl.when(ki == qi)
    def _():
        step(True)
        for hp in range(pairs):
            o = acc_ref[hp] / l_ref[hp]
            o = jnp.concatenate([o[:DH_A, :t], o[DH_A:, t:]], axis=0)
            o_ref[:, hp * LANES:(hp + 1) * LANES] = o.T.astype(o_ref.dtype)


def _fox(q, k, vt, fc, batch, seq):
    t = FOX_TILE
    nq = seq // t
    kv_map = lambda b, qi, ki: (b * nq + jnp.minimum(ki, qi), 0)
    return pl.pallas_call(
        _fox_kernel,
        grid=(batch, nq, nq),
        in_specs=[pl.BlockSpec((t, W_A), lambda b, qi, ki: (b * nq + qi, 0)),
                  pl.BlockSpec((t, W_A), kv_map),
                  pl.BlockSpec((W_A, t), lambda b, qi, ki: (0, b * nq + jnp.minimum(ki, qi))),
                  pl.BlockSpec((t, H_A), kv_map)],
        out_specs=pl.BlockSpec((t, W_A), lambda b, qi, ki: (b * nq + qi, 0)),
        out_shape=jax.ShapeDtypeStruct((batch * seq, W_A), BF16),
        scratch_shapes=[pltpu.VMEM((H_A // 2, 2 * t, LANES), BF16),
                        pltpu.VMEM((H_A // 2, 1, 2 * t), F32),
                        pltpu.VMEM((H_A // 2, 1, 2 * t), F32),
                        pltpu.VMEM((H_A // 2, LANES, 2 * t), F32)],
        compiler_params=_cparams(("arbitrary", "arbitrary", "arbitrary")),
        name="fox",
    )(q, k, vt, fc)


def _lfsuf_kernel(pt_ref, lf_hbm, o_ref, buf_ref, sem):
    b = pl.program_id(0)
    n_pages, _, w = buf_ref.shape

    def page_copy(j):
        return pltpu.make_async_copy(lf_hbm.at[0, pt_ref[b, j]], buf_ref.at[j], sem)

    def issue(j, carry):
        page_copy(j).start()
        return carry

    def drain(j, carry):
        page_copy(j).wait()
        return carry

    lax.fori_loop(0, n_pages, issue, 0)
    lax.fori_loop(0, n_pages, drain, 0)

    x = buf_ref[...]
    lane = lax.broadcasted_iota(I32, x.shape, 2)
    s = x
    shift = 1
    while shift < w:
        s = s + jnp.where(lane + shift < w, pltpu.roll(s, w - shift, axis=2), 0.0)
        shift *= 2
    within = s - x
    after = jnp.zeros((H_A, w), F32)
    for j in range(n_pages - 1, -1, -1):
        o_ref[j] = within[j] + after
        after = after + jnp.broadcast_to(s[j][:, 0:1], (H_A, w))


def _lfsuf(page_table, lf_pages):
    nb, n_pages = page_table.shape
    w = lf_pages.shape[3]
    return pl.pallas_call(
        _lfsuf_kernel,
        grid_spec=pltpu.PrefetchScalarGridSpec(
            num_scalar_prefetch=1,
            grid=(nb,),
            in_specs=[pl.BlockSpec(memory_space=pl.ANY)],
            out_specs=pl.BlockSpec((None, n_pages, H_A, w), lambda b, pt: (b, 0, 0, 0)),
            scratch_shapes=[pltpu.VMEM((n_pages, H_A, w), F32), pltpu.SemaphoreType.DMA(())]),
        out_shape=jax.ShapeDtypeStruct((nb, n_pages, H_A, w), F32),
        compiler_params=_cparams(("arbitrary",)),
        name="lfsuf",
    )(page_table, lf_pages)


def _foxdec_kernel(pt_ref, q_ref, kn_ref, vn_ref, lfn_ref, suf_ref, ck_hbm, cv_hbm, o_ref,
                   kbuf, vbuf, sem, m_ref, l_ref, acc_ref):
    b = pl.program_id(0)
    n_pages = suf_ref.shape[0]
    page_rows = kbuf.shape[-1]
    n_chunks = n_pages // DEC_PAGES
    tq = q_ref.shape[1]

    total = pl.num_programs(0) * n_chunks

    def page_copies(g, j):
        slot = g % DEC_RING
        pg = pt_ref[g // n_chunks, (g % n_chunks) * DEC_PAGES + j]
        return (pltpu.make_async_copy(ck_hbm.at[0, pg], kbuf.at[slot, j], sem.at[0, slot]),
                pltpu.make_async_copy(cv_hbm.at[0, pg], vbuf.at[slot, j], sem.at[1, slot]))

    def fetch(g):
        for j in range(DEC_PAGES):
            for cp in page_copies(g, j):
                cp.start()

    def wait(g):
        for j in range(DEC_PAGES):
            for cp in page_copies(g, j):
                cp.wait()

    q = q_ref[...].astype(BF16)
    m_ref[...] = jnp.full_like(m_ref, NEG)
    l_ref[...] = jnp.zeros_like(l_ref)
    acc_ref[...] = jnp.zeros_like(acc_ref)

    def attend(kt, vt, bias, ok):
        s = jnp.einsum('htd,hds->hts', q, kt, preferred_element_type=F32) + bias
        if ok is not None:
            s = jnp.where(ok, s, NEG)
        m_prev = m_ref[...]
        m_new = jnp.maximum(m_prev, jnp.max(s, axis=2, keepdims=True))
        alpha = jnp.exp(m_prev - m_new)
        p = jnp.exp(s - m_new)
        l_ref[...] = alpha * l_ref[...] + jnp.sum(p, axis=2, keepdims=True)
        acc_ref[...] = alpha * acc_ref[...] + jnp.einsum('hts,hds->htd', p.astype(BF16), vt,
                                                         preferred_element_type=F32)
        m_ref[...] = m_new

    @pl.when(b == 0)
    def _():
        for g in range(DEC_RING - 1):
            fetch(g)

    def chunk(c, carry):
        g = b * n_chunks + c
        slot = g % DEC_RING

        @pl.when(g + DEC_RING - 1 < total)
        def _():
            fetch(g + DEC_RING - 1)

        wait(g)
        kt = jnp.concatenate([kbuf[slot, j] for j in range(DEC_PAGES)], axis=2).astype(BF16)
        vt = jnp.concatenate([vbuf[slot, j] for j in range(DEC_PAGES)], axis=2).astype(BF16)
        suf = suf_ref[pl.ds(c * DEC_PAGES, DEC_PAGES)]
        bias = jnp.concatenate([suf[j] for j in range(DEC_PAGES)], axis=1)
        attend(kt, vt, bias[:, None, :], None)
        return carry

    lax.fori_loop(0, n_chunks, chunk, 0)

    r = lax.broadcasted_iota(I32, (page_rows, page_rows), 0)
    c = lax.broadcasted_iota(I32, (page_rows, page_rows), 1)
    fn = _dot(lfn_ref[...], jnp.where(r <= c, 1.0, 0.0).astype(F32), precision=HIGHEST)
    qi = lax.broadcasted_iota(I32, (H_A, tq, page_rows), 1)
    ki = lax.broadcasted_iota(I32, (H_A, tq, page_rows), 2)
    attend(kn_ref[...].astype(BF16), vn_ref[...].astype(BF16), -fn[:, None, :], ki <= qi)
    o_ref[...] = acc_ref[...] / l_ref[...]


def _foxdec(page_table, q, kn, vn, lfn, suf, cache_kt, cache_vt):
    nb, _, tq, _ = q.shape
    n_pages = page_table.shape[1]
    page_rows = cache_kt.shape[4]
    per_seq = lambda shape: pl.BlockSpec((None,) + shape, lambda b, pt: (b,) + (0,) * len(shape))
    page = (H_A, DH_A, page_rows)
    return pl.pallas_call(
        _foxdec_kernel,
        grid_spec=pltpu.PrefetchScalarGridSpec(
            num_scalar_prefetch=1,
            grid=(nb,),
            in_specs=[per_seq((H_A, tq, DH_A)), per_seq(page), per_seq(page), per_seq((H_A, page_rows)),
                      per_seq((n_pages, H_A, page_rows)),
                      pl.BlockSpec(memory_space=pl.ANY), pl.BlockSpec(memory_space=pl.ANY)],
            out_specs=per_seq((H_A, tq, DH_A)),
            scratch_shapes=[pltpu.VMEM((DEC_RING, DEC_PAGES) + page, F32),
                            pltpu.VMEM((DEC_RING, DEC_PAGES) + page, F32),
                            pltpu.SemaphoreType.DMA((2, DEC_RING)),
                            pltpu.VMEM((H_A, tq, 1), F32), pltpu.VMEM((H_A, tq, 1), F32),
                            pltpu.VMEM((H_A, tq, DH_A), F32)]),
        out_shape=jax.ShapeDtypeStruct((nb, H_A, tq, DH_A), F32),
        compiler_params=_cparams(("arbitrary",)),
        name="foxdec",
    )(page_table, q, kn, vn, lfn, suf, cache_kt, cache_vt)


def _gla_kernel(q_ref, k_ref, g_ref, v_ref, r_ref, s0_ref, gn_ref, y_ref, sfin_ref, state_ref):
    tb = pl.program_id(1)
    n_tb = pl.num_programs(1)
    c = GLA_CHUNK
    n_chunks = q_ref.shape[0] // c
    n_sub = c // GLA_SUB

    row_k = lax.broadcasted_iota(I32, (WK_B, WV_B), 0) // DK_B
    col_v = lax.broadcasted_iota(I32, (WK_B, WV_B), 1) // DV_B
    state_diag = row_k == col_v

    @pl.when(tb == 0)
    def _():
        state_ref[...] = jnp.zeros_like(state_ref)
        for h in range(H_B):
            state_ref[h * DK_B:(h + 1) * DK_B, h * DV_B:(h + 1) * DV_B] = s0_ref[h]

    r = lax.broadcasted_iota(I32, (c, c), 0)
    s = lax.broadcasted_iota(I32, (c, c), 1)
    same_sub = (r // GLA_SUB) == (s // GLA_SUB)
    cum_ops = jnp.concatenate([
        jnp.where(same_sub & (s <= r), 1.0, 0.0),
        jnp.where(s <= r, 1.0, 0.0),
        jnp.where(same_sub, 1.0, 0.0),
    ], axis=0).astype(F32)

    t_i = lax.broadcasted_iota(I32, (c, H_B * c), 0)
    s_i = lax.broadcasted_iota(I32, (c, H_B * c), 1) % c
    dsub = t_i // GLA_SUB - s_i // GLA_SUB
    intra = (dsub == 0) & (s_i <= t_i)
    head_k = lax.broadcasted_iota(I32, (c, WK_B), 1) // DK_B
    head_v = lax.broadcasted_iota(I32, (c, WV_B), 1) // DV_B

    def stack_heads(x, head_of_lane):
        zero = jnp.zeros_like(x)
        return jnp.concatenate([jnp.where(head_of_lane == h, x, zero) for h in range(H_B)], axis=0)

    def shift_rows(x, n):
        return jnp.concatenate([x[:n], x[:c - n]], axis=0)

    gn = gn_ref[...]
    for ci in range(n_chunks):
        rows = slice(ci * c, (ci + 1) * c)
        q = q_ref[rows, :]
        k = k_ref[rows, :]
        g = g_ref[rows, :]
        v = v_ref[rows, :]

        cums = _dot(cum_ops, g, precision=HIGHEST)
        bl, bc, tot = cums[:c], cums[c:2 * c], cums[2 * c:]
        pre = bc - bl
        q_loc = q * jnp.exp(bl)
        k_loc = k * jnp.exp(-bl)
        k_end = k * jnp.exp(tot - bl)
        q_far = [q_loc] + [q_loc * jnp.exp(pre - shift_rows(pre, GLA_SUB * d)) for d in range(1, n_sub - 1)]

        a0 = _dot_nt(q_loc.astype(BF16), stack_heads(k_loc, head_k).astype(BF16))
        af = _dot_nt(jnp.concatenate(q_far, axis=0).astype(BF16), stack_heads(k_end, head_k).astype(BF16))
        att = jnp.where(intra, a0, 0.0)
        for d in range(1, n_sub):
            att = jnp.where(dsub == d, af[(d - 1) * c:d * c], att)

        q_in = q * jnp.exp(bc)
        o = _dot(att.astype(BF16), stack_heads(v, head_v)) + _dot(q_in.astype(BF16), state_ref[...].astype(BF16))

        k_out = k * jnp.exp(bc[c - 1:c, :] - bc)
        kv = _dot_tn(k_out.astype(BF16), v)
        decay = jnp.exp(_dot_tn(g, jnp.ones((c, LANES), F32), precision=HIGHEST))
        state_ref[...] = (state_ref[...] * jnp.concatenate([decay] * (WV_B // LANES), axis=1)
                          + jnp.where(state_diag, kv, 0.0))

        for h in range(H_B):
            lanes = slice(h * DV_B, (h + 1) * DV_B)
            oh = o[:, lanes]
            yh = oh * lax.rsqrt(jnp.mean(oh * oh, axis=-1, keepdims=True) + EPS) * gn
            y_ref[rows, lanes] = (yh * r_ref[rows, lanes].astype(F32)).astype(y_ref.dtype)

    @pl.when(tb == n_tb - 1)
    def _():
        for h in range(H_B):
            sfin_ref[h] = state_ref[h * DK_B:(h + 1) * DK_B, h * DV_B:(h + 1) * DV_B]


def _gla(q, k, g, v, r, s0, gn, batch, seq, tb_rows):
    n_tb = seq // tb_rows
    row = lambda w: pl.BlockSpec((tb_rows, w), lambda b, t: (b * n_tb + t, 0))
    state = pl.BlockSpec((None, H_B, DK_B, DV_B), lambda b, t: (b, 0, 0, 0))
    return pl.pallas_call(
        _gla_kernel,
        grid=(batch, n_tb),
        in_specs=[row(WK_B), row(WK_B), row(WK_B), row(WV_B), row(WV_B), state,
                  pl.BlockSpec((1, DV_B), lambda b, t: (0, 0))],
        out_specs=[row(WV_B), state],
        out_shape=[jax.ShapeDtypeStruct((batch * seq, WV_B), BF16),
                   jax.ShapeDtypeStruct((batch, H_B, DK_B, DV_B), F32)],
        scratch_shapes=[pltpu.VMEM((WK_B, WV_B), F32)],
        compiler_params=_cparams(("arbitrary", "arbitrary")),
        name="gla",
    )(q, k, g, v, r, s0, gn)


def _mixout_kernel(x_ref, ya_ref, yb_ref, ga_ref, gb_ref, mod_ref, g2_ref, wpa_ref, wpb_ref, wo_ref, rw_ref,
                   x1_ref, hp_ref, lg_ref):
    br_a = _dot(ya_ref[...], wpa_ref[...])
    br_b = _dot(yb_ref[...], wpb_ref[...])
    merged = ga_ref[...].astype(F32) * br_a + gb_ref[...].astype(F32) * br_b
    out = _dot(merged.astype(BF16), wo_ref[...])
    x1 = x_ref[...] + mod_ref[2] * out
    x1_ref[...] = x1
    y = x1 * lax.rsqrt(jnp.mean(x1 * x1, axis=-1, keepdims=True) + EPS) * g2_ref[...]
    h = y * (1.0 + mod_ref[4]) + mod_ref[3]
    hp_ref[...] = _pack_bf16_pair(h)
    h_hi = h.astype(BF16)
    h_lo = (h - h_hi.astype(F32)).astype(BF16)
    lg_ref[...] = _dot_nt(rw_ref[0], h_hi) + (_dot_nt(rw_ref[0], h_lo) + _dot_nt(rw_ref[1], h_hi))


def _mixout(x, ya, yb, ga, gb, mod, rows_per_mod, tm, g2, wpa, wpb, wo, rw):
    n = x.shape[0]
    mod_rows = mod.shape[2]
    tiles_per_mod = rows_per_mod // tm
    row = lambda w: pl.BlockSpec((tm, w), lambda i: (i, 0))
    return pl.pallas_call(
        _mixout_kernel,
        grid=(n // tm,),
        in_specs=[row(D_MODEL), row(W_A), row(WV_B), row(D_MODEL), row(D_MODEL),
                  pl.BlockSpec((6, None, mod_rows, D_MODEL), lambda i: (0, i // tiles_per_mod, 0, 0)),
                  _const_spec(g2.shape), _const_spec(wpa.shape), _const_spec(wpb.shape), _const_spec(wo.shape),
                  _const_spec(rw.shape)],
        out_specs=[row(D_MODEL), row(D_MODEL // 2), pl.BlockSpec((N_EXPERTS, tm), lambda i: (0, i))],
        out_shape=[jax.ShapeDtypeStruct((n, D_MODEL), F32),
                   jax.ShapeDtypeStruct((n, D_MODEL // 2), U32),
                   jax.ShapeDtypeStruct((N_EXPERTS, n), F32)],
        compiler_params=_cparams(("arbitrary",)),
        name="mixout",
    )(x, ya, yb, ga, gb, mod, g2, wpa, wpb, wo, rw)


def _route_kernel(lg_ref, lg_last_ref, bias_ref, idx_ref, w_ref, rank_ref, cnt_ref, carry_ref):
    i = pl.program_id(0)
    tm = lg_ref.shape[1]

    @pl.when(i == 0)
    def _():
        carry_ref[...] = jnp.zeros_like(carry_ref)

    lg = jnp.where(i == pl.num_programs(0) - 1, lg_last_ref[...], lg_ref[...])
    s = _sigmoid(lg)
    sb = s + bias_ref[...][:, 0:1]
    ninf = -jnp.inf

    e_in_group = lax.broadcasted_iota(I32, (GROUP_SIZE, tm), 0)
    gscores = []
    for g in range(N_GROUPS):
        blk = sb[g * GROUP_SIZE:(g + 1) * GROUP_SIZE, :]
        m1 = jnp.max(blk, axis=0, keepdims=True)
        first = jnp.min(jnp.where(blk == m1, e_in_group, GROUP_SIZE), axis=0, keepdims=True)
        m2 = jnp.max(jnp.where(e_in_group == first, ninf, blk), axis=0, keepdims=True)
        gscores.append(m1 + m2)
    gs = jnp.concatenate(gscores, axis=0)

    g_iota = lax.broadcasted_iota(I32, (N_GROUPS, tm), 0)
    chosen = jnp.zeros((N_GROUPS, tm), F32)
    for _ in range(TOPK_GROUPS):
        m = jnp.max(gs, axis=0, keepdims=True)
        first = jnp.min(jnp.where(gs == m, g_iota, N_GROUPS), axis=0, keepdims=True)
        pick = g_iota == first
        chosen = jnp.where(pick, 1.0, chosen)
        gs = jnp.where(pick, ninf, gs)
    allowed = jnp.concatenate(
        [jnp.broadcast_to(chosen[g:g + 1, :], (GROUP_SIZE, tm)) for g in range(N_GROUPS)], axis=0) > 0.5
    cand = jnp.where(allowed, sb, ninf)

    e_iota = lax.broadcasted_iota(I32, (N_EXPERTS, tm), 0)
    onehot = jnp.zeros((N_EXPERTS, tm), F32)
    idxs, ws = [], []
    for _ in range(TOP_K):
        m = jnp.max(cand, axis=0, keepdims=True)
        first = jnp.min(jnp.where(cand == m, e_iota, N_EXPERTS), axis=0, keepdims=True)
        pick = e_iota == first
        idxs.append(first)
        ws.append(jnp.sum(jnp.where(pick, s, 0.0), axis=0, keepdims=True))
        onehot = jnp.where(pick, 1.0, onehot)
        cand = jnp.where(pick, ninf, cand)
    w = jnp.concatenate(ws, axis=0)
    w_ref[...] = w / jnp.sum(w, axis=0, keepdims=True) * ROUTE_SCALE
    idx_ref[...] = jnp.concatenate(idxs, axis=0)

    r = lax.broadcasted_iota(I32, (tm, tm), 0)
    c = lax.broadcasted_iota(I32, (tm, tm), 1)
    earlier = jnp.where(r < c, 1.0, 0.0).astype(BF16)
    carry = carry_ref[...]
    before = _dot(onehot.astype(BF16), earlier) + jnp.concatenate([carry] * (tm // LANES), axis=1)
    rank_ref[...] = jnp.concatenate(
        [jnp.sum(jnp.where(e_iota == ix, before, 0.0), axis=0, keepdims=True) for ix in idxs], axis=0).astype(I32)
    carry = carry + _dot(onehot.astype(BF16), jnp.ones((tm, LANES), BF16))
    carry_ref[...] = carry
    cnt_ref[...] = carry


def _route(logits_t, logits_last, bias):
    tm = MOE_TILE
    n_first = logits_t.shape[1] // tm
    n = logits_t.shape[1] + logits_last.shape[1]
    tok = lambda: pl.BlockSpec((TOP_K, tm), lambda i: (0, i))
    return pl.pallas_call(
        _route_kernel,
        grid=(n // tm,),
        in_specs=[pl.BlockSpec((N_EXPERTS, tm), lambda i: (0, jnp.minimum(i, n_first - 1))),
                  _const_spec(logits_last.shape), _const_spec(bias.shape)],
        out_specs=[tok(), tok(), tok(), _const_spec((N_EXPERTS, LANES))],
        out_shape=[jax.ShapeDtypeStruct((TOP_K, n), I32), jax.ShapeDtypeStruct((TOP_K, n), F32),
                   jax.ShapeDtypeStruct((TOP_K, n), I32), jax.ShapeDtypeStruct((N_EXPERTS, LANES), F32)],
        scratch_shapes=[pltpu.VMEM((N_EXPERTS, LANES), F32)],
        compiler_params=_cparams(("arbitrary",)),
        name="route",
    )(logits_t, logits_last, bias)


def _dest_kernel(idx_ref, rank_ref, cnt_ref, dest_ref, start_ref):
    i = pl.program_id(0)
    tm = idx_ref.shape[1]

    @pl.when(i == 0)
    def _():
        cnt = cnt_ref[...]
        padded = jnp.floor((cnt + (EXPERT_TILE - 1)) * (1.0 / EXPERT_TILE)) * EXPERT_TILE
        r = lax.broadcasted_iota(I32, (N_EXPERTS, N_EXPERTS), 0)
        c = lax.broadcasted_iota(I32, (N_EXPERTS, N_EXPERTS), 1)
        start_ref[...] = _dot(jnp.where(c < r, 1.0, 0.0).astype(F32), padded, precision=HIGHEST)

    start = jnp.concatenate([start_ref[...]] * (tm // LANES), axis=1)
    e_iota = lax.broadcasted_iota(I32, (N_EXPERTS, tm), 0)
    idx = idx_ref[...]
    first = jnp.concatenate(
        [jnp.sum(jnp.where(e_iota == idx[kk:kk + 1, :], start, 0.0), axis=0, keepdims=True) for kk in range(TOP_K)],
        axis=0)
    dest_ref[...] = first.astype(I32) + rank_ref[...]


def _dest(idx_t, rank_t, counts):
    n = idx_t.shape[1]
    tm = MOE_TILE
    tok = lambda: pl.BlockSpec((TOP_K, tm), lambda i: (0, i))
    return pl.pallas_call(
        _dest_kernel,
        grid=(n // tm,),
        in_specs=[tok(), tok(), _const_spec(counts.shape)],
        out_specs=pl.BlockSpec((None, TOP_K, tm), lambda i: (i, 0, 0)),
        out_shape=jax.ShapeDtypeStruct((n // tm, TOP_K, tm), I32),
        scratch_shapes=[pltpu.VMEM((N_EXPERTS, LANES), F32)],
        compiler_params=_cparams(("arbitrary",)),
        name="dest",
    )(idx_t, rank_t, counts)


def _sc_worker():
    return lax.axis_index("s") * SC_CORES + lax.axis_index("c")


def _sc_mesh():
    return plsc.VectorSubcoreMesh(core_axis_name="c", subcore_axis_name="s")


def _dispatch(dest_tiles, hp, hp_last, n_rows_out):
    n_tok_tiles = dest_tiles.shape[0]
    n_first = hp.shape[0] // MOE_TILE
    assert n_first + hp_last.shape[0] // MOE_TILE == n_tok_tiles
    w = hp.shape[1]
    iters = -(-n_tok_tiles // SC_WORKERS)

    def body(hp_hbm, hp_last_hbm, dest_hbm, xs_hbm, idx_v, rows_v, sem):
        wid = _sc_worker()

        def scatter():
            copies = [pltpu.async_copy(rows_v, xs_hbm.at[idx_v.at[kk]], sem) for kk in range(TOP_K)]
            for cp in copies:
                cp.wait()

        @pl.loop(0, iters)
        def _(j):
            tile = wid + SC_WORKERS * j

            @pl.when(tile < n_first)
            def _():
                pltpu.sync_copy(dest_hbm.at[tile], idx_v)
                pltpu.sync_copy(hp_hbm.at[pl.ds(tile * MOE_TILE, MOE_TILE)], rows_v)
                scatter()

            @pl.when((tile >= n_first) & (tile < n_tok_tiles))
            def _():
                pltpu.sync_copy(dest_hbm.at[tile], idx_v)
                pltpu.sync_copy(hp_last_hbm.at[pl.ds((tile - n_first) * MOE_TILE, MOE_TILE)], rows_v)
                scatter()

    return pl.kernel(
        body,
        out_type=jax.ShapeDtypeStruct((n_rows_out, w), U32),
        mesh=_sc_mesh(),
        scratch_types=[pltpu.VMEM((TOP_K, MOE_TILE), I32), pltpu.VMEM((MOE_TILE, w), U32), pltpu.SemaphoreType.DMA],
        name="dispatch",
    )(hp, hp_last, dest_tiles)


def _gather_back(ys, dest_tiles):
    n_tok_tiles = dest_tiles.shape[0]
    w = ys.shape[1]
    n_lists = n_tok_tiles * TOP_K
    iters = -(-n_lists // SC_WORKERS)

    def body(ys_hbm, dest_hbm, out_hbm, idx_v, rows_v, sem):
        wid = _sc_worker()

        @pl.loop(0, iters)
        def _(j):
            lst = wid + SC_WORKERS * j

            @pl.when(lst < n_lists)
            def _():
                pltpu.sync_copy(dest_hbm.at[pl.ds(lst * MOE_TILE, MOE_TILE)], idx_v)
                pltpu.async_copy(ys_hbm.at[idx_v], rows_v, sem).wait()
                pltpu.sync_copy(rows_v, out_hbm.at[pl.ds(lst * MOE_TILE, MOE_TILE)])

    out = pl.kernel(
        body,
        out_type=jax.ShapeDtypeStruct((n_lists * MOE_TILE, w), U32),
        mesh=_sc_mesh(),
        scratch_types=[pltpu.VMEM((MOE_TILE,), I32), pltpu.VMEM((MOE_TILE, w), U32), pltpu.SemaphoreType.DMA],
        name="gather_back",
    )(ys, dest_tiles.reshape(-1))
    return out.reshape(n_tok_tiles, TOP_K, MOE_TILE, w)


def _experts_kernel(first_ref, count_ref, rows_ref, xs_hbm, wg_ref, wu_ref, wd_ref, ys_hbm,
                    xbuf, ybuf, wgb_ref, wub_ref, wdb_ref, sem):
    e = pl.program_id(0)
    tm = EXPERT_TILE
    ring = EXPERT_RING
    n_e = count_ref[e]
    total = first_ref[N_EXPERTS - 1] + count_ref[N_EXPERTS - 1]

    def x_copy(t, slot):
        return pltpu.make_async_copy(xs_hbm.at[pl.ds(t * tm, tm), :], xbuf.at[slot], sem.at[0, slot])

    def y_copy(t, slot):
        return pltpu.make_async_copy(ybuf.at[slot], ys_hbm.at[pl.ds(t * tm, tm), :], sem.at[1, slot])

    @pl.when(e == 0)
    def _():
        for t in range(ring - 1):
            @pl.when(t < total)
            def _():
                x_copy(t, t).start()

    @pl.when(n_e > 0)
    def _():
        wgb_ref[...] = wg_ref[...].astype(BF16)
        wub_ref[...] = wu_ref[...].astype(BF16)
        wdb_ref[...] = wd_ref[...].astype(BF16)
        half = D_MODEL // 2

        def body(j, carry):
            t = first_ref[e] + j
            slot = t % ring
            x_copy(t, slot).wait()

            @pl.when(t + ring - 1 < total)
            def _():
                x_copy(t + ring - 1, (t + ring - 1) % ring).start()

            @pl.when(t >= ring)
            def _():
                y_copy(t - ring, slot).wait()

            xs = xbuf[slot]
            filled = lax.broadcasted_iota(I32, xs.shape, 0) < rows_ref[t]
            lo, hi = _unpack_bf16_pair(jnp.where(filled, xs, jnp.zeros_like(xs)))
            gate = _dot(lo, wgb_ref[:half, :]) + _dot(hi, wgb_ref[half:, :])
            up = _dot(lo, wub_ref[:half, :]) + _dot(hi, wub_ref[half:, :])
            act = (_silu(gate) * up).astype(BF16)
            ybuf[slot] = _pack_bf16_pair(_dot(act, wdb_ref[...]))
            y_copy(t, slot).start()
            return carry

        lax.fori_loop(0, n_e, body, 0)

    @pl.when(e == N_EXPERTS - 1)
    def _():
        for back in range(ring, 0, -1):
            @pl.when(total >= back)
            def _():
                y_copy(total - back, (total - back) % ring).wait()


def _experts(tile_first, tile_count, tile_rows, xs, wg, wu, wd):
    n_rows, w = xs.shape
    tm = EXPERT_TILE
    expert = lambda e, tf, tc, tr: (e, 0, 0)
    return pl.pallas_call(
        _experts_kernel,
        grid_spec=pltpu.PrefetchScalarGridSpec(
            num_scalar_prefetch=3,
            grid=(N_EXPERTS,),
            in_specs=[pl.BlockSpec(memory_space=pl.ANY),
                      pl.BlockSpec((None, D_MODEL, D_EXPERT), expert),
                      pl.BlockSpec((None, D_MODEL, D_EXPERT), expert),
                      pl.BlockSpec((None, D_EXPERT, D_MODEL), expert)],
            out_specs=pl.BlockSpec(memory_space=pl.ANY),
            scratch_shapes=[pltpu.VMEM((EXPERT_RING, tm, w), U32), pltpu.VMEM((EXPERT_RING, tm, w), U32),
                            pltpu.VMEM((D_MODEL, D_EXPERT), BF16), pltpu.VMEM((D_MODEL, D_EXPERT), BF16),
                            pltpu.VMEM((D_EXPERT, D_MODEL), BF16),
                            pltpu.SemaphoreType.DMA((2, EXPERT_RING))]),
        out_shape=jax.ShapeDtypeStruct((n_rows, w), U32),
        compiler_params=_cparams(("arbitrary",)),
        name="experts",
    )(tile_first, tile_count, tile_rows, xs, wg, wu, wd)


def _shared_kernel(hp_ref, sg_ref, su_ref, sd_ref, o_ref):
    half = D_MODEL // 2
    lo, hi = _unpack_bf16_pair(hp_ref[...])
    sg = sg_ref[...]
    su = su_ref[...]
    gate = _dot(lo, sg[:half]) + _dot(hi, sg[half:])
    up = _dot(lo, su[:half]) + _dot(hi, su[half:])
    o_ref[...] = _dot((_silu(gate) * up).astype(BF16), sd_ref[...]).astype(o_ref.dtype)


def _shared(hp, tm, sg, su, sd):
    n = hp.shape[0]
    return pl.pallas_call(
        _shared_kernel,
        grid=(n // tm,),
        in_specs=[pl.BlockSpec((tm, D_MODEL // 2), lambda i: (i, 0)),
                  _const_spec(sg.shape), _const_spec(su.shape), _const_spec(sd.shape)],
        out_specs=pl.BlockSpec((tm, D_MODEL), lambda i: (i, 0)),
        out_shape=jax.ShapeDtypeStruct((n, D_MODEL), BF16),
        compiler_params=_cparams(("arbitrary",)),
        name="shared",
    )(hp, sg, su, sd)


def _combine_kernel(yt_ref, x1_ref, ysh_ref, w_ref, mod_ref, fg_ref, o_ref):
    y = ysh_ref[...].astype(F32)
    w = w_ref[...]
    for kk in range(TOP_K):
        ylo, yhi = _unpack_bf16_pair(yt_ref[kk])
        y = y + w[:, kk:kk + 1] * jnp.concatenate([ylo.astype(F32), yhi.astype(F32)], axis=1)
    x2 = x1_ref[...] + mod_ref[5] * y
    o_ref[...] = x2 * lax.rsqrt(jnp.mean(x2 * x2, axis=-1, keepdims=True) + EPS) * fg_ref[...]


def _combine(y_tok, x1, y_shared, w, mod, rows_per_mod, fg, row_offset):
    n = x1.shape[0]
    tm = MOE_TILE
    mod_rows = mod.shape[2]
    tiles_per_mod = rows_per_mod // tm
    off = row_offset // tm
    return pl.pallas_call(
        _combine_kernel,
        grid=(n // tm,),
        in_specs=[pl.BlockSpec((None, TOP_K, tm, D_MODEL // 2), lambda i: (i + off, 0, 0, 0)),
                  pl.BlockSpec((tm, D_MODEL), lambda i: (i, 0)),
                  pl.BlockSpec((tm, D_MODEL), lambda i: (i, 0)),
                  pl.BlockSpec((tm, TOP_K), lambda i: (i + off, 0)),
                  pl.BlockSpec((6, None, mod_rows, D_MODEL), lambda i: (0, i // tiles_per_mod, 0, 0)),
                  _const_spec(fg.shape)],
        out_specs=pl.BlockSpec((tm, D_MODEL), lambda i: (i, 0)),
        out_shape=jax.ShapeDtypeStruct((n, D_MODEL), F32),
        compiler_params=_cparams(("arbitrary",)),
        name="combine",
    )(y_tok, x1, y_shared, w, mod, fg)


def _split_w_in(w_in):
    sizes = (W_A, W_A, W_A, H_A, WK_B, WK_B, WV_B, WV_B, GK_RANK, D_MODEL, D_MODEL)
    segs, o = [], 0
    for s in sizes:
        segs.append(w_in[:, o:o + s])
        o += s
    return segs


def kernel(x_prompt, x_sample, cache_k, cache_v, cache_logf, state_gla, page_table, c_prompt, c_sample, w_ada, b_ada,
           norm1_g, w_in, b_f, w_gk2, b_gk, gla_norm_g, w_pa, w_pb, w_o, norm2_g, router_w, router_bias, w_gate, w_up,
           w_down, ws_gate, ws_up, ws_down, final_g):
    assert w_ada.shape[0] == 1, "single layer"
    bp, tp, d = x_prompt.shape
    bs, ts, _ = x_sample.shape
    n_p, n_s = bp * tp, bs * ts
    n_tot = n_p + n_s
    assert n_s == MOE_TILE and n_p % MOE_TILE == 0 and tp % FOX_TILE == 0

    q_a, k_a, v_a, f_a, q_b, k_b, v_b, r_b, gk1, gate_a, gate_b = _split_w_in(w_in[0])
    wts = (jnp.concatenate([q_a, k_a], axis=1).astype(BF16),
           v_a.astype(BF16),
           jnp.concatenate([k_a, v_a], axis=1).T.astype(BF16),
           jnp.pad(f_a, ((0, 0), (0, LANES - H_A))).astype(BF16),
           jnp.pad(b_f[0].reshape(1, H_A), ((0, 0), (0, LANES - H_A))),
           jnp.concatenate([q_b, k_b, v_b, r_b], axis=1).astype(BF16),
           jnp.pad(gk1, ((0, 0), (0, LANES - GK_RANK))).astype(BF16),
           jnp.pad(w_gk2[0], ((0, LANES - GK_RANK), (0, 0))).astype(BF16),
           b_gk[0].reshape(1, WK_B),
           jnp.concatenate([gate_a, gate_b], axis=1).astype(BF16))
    g1 = norm1_g[0].reshape(1, d)
    g2 = norm2_g[0].reshape(1, d)
    gn = gla_norm_g[0].reshape(1, DV_B)
    fg = final_g.reshape(1, d)
    wpa, wpb, wo = w_pa[0].astype(BF16), w_pb[0].astype(BF16), w_o[0].astype(BF16)
    rw_hi = router_w[0].T.astype(BF16)
    rw_t = jnp.stack([rw_hi, (router_w[0].T - rw_hi.astype(F32)).astype(BF16)])
    rbias = jnp.broadcast_to(router_bias[0].reshape(N_EXPERTS, 1), (N_EXPERTS, LANES))
    sg, su, sd = ws_gate[0].astype(BF16), ws_up[0].astype(BF16), ws_down[0].astype(BF16)

    ada = _ada(jnp.concatenate([c_prompt, c_sample], axis=0), w_ada[0], b_ada[0].reshape(1, -1))
    ada = ada.reshape(bp + bs, 6, d)
    mod_p = ada[:bp].transpose(1, 0, 2)[:, :, None, :]
    mod_s = jnp.repeat(ada[bp:], ts, axis=0).transpose(1, 0, 2)[:, None, :, :]

    xp = x_prompt.reshape(n_p, d)
    tm_p = 512
    (q, kh, vth, kt, vt, lf, fc, qb, kb, gk, vb, rb, ga_p, gb_p) = _inproj(
        xp, mod_p, tp, tp // tm_p, tm_p, g1, wts, True)
    ya_p = _fox(q, kh, vth, fc, bp, tp)
    yb_p, s_p = _gla(qb, kb, gk, vb, rb, jnp.zeros((bp, H_B, DK_B, DV_B), F32), gn, bp, tp, 256)
    k_prompt = kt.reshape(bp, H_A, DH_A, tp).transpose(0, 3, 1, 2)[None]
    v_prompt = vt.reshape(bp, H_A, DH_A, tp).transpose(0, 3, 1, 2)[None]
    logf_prompt = lf.reshape(1, bp, tp, H_A)

    xs_in = x_sample.reshape(n_s, d)
    (q, k, v, lf, qb, kb, gk, vb, rb, ga, gb) = _inproj(xs_in, mod_s, n_s, 1, n_s, g1, wts, False)
    page_rows = cache_k.shape[2]
    tq = 8
    cache_kt = jnp.transpose(cache_k, (0, 1, 3, 4, 2))
    cache_vt = jnp.transpose(cache_v, (0, 1, 3, 4, 2))
    suf = _lfsuf(page_table, jnp.transpose(cache_logf, (0, 1, 3, 2)))

    def new_kv(a):
        a = a.reshape(bs, ts, H_A, DH_A).transpose(0, 2, 3, 1)
        return jnp.pad(a, ((0, 0), (0, 0), (0, 0), (0, page_rows - ts)))

    q_dec = jnp.pad(q.astype(F32).reshape(bs, ts, H_A, DH_A).transpose(0, 2, 1, 3),
                    ((0, 0), (0, 0), (0, tq - ts), (0, 0)))
    lfn = jnp.pad(lf.reshape(bs, ts, H_A).transpose(0, 2, 1), ((0, 0), (0, 0), (0, page_rows - ts)))
    ya = _foxdec(page_table, q_dec, new_kv(k), new_kv(v), lfn, suf, cache_kt, cache_vt)
    ya = ya[:, :, :ts].transpose(0, 2, 1, 3).reshape(n_s, W_A).astype(BF16)
    pad = lambda a: jnp.pad(a.reshape(bs, ts, -1), ((0, 0), (0, GLA_CHUNK - ts), (0, 0))).reshape(bs * GLA_CHUNK, -1)
    yb, s_s = _gla(pad(qb), pad(kb), pad(gk), pad(vb), pad(rb), state_gla[0], gn, bs, GLA_CHUNK, GLA_CHUNK)
    yb = yb.reshape(bs, GLA_CHUNK, WV_B)[:, :ts].reshape(n_s, WV_B)
    k_sample = k.reshape(1, bs, ts, H_A, DH_A)
    v_sample = v.reshape(1, bs, ts, H_A, DH_A)
    logf_sample = lf.reshape(1, bs, ts, H_A)
    x1_s, hp_s, logits_s = _mixout(xs_in, ya, yb, ga, gb, mod_s, n_s, n_s, g2, wpa, wpb, wo, rw_t)

    x1_p, hp_p, logits_p = _mixout(xp, ya_p, yb_p, ga_p, gb_p, mod_p, tp, 2 * MOE_TILE, g2, wpa, wpb, wo, rw_t)
    idx_t, w_t, rank_t, counts = _route(logits_p, logits_s, rbias)
    dest_tiles = _dest(idx_t, rank_t, counts)
    padded = (counts[:, 0].astype(I32) + EXPERT_TILE - 1) // EXPERT_TILE * EXPERT_TILE
    pad_end = jnp.cumsum(padded)
    n_tiles = -(-n_tot * TOP_K // EXPERT_TILE) + N_EXPERTS
    tile_first_row = jnp.arange(n_tiles, dtype=I32) * EXPERT_TILE
    tile_expert = jnp.minimum(jnp.sum((pad_end[None, :] <= tile_first_row[:, None]).astype(I32), axis=1),
                              N_EXPERTS - 1)
    used_end = pad_end - padded + counts[:, 0].astype(I32)
    tile_rows = jnp.clip(used_end[tile_expert] - tile_first_row, 0, EXPERT_TILE).astype(I32)
    tile_first = ((pad_end - padded) // EXPERT_TILE).astype(I32)
    tile_count = (padded // EXPERT_TILE).astype(I32)

    xs_sorted = _dispatch(dest_tiles, hp_p, hp_s, n_tiles * EXPERT_TILE)
    ys_sorted = _experts(tile_first, tile_count, tile_rows, xs_sorted, w_gate[0], w_up[0], w_down[0])
    y_tok = _gather_back(ys_sorted, dest_tiles)
    w_tok = w_t.T
    y_prompt = _combine(y_tok, x1_p, _shared(hp_p, 2 * MOE_TILE, sg, su, sd), w_tok, mod_p, tp, fg, 0)
    y_sample = _combine(y_tok, x1_s, _shared(hp_s, MOE_TILE, sg, su, sd), w_tok, mod_s, n_s, fg, n_p)

    return (y_prompt.reshape(bp, tp, d), y_sample.reshape(bs, ts, d),
            k_prompt, v_prompt, logf_prompt, s_p[None],
            k_sample, v_sample, logf_sample, s_s[None])
```

```python
import functools

import jax
import jax.numpy as jnp
from jax import lax
from jax.experimental import pallas as pl
from jax.experimental.pallas import tpu as pltpu
from jax.experimental.pallas import tpu_sc as plsc

F32 = jnp.float32
BF16 = jnp.bfloat16
U32 = jnp.uint32
I32 = jnp.int32

EPS = 1e-6
D_MODEL = 1024
H_A, DH_A, W_A = 8, 64, 512
H_B, DK_B, DV_B, WK_B, WV_B = 4, 64, 128, 256, 512
GK_RANK = 16
GATE_NORM = 16.0
N_EXPERTS, TOP_K, N_GROUPS, TOPK_GROUPS = 256, 8, 8, 4
GROUP_SIZE = N_EXPERTS // N_GROUPS
D_EXPERT = 256
ROUTE_SCALE = 2.5
MOE_TILE = 128
EXPERT_TILE = 256
EXPERT_RING = 4
LANES = 128
SC_CORES = 2
SC_WORKERS = 32
GLA_CHUNK = 64
GLA_SUB = 16
FOX_TILE = 512
FOX_SUB = 128
DEC_PAGES = 8
DEC_RING = 3
NEG = -1e30
LOG2E = 1.4426950408889634
VMEM_LIMIT = 56 * 1024 * 1024
HIGHEST = lax.Precision.HIGHEST

NT_DIMS = (((1,), (1,)), ((), ()))
TN_DIMS = (((0,), (0,)), ((), ()))


def _cparams(sem):
    return pltpu.CompilerParams(dimension_semantics=sem, vmem_limit_bytes=VMEM_LIMIT)


def _dot(a, b, **kw):
    return jnp.dot(a, b, preferred_element_type=F32, **kw)


def _dot_nt(a, b, **kw):
    return lax.dot_general(a, b, NT_DIMS, preferred_element_type=F32, **kw)


def _dot_tn(a, b, **kw):
    return lax.dot_general(a, b, TN_DIMS, preferred_element_type=F32, **kw)


def _sigmoid(x):
    return 0.5 * jnp.tanh(0.5 * x) + 0.5


def _silu(x):
    return x * _sigmoid(x)


def _log_sigmoid(x):
    return jnp.minimum(x, 0.0) - jnp.log(1.0 + jnp.exp(-jnp.abs(x)))


def _pack_bf16_pair(x):
    c = x.shape[1] // 2
    bits = pltpu.bitcast(x.astype(BF16).astype(F32), U32)
    return (bits[:, c:] & jnp.uint32(0xFFFF0000)) | (bits[:, :c] >> 16)


def _unpack_bf16_pair(p):
    lo = pltpu.bitcast(p << 16, F32).astype(BF16)
    hi = pltpu.bitcast(p & jnp.uint32(0xFFFF0000), F32).astype(BF16)
    return lo, hi


def _ada_kernel(c_ref, w_ref, b_ref, o_ref):
    c = c_ref[...]
    o_ref[...] = _dot(_silu(c).astype(BF16), w_ref[...].astype(BF16)) + b_ref[...]


def _ada(c, w, b):
    n, d = c.shape
    cols = w.shape[1]
    tn = 1536
    return pl.pallas_call(
        _ada_kernel,
        grid=(cols // tn,),
        in_specs=[pl.BlockSpec((n, d), lambda j: (0, 0)),
                  pl.BlockSpec((d, tn), lambda j: (0, j)),
                  pl.BlockSpec((1, tn), lambda j: (0, j))],
        out_specs=pl.BlockSpec((n, tn), lambda j: (0, j)),
        out_shape=jax.ShapeDtypeStruct((n, cols), F32),
        compiler_params=_cparams(("arbitrary",)),
        name="ada",
    )(c, w, b)


def _inproj_kernel(*refs, seq_tiles, prompt):
    (x_ref, mod_ref, g_ref, wqk_ref, wv_ref, wkvt_ref, wf_ref, bf_ref, wb_ref, wgk1_ref, wgk2_ref, bgk_ref,
     wgate_ref) = refs[:13]
    if prompt:
        (q_ref, kh_ref, vth_ref, kt_ref, vt_ref, lf_ref, fc_ref, qb_ref, kb_ref, gk_ref, vb_ref, rb_ref, ga_ref,
         gb_ref, carry_ref) = refs[13:]
    else:
        q_ref, k_ref, v_ref, lf_ref, qb_ref, kb_ref, gk_ref, vb_ref, rb_ref, ga_ref, gb_ref = refs[13:]
    i = pl.program_id(0)
    tm = x_ref.shape[0]
    x = x_ref[...]
    y = x * lax.rsqrt(jnp.mean(x * x, axis=-1, keepdims=True) + EPS) * g_ref[...]
    h = (y * (1.0 + mod_ref[1]) + mod_ref[0]).astype(BF16)

    z = _dot(h, wqk_ref[...])
    q_ref[...] = (z[:, :W_A] * (DH_A ** -0.5 * (LOG2E if prompt else 1.0))).astype(BF16)
    k = z[:, W_A:]

    lf = _log_sigmoid(_dot(h, wf_ref[...]) + bf_ref[...])
    lf_ref[...] = lf[:, :H_A]

    if prompt:
        kvt = _dot_nt(wkvt_ref[...], h)
        kt_ref[...] = kvt[:W_A]
        vt_ref[...] = kvt[W_A:]
        vth_ref[...] = kvt[W_A:].astype(BF16)
        kh_ref[...] = k.astype(BF16)

        @pl.when(i % seq_tiles == 0)
        def _():
            carry_ref[...] = jnp.zeros_like(carry_ref)

        r = lax.broadcasted_iota(I32, (tm, tm), 0)
        c = lax.broadcasted_iota(I32, (tm, tm), 1)
        lower = jnp.where(c <= r, 1.0, 0.0).astype(F32)
        carry = carry_ref[...]
        fc = _dot(lower, lf, precision=HIGHEST) + carry[0:1, :]
        fc_ref[...] = fc[:, :H_A] * LOG2E
        carry_ref[...] = carry + _dot(jnp.ones((8, tm), F32), lf, precision=HIGHEST)
    else:
        k_ref[...] = k
        v_ref[...] = _dot(h, wv_ref[...])

    z = _dot(h, wb_ref[...])
    qb_ref[...] = z[:, :WK_B] * DK_B ** -0.5
    kb_ref[...] = z[:, WK_B:2 * WK_B]
    vb_ref[...] = z[:, 2 * WK_B:2 * WK_B + WV_B].astype(BF16)
    rb_ref[...] = _silu(z[:, 2 * WK_B + WV_B:]).astype(BF16)

    lr = _dot(h, wgk1_ref[...]).astype(BF16)
    gk_ref[...] = _log_sigmoid(_dot(lr, wgk2_ref[...]) + bgk_ref[...]) * (1.0 / GATE_NORM)

    z = _dot(h, wgate_ref[...])
    ga_ref[...] = _sigmoid(z[:, :D_MODEL]).astype(BF16)
    gb_ref[...] = _sigmoid(z[:, D_MODEL:]).astype(BF16)


def _const_spec(shape):
    return pl.BlockSpec(shape, lambda i: (0,) * len(shape))


def _inproj(x, mod, rows_per_mod, seq_tiles, tm, g, wts, prompt):
    n = x.shape[0]
    mod_rows = mod.shape[2]
    tiles_per_mod = rows_per_mod // tm
    row = lambda w: pl.BlockSpec((tm, w), lambda i: (i, 0))
    if prompt:
        n_seq = n // (seq_tiles * tm)
        seq_t = pl.BlockSpec((None, W_A, tm), lambda i: (i // seq_tiles, 0, i % seq_tiles))
        kv_shape = (n_seq, W_A, seq_tiles * tm)
        outs = [
            (row(W_A), (n, W_A), BF16),
            (row(W_A), (n, W_A), BF16),
            (pl.BlockSpec((W_A, tm), lambda i: (0, i)), (W_A, n), BF16),
            (seq_t, kv_shape, F32),
            (seq_t, kv_shape, F32),
            (row(H_A), (n, H_A), F32),
            (row(H_A), (n, H_A), F32),
        ]
    else:
        outs = [
            (row(W_A), (n, W_A), BF16),
            (row(W_A), (n, W_A), F32),
            (row(W_A), (n, W_A), F32),
            (row(H_A), (n, H_A), F32),
        ]
    outs += [
        (row(WK_B), (n, WK_B), F32),
        (row(WK_B), (n, WK_B), F32),
        (row(WK_B), (n, WK_B), F32),
        (row(WV_B), (n, WV_B), BF16),
        (row(WV_B), (n, WV_B), BF16),
        (row(D_MODEL), (n, D_MODEL), BF16),
        (row(D_MODEL), (n, D_MODEL), BF16),
    ]
    return pl.pallas_call(
        functools.partial(_inproj_kernel, seq_tiles=seq_tiles, prompt=prompt),
        grid=(n // tm,),
        in_specs=[row(D_MODEL),
                  pl.BlockSpec((6, None, mod_rows, D_MODEL), lambda i: (0, i // tiles_per_mod, 0, 0)),
                  _const_spec(g.shape)] + [_const_spec(w.shape) for w in wts],
        out_specs=[o[0] for o in outs],
        out_shape=[jax.ShapeDtypeStruct(o[1], o[2]) for o in outs],
        scratch_shapes=[pltpu.VMEM((8, LANES), F32)] if prompt else [],
        compiler_params=_cparams(("arbitrary",)),
        name="inproj",
    )(x, mod, g, *wts)


def _fox_kernel(q_ref, k_ref, vt_ref, fc_ref, o_ref, qs_ref, m_ref, l_ref, acc_ref):
    qi = pl.program_id(1)
    ki = pl.program_id(2)
    t = q_ref.shape[0]
    pairs = H_A // 2
    sub = FOX_SUB

    @pl.when(ki == 0)
    def _():
        lane = lax.broadcasted_iota(I32, (t, LANES), 1)
        for hp in range(pairs):
            q = q_ref[:, hp * LANES:(hp + 1) * LANES]
            qs_ref[hp, :t, :] = jnp.where(lane < DH_A, q, jnp.zeros_like(q))
            qs_ref[hp, t:, :] = jnp.where(lane >= DH_A, q, jnp.zeros_like(q))
        m_ref[...] = jnp.full_like(m_ref, NEG)
        l_ref[...] = jnp.zeros_like(l_ref)
        acc_ref[...] = jnp.zeros_like(acc_ref)

    def step(diagonal):
        for hp in range(pairs):
            k = k_ref[:, hp * LANES:(hp + 1) * LANES]
            vt = vt_ref[hp * LANES:(hp + 1) * LANES, :]
            for h2 in range(2):
                head = 2 * hp + h2
                fb = jnp.broadcast_to(fc_ref[:, head:head + 1], (t, sub))
                for sb in range(t // sub):
                    cols = slice(h2 * t + sb * sub, h2 * t + (sb + 1) * sub)
                    s = _dot_nt(k, qs_ref[hp, cols, :]) - fb
                    if diagonal:
                        key = lax.broadcasted_iota(I32, (t, sub), 0)
                        qry = lax.broadcasted_iota(I32, (t, sub), 1) + sb * sub
                        s = jnp.where(key <= qry, s, NEG)
                    m_prev = m_ref[hp, :, cols]
                    m_new = jnp.maximum(m_prev, jnp.max(s, axis=0, keepdims=True))
                    alpha = jnp.exp2(m_prev - m_new)
                    p = jnp.exp2(s - m_new)
                    l_ref[hp, :, cols] = alpha * l_ref[hp, :, cols] + jnp.sum(p, axis=0, keepdims=True)
                    acc_ref[hp, :, cols] = alpha * acc_ref[hp, :, cols] + _dot(vt, p.astype(BF16))
                    m_ref[hp, :, cols] = m_new

    @pl.when(ki < qi)
    def _():
        step(False)

    @pl.when(ki == qi)
    def _():
        step(True)
        for hp in range(pairs):
            o = acc_ref[hp] / l_ref[hp]
            o = jnp.concatenate([o[:DH_A, :t], o[DH_A:, t:]], axis=0)
            o_ref[:, hp * LANES:(hp + 1) * LANES] = o.T.astype(o_ref.dtype)


def _fox(q, k, vt, fc, batch, seq):
    t = FOX_TILE
    nq = seq // t
    kv_map = lambda b, qi, ki: (b * nq + jnp.minimum(ki, qi), 0)
    return pl.pallas_call(
        _fox_kernel,
        grid=(batch, nq, nq),
        in_specs=[pl.BlockSpec((t, W_A), lambda b, qi, ki: (b * nq + qi, 0)),
                  pl.BlockSpec((t, W_A), kv_map),
                  pl.BlockSpec((W_A, t), lambda b, qi, ki: (0, b * nq + jnp.minimum(ki, qi))),
                  pl.BlockSpec((t, H_A), kv_map)],
        out_specs=pl.BlockSpec((t, W_A), lambda b, qi, ki: (b * nq + qi, 0)),
        out_shape=jax.ShapeDtypeStruct((batch * seq, W_A), BF16),
        scratch_shapes=[pltpu.VMEM((H_A // 2, 2 * t, LANES), BF16),
                        pltpu.VMEM((H_A // 2, 1, 2 * t), F32),
                        pltpu.VMEM((H_A // 2, 1, 2 * t), F32),
                        pltpu.VMEM((H_A // 2, LANES, 2 * t), F32)],
        compiler_params=_cparams(("arbitrary", "arbitrary", "arbitrary")),
        name="fox",
    )(q, k, vt, fc)


def _lfsuf_kernel(pt_ref, lf_hbm, o_ref, buf_ref, sem):
    b = pl.program_id(0)
    n_pages, _, w = buf_ref.shape

    def page_copy(j):
        return pltpu.make_async_copy(lf_hbm.at[0, pt_ref[b, j]], buf_ref.at[j], sem)

    def issue(j, carry):
        page_copy(j).start()
        return carry

    def drain(j, carry):
        page_copy(j).wait()
        return carry

    lax.fori_loop(0, n_pages, issue, 0)
    lax.fori_loop(0, n_pages, drain, 0)

    x = buf_ref[...]
    lane = lax.broadcasted_iota(I32, x.shape, 2)
    s = x
    shift = 1
    while shift < w:
        s = s + jnp.where(lane + shift < w, pltpu.roll(s, w - shift, axis=2), 0.0)
        shift *= 2
    within = s - x
    after = jnp.zeros((H_A, w), F32)
    for j in range(n_pages - 1, -1, -1):
        o_ref[j] = within[j] + after
        after = after + jnp.broadcast_to(s[j][:, 0:1], (H_A, w))


def _lfsuf(page_table, lf_pages):
    nb, n_pages = page_table.shape
    w = lf_pages.shape[3]
    return pl.pallas_call(
        _lfsuf_kernel,
        grid_spec=pltpu.PrefetchScalarGridSpec(
            num_scalar_prefetch=1,
            grid=(nb,),
            in_specs=[pl.BlockSpec(memory_space=pl.ANY)],
            out_specs=pl.BlockSpec((None, n_pages, H_A, w), lambda b, pt: (b, 0, 0, 0)),
            scratch_shapes=[pltpu.VMEM((n_pages, H_A, w), F32), pltpu.SemaphoreType.DMA(())]),
        out_shape=jax.ShapeDtypeStruct((nb, n_pages, H_A, w), F32),
        compiler_params=_cparams(("arbitrary",)),
        name="lfsuf",
    )(page_table, lf_pages)


def _foxdec_kernel(pt_ref, q_ref, kn_ref, vn_ref, lfn_ref, suf_ref, ck_hbm, cv_hbm, o_ref,
                   kbuf, vbuf, sem, m_ref, l_ref, acc_ref):
    b = pl.program_id(0)
    n_pages = suf_ref.shape[0]
    page_rows = kbuf.shape[-1]
    n_chunks = n_pages // DEC_PAGES
    tq = q_ref.shape[1]

    total = pl.num_programs(0) * n_chunks

    def page_copies(g, j):
        slot = g % DEC_RING
        pg = pt_ref[g // n_chunks, (g % n_chunks) * DEC_PAGES + j]
        return (pltpu.make_async_copy(ck_hbm.at[0, pg], kbuf.at[slot, j], sem.at[0, slot]),
                pltpu.make_async_copy(cv_hbm.at[0, pg], vbuf.at[slot, j], sem.at[1, slot]))

    def fetch(g):
        for j in range(DEC_PAGES):
            for cp in page_copies(g, j):
                cp.start()

    def wait(g):
        for j in range(DEC_PAGES):
            for cp in page_copies(g, j):
                cp.wait()

    q = q_ref[...].astype(BF16)
    m_ref[...] = jnp.full_like(m_ref, NEG)
    l_ref[...] = jnp.zeros_like(l_ref)
    acc_ref[...] = jnp.zeros_like(acc_ref)

    def attend(kt, vt, bias, ok):
        s = jnp.einsum('htd,hds->hts', q, kt, preferred_element_type=F32) + bias
        if ok is not None:
            s = jnp.where(ok, s, NEG)
        m_prev = m_ref[...]
        m_new = jnp.maximum(m_prev, jnp.max(s, axis=2, keepdims=True))
        alpha = jnp.exp(m_prev - m_new)
        p = jnp.exp(s - m_new)
        l_ref[...] = alpha * l_ref[...] + jnp.sum(p, axis=2, keepdims=True)
        acc_ref[...] = alpha * acc_ref[...] + jnp.einsum('hts,hds->htd', p.astype(BF16), vt,
                                                         preferred_element_type=F32)
        m_ref[...] = m_new

    @pl.when(b == 0)
    def _():
        for g in range(DEC_RING - 1):
            fetch(g)

    def chunk(c, carry):
        g = b * n_chunks + c
        slot = g % DEC_RING

        @pl.when(g + DEC_RING - 1 < total)
        def _():
            fetch(g + DEC_RING - 1)

        wait(g)
        kt = jnp.concatenate([kbuf[slot, j] for j in range(DEC_PAGES)], axis=2).astype(BF16)
        vt = jnp.concatenate([vbuf[slot, j] for j in range(DEC_PAGES)], axis=2).astype(BF16)
        suf = suf_ref[pl.ds(c * DEC_PAGES, DEC_PAGES)]
        bias = jnp.concatenate([suf[j] for j in range(DEC_PAGES)], axis=1)
        attend(kt, vt, bias[:, None, :], None)
        return carry

    lax.fori_loop(0, n_chunks, chunk, 0)

    r = lax.broadcasted_iota(I32, (page_rows, page_rows), 0)
    c = lax.broadcasted_iota(I32, (page_rows, page_rows), 1)
    fn = _dot(lfn_ref[...], jnp.where(r <= c, 1.0, 0.0).astype(F32), precision=HIGHEST)
    qi = lax.broadcasted_iota(I32, (H_A, tq, page_rows), 1)
    ki = lax.broadcasted_iota(I32, (H_A, tq, page_rows), 2)
    attend(kn_ref[...].astype(BF16), vn_ref[...].astype(BF16), -fn[:, None, :], ki <= qi)
    o_ref[...] = acc_ref[...] / l_ref[...]


def _foxdec(page_table, q, kn, vn, lfn, suf, cache_kt, cache_vt):
    nb, _, tq, _ = q.shape
    n_pages = page_table.shape[1]
    page_rows = cache_kt.shape[4]
    per_seq = lambda shape: pl.BlockSpec((None,) + shape, lambda b, pt: (b,) + (0,) * len(shape))
    page = (H_A, DH_A, page_rows)
    return pl.pallas_call(
        _foxdec_kernel,
        grid_spec=pltpu.PrefetchScalarGridSpec(
            num_scalar_prefetch=1,
            grid=(nb,),
            in_specs=[per_seq((H_A, tq, DH_A)), per_seq(page), per_seq(page), per_seq((H_A, page_rows)),
                      per_seq((n_pages, H_A, page_rows)),
                      pl.BlockSpec(memory_space=pl.ANY), pl.BlockSpec(memory_space=pl.ANY)],
            out_specs=per_seq((H_A, tq, DH_A)),
            scratch_shapes=[pltpu.VMEM((DEC_RING, DEC_PAGES) + page, F32),
                            pltpu.VMEM((DEC_RING, DEC_PAGES) + page, F32),
                            pltpu.SemaphoreType.DMA((2, DEC_RING)),
                            pltpu.VMEM((H_A, tq, 1), F32), pltpu.VMEM((H_A, tq, 1), F32),
                            pltpu.VMEM((H_A, tq, DH_A), F32)]),
        out_shape=jax.ShapeDtypeStruct((nb, H_A, tq, DH_A), F32),
        compiler_params=_cparams(("arbitrary",)),
        name="foxdec",
    )(page_table, q, kn, vn, lfn, suf, cache_kt, cache_vt)


def _gla_kernel(q_ref, k_ref, g_ref, v_ref, r_ref, s0_ref, gn_ref, y_ref, sfin_ref, state_ref):
    tb = pl.program_id(1)
    n_tb = pl.num_programs(1)
    c = GLA_CHUNK
    n_chunks = q_ref.shape[0] // c
    n_sub = c // GLA_SUB

    row_k = lax.broadcasted_iota(I32, (WK_B, WV_B), 0) // DK_B
    col_v = lax.broadcasted_iota(I32, (WK_B, WV_B), 1) // DV_B
    state_diag = row_k == col_v

    @pl.when(tb == 0)
    def _():
        state_ref[...] = jnp.zeros_like(state_ref)
        for h in range(H_B):
            state_ref[h * DK_B:(h + 1) * DK_B, h * DV_B:(h + 1) * DV_B] = s0_ref[h]

    r = lax.broadcasted_iota(I32, (c, c), 0)
    s = lax.broadcasted_iota(I32, (c, c), 1)
    same_sub = (r // GLA_SUB) == (s // GLA_SUB)
    cum_ops = jnp.concatenate([
        jnp.where(same_sub & (s <= r), 1.0, 0.0),
        jnp.where(s <= r, 1.0, 0.0),
        jnp.where(same_sub, 1.0, 0.0),
    ], axis=0).astype(BF16)

    t_i = lax.broadcasted_iota(I32, (c, H_B * c), 0)
    s_i = lax.broadcasted_iota(I32, (c, H_B * c), 1) % c
    dsub = t_i // GLA_SUB - s_i // GLA_SUB
    intra = (dsub == 0) & (s_i <= t_i)
    head_k = lax.broadcasted_iota(I32, (c, WK_B), 1) // DK_B
    head_v = lax.broadcasted_iota(I32, (c, WV_B), 1) // DV_B

    def stack_heads(x, head_of_lane):
        zero = jnp.zeros_like(x)
        return jnp.concatenate([jnp.where(head_of_lane == h, x, zero) for h in range(H_B)], axis=0)

    def shift_rows(x, n):
        return jnp.concatenate([x[:n], x[:c - n]], axis=0)

    gn = gn_ref[...]
    for ci in range(n_chunks):
        rows = slice(ci * c, (ci + 1) * c)
        q = q_ref[rows, :]
        k = k_ref[rows, :]
        g = g_ref[rows, :]
        v = v_ref[rows, :]

        g_hi = g.astype(BF16)
        g_r = g - g_hi.astype(F32)
        g_mid = g_r.astype(BF16)
        g_parts = jnp.concatenate([g_hi, g_mid, (g_r - g_mid.astype(F32)).astype(BF16)], axis=1)
        cums = _dot(cum_ops, g_parts)
        cums = cums[:, :WK_B] + (cums[:, WK_B:2 * WK_B] + cums[:, 2 * WK_B:])
        bl, bc, tot = cums[:c], cums[c:2 * c], cums[2 * c:]
        pre = bc - bl
        q_loc = q * jnp.exp(bl)
        k_loc = k * jnp.exp(-bl)
        k_end = k * jnp.exp(tot - bl)
        q_far = [q_loc] + [q_loc * jnp.exp(pre - shift_rows(pre, GLA_SUB * d)) for d in range(1, n_sub - 1)]

        a0 = _dot_nt(q_loc.astype(BF16), stack_heads(k_loc, head_k).astype(BF16))
        af = _dot_nt(jnp.concatenate(q_far, axis=0).astype(BF16), stack_heads(k_end, head_k).astype(BF16))
        att = jnp.where(intra, a0, 0.0)
        for d in range(1, n_sub):
            att = jnp.where(dsub == d, af[(d - 1) * c:d * c], att)

        q_in = q * jnp.exp(bc)
        o = _dot(att.astype(BF16), stack_heads(v, head_v)) + _dot(q_in.astype(BF16), state_ref[...].astype(BF16))

        k_out = k * jnp.exp(bc[c - 1:c, :] - bc)
        kv = _dot_tn(k_out.astype(BF16), v)
        gsum = _dot_tn(g_parts, jnp.ones((c, LANES), BF16))
        decay = jnp.exp(gsum[:WK_B] + (gsum[WK_B:2 * WK_B] + gsum[2 * WK_B:]))
        state_ref[...] = (state_ref[...] * jnp.concatenate([decay] * (WV_B // LANES), axis=1)
                          + jnp.where(state_diag, kv, 0.0))

        for h in range(H_B):
            lanes = slice(h * DV_B, (h + 1) * DV_B)
            oh = o[:, lanes]
            yh = oh * lax.rsqrt(jnp.mean(oh * oh, axis=-1, keepdims=True) + EPS) * gn
            y_ref[rows, lanes] = (yh * r_ref[rows, lanes].astype(F32)).astype(y_ref.dtype)

    @pl.when(tb == n_tb - 1)
    def _():
        for h in range(H_B):
            sfin_ref[h] = state_ref[h * DK_B:(h + 1) * DK_B, h * DV_B:(h + 1) * DV_B]


def _gla(q, k, g, v, r, s0, gn, batch, seq, tb_rows):
    n_tb = seq // tb_rows
    row = lambda w: pl.BlockSpec((tb_rows, w), lambda b, t: (b * n_tb + t, 0))
    state = pl.BlockSpec((None, H_B, DK_B, DV_B), lambda b, t: (b, 0, 0, 0))
    return pl.pallas_call(
        _gla_kernel,
        grid=(batch, n_tb),
        in_specs=[row(WK_B), row(WK_B), row(WK_B), row(WV_B), row(WV_B), state,
                  pl.BlockSpec((1, DV_B), lambda b, t: (0, 0))],
        out_specs=[row(WV_B), state],
        out_shape=[jax.ShapeDtypeStruct((batch * seq, WV_B), BF16),
                   jax.ShapeDtypeStruct((batch, H_B, DK_B, DV_B), F32)],
        scratch_shapes=[pltpu.VMEM((WK_B, WV_B), F32)],
        compiler_params=_cparams(("arbitrary", "arbitrary")),
        name="gla",
    )(q, k, g, v, r, s0, gn)


def _mixout_kernel(x_ref, ya_ref, yb_ref, ga_ref, gb_ref, mod_ref, g2_ref, wpa_ref, wpb_ref, wo_ref, rw_ref,
                   x1_ref, hp_ref, lg_ref):
    br_a = _dot(ya_ref[...], wpa_ref[...])
    br_b = _dot(yb_ref[...], wpb_ref[...])
    merged = ga_ref[...].astype(F32) * br_a + gb_ref[...].astype(F32) * br_b
    out = _dot(merged.astype(BF16), wo_ref[...])
    x1 = x_ref[...] + mod_ref[2] * out
    x1_ref[...] = x1
    y = x1 * lax.rsqrt(jnp.mean(x1 * x1, axis=-1, keepdims=True) + EPS) * g2_ref[...]
    h = y * (1.0 + mod_ref[4]) + mod_ref[3]
    hp_ref[...] = _pack_bf16_pair(h)
    h_hi = h.astype(BF16)
    h_lo = (h - h_hi.astype(F32)).astype(BF16)
    lg_ref[...] = _dot_nt(rw_ref[0], h_hi) + (_dot_nt(rw_ref[0], h_lo) + _dot_nt(rw_ref[1], h_hi))


def _mixout(x, ya, yb, ga, gb, mod, rows_per_mod, tm, g2, wpa, wpb, wo, rw):
    n = x.shape[0]
    mod_rows = mod.shape[2]
    tiles_per_mod = rows_per_mod // tm
    row = lambda w: pl.BlockSpec((tm, w), lambda i: (i, 0))
    return pl.pallas_call(
        _mixout_kernel,
        grid=(n // tm,),
        in_specs=[row(D_MODEL), row(W_A), row(WV_B), row(D_MODEL), row(D_MODEL),
                  pl.BlockSpec((6, None, mod_rows, D_MODEL), lambda i: (0, i // tiles_per_mod, 0, 0)),
                  _const_spec(g2.shape), _const_spec(wpa.shape), _const_spec(wpb.shape), _const_spec(wo.shape),
                  _const_spec(rw.shape)],
        out_specs=[row(D_MODEL), row(D_MODEL // 2), pl.BlockSpec((N_EXPERTS, tm), lambda i: (0, i))],
        out_shape=[jax.ShapeDtypeStruct((n, D_MODEL), F32),
                   jax.ShapeDtypeStruct((n, D_MODEL // 2), U32),
                   jax.ShapeDtypeStruct((N_EXPERTS, n), F32)],
        compiler_params=_cparams(("arbitrary",)),
        name="mixout",
    )(x, ya, yb, ga, gb, mod, g2, wpa, wpb, wo, rw)


def _route_kernel(lg_ref, lg_last_ref, bias_ref, idx_ref, w_ref, rank_ref, cnt_ref, carry_ref):
    i = pl.program_id(0)
    tm = lg_ref.shape[1]

    @pl.when(i == 0)
    def _():
        carry_ref[...] = jnp.zeros_like(carry_ref)

    lg = jnp.where(i == pl.num_programs(0) - 1, lg_last_ref[...], lg_ref[...])
    s = _sigmoid(lg)
    sb = s + bias_ref[...][:, 0:1]
    ninf = -jnp.inf

    e_in_group = lax.broadcasted_iota(I32, (GROUP_SIZE, tm), 0)
    gscores = []
    for g in range(N_GROUPS):
        blk = sb[g * GROUP_SIZE:(g + 1) * GROUP_SIZE, :]
        m1 = jnp.max(blk, axis=0, keepdims=True)
        first = jnp.min(jnp.where(blk == m1, e_in_group, GROUP_SIZE), axis=0, keepdims=True)
        m2 = jnp.max(jnp.where(e_in_group == first, ninf, blk), axis=0, keepdims=True)
        gscores.append(m1 + m2)
    gs = jnp.concatenate(gscores, axis=0)

    g_iota = lax.broadcasted_iota(I32, (N_GROUPS, tm), 0)
    chosen = jnp.zeros((N_GROUPS, tm), F32)
    for _ in range(TOPK_GROUPS):
        m = jnp.max(gs, axis=0, keepdims=True)
        first = jnp.min(jnp.where(gs == m, g_iota, N_GROUPS), axis=0, keepdims=True)
        pick = g_iota == first
        chosen = jnp.where(pick, 1.0, chosen)
        gs = jnp.where(pick, ninf, gs)
    allowed = jnp.concatenate(
        [jnp.broadcast_to(chosen[g:g + 1, :], (GROUP_SIZE, tm)) for g in range(N_GROUPS)], axis=0) > 0.5
    cand = jnp.where(allowed, sb, ninf)

    e_iota = lax.broadcasted_iota(I32, (N_EXPERTS, tm), 0)
    onehot = jnp.zeros((N_EXPERTS, tm), F32)
    idxs, ws = [], []
    for _ in range(TOP_K):
        m = jnp.max(cand, axis=0, keepdims=True)
        first = jnp.min(jnp.where(cand == m, e_iota, N_EXPERTS), axis=0, keepdims=True)
        pick = e_iota == first
        idxs.append(first)
        ws.append(jnp.sum(jnp.where(pick, s, 0.0), axis=0, keepdims=True))
        onehot = jnp.where(pick, 1.0, onehot)
        cand = jnp.where(pick, ninf, cand)
    w = jnp.concatenate(ws, axis=0)
    w_ref[...] = w / jnp.sum(w, axis=0, keepdims=True) * ROUTE_SCALE
    idx_ref[...] = jnp.concatenate(idxs, axis=0)

    r = lax.broadcasted_iota(I32, (tm, tm), 0)
    c = lax.broadcasted_iota(I32, (tm, tm), 1)
    earlier = jnp.where(r < c, 1.0, 0.0).astype(BF16)
    carry = carry_ref[...]
    before = _dot(onehot.astype(BF16), earlier) + jnp.concatenate([carry] * (tm // LANES), axis=1)
    rank_ref[...] = jnp.concatenate(
        [jnp.sum(jnp.where(e_iota == ix, before, 0.0), axis=0, keepdims=True) for ix in idxs], axis=0).astype(I32)
    carry = carry + _dot(onehot.astype(BF16), jnp.ones((tm, LANES), BF16))
    carry_ref[...] = carry
    cnt_ref[...] = carry


def _route(logits_t, logits_last, bias):
    tm = MOE_TILE
    n_first = logits_t.shape[1] // tm
    n = logits_t.shape[1] + logits_last.shape[1]
    tok = lambda: pl.BlockSpec((TOP_K, tm), lambda i: (0, i))
    return pl.pallas_call(
        _route_kernel,
        grid=(n // tm,),
        in_specs=[pl.BlockSpec((N_EXPERTS, tm), lambda i: (0, jnp.minimum(i, n_first - 1))),
                  _const_spec(logits_last.shape), _const_spec(bias.shape)],
        out_specs=[tok(), tok(), tok(), _const_spec((N_EXPERTS, LANES))],
        out_shape=[jax.ShapeDtypeStruct((TOP_K, n), I32), jax.ShapeDtypeStruct((TOP_K, n), F32),
                   jax.ShapeDtypeStruct((TOP_K, n), I32), jax.ShapeDtypeStruct((N_EXPERTS, LANES), F32)],
        scratch_shapes=[pltpu.VMEM((N_EXPERTS, LANES), F32)],
        compiler_params=_cparams(("arbitrary",)),
        name="route",
    )(logits_t, logits_last, bias)


def _dest_kernel(idx_ref, rank_ref, cnt_ref, dest_ref, start_ref):
    i = pl.program_id(0)
    tm = idx_ref.shape[1]

    @pl.when(i == 0)
    def _():
        cnt = cnt_ref[...]
        padded = jnp.floor((cnt + (EXPERT_TILE - 1)) * (1.0 / EXPERT_TILE)) * EXPERT_TILE
        r = lax.broadcasted_iota(I32, (N_EXPERTS, N_EXPERTS), 0)
        c = lax.broadcasted_iota(I32, (N_EXPERTS, N_EXPERTS), 1)
        start_ref[...] = _dot(jnp.where(c < r, 1.0, 0.0).astype(F32), padded, precision=HIGHEST)

    start = jnp.concatenate([start_ref[...]] * (tm // LANES), axis=1)
    e_iota = lax.broadcasted_iota(I32, (N_EXPERTS, tm), 0)
    idx = idx_ref[...]
    first = jnp.concatenate(
        [jnp.sum(jnp.where(e_iota == idx[kk:kk + 1, :], start, 0.0), axis=0, keepdims=True) for kk in range(TOP_K)],
        axis=0)
    dest_ref[...] = first.astype(I32) + rank_ref[...]


def _dest(idx_t, rank_t, counts):
    n = idx_t.shape[1]
    tm = MOE_TILE
    tok = lambda: pl.BlockSpec((TOP_K, tm), lambda i: (0, i))
    return pl.pallas_call(
        _dest_kernel,
        grid=(n // tm,),
        in_specs=[tok(), tok(), _const_spec(counts.shape)],
        out_specs=pl.BlockSpec((None, TOP_K, tm), lambda i: (i, 0, 0)),
        out_shape=jax.ShapeDtypeStruct((n // tm, TOP_K, tm), I32),
        scratch_shapes=[pltpu.VMEM((N_EXPERTS, LANES), F32)],
        compiler_params=_cparams(("arbitrary",)),
        name="dest",
    )(idx_t, rank_t, counts)


def _sc_worker():
    return lax.axis_index("s") * SC_CORES + lax.axis_index("c")


def _sc_mesh():
    return plsc.VectorSubcoreMesh(core_axis_name="c", subcore_axis_name="s")


def _dispatch(dest_tiles, hp, hp_last, n_rows_out):
    n_tok_tiles = dest_tiles.shape[0]
    n_first = hp.shape[0] // MOE_TILE
    assert n_first + hp_last.shape[0] // MOE_TILE == n_tok_tiles
    w = hp.shape[1]
    iters = -(-n_tok_tiles // SC_WORKERS)

    def body(hp_hbm, hp_last_hbm, dest_hbm, xs_hbm, idx_v, rows_v, sem):
        wid = _sc_worker()

        def scatter():
            copies = [pltpu.async_copy(rows_v, xs_hbm.at[idx_v.at[kk]], sem) for kk in range(TOP_K)]
            for cp in copies:
                cp.wait()

        @pl.loop(0, iters)
        def _(j):
            tile = wid + SC_WORKERS * j

            @pl.when(tile < n_first)
            def _():
                pltpu.sync_copy(dest_hbm.at[tile], idx_v)
                pltpu.sync_copy(hp_hbm.at[pl.ds(tile * MOE_TILE, MOE_TILE)], rows_v)
                scatter()

            @pl.when((tile >= n_first) & (tile < n_tok_tiles))
            def _():
                pltpu.sync_copy(dest_hbm.at[tile], idx_v)
                pltpu.sync_copy(hp_last_hbm.at[pl.ds((tile - n_first) * MOE_TILE, MOE_TILE)], rows_v)
                scatter()

    return pl.kernel(
        body,
        out_type=jax.ShapeDtypeStruct((n_rows_out, w), U32),
        mesh=_sc_mesh(),
        scratch_types=[pltpu.VMEM((TOP_K, MOE_TILE), I32), pltpu.VMEM((MOE_TILE, w), U32), pltpu.SemaphoreType.DMA],
        name="dispatch",
    )(hp, hp_last, dest_tiles)


def _gather_back(ys, dest_tiles):
    n_tok_tiles = dest_tiles.shape[0]
    w = ys.shape[1]
    n_lists = n_tok_tiles * TOP_K
    iters = -(-n_lists // SC_WORKERS)

    def body(ys_hbm, dest_hbm, out_hbm, idx_v, rows_v, sem):
        wid = _sc_worker()

        @pl.loop(0, iters)
        def _(j):
            lst = wid + SC_WORKERS * j

            @pl.when(lst < n_lists)
            def _():
                pltpu.sync_copy(dest_hbm.at[pl.ds(lst * MOE_TILE, MOE_TILE)], idx_v)
                pltpu.async_copy(ys_hbm.at[idx_v], rows_v, sem).wait()
                pltpu.sync_copy(rows_v, out_hbm.at[pl.ds(lst * MOE_TILE, MOE_TILE)])

    out = pl.kernel(
        body,
        out_type=jax.ShapeDtypeStruct((n_lists * MOE_TILE, w), U32),
        mesh=_sc_mesh(),
        scratch_types=[pltpu.VMEM((MOE_TILE,), I32), pltpu.VMEM((MOE_TILE, w), U32), pltpu.SemaphoreType.DMA],
        name="gather_back",
    )(ys, dest_tiles.reshape(-1))
    return out.reshape(n_tok_tiles, TOP_K, MOE_TILE, w)


def _experts_kernel(first_ref, count_ref, rows_ref, xs_hbm, wg_ref, wu_ref, wd_ref, ys_hbm,
                    xbuf, ybuf, wgb_ref, wub_ref, wdb_ref, sem):
    e = pl.program_id(0)
    tm = EXPERT_TILE
    ring = EXPERT_RING
    n_e = count_ref[e]
    total = first_ref[N_EXPERTS - 1] + count_ref[N_EXPERTS - 1]

    def x_copy(t, slot):
        return pltpu.make_async_copy(xs_hbm.at[pl.ds(t * tm, tm), :], xbuf.at[slot], sem.at[0, slot])

    def y_copy(t, slot):
        return pltpu.make_async_copy(ybuf.at[slot], ys_hbm.at[pl.ds(t * tm, tm), :], sem.at[1, slot])

    @pl.when(e == 0)
    def _():
        for t in range(ring - 1):
            @pl.when(t < total)
            def _():
                x_copy(t, t).start()

    @pl.when(n_e > 0)
    def _():
        wgb_ref[...] = wg_ref[...].astype(BF16)
        wub_ref[...] = wu_ref[...].astype(BF16)
        wdb_ref[...] = wd_ref[...].astype(BF16)
        half = D_MODEL // 2

        def body(j, carry):
            t = first_ref[e] + j
            slot = t % ring
            x_copy(t, slot).wait()

            @pl.when(t + ring - 1 < total)
            def _():
                x_copy(t + ring - 1, (t + ring - 1) % ring).start()

            @pl.when(t >= ring)
            def _():
                y_copy(t - ring, slot).wait()

            xs = xbuf[slot]
            filled = lax.broadcasted_iota(I32, xs.shape, 0) < rows_ref[t]
            lo, hi = _unpack_bf16_pair(jnp.where(filled, xs, jnp.zeros_like(xs)))
            gate = _dot(lo, wgb_ref[:half, :]) + _dot(hi, wgb_ref[half:, :])
            up = _dot(lo, wub_ref[:half, :]) + _dot(hi, wub_ref[half:, :])
            act = (_silu(gate) * up).astype(BF16)
            ybuf[slot] = _pack_bf16_pair(_dot(act, wdb_ref[...]))
            y_copy(t, slot).start()
            return carry

        lax.fori_loop(0, n_e, body, 0)

    @pl.when(e == N_EXPERTS - 1)
    def _():
        for back in range(ring, 0, -1):
            @pl.when(total >= back)
            def _():
                y_copy(total - back, (total - back) % ring).wait()


def _experts(tile_first, tile_count, tile_rows, xs, wg, wu, wd):
    n_rows, w = xs.shape
    tm = EXPERT_TILE
    expert = lambda e, tf, tc, tr: (e, 0, 0)
    return pl.pallas_call(
        _experts_kernel,
        grid_spec=pltpu.PrefetchScalarGridSpec(
            num_scalar_prefetch=3,
            grid=(N_EXPERTS,),
            in_specs=[pl.BlockSpec(memory_space=pl.ANY),
                      pl.BlockSpec((None, D_MODEL, D_EXPERT), expert),
                      pl.BlockSpec((None, D_MODEL, D_EXPERT), expert),
                      pl.BlockSpec((None, D_EXPERT, D_MODEL), expert)],
            out_specs=pl.BlockSpec(memory_space=pl.ANY),
            scratch_shapes=[pltpu.VMEM((EXPERT_RING, tm, w), U32), pltpu.VMEM((EXPERT_RING, tm, w), U32),
                            pltpu.VMEM((D_MODEL, D_EXPERT), BF16), pltpu.VMEM((D_MODEL, D_EXPERT), BF16),
                            pltpu.VMEM((D_EXPERT, D_MODEL), BF16),
                            pltpu.SemaphoreType.DMA((2, EXPERT_RING))]),
        out_shape=jax.ShapeDtypeStruct((n_rows, w), U32),
        compiler_params=_cparams(("arbitrary",)),
        name="experts",
    )(tile_first, tile_count, tile_rows, xs, wg, wu, wd)


def _shared_kernel(hp_ref, sg_ref, su_ref, sd_ref, o_ref):
    half = D_MODEL // 2
    lo, hi = _unpack_bf16_pair(hp_ref[...])
    sg = sg_ref[...]
    su = su_ref[...]
    gate = _dot(lo, sg[:half]) + _dot(hi, sg[half:])
    up = _dot(lo, su[:half]) + _dot(hi, su[half:])
    o_ref[...] = _dot((_silu(gate) * up).astype(BF16), sd_ref[...]).astype(o_ref.dtype)


def _shared(hp, tm, sg, su, sd):
    n = hp.shape[0]
    return pl.pallas_call(
        _shared_kernel,
        grid=(n // tm,),
        in_specs=[pl.BlockSpec((tm, D_MODEL // 2), lambda i: (i, 0)),
                  _const_spec(sg.shape), _const_spec(su.shape), _const_spec(sd.shape)],
        out_specs=pl.BlockSpec((tm, D_MODEL), lambda i: (i, 0)),
        out_shape=jax.ShapeDtypeStruct((n, D_MODEL), BF16),
        compiler_params=_cparams(("arbitrary",)),
        name="shared",
    )(hp, sg, su, sd)


def _combine_kernel(yt_ref, x1_ref, ysh_ref, w_ref, mod_ref, fg_ref, *rest):
    o_ref = rest[-1]
    y = ysh_ref[...].astype(F32)
    w = w_ref[...]
    for kk in range(TOP_K):
        ylo, yhi = _unpack_bf16_pair(yt_ref[kk])
        y = y + w[:, kk:kk + 1] * jnp.concatenate([ylo.astype(F32), yhi.astype(F32)], axis=1)
    x2 = x1_ref[...] + mod_ref[5] * y
    o_ref[...] = x2 * lax.rsqrt(jnp.mean(x2 * x2, axis=-1, keepdims=True) + EPS) * fg_ref[...]


def _combine(y_tok, y_tok_tile0, x1, y_shared, w, w_tile0, mod, rows_per_mod, fg, tile0, n_tiles, prev=None):
    n = x1.shape[0]
    tm = MOE_TILE
    mod_rows = mod.shape[2]
    tiles_per_mod = rows_per_mod // tm
    own = lambda i: (i + tile0, 0)
    return pl.pallas_call(
        _combine_kernel,
        grid=(n_tiles,),
        in_specs=[pl.BlockSpec((None, TOP_K, tm, D_MODEL // 2), lambda i: (i + y_tok_tile0, 0, 0, 0)),
                  pl.BlockSpec((tm, D_MODEL), own),
                  pl.BlockSpec((tm, D_MODEL), own),
                  pl.BlockSpec((tm, TOP_K), lambda i: (i + w_tile0, 0)),
                  pl.BlockSpec((6, None, mod_rows, D_MODEL), lambda i: (0, (i + tile0) // tiles_per_mod, 0, 0)),
                  _const_spec(fg.shape)] + ([] if prev is None else [pl.BlockSpec(memory_space=pl.ANY)]),
        out_specs=pl.BlockSpec((tm, D_MODEL), own),
        out_shape=jax.ShapeDtypeStruct((n, D_MODEL), F32),
        input_output_aliases={} if prev is None else {6: 0},
        compiler_params=_cparams(("arbitrary",)),
        name="combine",
    )(y_tok, x1, y_shared, w, mod, fg, *(() if prev is None else (prev,)))


def _split_w_in(w_in):
    sizes = (W_A, W_A, W_A, H_A, WK_B, WK_B, WV_B, WV_B, GK_RANK, D_MODEL, D_MODEL)
    segs, o = [], 0
    for s in sizes:
        segs.append(w_in[:, o:o + s])
        o += s
    return segs


def kernel(x_prompt, x_sample, cache_k, cache_v, cache_logf, state_gla, page_table, c_prompt, c_sample, w_ada, b_ada,
           norm1_g, w_in, b_f, w_gk2, b_gk, gla_norm_g, w_pa, w_pb, w_o, norm2_g, router_w, router_bias, w_gate, w_up,
           w_down, ws_gate, ws_up, ws_down, final_g):
    assert w_ada.shape[0] == 1, "single layer"
    bp, tp, d = x_prompt.shape
    bs, ts, _ = x_sample.shape
    n_p, n_s = bp * tp, bs * ts
    n_tot = n_p + n_s
    assert n_s == MOE_TILE and n_p % MOE_TILE == 0 and tp % FOX_TILE == 0

    q_a, k_a, v_a, f_a, q_b, k_b, v_b, r_b, gk1, gate_a, gate_b = _split_w_in(w_in[0])
    wts = (jnp.concatenate([q_a, k_a], axis=1).astype(BF16),
           v_a.astype(BF16),
           jnp.concatenate([k_a, v_a], axis=1).T.astype(BF16),
           jnp.pad(f_a, ((0, 0), (0, LANES - H_A))).astype(BF16),
           jnp.pad(b_f[0].reshape(1, H_A), ((0, 0), (0, LANES - H_A))),
           jnp.concatenate([q_b, k_b, v_b, r_b], axis=1).astype(BF16),
           jnp.pad(gk1, ((0, 0), (0, LANES - GK_RANK))).astype(BF16),
           jnp.pad(w_gk2[0], ((0, LANES - GK_RANK), (0, 0))).astype(BF16),
           b_gk[0].reshape(1, WK_B),
           jnp.concatenate([gate_a, gate_b], axis=1).astype(BF16))
    g1 = norm1_g[0].reshape(1, d)
    g2 = norm2_g[0].reshape(1, d)
    gn = gla_norm_g[0].reshape(1, DV_B)
    fg = final_g.reshape(1, d)
    wpa, wpb, wo = w_pa[0].astype(BF16), w_pb[0].astype(BF16), w_o[0].astype(BF16)
    rw_hi = router_w[0].T.astype(BF16)
    rw_t = jnp.stack([rw_hi, (router_w[0].T - rw_hi.astype(F32)).astype(BF16)])
    rbias = jnp.broadcast_to(router_bias[0].reshape(N_EXPERTS, 1), (N_EXPERTS, LANES))
    sg, su, sd = ws_gate[0].astype(BF16), ws_up[0].astype(BF16), ws_down[0].astype(BF16)

    ada = _ada(jnp.concatenate([c_prompt, c_sample], axis=0), w_ada[0], b_ada[0].reshape(1, -1))
    ada = ada.reshape(bp + bs, 6, d)
    mod_p = ada[:bp].transpose(1, 0, 2)[:, :, None, :]
    mod_s = jnp.repeat(ada[bp:], ts, axis=0).transpose(1, 0, 2)[:, None, :, :]

    xp = x_prompt.reshape(n_p, d)
    tm_p = 512
    (q, kh, vth, kt, vt, lf, fc, qb, kb, gk, vb, rb, ga_p, gb_p) = _inproj(
        xp, mod_p, tp, tp // tm_p, tm_p, g1, wts, True)
    ya_p = _fox(q, kh, vth, fc, bp, tp)
    yb_p, s_p = _gla(qb, kb, gk, vb, rb, jnp.zeros((bp, H_B, DK_B, DV_B), F32), gn, bp, tp, 256)
    k_prompt = kt.reshape(bp, H_A, DH_A, tp).transpose(0, 3, 1, 2)[None]
    v_prompt = vt.reshape(bp, H_A, DH_A, tp).transpose(0, 3, 1, 2)[None]
    logf_prompt = lf.reshape(1, bp, tp, H_A)

    xs_in = x_sample.reshape(n_s, d)
    (q, k, v, lf, qb, kb, gk, vb, rb, ga, gb) = _inproj(xs_in, mod_s, n_s, 1, n_s, g1, wts, False)
    page_rows = cache_k.shape[2]
    tq = 8
    cache_kt = jnp.transpose(cache_k, (0, 1, 3, 4, 2))
    cache_vt = jnp.transpose(cache_v, (0, 1, 3, 4, 2))
    suf = _lfsuf(page_table, jnp.transpose(cache_logf, (0, 1, 3, 2)))

    def new_kv(a):
        a = a.reshape(bs, ts, H_A, DH_A).transpose(0, 2, 3, 1)
        return jnp.pad(a, ((0, 0), (0, 0), (0, 0), (0, page_rows - ts)))

    q_dec = jnp.pad(q.astype(F32).reshape(bs, ts, H_A, DH_A).transpose(0, 2, 1, 3),
                    ((0, 0), (0, 0), (0, tq - ts), (0, 0)))
    lfn = jnp.pad(lf.reshape(bs, ts, H_A).transpose(0, 2, 1), ((0, 0), (0, 0), (0, page_rows - ts)))
    ya = _foxdec(page_table, q_dec, new_kv(k), new_kv(v), lfn, suf, cache_kt, cache_vt)
    ya = ya[:, :, :ts].transpose(0, 2, 1, 3).reshape(n_s, W_A).astype(BF16)
    pad = lambda a: jnp.pad(a.reshape(bs, ts, -1), ((0, 0), (0, GLA_CHUNK - ts), (0, 0))).reshape(bs * GLA_CHUNK, -1)
    yb, s_s = _gla(pad(qb), pad(kb), pad(gk), pad(vb), pad(rb), state_gla[0], gn, bs, GLA_CHUNK, GLA_CHUNK)
    yb = yb.reshape(bs, GLA_CHUNK, WV_B)[:, :ts].reshape(n_s, WV_B)
    k_sample = k.reshape(1, bs, ts, H_A, DH_A)
    v_sample = v.reshape(1, bs, ts, H_A, DH_A)
    logf_sample = lf.reshape(1, bs, ts, H_A)
    x1_s, hp_s, logits_s = _mixout(xs_in, ya, yb, ga, gb, mod_s, n_s, n_s, g2, wpa, wpb, wo, rw_t)

    x1_p, hp_p, logits_p = _mixout(xp, ya_p, yb_p, ga_p, gb_p, mod_p, tp, 2 * MOE_TILE, g2, wpa, wpb, wo, rw_t)
    idx_t, w_t, rank_t, counts = _route(logits_p, logits_s, rbias)
    dest_tiles = _dest(idx_t, rank_t, counts)
    padded = (counts[:, 0].astype(I32) + EXPERT_TILE - 1) // EXPERT_TILE * EXPERT_TILE
    pad_end = jnp.cumsum(padded)
    n_tiles = -(-n_tot * TOP_K // EXPERT_TILE) + N_EXPERTS
    tile_first_row = jnp.arange(n_tiles, dtype=I32) * EXPERT_TILE
    tile_expert = jnp.minimum(jnp.sum((pad_end[None, :] <= tile_first_row[:, None]).astype(I32), axis=1),
                              N_EXPERTS - 1)
    used_end = pad_end - padded + counts[:, 0].astype(I32)
    tile_rows = jnp.clip(used_end[tile_expert] - tile_first_row, 0, EXPERT_TILE).astype(I32)
    tile_first = ((pad_end - padded) // EXPERT_TILE).astype(I32)
    tile_count = (padded // EXPERT_TILE).astype(I32)

    xs_sorted = _dispatch(dest_tiles, hp_p, hp_s, n_tiles * EXPERT_TILE)
    ys_sorted = _experts(tile_first, tile_count, tile_rows, xs_sorted, w_gate[0], w_up[0], w_down[0])
    n_tok_tiles = n_tot // MOE_TILE
    part = n_tok_tiles // 2
    y_tok_a = _gather_back(ys_sorted, dest_tiles[:part])
    y_tok_b = _gather_back(ys_sorted, dest_tiles[part:])
    w_tok = w_t.T
    ysh_p = _shared(hp_p, 2 * MOE_TILE, sg, su, sd)
    ysh_s = _shared(hp_s, MOE_TILE, sg, su, sd)
    n_p_tiles = n_p // MOE_TILE
    y_prompt = _combine(y_tok_a, 0, x1_p, ysh_p, w_tok, 0, mod_p, tp, fg, 0, part)
    y_prompt = _combine(y_tok_b, 0, x1_p, ysh_p, w_tok, part, mod_p, tp, fg, part, n_p_tiles - part, prev=y_prompt)
    y_sample = _combine(y_tok_b, n_p_tiles - part, x1_s, ysh_s, w_tok, n_p_tiles, mod_s, n_s, fg, 0, n_s // MOE_TILE)

    return (y_prompt.reshape(bp, tp, d), y_sample.reshape(bs, ts, d),
            k_prompt, v_prompt, logf_prompt, s_p[None],
            k_sample, v_sample, logf_sample, s_s[None])
```

```python
import functools

import jax
import jax.numpy as jnp
from jax import lax
from jax.experimental import pallas as pl
from jax.experimental.pallas import tpu as pltpu
from jax.experimental.pallas import tpu_sc as plsc

F32 = jnp.float32
BF16 = jnp.bfloat16
U32 = jnp.uint32
I32 = jnp.int32

EPS = 1e-6
D_MODEL = 1024
H_A, DH_A, W_A = 8, 64, 512
H_B, DK_B, DV_B, WK_B, WV_B = 4, 64, 128, 256, 512
GK_RANK = 16
GATE_NORM = 16.0
N_EXPERTS, TOP_K, N_GROUPS, TOPK_GROUPS = 256, 8, 8, 4
GROUP_SIZE = N_EXPERTS // N_GROUPS
D_EXPERT = 256
ROUTE_SCALE = 2.5
MOE_TILE = 128
EXPERT_TILE = 256
EXPERT_RING = 4
GATHER_PARTS = 4
LANES = 128
SC_CORES = 2
SC_WORKERS = 32
GLA_CHUNK = 64
GLA_SUB = 16
FOX_TILE = 512
FOX_SUB = 128
DEC_PAGES = 8
DEC_RING = 3
NEG = -1e30
LOG2E = 1.4426950408889634
VMEM_LIMIT = 56 * 1024 * 1024
HIGHEST = lax.Precision.HIGHEST

NT_DIMS = (((1,), (1,)), ((), ()))
TN_DIMS = (((0,), (0,)), ((), ()))


def _cparams(sem):
    return pltpu.CompilerParams(dimension_semantics=sem, vmem_limit_bytes=VMEM_LIMIT)


def _dot(a, b, **kw):
    return jnp.dot(a, b, preferred_element_type=F32, **kw)


def _dot_nt(a, b, **kw):
    return lax.dot_general(a, b, NT_DIMS, preferred_element_type=F32, **kw)


def _dot_tn(a, b, **kw):
    return lax.dot_general(a, b, TN_DIMS, preferred_element_type=F32, **kw)


def _sigmoid(x):
    return 0.5 * jnp.tanh(0.5 * x) + 0.5


def _silu(x):
    return x * _sigmoid(x)


def _log_sigmoid(x):
    return jnp.minimum(x, 0.0) - jnp.log(1.0 + jnp.exp(-jnp.abs(x)))


def _pack_bf16_pair(x):
    c = x.shape[1] // 2
    bits = pltpu.bitcast(x.astype(BF16).astype(F32), U32)
    return (bits[:, c:] & jnp.uint32(0xFFFF0000)) | (bits[:, :c] >> 16)


def _unpack_bf16_pair(p):
    lo = pltpu.bitcast(p << 16, F32).astype(BF16)
    hi = pltpu.bitcast(p & jnp.uint32(0xFFFF0000), F32).astype(BF16)
    return lo, hi


def _ada_kernel(c_ref, w_ref, b_ref, o_ref):
    c = c_ref[...]
    o_ref[...] = _dot(_silu(c).astype(BF16), w_ref[...].astype(BF16)) + b_ref[...]


def _ada(c, w, b):
    n, d = c.shape
    cols = w.shape[1]
    tn = 1536
    return pl.pallas_call(
        _ada_kernel,
        grid=(cols // tn,),
        in_specs=[pl.BlockSpec((n, d), lambda j: (0, 0)),
                  pl.BlockSpec((d, tn), lambda j: (0, j)),
                  pl.BlockSpec((1, tn), lambda j: (0, j))],
        out_specs=pl.BlockSpec((n, tn), lambda j: (0, j)),
        out_shape=jax.ShapeDtypeStruct((n, cols), F32),
        compiler_params=_cparams(("arbitrary",)),
        name="ada",
    )(c, w, b)


def _inproj_kernel(*refs, seq_tiles, prompt):
    (x_ref, mod_ref, g_ref, wqk_ref, wv_ref, wkvt_ref, wf_ref, bf_ref, wb_ref, wgk1_ref, wgk2_ref, bgk_ref,
     wgate_ref) = refs[:13]
    if prompt:
        (q_ref, kh_ref, vth_ref, kt_ref, vt_ref, lf_ref, fc_ref, qb_ref, kb_ref, gk_ref, vb_ref, rb_ref, ga_ref,
         gb_ref, carry_ref) = refs[13:]
    else:
        q_ref, k_ref, v_ref, lf_ref, qb_ref, kb_ref, gk_ref, vb_ref, rb_ref, ga_ref, gb_ref = refs[13:]
    i = pl.program_id(0)
    tm = x_ref.shape[0]
    x = x_ref[...]
    y = x * lax.rsqrt(jnp.mean(x * x, axis=-1, keepdims=True) + EPS) * g_ref[...]
    h = (y * (1.0 + mod_ref[1]) + mod_ref[0]).astype(BF16)

    z = _dot(h, wqk_ref[...])
    q_ref[...] = (z[:, :W_A] * (DH_A ** -0.5 * (LOG2E if prompt else 1.0))).astype(BF16)
    k = z[:, W_A:]

    lf = _log_sigmoid(_dot(h, wf_ref[...]) + bf_ref[...])
    lf_ref[...] = lf[:, :H_A]

    if prompt:
        kvt = _dot_nt(wkvt_ref[...], h)
        kt_ref[...] = kvt[:W_A]
        vt_ref[...] = kvt[W_A:]
        vth_ref[...] = kvt[W_A:].astype(BF16)
        kh_ref[...] = k.astype(BF16)

        @pl.when(i % seq_tiles == 0)
        def _():
            carry_ref[...] = jnp.zeros_like(carry_ref)

        r = lax.broadcasted_iota(I32, (tm, tm), 0)
        c = lax.broadcasted_iota(I32, (tm, tm), 1)
        lower = jnp.where(c <= r, 1.0, 0.0).astype(F32)
        carry = carry_ref[...]
        fc = _dot(lower, lf, precision=HIGHEST) + carry[0:1, :]
        fc_ref[...] = fc[:, :H_A] * LOG2E
        carry_ref[...] = carry + _dot(jnp.ones((8, tm), F32), lf, precision=HIGHEST)
    else:
        k_ref[...] = k
        v_ref[...] = _dot(h, wv_ref[...])

    z = _dot(h, wb_ref[...])
    qb_ref[...] = z[:, :WK_B] * DK_B ** -0.5
    kb_ref[...] = z[:, WK_B:2 * WK_B]
    vb_ref[...] = z[:, 2 * WK_B:2 * WK_B + WV_B].astype(BF16)
    rb_ref[...] = _silu(z[:, 2 * WK_B + WV_B:]).astype(BF16)

    lr = _dot(h, wgk1_ref[...]).astype(BF16)
    gk_ref[...] = _log_sigmoid(_dot(lr, wgk2_ref[...]) + bgk_ref[...]) * (1.0 / GATE_NORM)

    z = _dot(h, wgate_ref[...])
    ga_ref[...] = _sigmoid(z[:, :D_MODEL]).astype(BF16)
    gb_ref[...] = _sigmoid(z[:, D_MODEL:]).astype(BF16)


def _const_spec(shape):
    return pl.BlockSpec(shape, lambda i: (0,) * len(shape))


def _inproj(x, mod, rows_per_mod, seq_tiles, tm, g, wts, prompt):
    n = x.shape[0]
    mod_rows = mod.shape[2]
    tiles_per_mod = rows_per_mod // tm
    row = lambda w: pl.BlockSpec((tm, w), lambda i: (i, 0))
    if prompt:
        n_seq = n // (seq_tiles * tm)
        seq_t = pl.BlockSpec((None, W_A, tm), lambda i: (i // seq_tiles, 0, i % seq_tiles))
        kv_shape = (n_seq, W_A, seq_tiles * tm)
        outs = [
            (row(W_A), (n, W_A), BF16),
            (row(W_A), (n, W_A), BF16),
            (pl.BlockSpec((W_A, tm), lambda i: (0, i)), (W_A, n), BF16),
            (seq_t, kv_shape, F32),
            (seq_t, kv_shape, F32),
            (row(H_A), (n, H_A), F32),
            (row(H_A), (n, H_A), F32),
        ]
    else:
        outs = [
            (row(W_A), (n, W_A), BF16),
            (row(W_A), (n, W_A), F32),
            (row(W_A), (n, W_A), F32),
            (row(H_A), (n, H_A), F32),
        ]
    outs += [
        (row(WK_B), (n, WK_B), F32),
        (row(WK_B), (n, WK_B), F32),
        (row(WK_B), (n, WK_B), F32),
        (row(WV_B), (n, WV_B), BF16),
        (row(WV_B), (n, WV_B), BF16),
        (row(D_MODEL), (n, D_MODEL), BF16),
        (row(D_MODEL), (n, D_MODEL), BF16),
    ]
    return pl.pallas_call(
        functools.partial(_inproj_kernel, seq_tiles=seq_tiles, prompt=prompt),
        grid=(n // tm,),
        in_specs=[row(D_MODEL),
                  pl.BlockSpec((6, None, mod_rows, D_MODEL), lambda i: (0, i // tiles_per_mod, 0, 0)),
                  _const_spec(g.shape)] + [_const_spec(w.shape) for w in wts],
        out_specs=[o[0] for o in outs],
        out_shape=[jax.ShapeDtypeStruct(o[1], o[2]) for o in outs],
        scratch_shapes=[pltpu.VMEM((8, LANES), F32)] if prompt else [],
        compiler_params=_cparams(("arbitrary",)),
        name="inproj",
    )(x, mod, g, *wts)


def _fox_kernel(q_ref, k_ref, vt_ref, fc_ref, o_ref, qs_ref, m_ref, l_ref, acc_ref):
    qi = pl.program_id(1)
    ki = pl.program_id(2)
    t = q_ref.shape[0]
    pairs = H_A // 2
    sub = FOX_SUB

    @pl.when(ki == 0)
    def _():
        lane = lax.broadcasted_iota(I32, (t, LANES), 1)
        for hp in range(pairs):
            q = q_ref[:, hp * LANES:(hp + 1) * LANES]
            qs_ref[hp, :t, :] = jnp.where(lane < DH_A, q, jnp.zeros_like(q))
            qs_ref[hp, t:, :] = jnp.where(lane >= DH_A, q, jnp.zeros_like(q))
        m_ref[...] = jnp.full_like(m_ref, NEG)
        l_ref[...] = jnp.zeros_like(l_ref)
        acc_ref[...] = jnp.zeros_like(acc_ref)

    def step(diagonal):
        for hp in range(pairs):
            k = k_ref[:, hp * LANES:(hp + 1) * LANES]
            vt = vt_ref[hp * LANES:(hp + 1) * LANES, :]
            for h2 in range(2):
                head = 2 * hp + h2
                fb = jnp.broadcast_to(fc_ref[:, head:head + 1], (t, sub))
                for sb in range(t // sub):
                    cols = slice(h2 * t + sb * sub, h2 * t + (sb + 1) * sub)
                    s = _dot_nt(k, qs_ref[hp, cols, :]) - fb
                    if diagonal:
                        key = lax.broadcasted_iota(I32, (t, sub), 0)
                        qry = lax.broadcasted_iota(I32, (t, sub), 1) + sb * sub
                        s = jnp.where(key <= qry, s, NEG)
                    m_prev = m_ref[hp, :, cols]
                    m_new = jnp.maximum(m_prev, jnp.max(s, axis=0, keepdims=True))
                    alpha = jnp.exp2(m_prev - m_new)
                    p = jnp.exp2(s - m_new)
                    l_ref[hp, :, cols] = alpha * l_ref[hp, :, cols] + jnp.sum(p, axis=0, keepdims=True)
                    acc_ref[hp, :, cols] = alpha * acc_ref[hp, :, cols] + _dot(vt, p.astype(BF16))
                    m_ref[hp, :, cols] = m_new

    @pl.when(ki < qi)
    def _():
        step(False)

    @pl.when(ki == qi)
    def _():
        step(True)
        for hp in range(pairs):
            o = acc_ref[hp] / l_ref[hp]
            o = jnp.concatenate([o[:DH_A, :t], o[DH_A:, t:]], axis=0)
            o_ref[:, hp * LANES:(hp + 1) * LANES] = o.T.astype(o_ref.dtype)


def _fox(q, k, vt, fc, batch, seq):
    t = FOX_TILE
    nq = seq // t
    kv_map = lambda b, qi, ki: (b * nq + jnp.minimum(ki, qi), 0)
    return pl.pallas_call(
        _fox_kernel,
        grid=(batch, nq, nq),
        in_specs=[pl.BlockSpec((t, W_A), lambda b, qi, ki: (b * nq + qi, 0)),
                  pl.BlockSpec((t, W_A), kv_map),
                  pl.BlockSpec((W_A, t), lambda b, qi, ki: (0, b * nq + jnp.minimum(ki, qi))),
                  pl.BlockSpec((t, H_A), kv_map)],
        out_specs=pl.BlockSpec((t, W_A), lambda b, qi, ki: (b * nq + qi, 0)),
        out_shape=jax.ShapeDtypeStruct((batch * seq, W_A), BF16),
        scratch_shapes=[pltpu.VMEM((H_A // 2, 2 * t, LANES), BF16),
                        pltpu.VMEM((H_A // 2, 1, 2 * t), F32),
                        pltpu.VMEM((H_A // 2, 1, 2 * t), F32),
                        pltpu.VMEM((H_A // 2, LANES, 2 * t), F32)],
        compiler_params=_cparams(("arbitrary", "arbitrary", "arbitrary")),
        name="fox",
    )(q, k, vt, fc)


def _lfsuf_kernel(pt_ref, lf_hbm, o_ref, buf_ref, sem):
    b = pl.program_id(0)
    n_pages, _, w = buf_ref.shape

    def page_copy(j):
        return pltpu.make_async_copy(lf_hbm.at[0, pt_ref[b, j]], buf_ref.at[j], sem)

    def issue(j, carry):
        page_copy(j).start()
        return carry

    def drain(j, carry):
        page_copy(j).wait()
        return carry

    lax.fori_loop(0, n_pages, issue, 0)
    lax.fori_loop(0, n_pages, drain, 0)

    x = buf_ref[...]
    lane = lax.broadcasted_iota(I32, x.shape, 2)
    s = x
    shift = 1
    while shift < w:
        s = s + jnp.where(lane + shift < w, pltpu.roll(s, w - shift, axis=2), 0.0)
        shift *= 2
    within = s - x
    after = jnp.zeros((H_A, w), F32)
    for j in range(n_pages - 1, -1, -1):
        o_ref[j] = within[j] + after
        after = after + jnp.broadcast_to(s[j][:, 0:1], (H_A, w))


def _lfsuf(page_table, lf_pages):
    nb, n_pages = page_table.shape
    w = lf_pages.shape[3]
    return pl.pallas_call(
        _lfsuf_kernel,
        grid_spec=pltpu.PrefetchScalarGridSpec(
            num_scalar_prefetch=1,
            grid=(nb,),
            in_specs=[pl.BlockSpec(memory_space=pl.ANY)],
            out_specs=pl.BlockSpec((None, n_pages, H_A, w), lambda b, pt: (b, 0, 0, 0)),
            scratch_shapes=[pltpu.VMEM((n_pages, H_A, w), F32), pltpu.SemaphoreType.DMA(())]),
        out_shape=jax.ShapeDtypeStruct((nb, n_pages, H_A, w), F32),
        compiler_params=_cparams(("arbitrary",)),
        name="lfsuf",
    )(page_table, lf_pages)


def _foxdec_kernel(pt_ref, q_ref, kn_ref, vn_ref, lfn_ref, suf_ref, ck_hbm, cv_hbm, o_ref,
                   kbuf, vbuf, sem, m_ref, l_ref, acc_ref):
    b = pl.program_id(0)
    n_pages = suf_ref.shape[0]
    page_rows = kbuf.shape[-1]
    n_chunks = n_pages // DEC_PAGES
    tq = q_ref.shape[1]

    total = pl.num_programs(0) * n_chunks

    def page_copies(g, j):
        slot = g % DEC_RING
        pg = pt_ref[g // n_chunks, (g % n_chunks) * DEC_PAGES + j]
        return (pltpu.make_async_copy(ck_hbm.at[0, pg], kbuf.at[slot, j], sem.at[0, slot]),
                pltpu.make_async_copy(cv_hbm.at[0, pg], vbuf.at[slot, j], sem.at[1, slot]))

    def fetch(g):
        for j in range(DEC_PAGES):
            for cp in page_copies(g, j):
                cp.start()

    def wait(g):
        for j in range(DEC_PAGES):
            for cp in page_copies(g, j):
                cp.wait()

    q = q_ref[...].astype(BF16)
    m_ref[...] = jnp.full_like(m_ref, NEG)
    l_ref[...] = jnp.zeros_like(l_ref)
    acc_ref[...] = jnp.zeros_like(acc_ref)

    def attend(kt, vt, bias, ok):
        s = jnp.einsum('htd,hds->hts', q, kt, preferred_element_type=F32) + bias
        if ok is not None:
            s = jnp.where(ok, s, NEG)
        m_prev = m_ref[...]
        m_new = jnp.maximum(m_prev, jnp.max(s, axis=2, keepdims=True))
        alpha = jnp.exp(m_prev - m_new)
        p = jnp.exp(s - m_new)
        l_ref[...] = alpha * l_ref[...] + jnp.sum(p, axis=2, keepdims=True)
        acc_ref[...] = alpha * acc_ref[...] + jnp.einsum('hts,hds->htd', p.astype(BF16), vt,
                                                         preferred_element_type=F32)
        m_ref[...] = m_new

    @pl.when(b == 0)
    def _():
        for g in range(DEC_RING - 1):
            fetch(g)

    def chunk(c, carry):
        g = b * n_chunks + c
        slot = g % DEC_RING

        @pl.when(g + DEC_RING - 1 < total)
        def _():
            fetch(g + DEC_RING - 1)

        wait(g)
        kt = jnp.concatenate([kbuf[slot, j] for j in range(DEC_PAGES)], axis=2).astype(BF16)
        vt = jnp.concatenate([vbuf[slot, j] for j in range(DEC_PAGES)], axis=2).astype(BF16)
        suf = suf_ref[pl.ds(c * DEC_PAGES, DEC_PAGES)]
        bias = jnp.concatenate([suf[j] for j in range(DEC_PAGES)], axis=1)
        attend(kt, vt, bias[:, None, :], None)
        return carry

    lax.fori_loop(0, n_chunks, chunk, 0)

    r = lax.broadcasted_iota(I32, (page_rows, page_rows), 0)
    c = lax.broadcasted_iota(I32, (page_rows, page_rows), 1)
    fn = _dot(lfn_ref[...], jnp.where(r <= c, 1.0, 0.0).astype(F32), precision=HIGHEST)
    qi = lax.broadcasted_iota(I32, (H_A, tq, page_rows), 1)
    ki = lax.broadcasted_iota(I32, (H_A, tq, page_rows), 2)
    attend(kn_ref[...].astype(BF16), vn_ref[...].astype(BF16), -fn[:, None, :], ki <= qi)
    o_ref[...] = acc_ref[...] / l_ref[...]


def _foxdec(page_table, q, kn, vn, lfn, suf, cache_kt, cache_vt):
    nb, _, tq, _ = q.shape
    n_pages = page_table.shape[1]
    page_rows = cache_kt.shape[4]
    per_seq = lambda shape: pl.BlockSpec((None,) + shape, lambda b, pt: (b,) + (0,) * len(shape))
    page = (H_A, DH_A, page_rows)
    return pl.pallas_call(
        _foxdec_kernel,
        grid_spec=pltpu.PrefetchScalarGridSpec(
            num_scalar_prefetch=1,
            grid=(nb,),
            in_specs=[per_seq((H_A, tq, DH_A)), per_seq(page), per_seq(page), per_seq((H_A, page_rows)),
                      per_seq((n_pages, H_A, page_rows)),
                      pl.BlockSpec(memory_space=pl.ANY), pl.BlockSpec(memory_space=pl.ANY)],
            out_specs=per_seq((H_A, tq, DH_A)),
            scratch_shapes=[pltpu.VMEM((DEC_RING, DEC_PAGES) + page, F32),
                            pltpu.VMEM((DEC_RING, DEC_PAGES) + page, F32),
                            pltpu.SemaphoreType.DMA((2, DEC_RING)),
                            pltpu.VMEM((H_A, tq, 1), F32), pltpu.VMEM((H_A, tq, 1), F32),
                            pltpu.VMEM((H_A, tq, DH_A), F32)]),
        out_shape=jax.ShapeDtypeStruct((nb, H_A, tq, DH_A), F32),
        compiler_params=_cparams(("arbitrary",)),
        name="foxdec",
    )(page_table, q, kn, vn, lfn, suf, cache_kt, cache_vt)


def _gla_kernel(q_ref, k_ref, g_ref, v_ref, r_ref, s0_ref, gn_ref, y_ref, sfin_ref, state_ref):
    tb = pl.program_id(1)
    n_tb = pl.num_programs(1)
    c = GLA_CHUNK
    n_chunks = q_ref.shape[0] // c
    n_sub = c // GLA_SUB

    row_k = lax.broadcasted_iota(I32, (WK_B, WV_B), 0) // DK_B
    col_v = lax.broadcasted_iota(I32, (WK_B, WV_B), 1) // DV_B
    state_diag = row_k == col_v

    @pl.when(tb == 0)
    def _():
        state_ref[...] = jnp.zeros_like(state_ref)
        for h in range(H_B):
            state_ref[h * DK_B:(h + 1) * DK_B, h * DV_B:(h + 1) * DV_B] = s0_ref[h]

    r = lax.broadcasted_iota(I32, (c, c), 0)
    s = lax.broadcasted_iota(I32, (c, c), 1)
    same_sub = (r // GLA_SUB) == (s // GLA_SUB)
    cum_ops = jnp.concatenate([
        jnp.where(same_sub & (s <= r), 1.0, 0.0),
        jnp.where(s <= r, 1.0, 0.0),
        jnp.where(same_sub, 1.0, 0.0),
    ], axis=0).astype(BF16)

    t_i = lax.broadcasted_iota(I32, (c, H_B * c), 0)
    s_i = lax.broadcasted_iota(I32, (c, H_B * c), 1) % c
    dsub = t_i // GLA_SUB - s_i // GLA_SUB
    intra = (dsub == 0) & (s_i <= t_i)
    head_k = lax.broadcasted_iota(I32, (c, WK_B), 1) // DK_B
    head_v = lax.broadcasted_iota(I32, (c, WV_B), 1) // DV_B

    def stack_heads(x, head_of_lane):
        zero = jnp.zeros_like(x)
        return jnp.concatenate([jnp.where(head_of_lane == h, x, zero) for h in range(H_B)], axis=0)

    def shift_rows(x, n):
        return jnp.concatenate([x[:n], x[:c - n]], axis=0)

    gn = gn_ref[...]
    for ci in range(n_chunks):
        rows = slice(ci * c, (ci + 1) * c)
        q = q_ref[rows, :]
        k = k_ref[rows, :]
        g = g_ref[rows, :]
        v = v_ref[rows, :]

        g_hi = g.astype(BF16)
        g_r = g - g_hi.astype(F32)
        g_mid = g_r.astype(BF16)
        g_parts = jnp.concatenate([g_hi, g_mid, (g_r - g_mid.astype(F32)).astype(BF16)], axis=1)
        cums = _dot(cum_ops, g_parts)
        cums = cums[:, :WK_B] + (cums[:, WK_B:2 * WK_B] + cums[:, 2 * WK_B:])
        bl, bc, tot = cums[:c], cums[c:2 * c], cums[2 * c:]
        pre = bc - bl
        q_loc = q * jnp.exp(bl)
        k_loc = k * jnp.exp(-bl)
        k_end = k * jnp.exp(tot - bl)
        q_far = [q_loc] + [q_loc * jnp.exp(pre - shift_rows(pre, GLA_SUB * d)) for d in range(1, n_sub - 1)]

        a0 = _dot_nt(q_loc.astype(BF16), stack_heads(k_loc, head_k).astype(BF16))
        af = _dot_nt(jnp.concatenate(q_far, axis=0).astype(BF16), stack_heads(k_end, head_k).astype(BF16))
        att = jnp.where(intra, a0, 0.0)
        for d in range(1, n_sub):
            att = jnp.where(dsub == d, af[(d - 1) * c:d * c], att)

        q_in = q * jnp.exp(bc)
        o = _dot(att.astype(BF16), stack_heads(v, head_v)) + _dot(q_in.astype(BF16), state_ref[...].astype(BF16))

        k_out = k * jnp.exp(bc[c - 1:c, :] - bc)
        kv = _dot_tn(k_out.astype(BF16), v)
        gsum = _dot_tn(g_parts, jnp.ones((c, LANES), BF16))
        decay = jnp.exp(gsum[:WK_B] + (gsum[WK_B:2 * WK_B] + gsum[2 * WK_B:]))
        state_ref[...] = (state_ref[...] * jnp.concatenate([decay] * (WV_B // LANES), axis=1)
                          + jnp.where(state_diag, kv, 0.0))

        for h in range(H_B):
            lanes = slice(h * DV_B, (h + 1) * DV_B)
            oh = o[:, lanes]
            yh = oh * lax.rsqrt(jnp.mean(oh * oh, axis=-1, keepdims=True) + EPS) * gn
            y_ref[rows, lanes] = (yh * r_ref[rows, lanes].astype(F32)).astype(y_ref.dtype)

    @pl.when(tb == n_tb - 1)
    def _():
        for h in range(H_B):
            sfin_ref[h] = state_ref[h * DK_B:(h + 1) * DK_B, h * DV_B:(h + 1) * DV_B]


def _gla(q, k, g, v, r, s0, gn, batch, seq, tb_rows):
    n_tb = seq // tb_rows
    row = lambda w: pl.BlockSpec((tb_rows, w), lambda b, t: (b * n_tb + t, 0))
    state = pl.BlockSpec((None, H_B, DK_B, DV_B), lambda b, t: (b, 0, 0, 0))
    return pl.pallas_call(
        _gla_kernel,
        grid=(batch, n_tb),
        in_specs=[row(WK_B), row(WK_B), row(WK_B), row(WV_B), row(WV_B), state,
                  pl.BlockSpec((1, DV_B), lambda b, t: (0, 0))],
        out_specs=[row(WV_B), state],
        out_shape=[jax.ShapeDtypeStruct((batch * seq, WV_B), BF16),
                   jax.ShapeDtypeStruct((batch, H_B, DK_B, DV_B), F32)],
        scratch_shapes=[pltpu.VMEM((WK_B, WV_B), F32)],
        compiler_params=_cparams(("arbitrary", "arbitrary")),
        name="gla",
    )(q, k, g, v, r, s0, gn)


def _mixout_kernel(x_ref, ya_ref, yb_ref, ga_ref, gb_ref, mod_ref, g2_ref, wpa_ref, wpb_ref, wo_ref, rw_ref,
                   x1_ref, hp_ref, lg_ref):
    br_a = _dot(ya_ref[...], wpa_ref[...])
    br_b = _dot(yb_ref[...], wpb_ref[...])
    merged = ga_ref[...].astype(F32) * br_a + gb_ref[...].astype(F32) * br_b
    out = _dot(merged.astype(BF16), wo_ref[...])
    x1 = x_ref[...] + mod_ref[2] * out
    x1_ref[...] = x1
    y = x1 * lax.rsqrt(jnp.mean(x1 * x1, axis=-1, keepdims=True) + EPS) * g2_ref[...]
    h = y * (1.0 + mod_ref[4]) + mod_ref[3]
    hp_ref[...] = _pack_bf16_pair(h)
    h_hi = h.astype(BF16)
    h_lo = (h - h_hi.astype(F32)).astype(BF16)
    lg_ref[...] = _dot_nt(rw_ref[0], h_hi) + (_dot_nt(rw_ref[0], h_lo) + _dot_nt(rw_ref[1], h_hi))


def _mixout(x, ya, yb, ga, gb, mod, rows_per_mod, tm, g2, wpa, wpb, wo, rw):
    n = x.shape[0]
    mod_rows = mod.shape[2]
    tiles_per_mod = rows_per_mod // tm
    row = lambda w: pl.BlockSpec((tm, w), lambda i: (i, 0))
    return pl.pallas_call(
        _mixout_kernel,
        grid=(n // tm,),
        in_specs=[row(D_MODEL), row(W_A), row(WV_B), row(D_MODEL), row(D_MODEL),
                  pl.BlockSpec((6, None, mod_rows, D_MODEL), lambda i: (0, i // tiles_per_mod, 0, 0)),
                  _const_spec(g2.shape), _const_spec(wpa.shape), _const_spec(wpb.shape), _const_spec(wo.shape),
                  _const_spec(rw.shape)],
        out_specs=[row(D_MODEL), row(D_MODEL // 2), pl.BlockSpec((N_EXPERTS, tm), lambda i: (0, i))],
        out_shape=[jax.ShapeDtypeStruct((n, D_MODEL), F32),
                   jax.ShapeDtypeStruct((n, D_MODEL // 2), U32),
                   jax.ShapeDtypeStruct((N_EXPERTS, n), F32)],
        compiler_params=_cparams(("arbitrary",)),
        name="mixout",
    )(x, ya, yb, ga, gb, mod, g2, wpa, wpb, wo, rw)


def _route_kernel(lg_ref, lg_last_ref, bias_ref, idx_ref, w_ref, rank_ref, cnt_ref, carry_ref):
    i = pl.program_id(0)
    tm = lg_ref.shape[1]

    @pl.when(i == 0)
    def _():
        carry_ref[...] = jnp.zeros_like(carry_ref)

    lg = jnp.where(i == pl.num_programs(0) - 1, lg_last_ref[...], lg_ref[...])
    s = _sigmoid(lg)
    sb = s + bias_ref[...][:, 0:1]
    ninf = -jnp.inf

    e_in_group = lax.broadcasted_iota(I32, (GROUP_SIZE, tm), 0)
    gscores = []
    for g in range(N_GROUPS):
        blk = sb[g * GROUP_SIZE:(g + 1) * GROUP_SIZE, :]
        m1 = jnp.max(blk, axis=0, keepdims=True)
        first = jnp.min(jnp.where(blk == m1, e_in_group, GROUP_SIZE), axis=0, keepdims=True)
        m2 = jnp.max(jnp.where(e_in_group == first, ninf, blk), axis=0, keepdims=True)
        gscores.append(m1 + m2)
    gs = jnp.concatenate(gscores, axis=0)

    g_iota = lax.broadcasted_iota(I32, (N_GROUPS, tm), 0)
    chosen = jnp.zeros((N_GROUPS, tm), F32)
    for _ in range(TOPK_GROUPS):
        m = jnp.max(gs, axis=0, keepdims=True)
        first = jnp.min(jnp.where(gs == m, g_iota, N_GROUPS), axis=0, keepdims=True)
        pick = g_iota == first
        chosen = jnp.where(pick, 1.0, chosen)
        gs = jnp.where(pick, ninf, gs)
    allowed = jnp.concatenate(
        [jnp.broadcast_to(chosen[g:g + 1, :], (GROUP_SIZE, tm)) for g in range(N_GROUPS)], axis=0) > 0.5
    cand = jnp.where(allowed, sb, ninf)

    e_iota = lax.broadcasted_iota(I32, (N_EXPERTS, tm), 0)
    onehot = jnp.zeros((N_EXPERTS, tm), F32)
    idxs, ws = [], []
    for _ in range(TOP_K):
        m = jnp.max(cand, axis=0, keepdims=True)
        first = jnp.min(jnp.where(cand == m, e_iota, N_EXPERTS), axis=0, keepdims=True)
        pick = e_iota == first
        idxs.append(first)
        ws.append(jnp.sum(jnp.where(pick, s, 0.0), axis=0, keepdims=True))
        onehot = jnp.where(pick, 1.0, onehot)
        cand = jnp.where(pick, ninf, cand)
    w = jnp.concatenate(ws, axis=0)
    w_ref[...] = w / jnp.sum(w, axis=0, keepdims=True) * ROUTE_SCALE
    idx_ref[...] = jnp.concatenate(idxs, axis=0)

    r = lax.broadcasted_iota(I32, (tm, tm), 0)
    c = lax.broadcasted_iota(I32, (tm, tm), 1)
    earlier = jnp.where(r < c, 1.0, 0.0).astype(BF16)
    carry = carry_ref[...]
    before = _dot(onehot.astype(BF16), earlier) + jnp.concatenate([carry] * (tm // LANES), axis=1)
    rank_ref[...] = jnp.concatenate(
        [jnp.sum(jnp.where(e_iota == ix, before, 0.0), axis=0, keepdims=True) for ix in idxs], axis=0).astype(I32)
    carry = carry + _dot(onehot.astype(BF16), jnp.ones((tm, LANES), BF16))
    carry_ref[...] = carry
    cnt_ref[...] = carry


def _route(logits_t, logits_last, bias):
    tm = MOE_TILE
    n_first = logits_t.shape[1] // tm
    n = logits_t.shape[1] + logits_last.shape[1]
    tok = lambda: pl.BlockSpec((TOP_K, tm), lambda i: (0, i))
    return pl.pallas_call(
        _route_kernel,
        grid=(n // tm,),
        in_specs=[pl.BlockSpec((N_EXPERTS, tm), lambda i: (0, jnp.minimum(i, n_first - 1))),
                  _const_spec(logits_last.shape), _const_spec(bias.shape)],
        out_specs=[tok(), tok(), tok(), _const_spec((N_EXPERTS, LANES))],
        out_shape=[jax.ShapeDtypeStruct((TOP_K, n), I32), jax.ShapeDtypeStruct((TOP_K, n), F32),
                   jax.ShapeDtypeStruct((TOP_K, n), I32), jax.ShapeDtypeStruct((N_EXPERTS, LANES), F32)],
        scratch_shapes=[pltpu.VMEM((N_EXPERTS, LANES), F32)],
        compiler_params=_cparams(("arbitrary",)),
        name="route",
    )(logits_t, logits_last, bias)


def _dest_kernel(idx_ref, rank_ref, cnt_ref, dest_ref, start_ref):
    i = pl.program_id(0)
    tm = idx_ref.shape[1]

    @pl.when(i == 0)
    def _():
        cnt = cnt_ref[...]
        padded = jnp.floor((cnt + (EXPERT_TILE - 1)) * (1.0 / EXPERT_TILE)) * EXPERT_TILE
        r = lax.broadcasted_iota(I32, (N_EXPERTS, N_EXPERTS), 0)
        c = lax.broadcasted_iota(I32, (N_EXPERTS, N_EXPERTS), 1)
        start_ref[...] = _dot(jnp.where(c < r, 1.0, 0.0).astype(F32), padded, precision=HIGHEST)

    start = jnp.concatenate([start_ref[...]] * (tm // LANES), axis=1)
    e_iota = lax.broadcasted_iota(I32, (N_EXPERTS, tm), 0)
    idx = idx_ref[...]
    first = jnp.concatenate(
        [jnp.sum(jnp.where(e_iota == idx[kk:kk + 1, :], start, 0.0), axis=0, keepdims=True) for kk in range(TOP_K)],
        axis=0)
    dest_ref[...] = first.astype(I32) + rank_ref[...]


def _dest(idx_t, rank_t, counts):
    n = idx_t.shape[1]
    tm = MOE_TILE
    tok = lambda: pl.BlockSpec((TOP_K, tm), lambda i: (0, i))
    return pl.pallas_call(
        _dest_kernel,
        grid=(n // tm,),
        in_specs=[tok(), tok(), _const_spec(counts.shape)],
        out_specs=pl.BlockSpec((None, TOP_K, tm), lambda i: (i, 0, 0)),
        out_shape=jax.ShapeDtypeStruct((n // tm, TOP_K, tm), I32),
        scratch_shapes=[pltpu.VMEM((N_EXPERTS, LANES), F32)],
        compiler_params=_cparams(("arbitrary",)),
        name="dest",
    )(idx_t, rank_t, counts)


def _sc_worker():
    return lax.axis_index("s") * SC_CORES + lax.axis_index("c")


def _sc_mesh():
    return plsc.VectorSubcoreMesh(core_axis_name="c", subcore_axis_name="s")


def _dispatch(dest_tiles, hp, hp_last, n_rows_out):
    n_tok_tiles = dest_tiles.shape[0]
    n_first = hp.shape[0] // MOE_TILE
    assert n_first + hp_last.shape[0] // MOE_TILE == n_tok_tiles
    w = hp.shape[1]
    iters = -(-n_tok_tiles // SC_WORKERS)

    def body(hp_hbm, hp_last_hbm, dest_hbm, xs_hbm, idx_v, rows_v, sem):
        wid = _sc_worker()

        def scatter():
            copies = [pltpu.async_copy(rows_v, xs_hbm.at[idx_v.at[kk]], sem) for kk in range(TOP_K)]
            for cp in copies:
                cp.wait()

        @pl.loop(0, iters)
        def _(j):
            tile = wid + SC_WORKERS * j

            @pl.when(tile < n_first)
            def _():
                pltpu.sync_copy(dest_hbm.at[tile], idx_v)
                pltpu.sync_copy(hp_hbm.at[pl.ds(tile * MOE_TILE, MOE_TILE)], rows_v)
                scatter()

            @pl.when((tile >= n_first) & (tile < n_tok_tiles))
            def _():
                pltpu.sync_copy(dest_hbm.at[tile], idx_v)
                pltpu.sync_copy(hp_last_hbm.at[pl.ds((tile - n_first) * MOE_TILE, MOE_TILE)], rows_v)
                scatter()

    return pl.kernel(
        body,
        out_type=jax.ShapeDtypeStruct((n_rows_out, w), U32),
        mesh=_sc_mesh(),
        scratch_types=[pltpu.VMEM((TOP_K, MOE_TILE), I32), pltpu.VMEM((MOE_TILE, w), U32), pltpu.SemaphoreType.DMA],
        name="dispatch",
    )(hp, hp_last, dest_tiles)


def _gather_back(ys, dest_tiles):
    n_tok_tiles = dest_tiles.shape[0]
    w = ys.shape[1]
    n_lists = n_tok_tiles * TOP_K
    iters = -(-n_lists // SC_WORKERS)

    def body(ys_hbm, dest_hbm, out_hbm, idx_v, rows_v, sem):
        wid = _sc_worker()

        @pl.loop(0, iters)
        def _(j):
            lst = wid + SC_WORKERS * j

            @pl.when(lst < n_lists)
            def _():
                pltpu.sync_copy(dest_hbm.at[pl.ds(lst * MOE_TILE, MOE_TILE)], idx_v)
                pltpu.async_copy(ys_hbm.at[idx_v], rows_v, sem).wait()
                pltpu.sync_copy(rows_v, out_hbm.at[pl.ds(lst * MOE_TILE, MOE_TILE)])

    out = pl.kernel(
        body,
        out_type=jax.ShapeDtypeStruct((n_lists * MOE_TILE, w), U32),
        mesh=_sc_mesh(),
        scratch_types=[pltpu.VMEM((MOE_TILE,), I32), pltpu.VMEM((MOE_TILE, w), U32), pltpu.SemaphoreType.DMA],
        name="gather_back",
    )(ys, dest_tiles.reshape(-1))
    return out.reshape(n_tok_tiles, TOP_K, MOE_TILE, w)


def _experts_kernel(first_ref, count_ref, rows_ref, xs_hbm, wg_ref, wu_ref, wd_ref, after_hbm, ys_hbm,
                    xbuf, ybuf, wgb_ref, wub_ref, wdb_ref, sem):
    del after_hbm
    e = pl.program_id(0)
    tm = EXPERT_TILE
    ring = EXPERT_RING
    n_e = count_ref[e]
    total = first_ref[N_EXPERTS - 1] + count_ref[N_EXPERTS - 1]

    def x_copy(t, slot):
        return pltpu.make_async_copy(xs_hbm.at[pl.ds(t * tm, tm), :], xbuf.at[slot], sem.at[0, slot])

    def y_copy(t, slot):
        return pltpu.make_async_copy(ybuf.at[slot], ys_hbm.at[pl.ds(t * tm, tm), :], sem.at[1, slot])

    @pl.when(e == 0)
    def _():
        for t in range(ring - 1):
            @pl.when(t < total)
            def _():
                x_copy(t, t).start()

    @pl.when(n_e > 0)
    def _():
        wgb_ref[...] = wg_ref[...].astype(BF16)
        wub_ref[...] = wu_ref[...].astype(BF16)
        wdb_ref[...] = wd_ref[...].astype(BF16)
        half = D_MODEL // 2

        def body(j, carry):
            t = first_ref[e] + j
            slot = t % ring
            x_copy(t, slot).wait()

            @pl.when(t + ring - 1 < total)
            def _():
                x_copy(t + ring - 1, (t + ring - 1) % ring).start()

            @pl.when(t >= ring)
            def _():
                y_copy(t - ring, slot).wait()

            xs = xbuf[slot]
            filled = lax.broadcasted_iota(I32, xs.shape, 0) < rows_ref[t]
            lo, hi = _unpack_bf16_pair(jnp.where(filled, xs, jnp.zeros_like(xs)))
            gate = _dot(lo, wgb_ref[:half, :]) + _dot(hi, wgb_ref[half:, :])
            up = _dot(lo, wub_ref[:half, :]) + _dot(hi, wub_ref[half:, :])
            act = (_silu(gate) * up).astype(BF16)
            ybuf[slot] = _pack_bf16_pair(_dot(act, wdb_ref[...]))
            y_copy(t, slot).start()
            return carry

        lax.fori_loop(0, n_e, body, 0)

    @pl.when(e == N_EXPERTS - 1)
    def _():
        for back in range(ring, 0, -1):
            @pl.when(total >= back)
            def _():
                y_copy(total - back, (total - back) % ring).wait()


def _experts(tile_first, tile_count, tile_rows, xs, wg, wu, wd, after):
    n_rows, w = xs.shape
    tm = EXPERT_TILE
    expert = lambda e, tf, tc, tr: (e, 0, 0)
    return pl.pallas_call(
        _experts_kernel,
        grid_spec=pltpu.PrefetchScalarGridSpec(
            num_scalar_prefetch=3,
            grid=(N_EXPERTS,),
            in_specs=[pl.BlockSpec(memory_space=pl.ANY),
                      pl.BlockSpec((None, D_MODEL, D_EXPERT), expert),
                      pl.BlockSpec((None, D_MODEL, D_EXPERT), expert),
                      pl.BlockSpec((None, D_EXPERT, D_MODEL), expert),
                      pl.BlockSpec(memory_space=pl.ANY)],
            out_specs=pl.BlockSpec(memory_space=pl.ANY),
            scratch_shapes=[pltpu.VMEM((EXPERT_RING, tm, w), U32), pltpu.VMEM((EXPERT_RING, tm, w), U32),
                            pltpu.VMEM((D_MODEL, D_EXPERT), BF16), pltpu.VMEM((D_MODEL, D_EXPERT), BF16),
                            pltpu.VMEM((D_EXPERT, D_MODEL), BF16),
                            pltpu.SemaphoreType.DMA((2, EXPERT_RING))]),
        out_shape=jax.ShapeDtypeStruct((n_rows, w), U32),
        compiler_params=_cparams(("arbitrary",)),
        name="experts",
    )(tile_first, tile_count, tile_rows, xs, wg, wu, wd, after)


def _shared_kernel(hp_ref, sg_ref, su_ref, sd_ref, o_ref):
    half = D_MODEL // 2
    lo, hi = _unpack_bf16_pair(hp_ref[...])
    sg = sg_ref[...]
    su = su_ref[...]
    gate = _dot(lo, sg[:half]) + _dot(hi, sg[half:])
    up = _dot(lo, su[:half]) + _dot(hi, su[half:])
    o_ref[...] = _dot((_silu(gate) * up).astype(BF16), sd_ref[...]).astype(o_ref.dtype)


def _shared(hp, tm, sg, su, sd):
    n = hp.shape[0]
    return pl.pallas_call(
        _shared_kernel,
        grid=(n // tm,),
        in_specs=[pl.BlockSpec((tm, D_MODEL // 2), lambda i: (i, 0)),
                  _const_spec(sg.shape), _const_spec(su.shape), _const_spec(sd.shape)],
        out_specs=pl.BlockSpec((tm, D_MODEL), lambda i: (i, 0)),
        out_shape=jax.ShapeDtypeStruct((n, D_MODEL), BF16),
        compiler_params=_cparams(("arbitrary",)),
        name="shared",
    )(hp, sg, su, sd)


def _combine_kernel(yt_ref, x1_ref, ysh_ref, w_ref, mod_ref, fg_ref, *rest):
    o_ref = rest[-1]
    y = ysh_ref[...].astype(F32)
    w = w_ref[...]
    for kk in range(TOP_K):
        ylo, yhi = _unpack_bf16_pair(yt_ref[kk])
        y = y + w[:, kk:kk + 1] * jnp.concatenate([ylo.astype(F32), yhi.astype(F32)], axis=1)
    x2 = x1_ref[...] + mod_ref[5] * y
    o_ref[...] = x2 * lax.rsqrt(jnp.mean(x2 * x2, axis=-1, keepdims=True) + EPS) * fg_ref[...]


def _combine(y_tok, y_tok_tile0, x1, y_shared, w, w_tile0, mod, rows_per_mod, fg, tile0, n_tiles, prev=None):
    n = x1.shape[0]
    tm = MOE_TILE
    mod_rows = mod.shape[2]
    tiles_per_mod = rows_per_mod // tm
    own = lambda i: (i + tile0, 0)
    return pl.pallas_call(
        _combine_kernel,
        grid=(n_tiles,),
        in_specs=[pl.BlockSpec((None, TOP_K, tm, D_MODEL // 2), lambda i: (i + y_tok_tile0, 0, 0, 0)),
                  pl.BlockSpec((tm, D_MODEL), own),
                  pl.BlockSpec((tm, D_MODEL), own),
                  pl.BlockSpec((tm, TOP_K), lambda i: (i + w_tile0, 0)),
                  pl.BlockSpec((6, None, mod_rows, D_MODEL), lambda i: (0, (i + tile0) // tiles_per_mod, 0, 0)),
                  _const_spec(fg.shape)] + ([] if prev is None else [pl.BlockSpec(memory_space=pl.ANY)]),
        out_specs=pl.BlockSpec((tm, D_MODEL), own),
        out_shape=jax.ShapeDtypeStruct((n, D_MODEL), F32),
        input_output_aliases={} if prev is None else {6: 0},
        compiler_params=_cparams(("arbitrary",)),
        name="combine",
    )(y_tok, x1, y_shared, w, mod, fg, *(() if prev is None else (prev,)))


def _split_w_in(w_in):
    sizes = (W_A, W_A, W_A, H_A, WK_B, WK_B, WV_B, WV_B, GK_RANK, D_MODEL, D_MODEL)
    segs, o = [], 0
    for s in sizes:
        segs.append(w_in[:, o:o + s])
        o += s
    return segs


def kernel(x_prompt, x_sample, cache_k, cache_v, cache_logf, state_gla, page_table, c_prompt, c_sample, w_ada, b_ada,
           norm1_g, w_in, b_f, w_gk2, b_gk, gla_norm_g, w_pa, w_pb, w_o, norm2_g, router_w, router_bias, w_gate, w_up,
           w_down, ws_gate, ws_up, ws_down, final_g):
    assert w_ada.shape[0] == 1, "single layer"
    bp, tp, d = x_prompt.shape
    bs, ts, _ = x_sample.shape
    n_p, n_s = bp * tp, bs * ts
    n_tot = n_p + n_s
    assert n_s == MOE_TILE and n_p % MOE_TILE == 0 and tp % FOX_TILE == 0

    q_a, k_a, v_a, f_a, q_b, k_b, v_b, r_b, gk1, gate_a, gate_b = _split_w_in(w_in[0])
    wts = (jnp.concatenate([q_a, k_a], axis=1).astype(BF16),
           v_a.astype(BF16),
           jnp.concatenate([k_a, v_a], axis=1).T.astype(BF16),
           jnp.pad(f_a, ((0, 0), (0, LANES - H_A))).astype(BF16),
           jnp.pad(b_f[0].reshape(1, H_A), ((0, 0), (0, LANES - H_A))),
           jnp.concatenate([q_b, k_b, v_b, r_b], axis=1).astype(BF16),
           jnp.pad(gk1, ((0, 0), (0, LANES - GK_RANK))).astype(BF16),
           jnp.pad(w_gk2[0], ((0, LANES - GK_RANK), (0, 0))).astype(BF16),
           b_gk[0].reshape(1, WK_B),
           jnp.concatenate([gate_a, gate_b], axis=1).astype(BF16))
    g1 = norm1_g[0].reshape(1, d)
    g2 = norm2_g[0].reshape(1, d)
    gn = gla_norm_g[0].reshape(1, DV_B)
    fg = final_g.reshape(1, d)
    wpa, wpb, wo = w_pa[0].astype(BF16), w_pb[0].astype(BF16), w_o[0].astype(BF16)
    rw_hi = router_w[0].T.astype(BF16)
    rw_t = jnp.stack([rw_hi, (router_w[0].T - rw_hi.astype(F32)).astype(BF16)])
    rbias = jnp.broadcast_to(router_bias[0].reshape(N_EXPERTS, 1), (N_EXPERTS, LANES))
    sg, su, sd = ws_gate[0].astype(BF16), ws_up[0].astype(BF16), ws_down[0].astype(BF16)

    ada = _ada(jnp.concatenate([c_prompt, c_sample], axis=0), w_ada[0], b_ada[0].reshape(1, -1))
    ada = ada.reshape(bp + bs, 6, d)
    mod_p = ada[:bp].transpose(1, 0, 2)[:, :, None, :]
    mod_s = jnp.repeat(ada[bp:], ts, axis=0).transpose(1, 0, 2)[:, None, :, :]

    xp = x_prompt.reshape(n_p, d)
    tm_p = 512
    (q, kh, vth, kt, vt, lf, fc, qb, kb, gk, vb, rb, ga_p, gb_p) = _inproj(
        xp, mod_p, tp, tp // tm_p, tm_p, g1, wts, True)
    ya_p = _fox(q, kh, vth, fc, bp, tp)
    yb_p, s_p = _gla(qb, kb, gk, vb, rb, jnp.zeros((bp, H_B, DK_B, DV_B), F32), gn, bp, tp, 256)
    k_prompt = kt.reshape(bp, H_A, DH_A, tp).transpose(0, 3, 1, 2)[None]
    v_prompt = vt.reshape(bp, H_A, DH_A, tp).transpose(0, 3, 1, 2)[None]
    logf_prompt = lf.reshape(1, bp, tp, H_A)

    xs_in = x_sample.reshape(n_s, d)
    (q, k, v, lf, qb, kb, gk, vb, rb, ga, gb) = _inproj(xs_in, mod_s, n_s, 1, n_s, g1, wts, False)
    page_rows = cache_k.shape[2]
    tq = 8
    cache_kt = jnp.transpose(cache_k, (0, 1, 3, 4, 2))
    cache_vt = jnp.transpose(cache_v, (0, 1, 3, 4, 2))
    suf = _lfsuf(page_table, jnp.transpose(cache_logf, (0, 1, 3, 2)))

    def new_kv(a):
        a = a.reshape(bs, ts, H_A, DH_A).transpose(0, 2, 3, 1)
        return jnp.pad(a, ((0, 0), (0, 0), (0, 0), (0, page_rows - ts)))

    q_dec = jnp.pad(q.astype(F32).reshape(bs, ts, H_A, DH_A).transpose(0, 2, 1, 3),
                    ((0, 0), (0, 0), (0, tq - ts), (0, 0)))
    lfn = jnp.pad(lf.reshape(bs, ts, H_A).transpose(0, 2, 1), ((0, 0), (0, 0), (0, page_rows - ts)))
    ya = _foxdec(page_table, q_dec, new_kv(k), new_kv(v), lfn, suf, cache_kt, cache_vt)
    ya = ya[:, :, :ts].transpose(0, 2, 1, 3).reshape(n_s, W_A).astype(BF16)
    pad = lambda a: jnp.pad(a.reshape(bs, ts, -1), ((0, 0), (0, GLA_CHUNK - ts), (0, 0))).reshape(bs * GLA_CHUNK, -1)
    yb, s_s = _gla(pad(qb), pad(kb), pad(gk), pad(vb), pad(rb), state_gla[0], gn, bs, GLA_CHUNK, GLA_CHUNK)
    yb = yb.reshape(bs, GLA_CHUNK, WV_B)[:, :ts].reshape(n_s, WV_B)
    k_sample = k.reshape(1, bs, ts, H_A, DH_A)
    v_sample = v.reshape(1, bs, ts, H_A, DH_A)
    logf_sample = lf.reshape(1, bs, ts, H_A)
    x1_s, hp_s, logits_s = _mixout(xs_in, ya, yb, ga, gb, mod_s, n_s, n_s, g2, wpa, wpb, wo, rw_t)

    x1_p, hp_p, logits_p = _mixout(xp, ya_p, yb_p, ga_p, gb_p, mod_p, tp, 2 * MOE_TILE, g2, wpa, wpb, wo, rw_t)
    idx_t, w_t, rank_t, counts = _route(logits_p, logits_s, rbias)
    dest_tiles = _dest(idx_t, rank_t, counts)
    padded = (counts[:, 0].astype(I32) + EXPERT_TILE - 1) // EXPERT_TILE * EXPERT_TILE
    pad_end = jnp.cumsum(padded)
    n_tiles = -(-n_tot * TOP_K // EXPERT_TILE) + N_EXPERTS
    tile_first_row = jnp.arange(n_tiles, dtype=I32) * EXPERT_TILE
    tile_expert = jnp.minimum(jnp.sum((pad_end[None, :] <= tile_first_row[:, None]).astype(I32), axis=1),
                              N_EXPERTS - 1)
    used_end = pad_end - padded + counts[:, 0].astype(I32)
    tile_rows = jnp.clip(used_end[tile_expert] - tile_first_row, 0, EXPERT_TILE).astype(I32)
    tile_first = ((pad_end - padded) // EXPERT_TILE).astype(I32)
    tile_count = (padded // EXPERT_TILE).astype(I32)

    xs_sorted = _dispatch(dest_tiles, hp_p, hp_s, n_tiles * EXPERT_TILE)
    ysh_p = _shared(hp_p, 2 * MOE_TILE, sg, su, sd)
    ysh_s = _shared(hp_s, MOE_TILE, sg, su, sd)
    ys_sorted = _experts(tile_first, tile_count, tile_rows, xs_sorted, w_gate[0], w_up[0], w_down[0], ysh_p)
    n_p_tiles = n_p // MOE_TILE
    part = n_p_tiles // GATHER_PARTS
    w_tok = w_t.T
    y_prompt = None
    for j in range(GATHER_PARTS):
        last = j == GATHER_PARTS - 1
        y_tok = _gather_back(ys_sorted, dest_tiles[j * part:] if last else dest_tiles[j * part:(j + 1) * part])
        y_prompt = _combine(y_tok, 0, x1_p, ysh_p, w_tok, j * part, mod_p, tp, fg, j * part, part, prev=y_prompt)
    y_sample = _combine(y_tok, part, x1_s, ysh_s, w_tok, n_p_tiles, mod_s, n_s, fg, 0, n_s // MOE_TILE)

    return (y_prompt.reshape(bp, tp, d), y_sample.reshape(bs, ts, d),
            k_prompt, v_prompt, logf_prompt, s_p[None],
            k_sample, v_sample, logf_sample, s_s[None])
```

```python
import functools

import jax
import jax.numpy as jnp
from jax import lax
from jax.experimental import pallas as pl
from jax.experimental.pallas import tpu as pltpu
from jax.experimental.pallas import tpu_sc as plsc

F32 = jnp.float32
BF16 = jnp.bfloat16
U32 = jnp.uint32
I32 = jnp.int32

EPS = 1e-6
D_MODEL = 1024
H_A, DH_A, W_A = 8, 64, 512
H_B, DK_B, DV_B, WK_B, WV_B = 4, 64, 128, 256, 512
GK_RANK = 16
GATE_NORM = 16.0
N_EXPERTS, TOP_K, N_GROUPS, TOPK_GROUPS = 256, 8, 8, 4
GROUP_SIZE = N_EXPERTS // N_GROUPS
D_EXPERT = 256
ROUTE_SCALE = 2.5
MOE_TILE = 128
EXPERT_TILE = 256
EXPERT_RING = 4
GATHER_PARTS = 4
LANES = 128
SC_CORES = 2
SC_WORKERS = 32
GLA_CHUNK = 64
GLA_SUB = 16
FOX_TILE = 512
FOX_SUB = 128
DEC_PAGES = 8
DEC_RING = 3
NEG = -1e30
LOG2E = 1.4426950408889634
VMEM_LIMIT = 56 * 1024 * 1024
HIGHEST = lax.Precision.HIGHEST

NT_DIMS = (((1,), (1,)), ((), ()))
TN_DIMS = (((0,), (0,)), ((), ()))


def _cparams(sem):
    return pltpu.CompilerParams(dimension_semantics=sem, vmem_limit_bytes=VMEM_LIMIT)


def _dot(a, b, **kw):
    return jnp.dot(a, b, preferred_element_type=F32, **kw)


def _dot_nt(a, b, **kw):
    return lax.dot_general(a, b, NT_DIMS, preferred_element_type=F32, **kw)


def _dot_tn(a, b, **kw):
    return lax.dot_general(a, b, TN_DIMS, preferred_element_type=F32, **kw)


def _sigmoid(x):
    return 0.5 * jnp.tanh(0.5 * x) + 0.5


def _silu(x):
    return x * _sigmoid(x)


def _log_sigmoid(x):
    return jnp.minimum(x, 0.0) - jnp.log(1.0 + jnp.exp(-jnp.abs(x)))


def _pack_bf16_pair(x):
    c = x.shape[1] // 2
    bits = pltpu.bitcast(x.astype(BF16).astype(F32), U32)
    return (bits[:, c:] & jnp.uint32(0xFFFF0000)) | (bits[:, :c] >> 16)


def _unpack_bf16_pair(p):
    lo = pltpu.bitcast(p << 16, F32).astype(BF16)
    hi = pltpu.bitcast(p & jnp.uint32(0xFFFF0000), F32).astype(BF16)
    return lo, hi


def _ada_kernel(c_ref, w_ref, b_ref, o_ref):
    c = c_ref[...]
    o_ref[...] = _dot(_silu(c).astype(BF16), w_ref[...].astype(BF16)) + b_ref[...]


def _ada(c, w, b):
    n, d = c.shape
    cols = w.shape[1]
    tn = 1536
    return pl.pallas_call(
        _ada_kernel,
        grid=(cols // tn,),
        in_specs=[pl.BlockSpec((n, d), lambda j: (0, 0)),
                  pl.BlockSpec((d, tn), lambda j: (0, j)),
                  pl.BlockSpec((1, tn), lambda j: (0, j))],
        out_specs=pl.BlockSpec((n, tn), lambda j: (0, j)),
        out_shape=jax.ShapeDtypeStruct((n, cols), F32),
        compiler_params=_cparams(("arbitrary",)),
        name="ada",
    )(c, w, b)


def _inproj_kernel(*refs, seq_tiles, prompt):
    (x_ref, mod_ref, g_ref, wqk_ref, wv_ref, wkvt_ref, wf_ref, bf_ref, wb_ref, wgk1_ref, wgk2_ref, bgk_ref,
     wgate_ref) = refs[:13]
    if prompt:
        (q_ref, kh_ref, vth_ref, kt_ref, vt_ref, lf_ref, fc_ref, qb_ref, kb_ref, gk_ref, vb_ref, rb_ref, ga_ref,
         gb_ref, carry_ref) = refs[13:]
    else:
        q_ref, k_ref, v_ref, lf_ref, qb_ref, kb_ref, gk_ref, vb_ref, rb_ref, ga_ref, gb_ref = refs[13:]
    i = pl.program_id(0)
    tm = x_ref.shape[0]
    x = x_ref[...]
    y = x * lax.rsqrt(jnp.mean(x * x, axis=-1, keepdims=True) + EPS) * g_ref[...]
    h = (y * (1.0 + mod_ref[1]) + mod_ref[0]).astype(BF16)

    z = _dot(h, wqk_ref[...])
    q_ref[...] = (z[:, :W_A] * (DH_A ** -0.5 * (LOG2E if prompt else 1.0))).astype(BF16)
    k = z[:, W_A:]

    lf = _log_sigmoid(_dot(h, wf_ref[...]) + bf_ref[...])
    lf_ref[...] = lf[:, :H_A]

    if prompt:
        kvt = _dot_nt(wkvt_ref[...], h)
        kt_ref[...] = kvt[:W_A]
        vt_ref[...] = kvt[W_A:]
        vth_ref[...] = kvt[W_A:].astype(BF16)
        kh_ref[...] = k.astype(BF16)

        @pl.when(i % seq_tiles == 0)
        def _():
            carry_ref[...] = jnp.zeros_like(carry_ref)

        r = lax.broadcasted_iota(I32, (tm, tm), 0)
        c = lax.broadcasted_iota(I32, (tm, tm), 1)
        lower = jnp.where(c <= r, 1.0, 0.0).astype(BF16)
        lf_hi = lf.astype(BF16)
        lf_r = lf - lf_hi.astype(F32)
        lf_mid = lf_r.astype(BF16)
        lf_parts = jnp.concatenate([lf_hi, lf_mid, (lf_r - lf_mid.astype(F32)).astype(BF16)], axis=1)
        sum3 = lambda a: a[:, :LANES] + (a[:, LANES:2 * LANES] + a[:, 2 * LANES:])
        carry = carry_ref[...]
        fc = sum3(_dot(lower, lf_parts)) + carry[0:1, :]
        fc_ref[...] = fc[:, :H_A] * LOG2E
        carry_ref[...] = carry + sum3(_dot(jnp.ones((8, tm), BF16), lf_parts))
    else:
        k_ref[...] = k
        v_ref[...] = _dot(h, wv_ref[...])

    z = _dot(h, wb_ref[...])
    qb_ref[...] = z[:, :WK_B] * DK_B ** -0.5
    kb_ref[...] = z[:, WK_B:2 * WK_B]
    vb_ref[...] = z[:, 2 * WK_B:2 * WK_B + WV_B].astype(BF16)
    rb_ref[...] = _silu(z[:, 2 * WK_B + WV_B:]).astype(BF16)

    lr = _dot(h, wgk1_ref[...]).astype(BF16)
    gk_ref[...] = _log_sigmoid(_dot(lr, wgk2_ref[...]) + bgk_ref[...]) * (1.0 / GATE_NORM)

    z = _dot(h, wgate_ref[...])
    ga_ref[...] = _sigmoid(z[:, :D_MODEL]).astype(BF16)
    gb_ref[...] = _sigmoid(z[:, D_MODEL:]).astype(BF16)


def _const_spec(shape):
    return pl.BlockSpec(shape, lambda i: (0,) * len(shape))


def _inproj(x, mod, rows_per_mod, seq_tiles, tm, g, wts, prompt):
    n = x.shape[0]
    mod_rows = mod.shape[2]
    tiles_per_mod = rows_per_mod // tm
    row = lambda w: pl.BlockSpec((tm, w), lambda i: (i, 0))
    if prompt:
        n_seq = n // (seq_tiles * tm)
        seq_t = pl.BlockSpec((None, W_A, tm), lambda i: (i // seq_tiles, 0, i % seq_tiles))
        kv_shape = (n_seq, W_A, seq_tiles * tm)
        outs = [
            (row(W_A), (n, W_A), BF16),
            (row(W_A), (n, W_A), BF16),
            (pl.BlockSpec((W_A, tm), lambda i: (0, i)), (W_A, n), BF16),
            (seq_t, kv_shape, F32),
            (seq_t, kv_shape, F32),
            (row(H_A), (n, H_A), F32),
            (row(H_A), (n, H_A), F32),
        ]
    else:
        outs = [
            (row(W_A), (n, W_A), BF16),
            (row(W_A), (n, W_A), F32),
            (row(W_A), (n, W_A), F32),
            (row(H_A), (n, H_A), F32),
        ]
    outs += [
        (row(WK_B), (n, WK_B), F32),
        (row(WK_B), (n, WK_B), F32),
        (row(WK_B), (n, WK_B), F32),
        (row(WV_B), (n, WV_B), BF16),
        (row(WV_B), (n, WV_B), BF16),
        (row(D_MODEL), (n, D_MODEL), BF16),
        (row(D_MODEL), (n, D_MODEL), BF16),
    ]
    return pl.pallas_call(
        functools.partial(_inproj_kernel, seq_tiles=seq_tiles, prompt=prompt),
        grid=(n // tm,),
        in_specs=[row(D_MODEL),
                  pl.BlockSpec((6, None, mod_rows, D_MODEL), lambda i: (0, i // tiles_per_mod, 0, 0)),
                  _const_spec(g.shape)] + [_const_spec(w.shape) for w in wts],
        out_specs=[o[0] for o in outs],
        out_shape=[jax.ShapeDtypeStruct(o[1], o[2]) for o in outs],
        scratch_shapes=[pltpu.VMEM((8, LANES), F32)] if prompt else [],
        compiler_params=_cparams(("arbitrary",)),
        name="inproj",
    )(x, mod, g, *wts)


def _fox_kernel(q_ref, k_ref, vt_ref, fc_ref, o_ref, qs_ref, m_ref, l_ref, acc_ref):
    qi = pl.program_id(1)
    ki = pl.program_id(2)
    t = q_ref.shape[0]
    pairs = H_A // 2
    sub = FOX_SUB

    @pl.when(ki == 0)
    def _():
        lane = lax.broadcasted_iota(I32, (t, LANES), 1)
        for hp in range(pairs):
            q = q_ref[:, hp * LANES:(hp + 1) * LANES]
            qs_ref[hp, :t, :] = jnp.where(lane < DH_A, q, jnp.zeros_like(q))
            qs_ref[hp, t:, :] = jnp.where(lane >= DH_A, q, jnp.zeros_like(q))
        m_ref[...] = jnp.full_like(m_ref, NEG)
        l_ref[...] = jnp.zeros_like(l_ref)
        acc_ref[...] = jnp.zeros_like(acc_ref)

    def step(diagonal):
        for hp in range(pairs):
            k = k_ref[:, hp * LANES:(hp + 1) * LANES]
            vt = vt_ref[hp * LANES:(hp + 1) * LANES, :]
            for h2 in range(2):
                head = 2 * hp + h2
                fb = jnp.broadcast_to(fc_ref[:, head:head + 1], (t, sub))
                for sb in range(t // sub):
                    cols = slice(h2 * t + sb * sub, h2 * t + (sb + 1) * sub)
                    s = _dot_nt(k, qs_ref[hp, cols, :]) - fb
                    if diagonal:
                        key = lax.broadcasted_iota(I32, (t, sub), 0)
                        qry = lax.broadcasted_iota(I32, (t, sub), 1) + sb * sub
                        s = jnp.where(key <= qry, s, NEG)
                    m_prev = m_ref[hp, :, cols]
                    m_new = jnp.maximum(m_prev, jnp.max(s, axis=0, keepdims=True))
                    alpha = jnp.exp2(m_prev - m_new)
                    p = jnp.exp2(s - m_new)
                    l_ref[hp, :, cols] = alpha * l_ref[hp, :, cols] + jnp.sum(p, axis=0, keepdims=True)
                    acc_ref[hp, :, cols] = alpha * acc_ref[hp, :, cols] + _dot(vt, p.astype(BF16))
                    m_ref[hp, :, cols] = m_new

    @pl.when(ki < qi)
    def _():
        step(False)

    @pl.when(ki == qi)
    def _():
        step(True)
        for hp in range(pairs):
            o = acc_ref[hp] / l_ref[hp]
            o = jnp.concatenate([o[:DH_A, :t], o[DH_A:, t:]], axis=0)
            o_ref[:, hp * LANES:(hp + 1) * LANES] = o.T.astype(o_ref.dtype)


def _fox(q, k, vt, fc, batch, seq):
    t = FOX_TILE
    nq = seq // t
    kv_map = lambda b, qi, ki: (b * nq + jnp.minimum(ki, qi), 0)
    return pl.pallas_call(
        _fox_kernel,
        grid=(batch, nq, nq),
        in_specs=[pl.BlockSpec((t, W_A), lambda b, qi, ki: (b * nq + qi, 0)),
                  pl.BlockSpec((t, W_A), kv_map),
                  pl.BlockSpec((W_A, t), lambda b, qi, ki: (0, b * nq + jnp.minimum(ki, qi))),
                  pl.BlockSpec((t, H_A), kv_map)],
        out_specs=pl.BlockSpec((t, W_A), lambda b, qi, ki: (b * nq + qi, 0)),
        out_shape=jax.ShapeDtypeStruct((batch * seq, W_A), BF16),
        scratch_shapes=[pltpu.VMEM((H_A // 2, 2 * t, LANES), BF16),
                        pltpu.VMEM((H_A // 2, 1, 2 * t), F32),
                        pltpu.VMEM((H_A // 2, 1, 2 * t), F32),
                        pltpu.VMEM((H_A // 2, LANES, 2 * t), F32)],
        compiler_params=_cparams(("arbitrary", "arbitrary", "arbitrary")),
        name="fox",
    )(q, k, vt, fc)


def _lfsuf_kernel(pt_ref, lf_hbm, o_ref, buf_ref, sem):
    b = pl.program_id(0)
    n_pages, _, w = buf_ref.shape

    def page_copy(j):
        return pltpu.make_async_copy(lf_hbm.at[0, pt_ref[b, j]], buf_ref.at[j], sem)

    def issue(j, carry):
        page_copy(j).start()
        return carry

    def drain(j, carry):
        page_copy(j).wait()
        return carry

    lax.fori_loop(0, n_pages, issue, 0)
    lax.fori_loop(0, n_pages, drain, 0)

    x = buf_ref[...]
    lane = lax.broadcasted_iota(I32, x.shape, 2)
    s = x
    shift = 1
    while shift < w:
        s = s + jnp.where(lane + shift < w, pltpu.roll(s, w - shift, axis=2), 0.0)
        shift *= 2
    within = s - x
    after = jnp.zeros((H_A, w), F32)
    for j in range(n_pages - 1, -1, -1):
        o_ref[j] = within[j] + after
        after = after + jnp.broadcast_to(s[j][:, 0:1], (H_A, w))


def _lfsuf(page_table, lf_pages):
    nb, n_pages = page_table.shape
    w = lf_pages.shape[3]
    return pl.pallas_call(
        _lfsuf_kernel,
        grid_spec=pltpu.PrefetchScalarGridSpec(
            num_scalar_prefetch=1,
            grid=(nb,),
            in_specs=[pl.BlockSpec(memory_space=pl.ANY)],
            out_specs=pl.BlockSpec((None, n_pages, H_A, w), lambda b, pt: (b, 0, 0, 0)),
            scratch_shapes=[pltpu.VMEM((n_pages, H_A, w), F32), pltpu.SemaphoreType.DMA(())]),
        out_shape=jax.ShapeDtypeStruct((nb, n_pages, H_A, w), F32),
        compiler_params=_cparams(("arbitrary",)),
        name="lfsuf",
    )(page_table, lf_pages)


def _foxdec_kernel(pt_ref, q_ref, kn_ref, vn_ref, lfn_ref, suf_ref, ck_hbm, cv_hbm, o_ref,
                   kbuf, vbuf, sem, m_ref, l_ref, acc_ref):
    b = pl.program_id(0)
    n_pages = suf_ref.shape[0]
    page_rows = kbuf.shape[-1]
    n_chunks = n_pages // DEC_PAGES
    tq = q_ref.shape[1]

    total = pl.num_programs(0) * n_chunks

    def page_copies(g, j):
        slot = g % DEC_RING
        pg = pt_ref[g // n_chunks, (g % n_chunks) * DEC_PAGES + j]
        return (pltpu.make_async_copy(ck_hbm.at[0, pg], kbuf.at[slot, j], sem.at[0, slot]),
                pltpu.make_async_copy(cv_hbm.at[0, pg], vbuf.at[slot, j], sem.at[1, slot]))

    def fetch(g):
        for j in range(DEC_PAGES):
            for cp in page_copies(g, j):
                cp.start()

    def wait(g):
        for j in range(DEC_PAGES):
            for cp in page_copies(g, j):
                cp.wait()

    q = q_ref[...].astype(BF16)
    m_ref[...] = jnp.full_like(m_ref, NEG)
    l_ref[...] = jnp.zeros_like(l_ref)
    acc_ref[...] = jnp.zeros_like(acc_ref)

    def attend(kt, vt, bias, ok):
        s = jnp.einsum('htd,hds->hts', q, kt, preferred_element_type=F32) + bias
        if ok is not None:
            s = jnp.where(ok, s, NEG)
        m_prev = m_ref[...]
        m_new = jnp.maximum(m_prev, jnp.max(s, axis=2, keepdims=True))
        alpha = jnp.exp(m_prev - m_new)
        p = jnp.exp(s - m_new)
        l_ref[...] = alpha * l_ref[...] + jnp.sum(p, axis=2, keepdims=True)
        acc_ref[...] = alpha * acc_ref[...] + jnp.einsum('hts,hds->htd', p.astype(BF16), vt,
                                                         preferred_element_type=F32)
        m_ref[...] = m_new

    @pl.when(b == 0)
    def _():
        for g in range(DEC_RING - 1):
            fetch(g)

    def chunk(c, carry):
        g = b * n_chunks + c
        slot = g % DEC_RING

        @pl.when(g + DEC_RING - 1 < total)
        def _():
            fetch(g + DEC_RING - 1)

        wait(g)
        kt = jnp.concatenate([kbuf[slot, j] for j in range(DEC_PAGES)], axis=2).astype(BF16)
        vt = jnp.concatenate([vbuf[slot, j] for j in range(DEC_PAGES)], axis=2).astype(BF16)
        suf = suf_ref[pl.ds(c * DEC_PAGES, DEC_PAGES)]
        bias = jnp.concatenate([suf[j] for j in range(DEC_PAGES)], axis=1)
        attend(kt, vt, bias[:, None, :], None)
        return carry

    lax.fori_loop(0, n_chunks, chunk, 0)

    r = lax.broadcasted_iota(I32, (page_rows, page_rows), 0)
    c = lax.broadcasted_iota(I32, (page_rows, page_rows), 1)
    fn = _dot(lfn_ref[...], jnp.where(r <= c, 1.0, 0.0).astype(F32), precision=HIGHEST)
    qi = lax.broadcasted_iota(I32, (H_A, tq, page_rows), 1)
    ki = lax.broadcasted_iota(I32, (H_A, tq, page_rows), 2)
    attend(kn_ref[...].astype(BF16), vn_ref[...].astype(BF16), -fn[:, None, :], ki <= qi)
    o_ref[...] = acc_ref[...] / l_ref[...]


def _foxdec(page_table, q, kn, vn, lfn, suf, cache_kt, cache_vt):
    nb, _, tq, _ = q.shape
    n_pages = page_table.shape[1]
    page_rows = cache_kt.shape[4]
    per_seq = lambda shape: pl.BlockSpec((None,) + shape, lambda b, pt: (b,) + (0,) * len(shape))
    page = (H_A, DH_A, page_rows)
    return pl.pallas_call(
        _foxdec_kernel,
        grid_spec=pltpu.PrefetchScalarGridSpec(
            num_scalar_prefetch=1,
            grid=(nb,),
            in_specs=[per_seq((H_A, tq, DH_A)), per_seq(page), per_seq(page), per_seq((H_A, page_rows)),
                      per_seq((n_pages, H_A, page_rows)),
                      pl.BlockSpec(memory_space=pl.ANY), pl.BlockSpec(memory_space=pl.ANY)],
            out_specs=per_seq((H_A, tq, DH_A)),
            scratch_shapes=[pltpu.VMEM((DEC_RING, DEC_PAGES) + page, F32),
                            pltpu.VMEM((DEC_RING, DEC_PAGES) + page, F32),
                            pltpu.SemaphoreType.DMA((2, DEC_RING)),
                            pltpu.VMEM((H_A, tq, 1), F32), pltpu.VMEM((H_A, tq, 1), F32),
                            pltpu.VMEM((H_A, tq, DH_A), F32)]),
        out_shape=jax.ShapeDtypeStruct((nb, H_A, tq, DH_A), F32),
        compiler_params=_cparams(("arbitrary",)),
        name="foxdec",
    )(page_table, q, kn, vn, lfn, suf, cache_kt, cache_vt)


def _gla_kernel(q_ref, k_ref, g_ref, v_ref, r_ref, s0_ref, gn_ref, y_ref, sfin_ref, state_ref):
    tb = pl.program_id(1)
    n_tb = pl.num_programs(1)
    c = GLA_CHUNK
    n_chunks = q_ref.shape[0] // c
    n_sub = c // GLA_SUB

    row_k = lax.broadcasted_iota(I32, (WK_B, WV_B), 0) // DK_B
    col_v = lax.broadcasted_iota(I32, (WK_B, WV_B), 1) // DV_B
    state_diag = row_k == col_v

    @pl.when(tb == 0)
    def _():
        state_ref[...] = jnp.zeros_like(state_ref)
        for h in range(H_B):
            state_ref[h * DK_B:(h + 1) * DK_B, h * DV_B:(h + 1) * DV_B] = s0_ref[h]

    r = lax.broadcasted_iota(I32, (c, c), 0)
    s = lax.broadcasted_iota(I32, (c, c), 1)
    same_sub = (r // GLA_SUB) == (s // GLA_SUB)
    cum_ops = jnp.concatenate([
        jnp.where(same_sub & (s <= r), 1.0, 0.0),
        jnp.where(s <= r, 1.0, 0.0),
        jnp.where(same_sub, 1.0, 0.0),
    ], axis=0).astype(BF16)

    t_i = lax.broadcasted_iota(I32, (c, H_B * c), 0)
    s_i = lax.broadcasted_iota(I32, (c, H_B * c), 1) % c
    dsub = t_i // GLA_SUB - s_i // GLA_SUB
    intra = (dsub == 0) & (s_i <= t_i)
    head_k = lax.broadcasted_iota(I32, (c, WK_B), 1) // DK_B
    head_v = lax.broadcasted_iota(I32, (c, WV_B), 1) // DV_B

    def stack_heads(x, head_of_lane):
        zero = jnp.zeros_like(x)
        return jnp.concatenate([jnp.where(head_of_lane == h, x, zero) for h in range(H_B)], axis=0)

    def shift_rows(x, n):
        return jnp.concatenate([x[:n], x[:c - n]], axis=0)

    gn = gn_ref[...]
    for ci in range(n_chunks):
        rows = slice(ci * c, (ci + 1) * c)
        q = q_ref[rows, :]
        k = k_ref[rows, :]
        g = g_ref[rows, :]
        v = v_ref[rows, :]

        g_hi = g.astype(BF16)
        g_r = g - g_hi.astype(F32)
        g_mid = g_r.astype(BF16)
        g_parts = jnp.concatenate([g_hi, g_mid, (g_r - g_mid.astype(F32)).astype(BF16)], axis=1)
        cums = _dot(cum_ops, g_parts)
        cums = cums[:, :WK_B] + (cums[:, WK_B:2 * WK_B] + cums[:, 2 * WK_B:])
        bl, bc, tot = cums[:c], cums[c:2 * c], cums[2 * c:]
        pre = bc - bl
        q_loc = q * jnp.exp(bl)
        k_loc = k * jnp.exp(-bl)
        k_end = k * jnp.exp(tot - bl)
        q_far = [q_loc] + [q_loc * jnp.exp(pre - shift_rows(pre, GLA_SUB * d)) for d in range(1, n_sub - 1)]

        a0 = _dot_nt(q_loc.astype(BF16), stack_heads(k_loc, head_k).astype(BF16))
        af = _dot_nt(jnp.concatenate(q_far, axis=0).astype(BF16), stack_heads(k_end, head_k).astype(BF16))
        att = jnp.where(intra, a0, 0.0)
        for d in range(1, n_sub):
            att = jnp.where(dsub == d, af[(d - 1) * c:d * c], att)

        q_in = q * jnp.exp(bc)
        o = _dot(att.astype(BF16), stack_heads(v, head_v)) + _dot(q_in.astype(BF16), state_ref[...].astype(BF16))

        k_out = k * jnp.exp(bc[c - 1:c, :] - bc)
        kv = _dot_tn(k_out.astype(BF16), v)
        gsum = _dot_tn(g_parts, jnp.ones((c, LANES), BF16))
        decay = jnp.exp(gsum[:WK_B] + (gsum[WK_B:2 * WK_B] + gsum[2 * WK_B:]))
        state_ref[...] = (state_ref[...] * jnp.concatenate([decay] * (WV_B // LANES), axis=1)
                          + jnp.where(state_diag, kv, 0.0))

        for h in range(H_B):
            lanes = slice(h * DV_B, (h + 1) * DV_B)
            oh = o[:, lanes]
            yh = oh * lax.rsqrt(jnp.mean(oh * oh, axis=-1, keepdims=True) + EPS) * gn
            y_ref[rows, lanes] = (yh * r_ref[rows, lanes].astype(F32)).astype(y_ref.dtype)

    @pl.when(tb == n_tb - 1)
    def _():
        for h in range(H_B):
            sfin_ref[h] = state_ref[h * DK_B:(h + 1) * DK_B, h * DV_B:(h + 1) * DV_B]


def _gla(q, k, g, v, r, s0, gn, batch, seq, tb_rows):
    n_tb = seq // tb_rows
    row = lambda w: pl.BlockSpec((tb_rows, w), lambda b, t: (b * n_tb + t, 0))
    state = pl.BlockSpec((None, H_B, DK_B, DV_B), lambda b, t: (b, 0, 0, 0))
    return pl.pallas_call(
        _gla_kernel,
        grid=(batch, n_tb),
        in_specs=[row(WK_B), row(WK_B), row(WK_B), row(WV_B), row(WV_B), state,
                  pl.BlockSpec((1, DV_B), lambda b, t: (0, 0))],
        out_specs=[row(WV_B), state],
        out_shape=[jax.ShapeDtypeStruct((batch * seq, WV_B), BF16),
                   jax.ShapeDtypeStruct((batch, H_B, DK_B, DV_B), F32)],
        scratch_shapes=[pltpu.VMEM((WK_B, WV_B), F32)],
        compiler_params=_cparams(("arbitrary", "arbitrary")),
        name="gla",
    )(q, k, g, v, r, s0, gn)


def _mixout_kernel(x_ref, ya_ref, yb_ref, ga_ref, gb_ref, mod_ref, g2_ref, wpa_ref, wpb_ref, wo_ref, rw_ref,
                   x1_ref, hp_ref, lg_ref):
    br_a = _dot(ya_ref[...], wpa_ref[...])
    br_b = _dot(yb_ref[...], wpb_ref[...])
    merged = ga_ref[...].astype(F32) * br_a + gb_ref[...].astype(F32) * br_b
    out = _dot(merged.astype(BF16), wo_ref[...])
    x1 = x_ref[...] + mod_ref[2] * out
    x1_ref[...] = x1
    y = x1 * lax.rsqrt(jnp.mean(x1 * x1, axis=-1, keepdims=True) + EPS) * g2_ref[...]
    h = y * (1.0 + mod_ref[4]) + mod_ref[3]
    hp_ref[...] = _pack_bf16_pair(h)
    h_hi = h.astype(BF16)
    h_lo = (h - h_hi.astype(F32)).astype(BF16)
    lg_ref[...] = _dot_nt(rw_ref[0], h_hi) + (_dot_nt(rw_ref[0], h_lo) + _dot_nt(rw_ref[1], h_hi))


def _mixout(x, ya, yb, ga, gb, mod, rows_per_mod, tm, g2, wpa, wpb, wo, rw):
    n = x.shape[0]
    mod_rows = mod.shape[2]
    tiles_per_mod = rows_per_mod // tm
    row = lambda w: pl.BlockSpec((tm, w), lambda i: (i, 0))
    return pl.pallas_call(
        _mixout_kernel,
        grid=(n // tm,),
        in_specs=[row(D_MODEL), row(W_A), row(WV_B), row(D_MODEL), row(D_MODEL),
                  pl.BlockSpec((6, None, mod_rows, D_MODEL), lambda i: (0, i // tiles_per_mod, 0, 0)),
                  _const_spec(g2.shape), _const_spec(wpa.shape), _const_spec(wpb.shape), _const_spec(wo.shape),
                  _const_spec(rw.shape)],
        out_specs=[row(D_MODEL), row(D_MODEL // 2), pl.BlockSpec((N_EXPERTS, tm), lambda i: (0, i))],
        out_shape=[jax.ShapeDtypeStruct((n, D_MODEL), F32),
                   jax.ShapeDtypeStruct((n, D_MODEL // 2), U32),
                   jax.ShapeDtypeStruct((N_EXPERTS, n), F32)],
        compiler_params=_cparams(("arbitrary",)),
        name="mixout",
    )(x, ya, yb, ga, gb, mod, g2, wpa, wpb, wo, rw)


def _route_kernel(lg_ref, lg_last_ref, bias_ref, idx_ref, w_ref, rank_ref, cnt_ref, carry_ref):
    i = pl.program_id(0)
    tm = lg_ref.shape[1]

    @pl.when(i == 0)
    def _():
        carry_ref[...] = jnp.zeros_like(carry_ref)

    lg = jnp.where(i == pl.num_programs(0) - 1, lg_last_ref[...], lg_ref[...])
    s = _sigmoid(lg)
    sb = s + bias_ref[...][:, 0:1]
    ninf = -jnp.inf

    e_in_group = lax.broadcasted_iota(I32, (GROUP_SIZE, tm), 0)
    gscores = []
    for g in range(N_GROUPS):
        blk = sb[g * GROUP_SIZE:(g + 1) * GROUP_SIZE, :]
        m1 = jnp.max(blk, axis=0, keepdims=True)
        first = jnp.min(jnp.where(blk == m1, e_in_group, GROUP_SIZE), axis=0, keepdims=True)
        m2 = jnp.max(jnp.where(e_in_group == first, ninf, blk), axis=0, keepdims=True)
        gscores.append(m1 + m2)
    gs = jnp.concatenate(gscores, axis=0)

    g_iota = lax.broadcasted_iota(I32, (N_GROUPS, tm), 0)
    chosen = jnp.zeros((N_GROUPS, tm), F32)
    for _ in range(TOPK_GROUPS):
        m = jnp.max(gs, axis=0, keepdims=True)
        first = jnp.min(jnp.where(gs == m, g_iota, N_GROUPS), axis=0, keepdims=True)
        pick = g_iota == first
        chosen = jnp.where(pick, 1.0, chosen)
        gs = jnp.where(pick, ninf, gs)
    allowed = jnp.concatenate(
        [jnp.broadcast_to(chosen[g:g + 1, :], (GROUP_SIZE, tm)) for g in range(N_GROUPS)], axis=0) > 0.5
    cand = jnp.where(allowed, sb, ninf)

    e_iota = lax.broadcasted_iota(I32, (N_EXPERTS, tm), 0)
    onehot = jnp.zeros((N_EXPERTS, tm), F32)
    idxs, ws = [], []
    for _ in range(TOP_K):
        m = jnp.max(cand, axis=0, keepdims=True)
        first = jnp.min(jnp.where(cand == m, e_iota, N_EXPERTS), axis=0, keepdims=True)
        pick = e_iota == first
        idxs.append(first)
        ws.append(jnp.sum(jnp.where(pick, s, 0.0), axis=0, keepdims=True))
        onehot = jnp.where(pick, 1.0, onehot)
        cand = jnp.where(pick, ninf, cand)
    w = jnp.concatenate(ws, axis=0)
    w_ref[...] = w / jnp.sum(w, axis=0, keepdims=True) * ROUTE_SCALE
    idx_ref[...] = jnp.concatenate(idxs, axis=0)

    r = lax.broadcasted_iota(I32, (tm, tm), 0)
    c = lax.broadcasted_iota(I32, (tm, tm), 1)
    earlier = jnp.where(r < c, 1.0, 0.0).astype(BF16)
    carry = carry_ref[...]
    before = _dot(onehot.astype(BF16), earlier) + jnp.concatenate([carry] * (tm // LANES), axis=1)
    rank_ref[...] = jnp.concatenate(
        [jnp.sum(jnp.where(e_iota == ix, before, 0.0), axis=0, keepdims=True) for ix in idxs], axis=0).astype(I32)
    carry = carry + _dot(onehot.astype(BF16), jnp.ones((tm, LANES), BF16))
    carry_ref[...] = carry
    cnt_ref[...] = carry


def _route(logits_t, logits_last, bias):
    tm = MOE_TILE
    n_first = logits_t.shape[1] // tm
    n = logits_t.shape[1] + logits_last.shape[1]
    tok = lambda: pl.BlockSpec((TOP_K, tm), lambda i: (0, i))
    return pl.pallas_call(
        _route_kernel,
        grid=(n // tm,),
        in_specs=[pl.BlockSpec((N_EXPERTS, tm), lambda i: (0, jnp.minimum(i, n_first - 1))),
                  _const_spec(logits_last.shape), _const_spec(bias.shape)],
        out_specs=[tok(), tok(), tok(), _const_spec((N_EXPERTS, LANES))],
        out_shape=[jax.ShapeDtypeStruct((TOP_K, n), I32), jax.ShapeDtypeStruct((TOP_K, n), F32),
                   jax.ShapeDtypeStruct((TOP_K, n), I32), jax.ShapeDtypeStruct((N_EXPERTS, LANES), F32)],
        scratch_shapes=[pltpu.VMEM((N_EXPERTS, LANES), F32)],
        compiler_params=_cparams(("arbitrary",)),
        name="route",
    )(logits_t, logits_last, bias)


def _dest_kernel(idx_ref, rank_ref, cnt_ref, dest_ref, start_ref):
    i = pl.program_id(0)
    tm = idx_ref.shape[1]

    @pl.when(i == 0)
    def _():
        cnt = cnt_ref[...]
        padded = jnp.floor((cnt + (EXPERT_TILE - 1)) * (1.0 / EXPERT_TILE)) * EXPERT_TILE
        r = lax.broadcasted_iota(I32, (N_EXPERTS, N_EXPERTS), 0)
        c = lax.broadcasted_iota(I32, (N_EXPERTS, N_EXPERTS), 1)
        start_ref[...] = _dot(jnp.where(c < r, 1.0, 0.0).astype(F32), padded, precision=HIGHEST)

    start = jnp.concatenate([start_ref[...]] * (tm // LANES), axis=1)
    e_iota = lax.broadcasted_iota(I32, (N_EXPERTS, tm), 0)
    idx = idx_ref[...]
    first = jnp.concatenate(
        [jnp.sum(jnp.where(e_iota == idx[kk:kk + 1, :], start, 0.0), axis=0, keepdims=True) for kk in range(TOP_K)],
        axis=0)
    dest_ref[...] = first.astype(I32) + rank_ref[...]


def _dest(idx_t, rank_t, counts):
    n = idx_t.shape[1]
    tm = MOE_TILE
    tok = lambda: pl.BlockSpec((TOP_K, tm), lambda i: (0, i))
    return pl.pallas_call(
        _dest_kernel,
        grid=(n // tm,),
        in_specs=[tok(), tok(), _const_spec(counts.shape)],
        out_specs=pl.BlockSpec((None, TOP_K, tm), lambda i: (i, 0, 0)),
        out_shape=jax.ShapeDtypeStruct((n // tm, TOP_K, tm), I32),
        scratch_shapes=[pltpu.VMEM((N_EXPERTS, LANES), F32)],
        compiler_params=_cparams(("arbitrary",)),
        name="dest",
    )(idx_t, rank_t, counts)


def _sc_worker():
    return lax.axis_index("s") * SC_CORES + lax.axis_index("c")


def _sc_mesh():
    return plsc.VectorSubcoreMesh(core_axis_name="c", subcore_axis_name="s")


def _dispatch(dest_tiles, hp, hp_last, n_rows_out):
    n_tok_tiles = dest_tiles.shape[0]
    n_first = hp.shape[0] // MOE_TILE
    assert n_first + hp_last.shape[0] // MOE_TILE == n_tok_tiles
    w = hp.shape[1]
    iters = -(-n_tok_tiles // SC_WORKERS)

    def body(hp_hbm, hp_last_hbm, dest_hbm, xs_hbm, idx_v, rows_v, sem):
        wid = _sc_worker()

        def scatter():
            copies = [pltpu.async_copy(rows_v, xs_hbm.at[idx_v.at[kk]], sem) for kk in range(TOP_K)]
            for cp in copies:
                cp.wait()

        @pl.loop(0, iters)
        def _(j):
            tile = wid + SC_WORKERS * j

            @pl.when(tile < n_first)
            def _():
                pltpu.sync_copy(dest_hbm.at[tile], idx_v)
                pltpu.sync_copy(hp_hbm.at[pl.ds(tile * MOE_TILE, MOE_TILE)], rows_v)
                scatter()

            @pl.when((tile >= n_first) & (tile < n_tok_tiles))
            def _():
                pltpu.sync_copy(dest_hbm.at[tile], idx_v)
                pltpu.sync_copy(hp_last_hbm.at[pl.ds((tile - n_first) * MOE_TILE, MOE_TILE)], rows_v)
                scatter()

    return pl.kernel(
        body,
        out_type=jax.ShapeDtypeStruct((n_rows_out, w), U32),
        mesh=_sc_mesh(),
        scratch_types=[pltpu.VMEM((TOP_K, MOE_TILE), I32), pltpu.VMEM((MOE_TILE, w), U32), pltpu.SemaphoreType.DMA],
        name="dispatch",
    )(hp, hp_last, dest_tiles)


def _gather_back(ys, dest_tiles):
    n_tok_tiles = dest_tiles.shape[0]
    w = ys.shape[1]
    n_lists = n_tok_tiles * TOP_K
    iters = -(-n_lists // SC_WORKERS)

    def body(ys_hbm, dest_hbm, out_hbm, idx_v, rows_v, sem):
        wid = _sc_worker()

        @pl.loop(0, iters)
        def _(j):
            lst = wid + SC_WORKERS * j

            @pl.when(lst < n_lists)
            def _():
                pltpu.sync_copy(dest_hbm.at[pl.ds(lst * MOE_TILE, MOE_TILE)], idx_v)
                pltpu.async_copy(ys_hbm.at[idx_v], rows_v, sem).wait()
                pltpu.sync_copy(rows_v, out_hbm.at[pl.ds(lst * MOE_TILE, MOE_TILE)])

    out = pl.kernel(
        body,
        out_type=jax.ShapeDtypeStruct((n_lists * MOE_TILE, w), U32),
        mesh=_sc_mesh(),
        scratch_types=[pltpu.VMEM((MOE_TILE,), I32), pltpu.VMEM((MOE_TILE, w), U32), pltpu.SemaphoreType.DMA],
        name="gather_back",
    )(ys, dest_tiles.reshape(-1))
    return out.reshape(n_tok_tiles, TOP_K, MOE_TILE, w)


def _experts_kernel(first_ref, count_ref, rows_ref, xs_hbm, wg_ref, wu_ref, wd_ref, after_hbm, ys_hbm,
                    xbuf, ybuf, wgb_ref, wub_ref, wdb_ref, sem):
    del after_hbm
    e = pl.program_id(0)
    tm = EXPERT_TILE
    ring = EXPERT_RING
    n_e = count_ref[e]
    total = first_ref[N_EXPERTS - 1] + count_ref[N_EXPERTS - 1]

    def x_copy(t, slot):
        return pltpu.make_async_copy(xs_hbm.at[pl.ds(t * tm, tm), :], xbuf.at[slot], sem.at[0, slot])

    def y_copy(t, slot):
        return pltpu.make_async_copy(ybuf.at[slot], ys_hbm.at[pl.ds(t * tm, tm), :], sem.at[1, slot])

    @pl.when(e == 0)
    def _():
        for t in range(ring - 1):
            @pl.when(t < total)
            def _():
                x_copy(t, t).start()

    @pl.when(n_e > 0)
    def _():
        wgb_ref[...] = wg_ref[...].astype(BF16)
        wub_ref[...] = wu_ref[...].astype(BF16)
        wdb_ref[...] = wd_ref[...].astype(BF16)
        half = D_MODEL // 2

        def body(j, carry):
            t = first_ref[e] + j
            slot = t % ring
            x_copy(t, slot).wait()

            @pl.when(t + ring - 1 < total)
            def _():
                x_copy(t + ring - 1, (t + ring - 1) % ring).start()

            @pl.when(t >= ring)
            def _():
                y_copy(t - ring, slot).wait()

            xs = xbuf[slot]
            filled = lax.broadcasted_iota(I32, xs.shape, 0) < rows_ref[t]
            lo, hi = _unpack_bf16_pair(jnp.where(filled, xs, jnp.zeros_like(xs)))
            gate = _dot(lo, wgb_ref[:half, :]) + _dot(hi, wgb_ref[half:, :])
            up = _dot(lo, wub_ref[:half, :]) + _dot(hi, wub_ref[half:, :])
            act = (_silu(gate) * up).astype(BF16)
            ybuf[slot] = _pack_bf16_pair(_dot(act, wdb_ref[...]))
            y_copy(t, slot).start()
            return carry

        lax.fori_loop(0, n_e, body, 0)

    @pl.when(e == N_EXPERTS - 1)
    def _():
        for back in range(ring, 0, -1):
            @pl.when(total >= back)
            def _():
                y_copy(total - back, (total - back) % ring).wait()


def _experts(tile_first, tile_count, tile_rows, xs, wg, wu, wd, after):
    n_rows, w = xs.shape
    tm = EXPERT_TILE
    expert = lambda e, tf, tc, tr: (e, 0, 0)
    return pl.pallas_call(
        _experts_kernel,
        grid_spec=pltpu.PrefetchScalarGridSpec(
            num_scalar_prefetch=3,
            grid=(N_EXPERTS,),
            in_specs=[pl.BlockSpec(memory_space=pl.ANY),
                      pl.BlockSpec((None, D_MODEL, D_EXPERT), expert),
                      pl.BlockSpec((None, D_MODEL, D_EXPERT), expert),
                      pl.BlockSpec((None, D_EXPERT, D_MODEL), expert),
                      pl.BlockSpec(memory_space=pl.ANY)],
            out_specs=pl.BlockSpec(memory_space=pl.ANY),
            scratch_shapes=[pltpu.VMEM((EXPERT_RING, tm, w), U32), pltpu.VMEM((EXPERT_RING, tm, w), U32),
                            pltpu.VMEM((D_MODEL, D_EXPERT), BF16), pltpu.VMEM((D_MODEL, D_EXPERT), BF16),
                            pltpu.VMEM((D_EXPERT, D_MODEL), BF16),
                            pltpu.SemaphoreType.DMA((2, EXPERT_RING))]),
        out_shape=jax.ShapeDtypeStruct((n_rows, w), U32),
        compiler_params=_cparams(("arbitrary",)),
        name="experts",
    )(tile_first, tile_count, tile_rows, xs, wg, wu, wd, after)


def _shared_kernel(hp_ref, sg_ref, su_ref, sd_ref, o_ref):
    half = D_MODEL // 2
    lo, hi = _unpack_bf16_pair(hp_ref[...])
    sg = sg_ref[...]
    su = su_ref[...]
    gate = _dot(lo, sg[:half]) + _dot(hi, sg[half:])
    up = _dot(lo, su[:half]) + _dot(hi, su[half:])
    o_ref[...] = _dot((_silu(gate) * up).astype(BF16), sd_ref[...]).astype(o_ref.dtype)


def _shared(hp, tm, sg, su, sd):
    n = hp.shape[0]
    return pl.pallas_call(
        _shared_kernel,
        grid=(n // tm,),
        in_specs=[pl.BlockSpec((tm, D_MODEL // 2), lambda i: (i, 0)),
                  _const_spec(sg.shape), _const_spec(su.shape), _const_spec(sd.shape)],
        out_specs=pl.BlockSpec((tm, D_MODEL), lambda i: (i, 0)),
        out_shape=jax.ShapeDtypeStruct((n, D_MODEL), BF16),
        compiler_params=_cparams(("arbitrary",)),
        name="shared",
    )(hp, sg, su, sd)


def _combine_kernel(yt_ref, x1_ref, ysh_ref, w_ref, mod_ref, fg_ref, *rest):
    o_ref = rest[-1]
    y = ysh_ref[...].astype(F32)
    w = w_ref[...]
    for kk in range(TOP_K):
        ylo, yhi = _unpack_bf16_pair(yt_ref[kk])
        y = y + w[:, kk:kk + 1] * jnp.concatenate([ylo.astype(F32), yhi.astype(F32)], axis=1)
    x2 = x1_ref[...] + mod_ref[5] * y
    o_ref[...] = x2 * lax.rsqrt(jnp.mean(x2 * x2, axis=-1, keepdims=True) + EPS) * fg_ref[...]


def _combine(y_tok, y_tok_tile0, x1, y_shared, w, w_tile0, mod, rows_per_mod, fg, tile0, n_tiles, prev=None):
    n = x1.shape[0]
    tm = MOE_TILE
    mod_rows = mod.shape[2]
    tiles_per_mod = rows_per_mod // tm
    own = lambda i: (i + tile0, 0)
    return pl.pallas_call(
        _combine_kernel,
        grid=(n_tiles,),
        in_specs=[pl.BlockSpec((None, TOP_K, tm, D_MODEL // 2), lambda i: (i + y_tok_tile0, 0, 0, 0)),
                  pl.BlockSpec((tm, D_MODEL), own),
                  pl.BlockSpec((tm, D_MODEL), own),
                  pl.BlockSpec((tm, TOP_K), lambda i: (i + w_tile0, 0)),
                  pl.BlockSpec((6, None, mod_rows, D_MODEL), lambda i: (0, (i + tile0) // tiles_per_mod, 0, 0)),
                  _const_spec(fg.shape)] + ([] if prev is None else [pl.BlockSpec(memory_space=pl.ANY)]),
        out_specs=pl.BlockSpec((tm, D_MODEL), own),
        out_shape=jax.ShapeDtypeStruct((n, D_MODEL), F32),
        input_output_aliases={} if prev is None else {6: 0},
        compiler_params=_cparams(("arbitrary",)),
        name="combine",
    )(y_tok, x1, y_shared, w, mod, fg, *(() if prev is None else (prev,)))


def _split_w_in(w_in):
    sizes = (W_A, W_A, W_A, H_A, WK_B, WK_B, WV_B, WV_B, GK_RANK, D_MODEL, D_MODEL)
    segs, o = [], 0
    for s in sizes:
        segs.append(w_in[:, o:o + s])
        o += s
    return segs


def kernel(x_prompt, x_sample, cache_k, cache_v, cache_logf, state_gla, page_table, c_prompt, c_sample, w_ada, b_ada,
           norm1_g, w_in, b_f, w_gk2, b_gk, gla_norm_g, w_pa, w_pb, w_o, norm2_g, router_w, router_bias, w_gate, w_up,
           w_down, ws_gate, ws_up, ws_down, final_g):
    assert w_ada.shape[0] == 1, "single layer"
    bp, tp, d = x_prompt.shape
    bs, ts, _ = x_sample.shape
    n_p, n_s = bp * tp, bs * ts
    n_tot = n_p + n_s
    assert n_s == MOE_TILE and n_p % MOE_TILE == 0 and tp % FOX_TILE == 0

    q_a, k_a, v_a, f_a, q_b, k_b, v_b, r_b, gk1, gate_a, gate_b = _split_w_in(w_in[0])
    wts = (jnp.concatenate([q_a, k_a], axis=1).astype(BF16),
           v_a.astype(BF16),
           jnp.concatenate([k_a, v_a], axis=1).T.astype(BF16),
           jnp.pad(f_a, ((0, 0), (0, LANES - H_A))).astype(BF16),
           jnp.pad(b_f[0].reshape(1, H_A), ((0, 0), (0, LANES - H_A))),
           jnp.concatenate([q_b, k_b, v_b, r_b], axis=1).astype(BF16),
           jnp.pad(gk1, ((0, 0), (0, LANES - GK_RANK))).astype(BF16),
           jnp.pad(w_gk2[0], ((0, LANES - GK_RANK), (0, 0))).astype(BF16),
           b_gk[0].reshape(1, WK_B),
           jnp.concatenate([gate_a, gate_b], axis=1).astype(BF16))
    g1 = norm1_g[0].reshape(1, d)
    g2 = norm2_g[0].reshape(1, d)
    gn = gla_norm_g[0].reshape(1, DV_B)
    fg = final_g.reshape(1, d)
    wpa, wpb, wo = w_pa[0].astype(BF16), w_pb[0].astype(BF16), w_o[0].astype(BF16)
    rw_hi = router_w[0].T.astype(BF16)
    rw_t = jnp.stack([rw_hi, (router_w[0].T - rw_hi.astype(F32)).astype(BF16)])
    rbias = jnp.broadcast_to(router_bias[0].reshape(N_EXPERTS, 1), (N_EXPERTS, LANES))
    sg, su, sd = ws_gate[0].astype(BF16), ws_up[0].astype(BF16), ws_down[0].astype(BF16)

    ada = _ada(jnp.concatenate([c_prompt, c_sample], axis=0), w_ada[0], b_ada[0].reshape(1, -1))
    ada = ada.reshape(bp + bs, 6, d)
    mod_p = ada[:bp].transpose(1, 0, 2)[:, :, None, :]
    mod_s = jnp.repeat(ada[bp:], ts, axis=0).transpose(1, 0, 2)[:, None, :, :]

    xp = x_prompt.reshape(n_p, d)
    tm_p = 512
    (q, kh, vth, kt, vt, lf, fc, qb, kb, gk, vb, rb, ga_p, gb_p) = _inproj(
        xp, mod_p, tp, tp // tm_p, tm_p, g1, wts, True)
    ya_p = _fox(q, kh, vth, fc, bp, tp)
    yb_p, s_p = _gla(qb, kb, gk, vb, rb, jnp.zeros((bp, H_B, DK_B, DV_B), F32), gn, bp, tp, 256)
    k_prompt = kt.reshape(bp, H_A, DH_A, tp).transpose(0, 3, 1, 2)[None]
    v_prompt = vt.reshape(bp, H_A, DH_A, tp).transpose(0, 3, 1, 2)[None]
    logf_prompt = lf.reshape(1, bp, tp, H_A)

    xs_in = x_sample.reshape(n_s, d)
    (q, k, v, lf, qb, kb, gk, vb, rb, ga, gb) = _inproj(xs_in, mod_s, n_s, 1, n_s, g1, wts, False)
    page_rows = cache_k.shape[2]
    tq = 8
    cache_kt = jnp.transpose(cache_k, (0, 1, 3, 4, 2))
    cache_vt = jnp.transpose(cache_v, (0, 1, 3, 4, 2))
    suf = _lfsuf(page_table, jnp.transpose(cache_logf, (0, 1, 3, 2)))

    def new_kv(a):
        a = a.reshape(bs, ts, H_A, DH_A).transpose(0, 2, 3, 1)
        return jnp.pad(a, ((0, 0), (0, 0), (0, 0), (0, page_rows - ts)))

    q_dec = jnp.pad(q.astype(F32).reshape(bs, ts, H_A, DH_A).transpose(0, 2, 1, 3),
                    ((0, 0), (0, 0), (0, tq - ts), (0, 0)))
    lfn = jnp.pad(lf.reshape(bs, ts, H_A).transpose(0, 2, 1), ((0, 0), (0, 0), (0, page_rows - ts)))
    ya = _foxdec(page_table, q_dec, new_kv(k), new_kv(v), lfn, suf, cache_kt, cache_vt)
    ya = ya[:, :, :ts].transpose(0, 2, 1, 3).reshape(n_s, W_A).astype(BF16)
    pad = lambda a: jnp.pad(a.reshape(bs, ts, -1), ((0, 0), (0, GLA_CHUNK - ts), (0, 0))).reshape(bs * GLA_CHUNK, -1)
    yb, s_s = _gla(pad(qb), pad(kb), pad(gk), pad(vb), pad(rb), state_gla[0], gn, bs, GLA_CHUNK, GLA_CHUNK)
    yb = yb.reshape(bs, GLA_CHUNK, WV_B)[:, :ts].reshape(n_s, WV_B)
    k_sample = k.reshape(1, bs, ts, H_A, DH_A)
    v_sample = v.reshape(1, bs, ts, H_A, DH_A)
    logf_sample = lf.reshape(1, bs, ts, H_A)
    x1_s, hp_s, logits_s = _mixout(xs_in, ya, yb, ga, gb, mod_s, n_s, n_s, g2, wpa, wpb, wo, rw_t)

    x1_p, hp_p, logits_p = _mixout(xp, ya_p, yb_p, ga_p, gb_p, mod_p, tp, 2 * MOE_TILE, g2, wpa, wpb, wo, rw_t)
    idx_t, w_t, rank_t, counts = _route(logits_p, logits_s, rbias)
    dest_tiles = _dest(idx_t, rank_t, counts)
    padded = (counts[:, 0].astype(I32) + EXPERT_TILE - 1) // EXPERT_TILE * EXPERT_TILE
    pad_end = jnp.cumsum(padded)
    n_tiles = -(-n_tot * TOP_K // EXPERT_TILE) + N_EXPERTS
    tile_first_row = jnp.arange(n_tiles, dtype=I32) * EXPERT_TILE
    tile_expert = jnp.minimum(jnp.sum((pad_end[None, :] <= tile_first_row[:, None]).astype(I32), axis=1),
                              N_EXPERTS - 1)
    used_end = pad_end - padded + counts[:, 0].astype(I32)
    tile_rows = jnp.clip(used_end[tile_expert] - tile_first_row, 0, EXPERT_TILE).astype(I32)
    tile_first = ((pad_end - padded) // EXPERT_TILE).astype(I32)
    tile_count = (padded // EXPERT_TILE).astype(I32)

    xs_sorted = _dispatch(dest_tiles, hp_p, hp_s, n_tiles * EXPERT_TILE)
    ysh_p = _shared(hp_p, 2 * MOE_TILE, sg, su, sd)
    ysh_s = _shared(hp_s, MOE_TILE, sg, su, sd)
    ys_sorted = _experts(tile_first, tile_count, tile_rows, xs_sorted, w_gate[0], w_up[0], w_down[0], ysh_p)
    n_p_tiles = n_p // MOE_TILE
    part = n_p_tiles // GATHER_PARTS
    w_tok = w_t.T
    y_prompt = None
    for j in range(GATHER_PARTS):
        last = j == GATHER_PARTS - 1
        y_tok = _gather_back(ys_sorted, dest_tiles[j * part:] if last else dest_tiles[j * part:(j + 1) * part])
        y_prompt = _combine(y_tok, 0, x1_p, ysh_p, w_tok, j * part, mod_p, tp, fg, j * part, part, prev=y_prompt)
    y_sample = _combine(y_tok, part, x1_s, ysh_s, w_tok, n_p_tiles, mod_s, n_s, fg, 0, n_s // MOE_TILE)

    return (y_prompt.reshape(bp, tp, d), y_sample.reshape(bs, ts, d),
            k_prompt, v_prompt, logf_prompt, s_p[None],
            k_sample, v_sample, logf_sample, s_s[None])
```

```python
import functools

import jax
import jax.numpy as jnp
from jax import lax
from jax.experimental import pallas as pl
from jax.experimental.pallas import tpu as pltpu
from jax.experimental.pallas import tpu_sc as plsc

F32 = jnp.float32
BF16 = jnp.bfloat16
U32 = jnp.uint32
I32 = jnp.int32

EPS = 1e-6
D_MODEL = 1024
H_A, DH_A, W_A = 8, 64, 512
H_B, DK_B, DV_B, WK_B, WV_B = 4, 64, 128, 256, 512
GK_RANK = 16
GATE_NORM = 16.0
N_EXPERTS, TOP_K, N_GROUPS, TOPK_GROUPS = 256, 8, 8, 4
GROUP_SIZE = N_EXPERTS // N_GROUPS
D_EXPERT = 256
ROUTE_SCALE = 2.5
MOE_TILE = 128
EXPERT_TILE = 256
EXPERT_RING = 4
GATHER_PARTS = 4
LANES = 128
SC_CORES = 2
SC_WORKERS = 32
GLA_CHUNK = 64
GLA_SUB = 16
FOX_TILE = 512
FOX_SUB = 128
DEC_PAGES = 8
DEC_RING = 3
NEG = -1e30
LOG2E = 1.4426950408889634
VMEM_LIMIT = 56 * 1024 * 1024
HIGHEST = lax.Precision.HIGHEST

NT_DIMS = (((1,), (1,)), ((), ()))
TN_DIMS = (((0,), (0,)), ((), ()))


def _cparams(sem):
    return pltpu.CompilerParams(dimension_semantics=sem, vmem_limit_bytes=VMEM_LIMIT)


def _dot(a, b, **kw):
    return jnp.dot(a, b, preferred_element_type=F32, **kw)


def _dot_nt(a, b, **kw):
    return lax.dot_general(a, b, NT_DIMS, preferred_element_type=F32, **kw)


def _dot_tn(a, b, **kw):
    return lax.dot_general(a, b, TN_DIMS, preferred_element_type=F32, **kw)


def _sigmoid(x):
    return 0.5 * jnp.tanh(0.5 * x) + 0.5


def _silu(x):
    return x * _sigmoid(x)


def _log_sigmoid(x):
    return jnp.minimum(x, 0.0) - jnp.log(1.0 + jnp.exp(-jnp.abs(x)))


def _pack_bf16_pair(x):
    c = x.shape[1] // 2
    bits = pltpu.bitcast(x.astype(BF16).astype(F32), U32)
    return (bits[:, c:] & jnp.uint32(0xFFFF0000)) | (bits[:, :c] >> 16)


def _unpack_bf16_pair(p):
    lo = pltpu.bitcast(p << 16, F32).astype(BF16)
    hi = pltpu.bitcast(p & jnp.uint32(0xFFFF0000), F32).astype(BF16)
    return lo, hi


def _ada_kernel(c_ref, w_ref, b_ref, o_ref):
    c = c_ref[...]
    o_ref[...] = _dot(_silu(c).astype(BF16), w_ref[...].astype(BF16)) + b_ref[...]


def _ada(c, w, b):
    n, d = c.shape
    cols = w.shape[1]
    tn = 1536
    return pl.pallas_call(
        _ada_kernel,
        grid=(cols // tn,),
        in_specs=[pl.BlockSpec((n, d), lambda j: (0, 0)),
                  pl.BlockSpec((d, tn), lambda j: (0, j)),
                  pl.BlockSpec((1, tn), lambda j: (0, j))],
        out_specs=pl.BlockSpec((n, tn), lambda j: (0, j)),
        out_shape=jax.ShapeDtypeStruct((n, cols), F32),
        compiler_params=_cparams(("arbitrary",)),
        name="ada",
    )(c, w, b)


def _inproj_kernel(*refs, seq_tiles, prompt):
    (x_ref, mod_ref, g_ref, wqk_ref, wv_ref, wkvt_ref, wf_ref, bf_ref, wb_ref, wgk1_ref, wgk2_ref, bgk_ref,
     wgate_ref) = refs[:13]
    if prompt:
        (q_ref, kh_ref, vth_ref, kt_ref, vt_ref, lf_ref, fc_ref, qb_ref, kb_ref, gk_ref, vb_ref, rb_ref, ga_ref,
         gb_ref, carry_ref) = refs[13:]
    else:
        q_ref, k_ref, v_ref, lf_ref, qb_ref, kb_ref, gk_ref, vb_ref, rb_ref, ga_ref, gb_ref = refs[13:]
    i = pl.program_id(0)
    tm = x_ref.shape[0]
    x = x_ref[...]
    y = x * lax.rsqrt(jnp.mean(x * x, axis=-1, keepdims=True) + EPS) * g_ref[...]
    h = (y * (1.0 + mod_ref[1]) + mod_ref[0]).astype(BF16)

    z = _dot(h, wqk_ref[...])
    q_ref[...] = (z[:, :W_A] * (DH_A ** -0.5 * (LOG2E if prompt else 1.0))).astype(BF16)
    k = z[:, W_A:]

    lf = _log_sigmoid(_dot(h, wf_ref[...]) + bf_ref[...])
    lf_ref[...] = lf[:, :H_A]

    if prompt:
        kvt = _dot_nt(wkvt_ref[...], h)
        kt_ref[...] = kvt[:W_A]
        vt_ref[...] = kvt[W_A:]
        vth_ref[...] = kvt[W_A:].astype(BF16)
        kh_ref[...] = k.astype(BF16)

        @pl.when(i % seq_tiles == 0)
        def _():
            carry_ref[...] = jnp.zeros_like(carry_ref)

        r = lax.broadcasted_iota(I32, (tm, tm), 0)
        c = lax.broadcasted_iota(I32, (tm, tm), 1)
        lower = jnp.where(c <= r, 1.0, 0.0).astype(BF16)
        lf_hi = lf.astype(BF16)
        lf_r = lf - lf_hi.astype(F32)
        lf_mid = lf_r.astype(BF16)
        lf_parts = jnp.concatenate([lf_hi, lf_mid, (lf_r - lf_mid.astype(F32)).astype(BF16)], axis=1)
        sum3 = lambda a: a[:, :LANES] + (a[:, LANES:2 * LANES] + a[:, 2 * LANES:])
        carry = carry_ref[...]
        fc = sum3(_dot(lower, lf_parts)) + carry[0:1, :]
        fc_ref[...] = fc[:, :H_A] * LOG2E
        carry_ref[...] = carry + sum3(_dot(jnp.ones((8, tm), BF16), lf_parts))
    else:
        k_ref[...] = k
        v_ref[...] = _dot(h, wv_ref[...])

    z = _dot(h, wb_ref[...])
    qb_ref[...] = z[:, :WK_B] * DK_B ** -0.5
    kb_ref[...] = z[:, WK_B:2 * WK_B]
    vb_ref[...] = z[:, 2 * WK_B:2 * WK_B + WV_B].astype(BF16)
    rb_ref[...] = _silu(z[:, 2 * WK_B + WV_B:]).astype(BF16)

    lr = _dot(h, wgk1_ref[...]).astype(BF16)
    gk_ref[...] = _log_sigmoid(_dot(lr, wgk2_ref[...]) + bgk_ref[...]) * (1.0 / GATE_NORM)

    z = _dot(h, wgate_ref[...])
    ga_ref[...] = _sigmoid(z[:, :D_MODEL]).astype(BF16)
    gb_ref[...] = _sigmoid(z[:, D_MODEL:]).astype(BF16)


def _const_spec(shape):
    return pl.BlockSpec(shape, lambda i: (0,) * len(shape))


def _inproj(x, mod, rows_per_mod, seq_tiles, tm, g, wts, prompt):
    n = x.shape[0]
    mod_rows = mod.shape[2]
    tiles_per_mod = rows_per_mod // tm
    row = lambda w: pl.BlockSpec((tm, w), lambda i: (i, 0))
    if prompt:
        n_seq = n // (seq_tiles * tm)
        seq_t = pl.BlockSpec((None, W_A, tm), lambda i: (i // seq_tiles, 0, i % seq_tiles))
        kv_shape = (n_seq, W_A, seq_tiles * tm)
        outs = [
            (row(W_A), (n, W_A), BF16),
            (row(W_A), (n, W_A), BF16),
            (pl.BlockSpec((W_A, tm), lambda i: (0, i)), (W_A, n), BF16),
            (seq_t, kv_shape, F32),
            (seq_t, kv_shape, F32),
            (row(H_A), (n, H_A), F32),
            (row(H_A), (n, H_A), F32),
        ]
    else:
        outs = [
            (row(W_A), (n, W_A), BF16),
            (row(W_A), (n, W_A), F32),
            (row(W_A), (n, W_A), F32),
            (row(H_A), (n, H_A), F32),
        ]
    outs += [
        (row(WK_B), (n, WK_B), F32),
        (row(WK_B), (n, WK_B), F32),
        (row(WK_B), (n, WK_B), F32),
        (row(WV_B), (n, WV_B), BF16),
        (row(WV_B), (n, WV_B), BF16),
        (row(D_MODEL), (n, D_MODEL), BF16),
        (row(D_MODEL), (n, D_MODEL), BF16),
    ]
    return pl.pallas_call(
        functools.partial(_inproj_kernel, seq_tiles=seq_tiles, prompt=prompt),
        grid=(n // tm,),
        in_specs=[row(D_MODEL),
                  pl.BlockSpec((6, None, mod_rows, D_MODEL), lambda i: (0, i // tiles_per_mod, 0, 0)),
                  _const_spec(g.shape)] + [_const_spec(w.shape) for w in wts],
        out_specs=[o[0] for o in outs],
        out_shape=[jax.ShapeDtypeStruct(o[1], o[2]) for o in outs],
        scratch_shapes=[pltpu.VMEM((8, LANES), F32)] if prompt else [],
        compiler_params=_cparams(("arbitrary",)),
        name="inproj",
    )(x, mod, g, *wts)


def _fox_kernel(q_ref, k_ref, vt_ref, fc_ref, o_ref, qs_ref, m_ref, l_ref, acc_ref):
    qi = pl.program_id(1)
    ki = pl.program_id(2)
    t = q_ref.shape[0]
    pairs = H_A // 2
    sub = FOX_SUB

    @pl.when(ki == 0)
    def _():
        lane = lax.broadcasted_iota(I32, (t, LANES), 1)
        for hp in range(pairs):
            q = q_ref[:, hp * LANES:(hp + 1) * LANES]
            qs_ref[hp, :t, :] = jnp.where(lane < DH_A, q, jnp.zeros_like(q))
            qs_ref[hp, t:, :] = jnp.where(lane >= DH_A, q, jnp.zeros_like(q))
        m_ref[...] = jnp.full_like(m_ref, NEG)
        l_ref[...] = jnp.zeros_like(l_ref)
        acc_ref[...] = jnp.zeros_like(acc_ref)

    def step(diagonal):
        for hp in range(pairs):
            k = k_ref[:, hp * LANES:(hp + 1) * LANES]
            vt = vt_ref[hp * LANES:(hp + 1) * LANES, :]
            for h2 in range(2):
                head = 2 * hp + h2
                fb = jnp.broadcast_to(fc_ref[:, head:head + 1], (t, sub))
                for sb in range(t // sub):
                    cols = slice(h2 * t + sb * sub, h2 * t + (sb + 1) * sub)
                    s = _dot_nt(k, qs_ref[hp, cols, :]) - fb
                    if diagonal:
                        key = lax.broadcasted_iota(I32, (t, sub), 0)
                        qry = lax.broadcasted_iota(I32, (t, sub), 1) + sb * sub
                        s = jnp.where(key <= qry, s, NEG)
                    m_prev = m_ref[hp, :, cols]
                    m_new = jnp.maximum(m_prev, jnp.max(s, axis=0, keepdims=True))
                    alpha = jnp.exp2(m_prev - m_new)
                    p = jnp.exp2(s - m_new)
                    l_ref[hp, :, cols] = alpha * l_ref[hp, :, cols] + jnp.sum(p, axis=0, keepdims=True)
                    acc_ref[hp, :, cols] = alpha * acc_ref[hp, :, cols] + _dot(vt, p.astype(BF16))
                    m_ref[hp, :, cols] = m_new

    @pl.when(ki < qi)
    def _():
        step(False)

    @pl.when(ki == qi)
    def _():
        step(True)
        for hp in range(pairs):
            o = acc_ref[hp] / l_ref[hp]
            o = jnp.concatenate([o[:DH_A, :t], o[DH_A:, t:]], axis=0)
            o_ref[:, hp * LANES:(hp + 1) * LANES] = o.T.astype(o_ref.dtype)


def _fox(q, k, vt, fc, batch, seq):
    t = FOX_TILE
    nq = seq // t
    kv_map = lambda b, qi, ki: (b * nq + jnp.minimum(ki, qi), 0)
    return pl.pallas_call(
        _fox_kernel,
        grid=(batch, nq, nq),
        in_specs=[pl.BlockSpec((t, W_A), lambda b, qi, ki: (b * nq + qi, 0)),
                  pl.BlockSpec((t, W_A), kv_map),
                  pl.BlockSpec((W_A, t), lambda b, qi, ki: (0, b * nq + jnp.minimum(ki, qi))),
                  pl.BlockSpec((t, H_A), kv_map)],
        out_specs=pl.BlockSpec((t, W_A), lambda b, qi, ki: (b * nq + qi, 0)),
        out_shape=jax.ShapeDtypeStruct((batch * seq, W_A), BF16),
        scratch_shapes=[pltpu.VMEM((H_A // 2, 2 * t, LANES), BF16),
                        pltpu.VMEM((H_A // 2, 1, 2 * t), F32),
                        pltpu.VMEM((H_A // 2, 1, 2 * t), F32),
                        pltpu.VMEM((H_A // 2, LANES, 2 * t), F32)],
        compiler_params=_cparams(("arbitrary", "arbitrary", "arbitrary")),
        name="fox",
    )(q, k, vt, fc)


def _lfsuf_kernel(pt_ref, lf_hbm, o_ref, buf_ref, sem):
    b = pl.program_id(0)
    n_pages, _, w = buf_ref.shape

    def page_copy(j):
        return pltpu.make_async_copy(lf_hbm.at[0, pt_ref[b, j]], buf_ref.at[j], sem)

    def issue(j, carry):
        page_copy(j).start()
        return carry

    def drain(j, carry):
        page_copy(j).wait()
        return carry

    lax.fori_loop(0, n_pages, issue, 0)
    lax.fori_loop(0, n_pages, drain, 0)

    x = buf_ref[...]
    lane = lax.broadcasted_iota(I32, x.shape, 2)
    s = x
    shift = 1
    while shift < w:
        s = s + jnp.where(lane + shift < w, pltpu.roll(s, w - shift, axis=2), 0.0)
        shift *= 2
    within = s - x
    after = jnp.zeros((H_A, w), F32)
    for j in range(n_pages - 1, -1, -1):
        o_ref[j] = within[j] + after
        after = after + jnp.broadcast_to(s[j][:, 0:1], (H_A, w))


def _lfsuf(page_table, lf_pages):
    nb, n_pages = page_table.shape
    w = lf_pages.shape[3]
    return pl.pallas_call(
        _lfsuf_kernel,
        grid_spec=pltpu.PrefetchScalarGridSpec(
            num_scalar_prefetch=1,
            grid=(nb,),
            in_specs=[pl.BlockSpec(memory_space=pl.ANY)],
            out_specs=pl.BlockSpec((None, n_pages, H_A, w), lambda b, pt: (b, 0, 0, 0)),
            scratch_shapes=[pltpu.VMEM((n_pages, H_A, w), F32), pltpu.SemaphoreType.DMA(())]),
        out_shape=jax.ShapeDtypeStruct((nb, n_pages, H_A, w), F32),
        compiler_params=_cparams(("arbitrary",)),
        name="lfsuf",
    )(page_table, lf_pages)


def _foxdec_kernel(pt_ref, q_ref, kn_ref, vn_ref, lfn_ref, suf_ref, ck_hbm, cv_hbm, o_ref,
                   kbuf, vbuf, sem, m_ref, l_ref, acc_ref):
    b = pl.program_id(0)
    n_pages = suf_ref.shape[0]
    page_rows = kbuf.shape[-1]
    n_chunks = n_pages // DEC_PAGES
    tq = q_ref.shape[1]

    total = pl.num_programs(0) * n_chunks

    def page_copies(g, j):
        slot = g % DEC_RING
        pg = pt_ref[g // n_chunks, (g % n_chunks) * DEC_PAGES + j]
        return (pltpu.make_async_copy(ck_hbm.at[0, pg], kbuf.at[slot, j], sem.at[0, slot]),
                pltpu.make_async_copy(cv_hbm.at[0, pg], vbuf.at[slot, j], sem.at[1, slot]))

    def fetch(g):
        for j in range(DEC_PAGES):
            for cp in page_copies(g, j):
                cp.start()

    def wait(g):
        for j in range(DEC_PAGES):
            for cp in page_copies(g, j):
                cp.wait()

    q = q_ref[...].astype(BF16)
    m_ref[...] = jnp.full_like(m_ref, NEG)
    l_ref[...] = jnp.zeros_like(l_ref)
    acc_ref[...] = jnp.zeros_like(acc_ref)

    def attend(kt, vt, bias, ok):
        s = jnp.einsum('htd,hds->hts', q, kt, preferred_element_type=F32) + bias
        if ok is not None:
            s = jnp.where(ok, s, NEG)
        m_prev = m_ref[...]
        m_new = jnp.maximum(m_prev, jnp.max(s, axis=2, keepdims=True))
        alpha = jnp.exp(m_prev - m_new)
        p = jnp.exp(s - m_new)
        l_ref[...] = alpha * l_ref[...] + jnp.sum(p, axis=2, keepdims=True)
        acc_ref[...] = alpha * acc_ref[...] + jnp.einsum('hts,hds->htd', p.astype(BF16), vt,
                                                         preferred_element_type=F32)
        m_ref[...] = m_new

    @pl.when(b == 0)
    def _():
        for g in range(DEC_RING - 1):
            fetch(g)

    def chunk(c, carry):
        g = b * n_chunks + c
        slot = g % DEC_RING

        @pl.when(g + DEC_RING - 1 < total)
        def _():
            fetch(g + DEC_RING - 1)

        wait(g)
        kt = jnp.concatenate([kbuf[slot, j] for j in range(DEC_PAGES)], axis=2).astype(BF16)
        vt = jnp.concatenate([vbuf[slot, j] for j in range(DEC_PAGES)], axis=2).astype(BF16)
        suf = suf_ref[pl.ds(c * DEC_PAGES, DEC_PAGES)]
        bias = jnp.concatenate([suf[j] for j in range(DEC_PAGES)], axis=1)
        attend(kt, vt, bias[:, None, :], None)
        return carry

    lax.fori_loop(0, n_chunks, chunk, 0)

    r = lax.broadcasted_iota(I32, (page_rows, page_rows), 0)
    c = lax.broadcasted_iota(I32, (page_rows, page_rows), 1)
    fn = _dot(lfn_ref[...], jnp.where(r <= c, 1.0, 0.0).astype(F32), precision=HIGHEST)
    qi = lax.broadcasted_iota(I32, (H_A, tq, page_rows), 1)
    ki = lax.broadcasted_iota(I32, (H_A, tq, page_rows), 2)
    attend(kn_ref[...].astype(BF16), vn_ref[...].astype(BF16), -fn[:, None, :], ki <= qi)
    o_ref[...] = acc_ref[...] / l_ref[...]


def _foxdec(page_table, q, kn, vn, lfn, suf, cache_kt, cache_vt):
    nb, _, tq, _ = q.shape
    n_pages = page_table.shape[1]
    page_rows = cache_kt.shape[4]
    per_seq = lambda shape: pl.BlockSpec((None,) + shape, lambda b, pt: (b,) + (0,) * len(shape))
    page = (H_A, DH_A, page_rows)
    return pl.pallas_call(
        _foxdec_kernel,
        grid_spec=pltpu.PrefetchScalarGridSpec(
            num_scalar_prefetch=1,
            grid=(nb,),
            in_specs=[per_seq((H_A, tq, DH_A)), per_seq(page), per_seq(page), per_seq((H_A, page_rows)),
                      per_seq((n_pages, H_A, page_rows)),
                      pl.BlockSpec(memory_space=pl.ANY), pl.BlockSpec(memory_space=pl.ANY)],
            out_specs=per_seq((H_A, tq, DH_A)),
            scratch_shapes=[pltpu.VMEM((DEC_RING, DEC_PAGES) + page, F32),
                            pltpu.VMEM((DEC_RING, DEC_PAGES) + page, F32),
                            pltpu.SemaphoreType.DMA((2, DEC_RING)),
                            pltpu.VMEM((H_A, tq, 1), F32), pltpu.VMEM((H_A, tq, 1), F32),
                            pltpu.VMEM((H_A, tq, DH_A), F32)]),
        out_shape=jax.ShapeDtypeStruct((nb, H_A, tq, DH_A), F32),
        compiler_params=_cparams(("arbitrary",)),
        name="foxdec",
    )(page_table, q, kn, vn, lfn, suf, cache_kt, cache_vt)


def _gla_kernel(q_ref, k_ref, g_ref, v_ref, r_ref, s0_ref, gn_ref, y_ref, sfin_ref, state_ref):
    tb = pl.program_id(1)
    n_tb = pl.num_programs(1)
    c = GLA_CHUNK
    n_chunks = q_ref.shape[0] // c
    n_sub = c // GLA_SUB

    row_k = lax.broadcasted_iota(I32, (WK_B, WV_B), 0) // DK_B
    col_v = lax.broadcasted_iota(I32, (WK_B, WV_B), 1) // DV_B
    state_diag = row_k == col_v

    @pl.when(tb == 0)
    def _():
        state_ref[...] = jnp.zeros_like(state_ref)
        for h in range(H_B):
            state_ref[h * DK_B:(h + 1) * DK_B, h * DV_B:(h + 1) * DV_B] = s0_ref[h]

    r = lax.broadcasted_iota(I32, (c, c), 0)
    s = lax.broadcasted_iota(I32, (c, c), 1)
    same_sub = (r // GLA_SUB) == (s // GLA_SUB)
    cum_ops = jnp.concatenate([
        jnp.where(same_sub & (s <= r), 1.0, 0.0),
        jnp.where(s <= r, 1.0, 0.0),
        jnp.where(same_sub, 1.0, 0.0),
    ], axis=0).astype(BF16)

    t_i = lax.broadcasted_iota(I32, (c, H_B * c), 0)
    s_i = lax.broadcasted_iota(I32, (c, H_B * c), 1) % c
    dsub = t_i // GLA_SUB - s_i // GLA_SUB
    intra = (dsub == 0) & (s_i <= t_i)
    head_k = lax.broadcasted_iota(I32, (c, WK_B), 1) // DK_B
    head_v = lax.broadcasted_iota(I32, (c, WV_B), 1) // DV_B

    def stack_heads(x, head_of_lane):
        zero = jnp.zeros_like(x)
        return jnp.concatenate([jnp.where(head_of_lane == h, x, zero) for h in range(H_B)], axis=0)

    def shift_rows(x, n):
        return jnp.concatenate([x[:n], x[:c - n]], axis=0)

    gn = gn_ref[...]
    for ci in range(n_chunks):
        rows = slice(ci * c, (ci + 1) * c)
        q = q_ref[rows, :]
        k = k_ref[rows, :]
        g = g_ref[rows, :]
        v = v_ref[rows, :]

        g_hi = g.astype(BF16)
        g_r = g - g_hi.astype(F32)
        g_mid = g_r.astype(BF16)
        g_parts = jnp.concatenate([g_hi, g_mid, (g_r - g_mid.astype(F32)).astype(BF16)], axis=1)
        cums = _dot(cum_ops, g_parts)
        cums = cums[:, :WK_B] + (cums[:, WK_B:2 * WK_B] + cums[:, 2 * WK_B:])
        bl, bc, tot = cums[:c], cums[c:2 * c], cums[2 * c:]
        pre = bc - bl
        q_loc = q * jnp.exp(bl)
        k_loc = k * jnp.exp(-bl)
        k_end = k * jnp.exp(tot - bl)
        q_far = [q_loc] + [q_loc * jnp.exp(pre - shift_rows(pre, GLA_SUB * d)) for d in range(1, n_sub - 1)]

        a0 = _dot_nt(q_loc.astype(BF16), stack_heads(k_loc, head_k).astype(BF16))
        af = _dot_nt(jnp.concatenate(q_far, axis=0).astype(BF16), stack_heads(k_end, head_k).astype(BF16))
        att = jnp.where(intra, a0, 0.0)
        for d in range(1, n_sub):
            att = jnp.where(dsub == d, af[(d - 1) * c:d * c], att)

        q_in = q * jnp.exp(bc)
        o = _dot(att.astype(BF16), stack_heads(v, head_v)) + _dot(q_in.astype(BF16), state_ref[...].astype(BF16))

        k_out = k * jnp.exp(bc[c - 1:c, :] - bc)
        kv = _dot_tn(k_out.astype(BF16), v)
        gsum = _dot_tn(g_parts, jnp.ones((c, LANES), BF16))
        decay = jnp.exp(gsum[:WK_B] + (gsum[WK_B:2 * WK_B] + gsum[2 * WK_B:]))
        state_ref[...] = (state_ref[...] * jnp.concatenate([decay] * (WV_B // LANES), axis=1)
                          + jnp.where(state_diag, kv, 0.0))

        for h in range(H_B):
            lanes = slice(h * DV_B, (h + 1) * DV_B)
            oh = o[:, lanes]
            yh = oh * lax.rsqrt(jnp.mean(oh * oh, axis=-1, keepdims=True) + EPS) * gn
            y_ref[rows, lanes] = (yh * r_ref[rows, lanes].astype(F32)).astype(y_ref.dtype)

    @pl.when(tb == n_tb - 1)
    def _():
        for h in range(H_B):
            sfin_ref[h] = state_ref[h * DK_B:(h + 1) * DK_B, h * DV_B:(h + 1) * DV_B]


def _gla(q, k, g, v, r, s0, gn, batch, seq, tb_rows):
    n_tb = seq // tb_rows
    row = lambda w: pl.BlockSpec((tb_rows, w), lambda b, t: (b * n_tb + t, 0))
    state = pl.BlockSpec((None, H_B, DK_B, DV_B), lambda b, t: (b, 0, 0, 0))
    return pl.pallas_call(
        _gla_kernel,
        grid=(batch, n_tb),
        in_specs=[row(WK_B), row(WK_B), row(WK_B), row(WV_B), row(WV_B), state,
                  pl.BlockSpec((1, DV_B), lambda b, t: (0, 0))],
        out_specs=[row(WV_B), state],
        out_shape=[jax.ShapeDtypeStruct((batch * seq, WV_B), BF16),
                   jax.ShapeDtypeStruct((batch, H_B, DK_B, DV_B), F32)],
        scratch_shapes=[pltpu.VMEM((WK_B, WV_B), F32)],
        compiler_params=_cparams(("arbitrary", "arbitrary")),
        name="gla",
    )(q, k, g, v, r, s0, gn)


def _mixout_kernel(x_ref, ya_ref, yb_ref, ga_ref, gb_ref, mod_ref, g2_ref, wpa_ref, wpb_ref, wo_ref, rw_ref,
                   x1_ref, hp_ref, lg_ref):
    br_a = _dot(ya_ref[...], wpa_ref[...])
    br_b = _dot(yb_ref[...], wpb_ref[...])
    merged = ga_ref[...].astype(F32) * br_a + gb_ref[...].astype(F32) * br_b
    out = _dot(merged.astype(BF16), wo_ref[...])
    x1 = x_ref[...] + mod_ref[2] * out
    x1_ref[...] = x1
    y = x1 * lax.rsqrt(jnp.mean(x1 * x1, axis=-1, keepdims=True) + EPS) * g2_ref[...]
    h = y * (1.0 + mod_ref[4]) + mod_ref[3]
    hp_ref[...] = _pack_bf16_pair(h)
    h_hi = h.astype(BF16)
    h_lo = (h - h_hi.astype(F32)).astype(BF16)
    lg_ref[...] = _dot_nt(rw_ref[0], h_hi) + (_dot_nt(rw_ref[0], h_lo) + _dot_nt(rw_ref[1], h_hi))


def _mixout(x, ya, yb, ga, gb, mod, rows_per_mod, tm, g2, wpa, wpb, wo, rw):
    n = x.shape[0]
    mod_rows = mod.shape[2]
    tiles_per_mod = rows_per_mod // tm
    row = lambda w: pl.BlockSpec((tm, w), lambda i: (i, 0))
    return pl.pallas_call(
        _mixout_kernel,
        grid=(n // tm,),
        in_specs=[row(D_MODEL), row(W_A), row(WV_B), row(D_MODEL), row(D_MODEL),
                  pl.BlockSpec((6, None, mod_rows, D_MODEL), lambda i: (0, i // tiles_per_mod, 0, 0)),
                  _const_spec(g2.shape), _const_spec(wpa.shape), _const_spec(wpb.shape), _const_spec(wo.shape),
                  _const_spec(rw.shape)],
        out_specs=[row(D_MODEL), row(D_MODEL // 2), pl.BlockSpec((N_EXPERTS, tm), lambda i: (0, i))],
        out_shape=[jax.ShapeDtypeStruct((n, D_MODEL), F32),
                   jax.ShapeDtypeStruct((n, D_MODEL // 2), U32),
                   jax.ShapeDtypeStruct((N_EXPERTS, n), F32)],
        compiler_params=_cparams(("arbitrary",)),
        name="mixout",
    )(x, ya, yb, ga, gb, mod, g2, wpa, wpb, wo, rw)


def _route_kernel(lg_ref, lg_last_ref, bias_ref, idx_ref, w_ref, rank_ref, cnt_ref, carry_ref):
    i = pl.program_id(0)
    tm = lg_ref.shape[1]

    @pl.when(i == 0)
    def _():
        carry_ref[...] = jnp.zeros_like(carry_ref)

    lg = jnp.where(i == pl.num_programs(0) - 1, lg_last_ref[...], lg_ref[...])
    s = _sigmoid(lg)
    sb = s + bias_ref[...][:, 0:1]
    ninf = -jnp.inf

    e_in_group = lax.broadcasted_iota(I32, (GROUP_SIZE, tm), 0)
    gscores = []
    for g in range(N_GROUPS):
        blk = sb[g * GROUP_SIZE:(g + 1) * GROUP_SIZE, :]
        m1 = jnp.max(blk, axis=0, keepdims=True)
        first = jnp.min(jnp.where(blk == m1, e_in_group, GROUP_SIZE), axis=0, keepdims=True)
        m2 = jnp.max(jnp.where(e_in_group == first, ninf, blk), axis=0, keepdims=True)
        gscores.append(m1 + m2)
    gs = jnp.concatenate(gscores, axis=0)

    g_iota = lax.broadcasted_iota(I32, (N_GROUPS, tm), 0)
    chosen = jnp.zeros((N_GROUPS, tm), F32)
    for _ in range(TOPK_GROUPS):
        m = jnp.max(gs, axis=0, keepdims=True)
        first = jnp.min(jnp.where(gs == m, g_iota, N_GROUPS), axis=0, keepdims=True)
        pick = g_iota == first
        chosen = jnp.where(pick, 1.0, chosen)
        gs = jnp.where(pick, ninf, gs)
    allowed = jnp.concatenate(
        [jnp.broadcast_to(chosen[g:g + 1, :], (GROUP_SIZE, tm)) for g in range(N_GROUPS)], axis=0) > 0.5
    cand = jnp.where(allowed, sb, ninf)

    e_iota = lax.broadcasted_iota(I32, (N_EXPERTS, tm), 0)
    onehot = jnp.zeros((N_EXPERTS, tm), F32)
    idxs, ws = [], []
    for _ in range(TOP_K):
        m = jnp.max(cand, axis=0, keepdims=True)
        first = jnp.min(jnp.where(cand == m, e_iota, N_EXPERTS), axis=0, keepdims=True)
        pick = e_iota == first
        idxs.append(first)
        ws.append(jnp.sum(jnp.where(pick, s, 0.0), axis=0, keepdims=True))
        onehot = jnp.where(pick, 1.0, onehot)
        cand = jnp.where(pick, ninf, cand)
    w = jnp.concatenate(ws, axis=0)
    w_ref[...] = w / jnp.sum(w, axis=0, keepdims=True) * ROUTE_SCALE
    idx_ref[...] = jnp.concatenate(idxs, axis=0)

    r = lax.broadcasted_iota(I32, (tm, tm), 0)
    c = lax.broadcasted_iota(I32, (tm, tm), 1)
    earlier = jnp.where(r < c, 1.0, 0.0).astype(BF16)
    carry = carry_ref[...]
    before = _dot(onehot.astype(BF16), earlier) + jnp.concatenate([carry] * (tm // LANES), axis=1)
    rank_ref[...] = jnp.concatenate(
        [jnp.sum(jnp.where(e_iota == ix, before, 0.0), axis=0, keepdims=True) for ix in idxs], axis=0).astype(I32)
    carry = carry + _dot(onehot.astype(BF16), jnp.ones((tm, LANES), BF16))
    carry_ref[...] = carry
    cnt_ref[...] = carry


def _route(logits_t, logits_last, bias):
    tm = MOE_TILE
    n_first = logits_t.shape[1] // tm
    n = logits_t.shape[1] + logits_last.shape[1]
    tok = lambda: pl.BlockSpec((TOP_K, tm), lambda i: (0, i))
    return pl.pallas_call(
        _route_kernel,
        grid=(n // tm,),
        in_specs=[pl.BlockSpec((N_EXPERTS, tm), lambda i: (0, jnp.minimum(i, n_first - 1))),
                  _const_spec(logits_last.shape), _const_spec(bias.shape)],
        out_specs=[tok(), tok(), tok(), _const_spec((N_EXPERTS, LANES))],
        out_shape=[jax.ShapeDtypeStruct((TOP_K, n), I32), jax.ShapeDtypeStruct((TOP_K, n), F32),
                   jax.ShapeDtypeStruct((TOP_K, n), I32), jax.ShapeDtypeStruct((N_EXPERTS, LANES), F32)],
        scratch_shapes=[pltpu.VMEM((N_EXPERTS, LANES), F32)],
        compiler_params=_cparams(("arbitrary",)),
        name="route",
    )(logits_t, logits_last, bias)


def _dest_kernel(idx_ref, rank_ref, cnt_ref, dest_ref, start_ref):
    i = pl.program_id(0)
    tm = idx_ref.shape[1]

    @pl.when(i == 0)
    def _():
        cnt = cnt_ref[...]
        padded = jnp.floor((cnt + (EXPERT_TILE - 1)) * (1.0 / EXPERT_TILE)) * EXPERT_TILE
        r = lax.broadcasted_iota(I32, (N_EXPERTS, N_EXPERTS), 0)
        c = lax.broadcasted_iota(I32, (N_EXPERTS, N_EXPERTS), 1)
        start_ref[...] = _dot(jnp.where(c < r, 1.0, 0.0).astype(F32), padded, precision=HIGHEST)

    start = jnp.concatenate([start_ref[...]] * (tm // LANES), axis=1)
    e_iota = lax.broadcasted_iota(I32, (N_EXPERTS, tm), 0)
    idx = idx_ref[...]
    first = jnp.concatenate(
        [jnp.sum(jnp.where(e_iota == idx[kk:kk + 1, :], start, 0.0), axis=0, keepdims=True) for kk in range(TOP_K)],
        axis=0)
    dest_ref[...] = first.astype(I32) + rank_ref[...]


def _dest(idx_t, rank_t, counts):
    n = idx_t.shape[1]
    tm = MOE_TILE
    tok = lambda: pl.BlockSpec((TOP_K, tm), lambda i: (0, i))
    return pl.pallas_call(
        _dest_kernel,
        grid=(n // tm,),
        in_specs=[tok(), tok(), _const_spec(counts.shape)],
        out_specs=pl.BlockSpec((None, TOP_K, tm), lambda i: (i, 0, 0)),
        out_shape=jax.ShapeDtypeStruct((n // tm, TOP_K, tm), I32),
        scratch_shapes=[pltpu.VMEM((N_EXPERTS, LANES), F32)],
        compiler_params=_cparams(("arbitrary",)),
        name="dest",
    )(idx_t, rank_t, counts)


def _sc_worker():
    return lax.axis_index("s") * SC_CORES + lax.axis_index("c")


def _sc_mesh():
    return plsc.VectorSubcoreMesh(core_axis_name="c", subcore_axis_name="s")


def _dispatch(dest_tiles, hp, hp_last, n_rows_out):
    n_tok_tiles = dest_tiles.shape[0]
    n_first = hp.shape[0] // MOE_TILE
    assert n_first + hp_last.shape[0] // MOE_TILE == n_tok_tiles
    w = hp.shape[1]
    iters = -(-n_tok_tiles // SC_WORKERS)

    def body(hp_hbm, hp_last_hbm, dest_hbm, xs_hbm, idx_v, rows_v, sem):
        wid = _sc_worker()

        def scatter():
            copies = [pltpu.async_copy(rows_v, xs_hbm.at[idx_v.at[kk]], sem) for kk in range(TOP_K)]
            for cp in copies:
                cp.wait()

        @pl.loop(0, iters)
        def _(j):
            tile = wid + SC_WORKERS * j

            @pl.when(tile < n_first)
            def _():
                pltpu.sync_copy(dest_hbm.at[tile], idx_v)
                pltpu.sync_copy(hp_hbm.at[pl.ds(tile * MOE_TILE, MOE_TILE)], rows_v)
                scatter()

            @pl.when((tile >= n_first) & (tile < n_tok_tiles))
            def _():
                pltpu.sync_copy(dest_hbm.at[tile], idx_v)
                pltpu.sync_copy(hp_last_hbm.at[pl.ds((tile - n_first) * MOE_TILE, MOE_TILE)], rows_v)
                scatter()

    return pl.kernel(
        body,
        out_type=jax.ShapeDtypeStruct((n_rows_out, w), U32),
        mesh=_sc_mesh(),
        scratch_types=[pltpu.VMEM((TOP_K, MOE_TILE), I32), pltpu.VMEM((MOE_TILE, w), U32), pltpu.SemaphoreType.DMA],
        name="dispatch",
    )(hp, hp_last, dest_tiles)


def _gather_back(ys, dest_tiles):
    n_tok_tiles = dest_tiles.shape[0]
    w = ys.shape[1]
    n_lists = n_tok_tiles * TOP_K
    iters = -(-n_lists // SC_WORKERS)

    def body(ys_hbm, dest_hbm, out_hbm, idx_v, rows_v, sem):
        wid = _sc_worker()

        @pl.loop(0, iters)
        def _(j):
            lst = wid + SC_WORKERS * j

            @pl.when(lst < n_lists)
            def _():
                pltpu.sync_copy(dest_hbm.at[pl.ds(lst * MOE_TILE, MOE_TILE)], idx_v)
                pltpu.async_copy(ys_hbm.at[idx_v], rows_v, sem).wait()
                pltpu.sync_copy(rows_v, out_hbm.at[pl.ds(lst * MOE_TILE, MOE_TILE)])

    out = pl.kernel(
        body,
        out_type=jax.ShapeDtypeStruct((n_lists * MOE_TILE, w), U32),
        mesh=_sc_mesh(),
        scratch_types=[pltpu.VMEM((MOE_TILE,), I32), pltpu.VMEM((MOE_TILE, w), U32), pltpu.SemaphoreType.DMA],
        name="gather_back",
    )(ys, dest_tiles.reshape(-1))
    return out.reshape(n_tok_tiles, TOP_K, MOE_TILE, w)


def _experts_kernel(first_ref, count_ref, rows_ref, xs_hbm, wg_ref, wu_ref, wd_ref, after_hbm, ys_hbm,
                    xbuf, ybuf, wgb_ref, wub_ref, wdb_ref, sem):
    del after_hbm
    e = pl.program_id(0)
    tm = EXPERT_TILE
    ring = EXPERT_RING
    n_e = count_ref[e]
    total = first_ref[N_EXPERTS - 1] + count_ref[N_EXPERTS - 1]

    def x_copy(t, slot):
        return pltpu.make_async_copy(xs_hbm.at[pl.ds(t * tm, tm), :], xbuf.at[slot], sem.at[0, slot])

    def y_copy(t, slot):
        return pltpu.make_async_copy(ybuf.at[slot], ys_hbm.at[pl.ds(t * tm, tm), :], sem.at[1, slot])

    @pl.when(e == 0)
    def _():
        for t in range(ring - 1):
            @pl.when(t < total)
            def _():
                x_copy(t, t).start()

    @pl.when(n_e > 0)
    def _():
        wgb_ref[...] = wg_ref[...].astype(BF16)
        wub_ref[...] = wu_ref[...].astype(BF16)
        wdb_ref[...] = wd_ref[...].astype(BF16)
        half = D_MODEL // 2

        def body(j, carry):
            t = first_ref[e] + j
            slot = t % ring
            x_copy(t, slot).wait()

            @pl.when(t + ring - 1 < total)
            def _():
                x_copy(t + ring - 1, (t + ring - 1) % ring).start()

            @pl.when(t >= ring)
            def _():
                y_copy(t - ring, slot).wait()

            xs = xbuf[slot]
            filled = lax.broadcasted_iota(I32, xs.shape, 0) < rows_ref[t]
            lo, hi = _unpack_bf16_pair(jnp.where(filled, xs, jnp.zeros_like(xs)))
            gate = _dot(lo, wgb_ref[:half, :]) + _dot(hi, wgb_ref[half:, :])
            up = _dot(lo, wub_ref[:half, :]) + _dot(hi, wub_ref[half:, :])
            act = (_silu(gate) * up).astype(BF16)
            ybuf[slot] = _pack_bf16_pair(_dot(act, wdb_ref[...]))
            y_copy(t, slot).start()
            return carry

        lax.fori_loop(0, n_e, body, 0)

    @pl.when(e == N_EXPERTS - 1)
    def _():
        for back in range(ring, 0, -1):
            @pl.when(total >= back)
            def _():
                y_copy(total - back, (total - back) % ring).wait()


def _experts(tile_first, tile_count, tile_rows, xs, wg, wu, wd, after):
    n_rows, w = xs.shape
    tm = EXPERT_TILE
    expert = lambda e, tf, tc, tr: (e, 0, 0)
    return pl.pallas_call(
        _experts_kernel,
        grid_spec=pltpu.PrefetchScalarGridSpec(
            num_scalar_prefetch=3,
            grid=(N_EXPERTS,),
            in_specs=[pl.BlockSpec(memory_space=pl.ANY),
                      pl.BlockSpec((None, D_MODEL, D_EXPERT), expert),
                      pl.BlockSpec((None, D_MODEL, D_EXPERT), expert),
                      pl.BlockSpec((None, D_EXPERT, D_MODEL), expert),
                      pl.BlockSpec(memory_space=pl.ANY)],
            out_specs=pl.BlockSpec(memory_space=pl.ANY),
            scratch_shapes=[pltpu.VMEM((EXPERT_RING, tm, w), U32), pltpu.VMEM((EXPERT_RING, tm, w), U32),
                            pltpu.VMEM((D_MODEL, D_EXPERT), BF16), pltpu.VMEM((D_MODEL, D_EXPERT), BF16),
                            pltpu.VMEM((D_EXPERT, D_MODEL), BF16),
                            pltpu.SemaphoreType.DMA((2, EXPERT_RING))]),
        out_shape=jax.ShapeDtypeStruct((n_rows, w), U32),
        compiler_params=_cparams(("arbitrary",)),
        name="experts",
    )(tile_first, tile_count, tile_rows, xs, wg, wu, wd, after)


def _shared_kernel(hp_ref, sg_ref, su_ref, sd_ref, o_ref):
    half = D_MODEL // 2
    lo, hi = _unpack_bf16_pair(hp_ref[...])
    sg = sg_ref[...]
    su = su_ref[...]
    gate = _dot(lo, sg[:half]) + _dot(hi, sg[half:])
    up = _dot(lo, su[:half]) + _dot(hi, su[half:])
    o_ref[...] = _dot((_silu(gate) * up).astype(BF16), sd_ref[...]).astype(o_ref.dtype)


def _shared(hp, tm, sg, su, sd):
    n = hp.shape[0]
    return pl.pallas_call(
        _shared_kernel,
        grid=(n // tm,),
        in_specs=[pl.BlockSpec((tm, D_MODEL // 2), lambda i: (i, 0)),
                  _const_spec(sg.shape), _const_spec(su.shape), _const_spec(sd.shape)],
        out_specs=pl.BlockSpec((tm, D_MODEL), lambda i: (i, 0)),
        out_shape=jax.ShapeDtypeStruct((n, D_MODEL), BF16),
        compiler_params=_cparams(("arbitrary",)),
        name="shared",
    )(hp, sg, su, sd)


def _combine_kernel(yt_ref, x1_ref, ysh_ref, w_ref, mod_ref, fg_ref, *rest):
    o_ref = rest[-1]
    y = ysh_ref[...].astype(F32)
    w = w_ref[...]
    for kk in range(TOP_K):
        ylo, yhi = _unpack_bf16_pair(yt_ref[kk])
        y = y + w[:, kk:kk + 1] * jnp.concatenate([ylo.astype(F32), yhi.astype(F32)], axis=1)
    x2 = x1_ref[...] + mod_ref[5] * y
    o_ref[...] = x2 * lax.rsqrt(jnp.mean(x2 * x2, axis=-1, keepdims=True) + EPS) * fg_ref[...]


def _combine(y_tok, y_tok_tile0, x1, y_shared, w, w_tile0, mod, rows_per_mod, fg, tile0, n_tiles, prev=None):
    n = x1.shape[0]
    tm = MOE_TILE
    mod_rows = mod.shape[2]
    tiles_per_mod = rows_per_mod // tm
    own = lambda i: (i + tile0, 0)
    return pl.pallas_call(
        _combine_kernel,
        grid=(n_tiles,),
        in_specs=[pl.BlockSpec((None, TOP_K, tm, D_MODEL // 2), lambda i: (i + y_tok_tile0, 0, 0, 0)),
                  pl.BlockSpec((tm, D_MODEL), own),
                  pl.BlockSpec((tm, D_MODEL), own),
                  pl.BlockSpec((tm, TOP_K), lambda i: (i + w_tile0, 0)),
                  pl.BlockSpec((6, None, mod_rows, D_MODEL), lambda i: (0, (i + tile0) // tiles_per_mod, 0, 0)),
                  _const_spec(fg.shape)] + ([] if prev is None else [pl.BlockSpec(memory_space=pl.ANY)]),
        out_specs=pl.BlockSpec((tm, D_MODEL), own),
        out_shape=jax.ShapeDtypeStruct((n, D_MODEL), F32),
        input_output_aliases={} if prev is None else {6: 0},
        compiler_params=_cparams(("arbitrary",)),
        name="combine",
    )(y_tok, x1, y_shared, w, mod, fg, *(() if prev is None else (prev,)))


def _split_w_in(w_in):
    sizes = (W_A, W_A, W_A, H_A, WK_B, WK_B, WV_B, WV_B, GK_RANK, D_MODEL, D_MODEL)
    segs, o = [], 0
    for s in sizes:
        segs.append(w_in[:, o:o + s])
        o += s
    return segs


def kernel(x_prompt, x_sample, cache_k, cache_v, cache_logf, state_gla, page_table, c_prompt, c_sample, w_ada, b_ada,
           norm1_g, w_in, b_f, w_gk2, b_gk, gla_norm_g, w_pa, w_pb, w_o, norm2_g, router_w, router_bias, w_gate, w_up,
           w_down, ws_gate, ws_up, ws_down, final_g):
    assert w_ada.shape[0] == 1, "single layer"
    bp, tp, d = x_prompt.shape
    bs, ts, _ = x_sample.shape
    n_p, n_s = bp * tp, bs * ts
    n_tot = n_p + n_s
    assert n_s == MOE_TILE and n_p % MOE_TILE == 0 and tp % FOX_TILE == 0

    q_a, k_a, v_a, f_a, q_b, k_b, v_b, r_b, gk1, gate_a, gate_b = _split_w_in(w_in[0])
    wts = (jnp.concatenate([q_a, k_a], axis=1).astype(BF16),
           v_a.astype(BF16),
           jnp.concatenate([k_a, v_a], axis=1).T.astype(BF16),
           jnp.pad(f_a, ((0, 0), (0, LANES - H_A))).astype(BF16),
           jnp.pad(b_f[0].reshape(1, H_A), ((0, 0), (0, LANES - H_A))),
           jnp.concatenate([q_b, k_b, v_b, r_b], axis=1).astype(BF16),
           jnp.pad(gk1, ((0, 0), (0, LANES - GK_RANK))).astype(BF16),
           jnp.pad(w_gk2[0], ((0, LANES - GK_RANK), (0, 0))).astype(BF16),
           b_gk[0].reshape(1, WK_B),
           jnp.concatenate([gate_a, gate_b], axis=1).astype(BF16))
    g1 = norm1_g[0].reshape(1, d)
    g2 = norm2_g[0].reshape(1, d)
    gn = gla_norm_g[0].reshape(1, DV_B)
    fg = final_g.reshape(1, d)
    wpa, wpb, wo = w_pa[0].astype(BF16), w_pb[0].astype(BF16), w_o[0].astype(BF16)
    rw_hi = router_w[0].T.astype(BF16)
    rw_t = jnp.stack([rw_hi, (router_w[0].T - rw_hi.astype(F32)).astype(BF16)])
    rbias = jnp.broadcast_to(router_bias[0].reshape(N_EXPERTS, 1), (N_EXPERTS, LANES))
    sg, su, sd = ws_gate[0].astype(BF16), ws_up[0].astype(BF16), ws_down[0].astype(BF16)

    ada = _ada(jnp.concatenate([c_prompt, c_sample], axis=0), w_ada[0], b_ada[0].reshape(1, -1))
    ada = ada.reshape(bp + bs, 6, d)
    mod_p = ada[:bp].transpose(1, 0, 2)[:, :, None, :]
    mod_s = jnp.repeat(ada[bp:], ts, axis=0).transpose(1, 0, 2)[:, None, :, :]

    xp = x_prompt.reshape(n_p, d)
    tm_p = 512
    (q, kh, vth, kt, vt, lf, fc, qb, kb, gk, vb, rb, ga_p, gb_p) = _inproj(
        xp, mod_p, tp, tp // tm_p, tm_p, g1, wts, True)
    ya_p = _fox(q, kh, vth, fc, bp, tp)
    yb_p, s_p = _gla(qb, kb, gk, vb, rb, jnp.zeros((bp, H_B, DK_B, DV_B), F32), gn, bp, tp, 256)
    k_prompt = kt.reshape(bp, H_A, DH_A, tp).transpose(0, 3, 1, 2)[None]
    v_prompt = vt.reshape(bp, H_A, DH_A, tp).transpose(0, 3, 1, 2)[None]
    logf_prompt = lf.reshape(1, bp, tp, H_A)

    xs_in = x_sample.reshape(n_s, d)
    (q, k, v, lf, qb, kb, gk, vb, rb, ga, gb) = _inproj(xs_in, mod_s, n_s, 1, n_s, g1, wts, False)
    page_rows = cache_k.shape[2]
    tq = 8
    cache_kt = jnp.transpose(cache_k, (0, 1, 3, 4, 2))
    cache_vt = jnp.transpose(cache_v, (0, 1, 3, 4, 2))
    suf = _lfsuf(page_table, jnp.transpose(cache_logf, (0, 1, 3, 2)))

    def new_kv(a):
        a = a.reshape(bs, ts, H_A, DH_A).transpose(0, 2, 3, 1)
        return jnp.pad(a, ((0, 0), (0, 0), (0, 0), (0, page_rows - ts)))

    q_dec = jnp.pad(q.astype(F32).reshape(bs, ts, H_A, DH_A).transpose(0, 2, 1, 3),
                    ((0, 0), (0, 0), (0, tq - ts), (0, 0)))
    lfn = jnp.pad(lf.reshape(bs, ts, H_A).transpose(0, 2, 1), ((0, 0), (0, 0), (0, page_rows - ts)))
    ya = _foxdec(page_table, q_dec, new_kv(k), new_kv(v), lfn, suf, cache_kt, cache_vt)
    ya = ya[:, :, :ts].transpose(0, 2, 1, 3).reshape(n_s, W_A).astype(BF16)
    pad = lambda a: jnp.pad(a.reshape(bs, ts, -1), ((0, 0), (0, GLA_CHUNK - ts), (0, 0))).reshape(bs * GLA_CHUNK, -1)
    yb, s_s = _gla(pad(qb), pad(kb), pad(gk), pad(vb), pad(rb), state_gla[0], gn, bs, GLA_CHUNK, GLA_CHUNK)
    yb = yb.reshape(bs, GLA_CHUNK, WV_B)[:, :ts].reshape(n_s, WV_B)
    k_sample = k.reshape(1, bs, ts, H_A, DH_A)
    v_sample = v.reshape(1, bs, ts, H_A, DH_A)
    logf_sample = lf.reshape(1, bs, ts, H_A)
    x1_s, hp_s, logits_s = _mixout(xs_in, ya, yb, ga, gb, mod_s, n_s, n_s, g2, wpa, wpb, wo, rw_t)

    x1_p, hp_p, logits_p = _mixout(xp, ya_p, yb_p, ga_p, gb_p, mod_p, tp, 4 * MOE_TILE, g2, wpa, wpb, wo, rw_t)
    idx_t, w_t, rank_t, counts = _route(logits_p, logits_s, rbias)
    dest_tiles = _dest(idx_t, rank_t, counts)
    padded = (counts[:, 0].astype(I32) + EXPERT_TILE - 1) // EXPERT_TILE * EXPERT_TILE
    pad_end = jnp.cumsum(padded)
    n_tiles = -(-n_tot * TOP_K // EXPERT_TILE) + N_EXPERTS
    tile_first_row = jnp.arange(n_tiles, dtype=I32) * EXPERT_TILE
    tile_expert = jnp.minimum(jnp.sum((pad_end[None, :] <= tile_first_row[:, None]).astype(I32), axis=1),
                              N_EXPERTS - 1)
    used_end = pad_end - padded + counts[:, 0].astype(I32)
    tile_rows = jnp.clip(used_end[tile_expert] - tile_first_row, 0, EXPERT_TILE).astype(I32)
    tile_first = ((pad_end - padded) // EXPERT_TILE).astype(I32)
    tile_count = (padded // EXPERT_TILE).astype(I32)

    xs_sorted = _dispatch(dest_tiles, hp_p, hp_s, n_tiles * EXPERT_TILE)
    ysh_p = _shared(hp_p, 2 * MOE_TILE, sg, su, sd)
    ysh_s = _shared(hp_s, MOE_TILE, sg, su, sd)
    ys_sorted = _experts(tile_first, tile_count, tile_rows, xs_sorted, w_gate[0], w_up[0], w_down[0], ysh_p)
    n_p_tiles = n_p // MOE_TILE
    part = n_p_tiles // GATHER_PARTS
    w_tok = w_t.T
    y_prompt = None
    for j in range(GATHER_PARTS):
        last = j == GATHER_PARTS - 1
        y_tok = _gather_back(ys_sorted, dest_tiles[j * part:] if last else dest_tiles[j * part:(j + 1) * part])
        y_prompt = _combine(y_tok, 0, x1_p, ysh_p, w_tok, j * part, mod_p, tp, fg, j * part, part, prev=y_prompt)
    y_sample = _combine(y_tok, part, x1_s, ysh_s, w_tok, n_p_tiles, mod_s, n_s, fg, 0, n_s // MOE_TILE)

    return (y_prompt.reshape(bp, tp, d), y_sample.reshape(bs, ts, d),
            k_prompt, v_prompt, logf_prompt, s_p[None],
            k_sample, v_sample, logf_sample, s_s[None])
```

```python
import functools

import jax
import jax.numpy as jnp
from jax import lax
from jax.experimental import pallas as pl
from jax.experimental.pallas import tpu as pltpu
from jax.experimental.pallas import tpu_sc as plsc

F32 = jnp.float32
BF16 = jnp.bfloat16
U32 = jnp.uint32
I32 = jnp.int32

EPS = 1e-6
D_MODEL = 1024
H_A, DH_A, W_A = 8, 64, 512
H_B, DK_B, DV_B, WK_B, WV_B = 4, 64, 128, 256, 512
GK_RANK = 16
GATE_NORM = 16.0
N_EXPERTS, TOP_K, N_GROUPS, TOPK_GROUPS = 256, 8, 8, 4
GROUP_SIZE = N_EXPERTS // N_GROUPS
D_EXPERT = 256
ROUTE_SCALE = 2.5
MOE_TILE = 128
EXPERT_TILE = 256
EXPERT_RING = 4
GATHER_PARTS = 4
LANES = 128
SC_CORES = 2
SC_WORKERS = 32
GLA_CHUNK = 64
GLA_SUB = 16
FOX_TILE = 512
FOX_SUB = 128
DEC_PAGES = 8
DEC_RING = 3
NEG = -1e30
LOG2E = 1.4426950408889634
VMEM_LIMIT = 56 * 1024 * 1024
HIGHEST = lax.Precision.HIGHEST

NT_DIMS = (((1,), (1,)), ((), ()))
TN_DIMS = (((0,), (0,)), ((), ()))


def _cparams(sem):
    return pltpu.CompilerParams(dimension_semantics=sem, vmem_limit_bytes=VMEM_LIMIT)


def _dot(a, b, **kw):
    return jnp.dot(a, b, preferred_element_type=F32, **kw)


def _dot_nt(a, b, **kw):
    return lax.dot_general(a, b, NT_DIMS, preferred_element_type=F32, **kw)


def _dot_tn(a, b, **kw):
    return lax.dot_general(a, b, TN_DIMS, preferred_element_type=F32, **kw)


def _sigmoid(x):
    return 0.5 * jnp.tanh(0.5 * x) + 0.5


def _silu(x):
    return x * _sigmoid(x)


def _log_sigmoid(x):
    return jnp.minimum(x, 0.0) - jnp.log(1.0 + jnp.exp(-jnp.abs(x)))


def _pack_bf16_pair(x):
    c = x.shape[1] // 2
    bits = pltpu.bitcast(x.astype(BF16).astype(F32), U32)
    return (bits[:, c:] & jnp.uint32(0xFFFF0000)) | (bits[:, :c] >> 16)


def _unpack_bf16_pair(p):
    lo = pltpu.bitcast(p << 16, F32).astype(BF16)
    hi = pltpu.bitcast(p & jnp.uint32(0xFFFF0000), F32).astype(BF16)
    return lo, hi


def _ada_kernel(c_ref, w_ref, b_ref, o_ref):
    c = c_ref[...]
    o_ref[...] = _dot(_silu(c).astype(BF16), w_ref[...].astype(BF16)) + b_ref[...]


def _ada(c, w, b):
    n, d = c.shape
    cols = w.shape[1]
    tn = 1536
    return pl.pallas_call(
        _ada_kernel,
        grid=(cols // tn,),
        in_specs=[pl.BlockSpec((n, d), lambda j: (0, 0)),
                  pl.BlockSpec((d, tn), lambda j: (0, j)),
                  pl.BlockSpec((1, tn), lambda j: (0, j))],
        out_specs=pl.BlockSpec((n, tn), lambda j: (0, j)),
        out_shape=jax.ShapeDtypeStruct((n, cols), F32),
        compiler_params=_cparams(("arbitrary",)),
        name="ada",
    )(c, w, b)


def _inproj_kernel(*refs, seq_tiles, prompt):
    (x_ref, mod_ref, g_ref, wqk_ref, wv_ref, wkvt_ref, wf_ref, bf_ref, wb_ref, wgk1_ref, wgk2_ref, bgk_ref,
     wgate_ref) = refs[:13]
    if prompt:
        (q_ref, kh_ref, vth_ref, kt_ref, vt_ref, lf_ref, fc_ref, qb_ref, kb_ref, gk_ref, vb_ref, rb_ref, ga_ref,
         gb_ref, carry_ref) = refs[13:]
    else:
        q_ref, k_ref, v_ref, lf_ref, qb_ref, kb_ref, gk_ref, vb_ref, rb_ref, ga_ref, gb_ref = refs[13:]
    i = pl.program_id(0)
    tm = x_ref.shape[0]
    x = x_ref[...]
    y = x * lax.rsqrt(jnp.mean(x * x, axis=-1, keepdims=True) + EPS) * g_ref[...]
    h = (y * (1.0 + mod_ref[1]) + mod_ref[0]).astype(BF16)

    z = _dot(h, wqk_ref[...])
    q_ref[...] = (z[:, :W_A] * (DH_A ** -0.5 * (LOG2E if prompt else 1.0))).astype(BF16)
    k = z[:, W_A:]

    lf = _log_sigmoid(_dot(h, wf_ref[...]) + bf_ref[...])
    lf_ref[...] = lf[:, :H_A]

    if prompt:
        kvt = _dot_nt(wkvt_ref[...], h)
        kt_ref[...] = kvt[:W_A]
        vt_ref[...] = kvt[W_A:]
        vth_ref[...] = kvt[W_A:].astype(BF16)
        kh_ref[...] = k.astype(BF16)

        @pl.when(i % seq_tiles == 0)
        def _():
            carry_ref[...] = jnp.zeros_like(carry_ref)

        r = lax.broadcasted_iota(I32, (tm, tm), 0)
        c = lax.broadcasted_iota(I32, (tm, tm), 1)
        lower = jnp.where(c <= r, 1.0, 0.0).astype(BF16)
        lf_hi = lf.astype(BF16)
        lf_r = lf - lf_hi.astype(F32)
        lf_mid = lf_r.astype(BF16)
        lf_parts = jnp.concatenate([lf_hi, lf_mid, (lf_r - lf_mid.astype(F32)).astype(BF16)], axis=1)
        sum3 = lambda a: a[:, :LANES] + (a[:, LANES:2 * LANES] + a[:, 2 * LANES:])
        carry = carry_ref[...]
        fc = sum3(_dot(lower, lf_parts)) + carry[0:1, :]
        fc_ref[...] = fc[:, :H_A] * LOG2E
        carry_ref[...] = carry + sum3(_dot(jnp.ones((8, tm), BF16), lf_parts))
    else:
        k_ref[...] = k
        v_ref[...] = _dot(h, wv_ref[...])

    z = _dot(h, wb_ref[...])
    qb_ref[...] = z[:, :WK_B] * DK_B ** -0.5
    kb_ref[...] = z[:, WK_B:2 * WK_B]
    vb_ref[...] = z[:, 2 * WK_B:2 * WK_B + WV_B].astype(BF16)
    rb_ref[...] = _silu(z[:, 2 * WK_B + WV_B:]).astype(BF16)

    lr = _dot(h, wgk1_ref[...]).astype(BF16)
    gk_ref[...] = _log_sigmoid(_dot(lr, wgk2_ref[...]) + bgk_ref[...]) * (1.0 / GATE_NORM)

    z = _dot(h, wgate_ref[...])
    ga_ref[...] = _sigmoid(z[:, :D_MODEL]).astype(BF16)
    gb_ref[...] = _sigmoid(z[:, D_MODEL:]).astype(BF16)


def _const_spec(shape):
    return pl.BlockSpec(shape, lambda i: (0,) * len(shape))


def _inproj(x, mod, rows_per_mod, seq_tiles, tm, g, wts, prompt):
    n = x.shape[0]
    mod_rows = mod.shape[2]
    tiles_per_mod = rows_per_mod // tm
    row = lambda w: pl.BlockSpec((tm, w), lambda i: (i, 0))
    if prompt:
        n_seq = n // (seq_tiles * tm)
        seq_t = pl.BlockSpec((None, W_A, tm), lambda i: (i // seq_tiles, 0, i % seq_tiles))
        kv_shape = (n_seq, W_A, seq_tiles * tm)
        outs = [
            (row(W_A), (n, W_A), BF16),
            (row(W_A), (n, W_A), BF16),
            (pl.BlockSpec((W_A, tm), lambda i: (0, i)), (W_A, n), BF16),
            (seq_t, kv_shape, F32),
            (seq_t, kv_shape, F32),
            (row(H_A), (n, H_A), F32),
            (row(H_A), (n, H_A), F32),
        ]
    else:
        outs = [
            (row(W_A), (n, W_A), BF16),
            (row(W_A), (n, W_A), F32),
            (row(W_A), (n, W_A), F32),
            (row(H_A), (n, H_A), F32),
        ]
    outs += [
        (row(WK_B), (n, WK_B), F32),
        (row(WK_B), (n, WK_B), F32),
        (row(WK_B), (n, WK_B), F32),
        (row(WV_B), (n, WV_B), BF16),
        (row(WV_B), (n, WV_B), BF16),
        (row(D_MODEL), (n, D_MODEL), BF16),
        (row(D_MODEL), (n, D_MODEL), BF16),
    ]
    return pl.pallas_call(
        functools.partial(_inproj_kernel, seq_tiles=seq_tiles, prompt=prompt),
        grid=(n // tm,),
        in_specs=[row(D_MODEL),
                  pl.BlockSpec((6, None, mod_rows, D_MODEL), lambda i: (0, i // tiles_per_mod, 0, 0)),
                  _const_spec(g.shape)] + [_const_spec(w.shape) for w in wts],
        out_specs=[o[0] for o in outs],
        out_shape=[jax.ShapeDtypeStruct(o[1], o[2]) for o in outs],
        scratch_shapes=[pltpu.VMEM((8, LANES), F32)] if prompt else [],
        compiler_params=_cparams(("arbitrary",)),
        name="inproj",
    )(x, mod, g, *wts)


def _fox_kernel(q_ref, k_ref, vt_ref, fc_ref, o_ref, qs_ref, m_ref, l_ref, acc_ref):
    qi = pl.program_id(1)
    ki = pl.program_id(2)
    t = q_ref.shape[0]
    pairs = H_A // 2
    sub = FOX_SUB

    @pl.when(ki == 0)
    def _():
        lane = lax.broadcasted_iota(I32, (t, LANES), 1)
        for hp in range(pairs):
            q = q_ref[:, hp * LANES:(hp + 1) * LANES]
            qs_ref[hp, :t, :] = jnp.where(lane < DH_A, q, jnp.zeros_like(q))
            qs_ref[hp, t:, :] = jnp.where(lane >= DH_A, q, jnp.zeros_like(q))
        m_ref[...] = jnp.full_like(m_ref, NEG)
        l_ref[...] = jnp.zeros_like(l_ref)
        acc_ref[...] = jnp.zeros_like(acc_ref)

    def step(diagonal):
        for hp in range(pairs):
            k = k_ref[:, hp * LANES:(hp + 1) * LANES]
            vt = vt_ref[hp * LANES:(hp + 1) * LANES, :]
            for h2 in range(2):
                head = 2 * hp + h2
                fb = jnp.broadcast_to(fc_ref[:, head:head + 1], (t, sub))
                for sb in range(t // sub):
                    cols = slice(h2 * t + sb * sub, h2 * t + (sb + 1) * sub)
                    s = _dot_nt(k, qs_ref[hp, cols, :]) - fb
                    if diagonal:
                        key = lax.broadcasted_iota(I32, (t, sub), 0)
                        qry = lax.broadcasted_iota(I32, (t, sub), 1) + sb * sub
                        s = jnp.where(key <= qry, s, NEG)
                    m_prev = m_ref[hp, :, cols]
                    m_new = jnp.maximum(m_prev, jnp.max(s, axis=0, keepdims=True))
                    alpha = jnp.exp2(m_prev - m_new)
                    p = jnp.exp2(s - m_new)
                    l_ref[hp, :, cols] = alpha * l_ref[hp, :, cols] + jnp.sum(p, axis=0, keepdims=True)
                    acc_ref[hp, :, cols] = alpha * acc_ref[hp, :, cols] + _dot(vt, p.astype(BF16))
                    m_ref[hp, :, cols] = m_new

    @pl.when(ki < qi)
    def _():
        step(False)

    @pl.when(ki == qi)
    def _():
        step(True)
        for hp in range(pairs):
            o = acc_ref[hp] / l_ref[hp]
            o = jnp.concatenate([o[:DH_A, :t], o[DH_A:, t:]], axis=0)
            o_ref[:, hp * LANES:(hp + 1) * LANES] = o.T.astype(o_ref.dtype)


def _fox(q, k, vt, fc, batch, seq):
    t = FOX_TILE
    nq = seq // t
    kv_map = lambda b, qi, ki: (b * nq + jnp.minimum(ki, qi), 0)
    return pl.pallas_call(
        _fox_kernel,
        grid=(batch, nq, nq),
        in_specs=[pl.BlockSpec((t, W_A), lambda b, qi, ki: (b * nq + qi, 0)),
                  pl.BlockSpec((t, W_A), kv_map),
                  pl.BlockSpec((W_A, t), lambda b, qi, ki: (0, b * nq + jnp.minimum(ki, qi))),
                  pl.BlockSpec((t, H_A), kv_map)],
        out_specs=pl.BlockSpec((t, W_A), lambda b, qi, ki: (b * nq + qi, 0)),
        out_shape=jax.ShapeDtypeStruct((batch * seq, W_A), BF16),
        scratch_shapes=[pltpu.VMEM((H_A // 2, 2 * t, LANES), BF16),
                        pltpu.VMEM((H_A // 2, 1, 2 * t), F32),
                        pltpu.VMEM((H_A // 2, 1, 2 * t), F32),
                        pltpu.VMEM((H_A // 2, LANES, 2 * t), F32)],
        compiler_params=_cparams(("arbitrary", "arbitrary", "arbitrary")),
        name="fox",
    )(q, k, vt, fc)


def _lfsuf_kernel(pt_ref, lf_hbm, o_ref, buf_ref, sem):
    b = pl.program_id(0)
    _, n_pages, _, w = buf_ref.shape
    slot = b % 2

    def page_copy(seq, j):
        return pltpu.make_async_copy(lf_hbm.at[0, pt_ref[seq, j]], buf_ref.at[seq % 2, j], sem.at[seq % 2])

    def fetch(seq):
        def issue(j, carry):
            page_copy(seq, j).start()
            return carry
        lax.fori_loop(0, n_pages, issue, 0)

    @pl.when(b == 0)
    def _():
        fetch(0)

    @pl.when(b + 1 < pl.num_programs(0))
    def _():
        fetch(b + 1)

    def drain(j, carry):
        page_copy(b, j).wait()
        return carry

    lax.fori_loop(0, n_pages, drain, 0)

    x = buf_ref[slot]
    lane = lax.broadcasted_iota(I32, x.shape, 2)
    s = x
    shift = 1
    while shift < w:
        s = s + jnp.where(lane + shift < w, pltpu.roll(s, w - shift, axis=2), 0.0)
        shift *= 2
    within = s - x
    after = jnp.zeros((H_A, w), F32)
    for j in range(n_pages - 1, -1, -1):
        o_ref[j] = within[j] + after
        after = after + jnp.broadcast_to(s[j][:, 0:1], (H_A, w))


def _lfsuf(page_table, lf_pages):
    nb, n_pages = page_table.shape
    w = lf_pages.shape[3]
    return pl.pallas_call(
        _lfsuf_kernel,
        grid_spec=pltpu.PrefetchScalarGridSpec(
            num_scalar_prefetch=1,
            grid=(nb,),
            in_specs=[pl.BlockSpec(memory_space=pl.ANY)],
            out_specs=pl.BlockSpec((None, n_pages, H_A, w), lambda b, pt: (b, 0, 0, 0)),
            scratch_shapes=[pltpu.VMEM((2, n_pages, H_A, w), F32), pltpu.SemaphoreType.DMA((2,))]),
        out_shape=jax.ShapeDtypeStruct((nb, n_pages, H_A, w), F32),
        compiler_params=_cparams(("arbitrary",)),
        name="lfsuf",
    )(page_table, lf_pages)


def _foxdec_kernel(pt_ref, q_ref, kn_ref, vn_ref, lfn_ref, suf_ref, ck_hbm, cv_hbm, o_ref,
                   kbuf, vbuf, sem, m_ref, l_ref, acc_ref):
    b = pl.program_id(0)
    n_pages = suf_ref.shape[0]
    page_rows = kbuf.shape[-1]
    n_chunks = n_pages // DEC_PAGES
    tq = q_ref.shape[1]

    total = pl.num_programs(0) * n_chunks

    def page_copies(g, j):
        slot = g % DEC_RING
        pg = pt_ref[g // n_chunks, (g % n_chunks) * DEC_PAGES + j]
        return (pltpu.make_async_copy(ck_hbm.at[0, pg], kbuf.at[slot, j], sem.at[0, slot]),
                pltpu.make_async_copy(cv_hbm.at[0, pg], vbuf.at[slot, j], sem.at[1, slot]))

    def fetch(g):
        for j in range(DEC_PAGES):
            for cp in page_copies(g, j):
                cp.start()

    def wait(g):
        for j in range(DEC_PAGES):
            for cp in page_copies(g, j):
                cp.wait()

    q = q_ref[...].astype(BF16)
    m_ref[...] = jnp.full_like(m_ref, NEG)
    l_ref[...] = jnp.zeros_like(l_ref)
    acc_ref[...] = jnp.zeros_like(acc_ref)

    def attend(kt, vt, bias, ok):
        s = jnp.einsum('htd,hds->hts', q, kt, preferred_element_type=F32) + bias
        if ok is not None:
            s = jnp.where(ok, s, NEG)
        m_prev = m_ref[...]
        m_new = jnp.maximum(m_prev, jnp.max(s, axis=2, keepdims=True))
        alpha = jnp.exp(m_prev - m_new)
        p = jnp.exp(s - m_new)
        l_ref[...] = alpha * l_ref[...] + jnp.sum(p, axis=2, keepdims=True)
        acc_ref[...] = alpha * acc_ref[...] + jnp.einsum('hts,hds->htd', p.astype(BF16), vt,
                                                         preferred_element_type=F32)
        m_ref[...] = m_new

    @pl.when(b == 0)
    def _():
        for g in range(DEC_RING - 1):
            fetch(g)

    def chunk(c, carry):
        g = b * n_chunks + c
        slot = g % DEC_RING

        @pl.when(g + DEC_RING - 1 < total)
        def _():
            fetch(g + DEC_RING - 1)

        wait(g)
        kt = jnp.concatenate([kbuf[slot, j] for j in range(DEC_PAGES)], axis=2).astype(BF16)
        vt = jnp.concatenate([vbuf[slot, j] for j in range(DEC_PAGES)], axis=2).astype(BF16)
        suf = suf_ref[pl.ds(c * DEC_PAGES, DEC_PAGES)]
        bias = jnp.concatenate([suf[j] for j in range(DEC_PAGES)], axis=1)
        attend(kt, vt, bias[:, None, :], None)
        return carry

    lax.fori_loop(0, n_chunks, chunk, 0)

    r = lax.broadcasted_iota(I32, (page_rows, page_rows), 0)
    c = lax.broadcasted_iota(I32, (page_rows, page_rows), 1)
    fn = _dot(lfn_ref[...], jnp.where(r <= c, 1.0, 0.0).astype(F32), precision=HIGHEST)
    qi = lax.broadcasted_iota(I32, (H_A, tq, page_rows), 1)
    ki = lax.broadcasted_iota(I32, (H_A, tq, page_rows), 2)
    attend(kn_ref[...].astype(BF16), vn_ref[...].astype(BF16), -fn[:, None, :], ki <= qi)
    o_ref[...] = acc_ref[...] / l_ref[...]


def _foxdec(page_table, q, kn, vn, lfn, suf, cache_kt, cache_vt):
    nb, _, tq, _ = q.shape
    n_pages = page_table.shape[1]
    page_rows = cache_kt.shape[4]
    per_seq = lambda shape: pl.BlockSpec((None,) + shape, lambda b, pt: (b,) + (0,) * len(shape))
    page = (H_A, DH_A, page_rows)
    return pl.pallas_call(
        _foxdec_kernel,
        grid_spec=pltpu.PrefetchScalarGridSpec(
            num_scalar_prefetch=1,
            grid=(nb,),
            in_specs=[per_seq((H_A, tq, DH_A)), per_seq(page), per_seq(page), per_seq((H_A, page_rows)),
                      per_seq((n_pages, H_A, page_rows)),
                      pl.BlockSpec(memory_space=pl.ANY), pl.BlockSpec(memory_space=pl.ANY)],
            out_specs=per_seq((H_A, tq, DH_A)),
            scratch_shapes=[pltpu.VMEM((DEC_RING, DEC_PAGES) + page, F32),
                            pltpu.VMEM((DEC_RING, DEC_PAGES) + page, F32),
                            pltpu.SemaphoreType.DMA((2, DEC_RING)),
                            pltpu.VMEM((H_A, tq, 1), F32), pltpu.VMEM((H_A, tq, 1), F32),
                            pltpu.VMEM((H_A, tq, DH_A), F32)]),
        out_shape=jax.ShapeDtypeStruct((nb, H_A, tq, DH_A), F32),
        compiler_params=_cparams(("arbitrary",)),
        name="foxdec",
    )(page_table, q, kn, vn, lfn, suf, cache_kt, cache_vt)


def _gla_kernel(q_ref, k_ref, g_ref, v_ref, r_ref, s0_ref, gn_ref, y_ref, sfin_ref, state_ref):
    tb = pl.program_id(1)
    n_tb = pl.num_programs(1)
    c = GLA_CHUNK
    n_chunks = q_ref.shape[0] // c
    n_sub = c // GLA_SUB

    row_k = lax.broadcasted_iota(I32, (WK_B, WV_B), 0) // DK_B
    col_v = lax.broadcasted_iota(I32, (WK_B, WV_B), 1) // DV_B
    state_diag = row_k == col_v

    @pl.when(tb == 0)
    def _():
        state_ref[...] = jnp.zeros_like(state_ref)
        for h in range(H_B):
            state_ref[h * DK_B:(h + 1) * DK_B, h * DV_B:(h + 1) * DV_B] = s0_ref[h]

    r = lax.broadcasted_iota(I32, (c, c), 0)
    s = lax.broadcasted_iota(I32, (c, c), 1)
    same_sub = (r // GLA_SUB) == (s // GLA_SUB)
    cum_ops = jnp.concatenate([
        jnp.where(same_sub & (s <= r), 1.0, 0.0),
        jnp.where(s <= r, 1.0, 0.0),
        jnp.where(same_sub, 1.0, 0.0),
    ], axis=0).astype(BF16)

    t_i = lax.broadcasted_iota(I32, (c, H_B * c), 0)
    s_i = lax.broadcasted_iota(I32, (c, H_B * c), 1) % c
    dsub = t_i // GLA_SUB - s_i // GLA_SUB
    intra = (dsub == 0) & (s_i <= t_i)
    head_k = lax.broadcasted_iota(I32, (c, WK_B), 1) // DK_B
    head_v = lax.broadcasted_iota(I32, (c, WV_B), 1) // DV_B

    def stack_heads(x, head_of_lane):
        zero = jnp.zeros_like(x)
        return jnp.concatenate([jnp.where(head_of_lane == h, x, zero) for h in range(H_B)], axis=0)

    def shift_rows(x, n):
        return jnp.concatenate([x[:n], x[:c - n]], axis=0)

    gn = gn_ref[...]
    for ci in range(n_chunks):
        rows = slice(ci * c, (ci + 1) * c)
        q = q_ref[rows, :]
        k = k_ref[rows, :]
        g = g_ref[rows, :]
        v = v_ref[rows, :]

        g_hi = g.astype(BF16)
        g_r = g - g_hi.astype(F32)
        g_mid = g_r.astype(BF16)
        g_parts = jnp.concatenate([g_hi, g_mid, (g_r - g_mid.astype(F32)).astype(BF16)], axis=1)
        cums = _dot(cum_ops, g_parts)
        cums = cums[:, :WK_B] + (cums[:, WK_B:2 * WK_B] + cums[:, 2 * WK_B:])
        bl, bc, tot = cums[:c], cums[c:2 * c], cums[2 * c:]
        pre = bc - bl
        q_loc = q * jnp.exp(bl)
        k_loc = k * jnp.exp(-bl)
        k_end = k * jnp.exp(tot - bl)
        q_far = [q_loc] + [q_loc * jnp.exp(pre - shift_rows(pre, GLA_SUB * d)) for d in range(1, n_sub - 1)]

        a0 = _dot_nt(q_loc.astype(BF16), stack_heads(k_loc, head_k).astype(BF16))
        af = _dot_nt(jnp.concatenate(q_far, axis=0).astype(BF16), stack_heads(k_end, head_k).astype(BF16))
        att = jnp.where(intra, a0, 0.0)
        for d in range(1, n_sub):
            att = jnp.where(dsub == d, af[(d - 1) * c:d * c], att)

        q_in = q * jnp.exp(bc)
        o = _dot(att.astype(BF16), stack_heads(v, head_v)) + _dot(q_in.astype(BF16), state_ref[...].astype(BF16))

        k_out = k * jnp.exp(bc[c - 1:c, :] - bc)
        kv = _dot_tn(k_out.astype(BF16), v)
        gsum = _dot_tn(g_parts, jnp.ones((c, LANES), BF16))
        decay = jnp.exp(gsum[:WK_B] + (gsum[WK_B:2 * WK_B] + gsum[2 * WK_B:]))
        state_ref[...] = (state_ref[...] * jnp.concatenate([decay] * (WV_B // LANES), axis=1)
                          + jnp.where(state_diag, kv, 0.0))

        for h in range(H_B):
            lanes = slice(h * DV_B, (h + 1) * DV_B)
            oh = o[:, lanes]
            yh = oh * lax.rsqrt(jnp.mean(oh * oh, axis=-1, keepdims=True) + EPS) * gn
            y_ref[rows, lanes] = (yh * r_ref[rows, lanes].astype(F32)).astype(y_ref.dtype)

    @pl.when(tb == n_tb - 1)
    def _():
        for h in range(H_B):
            sfin_ref[h] = state_ref[h * DK_B:(h + 1) * DK_B, h * DV_B:(h + 1) * DV_B]


def _gla(q, k, g, v, r, s0, gn, batch, seq, tb_rows):
    n_tb = seq // tb_rows
    row = lambda w: pl.BlockSpec((tb_rows, w), lambda b, t: (b * n_tb + t, 0))
    state = pl.BlockSpec((None, H_B, DK_B, DV_B), lambda b, t: (b, 0, 0, 0))
    return pl.pallas_call(
        _gla_kernel,
        grid=(batch, n_tb),
        in_specs=[row(WK_B), row(WK_B), row(WK_B), row(WV_B), row(WV_B), state,
                  pl.BlockSpec((1, DV_B), lambda b, t: (0, 0))],
        out_specs=[row(WV_B), state],
        out_shape=[jax.ShapeDtypeStruct((batch * seq, WV_B), BF16),
                   jax.ShapeDtypeStruct((batch, H_B, DK_B, DV_B), F32)],
        scratch_shapes=[pltpu.VMEM((WK_B, WV_B), F32)],
        compiler_params=_cparams(("arbitrary", "arbitrary")),
        name="gla",
    )(q, k, g, v, r, s0, gn)


def _mixout_kernel(x_ref, ya_ref, yb_ref, ga_ref, gb_ref, mod_ref, g2_ref, wpa_ref, wpb_ref, wo_ref, rw_ref,
                   x1_ref, hp_ref, lg_ref):
    br_a = _dot(ya_ref[...], wpa_ref[...])
    br_b = _dot(yb_ref[...], wpb_ref[...])
    merged = ga_ref[...].astype(F32) * br_a + gb_ref[...].astype(F32) * br_b
    out = _dot(merged.astype(BF16), wo_ref[...])
    x1 = x_ref[...] + mod_ref[2] * out
    x1_ref[...] = x1
    y = x1 * lax.rsqrt(jnp.mean(x1 * x1, axis=-1, keepdims=True) + EPS) * g2_ref[...]
    h = y * (1.0 + mod_ref[4]) + mod_ref[3]
    hp_ref[...] = _pack_bf16_pair(h)
    h_hi = h.astype(BF16)
    h_lo = (h - h_hi.astype(F32)).astype(BF16)
    lg_ref[...] = _dot_nt(rw_ref[0], h_hi) + (_dot_nt(rw_ref[0], h_lo) + _dot_nt(rw_ref[1], h_hi))


def _mixout(x, ya, yb, ga, gb, mod, rows_per_mod, tm, g2, wpa, wpb, wo, rw):
    n = x.shape[0]
    mod_rows = mod.shape[2]
    tiles_per_mod = rows_per_mod // tm
    row = lambda w: pl.BlockSpec((tm, w), lambda i: (i, 0))
    return pl.pallas_call(
        _mixout_kernel,
        grid=(n // tm,),
        in_specs=[row(D_MODEL), row(W_A), row(WV_B), row(D_MODEL), row(D_MODEL),
                  pl.BlockSpec((6, None, mod_rows, D_MODEL), lambda i: (0, i // tiles_per_mod, 0, 0)),
                  _const_spec(g2.shape), _const_spec(wpa.shape), _const_spec(wpb.shape), _const_spec(wo.shape),
                  _const_spec(rw.shape)],
        out_specs=[row(D_MODEL), row(D_MODEL // 2), pl.BlockSpec((N_EXPERTS, tm), lambda i: (0, i))],
        out_shape=[jax.ShapeDtypeStruct((n, D_MODEL), F32),
                   jax.ShapeDtypeStruct((n, D_MODEL // 2), U32),
                   jax.ShapeDtypeStruct((N_EXPERTS, n), F32)],
        compiler_params=_cparams(("arbitrary",)),
        name="mixout",
    )(x, ya, yb, ga, gb, mod, g2, wpa, wpb, wo, rw)


def _route_kernel(lg_ref, lg_last_ref, bias_ref, idx_ref, w_ref, rank_ref, cnt_ref, carry_ref):
    i = pl.program_id(0)
    tm = lg_ref.shape[1]

    @pl.when(i == 0)
    def _():
        carry_ref[...] = jnp.zeros_like(carry_ref)

    lg = jnp.where(i == pl.num_programs(0) - 1, lg_last_ref[...], lg_ref[...])
    s = _sigmoid(lg)
    sb = s + bias_ref[...][:, 0:1]
    ninf = -jnp.inf

    e_in_group = lax.broadcasted_iota(I32, (GROUP_SIZE, tm), 0)
    gscores = []
    for g in range(N_GROUPS):
        blk = sb[g * GROUP_SIZE:(g + 1) * GROUP_SIZE, :]
        m1 = jnp.max(blk, axis=0, keepdims=True)
        first = jnp.min(jnp.where(blk == m1, e_in_group, GROUP_SIZE), axis=0, keepdims=True)
        m2 = jnp.max(jnp.where(e_in_group == first, ninf, blk), axis=0, keepdims=True)
        gscores.append(m1 + m2)
    gs = jnp.concatenate(gscores, axis=0)

    g_iota = lax.broadcasted_iota(I32, (N_GROUPS, tm), 0)
    chosen = jnp.zeros((N_GROUPS, tm), F32)
    for _ in range(TOPK_GROUPS):
        m = jnp.max(gs, axis=0, keepdims=True)
        first = jnp.min(jnp.where(gs == m, g_iota, N_GROUPS), axis=0, keepdims=True)
        pick = g_iota == first
        chosen = jnp.where(pick, 1.0, chosen)
        gs = jnp.where(pick, ninf, gs)
    allowed = jnp.concatenate(
        [jnp.broadcast_to(chosen[g:g + 1, :], (GROUP_SIZE, tm)) for g in range(N_GROUPS)], axis=0) > 0.5
    cand = jnp.where(allowed, sb, ninf)

    e_iota = lax.broadcasted_iota(I32, (N_EXPERTS, tm), 0)
    onehot = jnp.zeros((N_EXPERTS, tm), F32)
    idxs, ws = [], []
    for _ in range(TOP_K):
        m = jnp.max(cand, axis=0, keepdims=True)
        first = jnp.min(jnp.where(cand == m, e_iota, N_EXPERTS), axis=0, keepdims=True)
        pick = e_iota == first
        idxs.append(first)
        ws.append(jnp.sum(jnp.where(pick, s, 0.0), axis=0, keepdims=True))
        onehot = jnp.where(pick, 1.0, onehot)
        cand = jnp.where(pick, ninf, cand)
    w = jnp.concatenate(ws, axis=0)
    w_ref[...] = w / jnp.sum(w, axis=0, keepdims=True) * ROUTE_SCALE
    idx_ref[...] = jnp.concatenate(idxs, axis=0)

    r = lax.broadcasted_iota(I32, (tm, tm), 0)
    c = lax.broadcasted_iota(I32, (tm, tm), 1)
    earlier = jnp.where(r < c, 1.0, 0.0).astype(BF16)
    carry = carry_ref[...]
    before = _dot(onehot.astype(BF16), earlier) + jnp.concatenate([carry] * (tm // LANES), axis=1)
    rank_ref[...] = jnp.concatenate(
        [jnp.sum(jnp.where(e_iota == ix, before, 0.0), axis=0, keepdims=True) for ix in idxs], axis=0).astype(I32)
    carry = carry + _dot(onehot.astype(BF16), jnp.ones((tm, LANES), BF16))
    carry_ref[...] = carry
    cnt_ref[...] = carry


def _route(logits_t, logits_last, bias):
    tm = MOE_TILE
    n_first = logits_t.shape[1] // tm
    n = logits_t.shape[1] + logits_last.shape[1]
    tok = lambda: pl.BlockSpec((TOP_K, tm), lambda i: (0, i))
    return pl.pallas_call(
        _route_kernel,
        grid=(n // tm,),
        in_specs=[pl.BlockSpec((N_EXPERTS, tm), lambda i: (0, jnp.minimum(i, n_first - 1))),
                  _const_spec(logits_last.shape), _const_spec(bias.shape)],
        out_specs=[tok(), tok(), tok(), _const_spec((N_EXPERTS, LANES))],
        out_shape=[jax.ShapeDtypeStruct((TOP_K, n), I32), jax.ShapeDtypeStruct((TOP_K, n), F32),
                   jax.ShapeDtypeStruct((TOP_K, n), I32), jax.ShapeDtypeStruct((N_EXPERTS, LANES), F32)],
        scratch_shapes=[pltpu.VMEM((N_EXPERTS, LANES), F32)],
        compiler_params=_cparams(("arbitrary",)),
        name="route",
    )(logits_t, logits_last, bias)


def _dest_kernel(idx_ref, rank_ref, cnt_ref, dest_ref, start_ref):
    i = pl.program_id(0)
    tm = idx_ref.shape[1]

    @pl.when(i == 0)
    def _():
        cnt = cnt_ref[...]
        padded = jnp.floor((cnt + (EXPERT_TILE - 1)) * (1.0 / EXPERT_TILE)) * EXPERT_TILE
        r = lax.broadcasted_iota(I32, (N_EXPERTS, N_EXPERTS), 0)
        c = lax.broadcasted_iota(I32, (N_EXPERTS, N_EXPERTS), 1)
        start_ref[...] = _dot(jnp.where(c < r, 1.0, 0.0).astype(F32), padded, precision=HIGHEST)

    start = jnp.concatenate([start_ref[...]] * (tm // LANES), axis=1)
    e_iota = lax.broadcasted_iota(I32, (N_EXPERTS, tm), 0)
    idx = idx_ref[...]
    first = jnp.concatenate(
        [jnp.sum(jnp.where(e_iota == idx[kk:kk + 1, :], start, 0.0), axis=0, keepdims=True) for kk in range(TOP_K)],
        axis=0)
    dest_ref[...] = first.astype(I32) + rank_ref[...]


def _dest(idx_t, rank_t, counts):
    n = idx_t.shape[1]
    tm = MOE_TILE
    tok = lambda: pl.BlockSpec((TOP_K, tm), lambda i: (0, i))
    return pl.pallas_call(
        _dest_kernel,
        grid=(n // tm,),
        in_specs=[tok(), tok(), _const_spec(counts.shape)],
        out_specs=pl.BlockSpec((None, TOP_K, tm), lambda i: (i, 0, 0)),
        out_shape=jax.ShapeDtypeStruct((n // tm, TOP_K, tm), I32),
        scratch_shapes=[pltpu.VMEM((N_EXPERTS, LANES), F32)],
        compiler_params=_cparams(("arbitrary",)),
        name="dest",
    )(idx_t, rank_t, counts)


def _sc_worker():
    return lax.axis_index("s") * SC_CORES + lax.axis_index("c")


def _sc_mesh():
    return plsc.VectorSubcoreMesh(core_axis_name="c", subcore_axis_name="s")


def _dispatch(dest_tiles, hp, hp_last, n_rows_out):
    n_tok_tiles = dest_tiles.shape[0]
    n_first = hp.shape[0] // MOE_TILE
    assert n_first + hp_last.shape[0] // MOE_TILE == n_tok_tiles
    w = hp.shape[1]
    iters = -(-n_tok_tiles // SC_WORKERS)

    def body(hp_hbm, hp_last_hbm, dest_hbm, xs_hbm, idx_v, rows_v, sem):
        wid = _sc_worker()

        def scatter():
            copies = [pltpu.async_copy(rows_v, xs_hbm.at[idx_v.at[kk]], sem) for kk in range(TOP_K)]
            for cp in copies:
                cp.wait()

        @pl.loop(0, iters)
        def _(j):
            tile = wid + SC_WORKERS * j

            @pl.when(tile < n_first)
            def _():
                pltpu.sync_copy(dest_hbm.at[tile], idx_v)
                pltpu.sync_copy(hp_hbm.at[pl.ds(tile * MOE_TILE, MOE_TILE)], rows_v)
                scatter()

            @pl.when((tile >= n_first) & (tile < n_tok_tiles))
            def _():
                pltpu.sync_copy(dest_hbm.at[tile], idx_v)
                pltpu.sync_copy(hp_last_hbm.at[pl.ds((tile - n_first) * MOE_TILE, MOE_TILE)], rows_v)
                scatter()

    return pl.kernel(
        body,
        out_type=jax.ShapeDtypeStruct((n_rows_out, w), U32),
        mesh=_sc_mesh(),
        scratch_types=[pltpu.VMEM((TOP_K, MOE_TILE), I32), pltpu.VMEM((MOE_TILE, w), U32), pltpu.SemaphoreType.DMA],
        name="dispatch",
    )(hp, hp_last, dest_tiles)


def _gather_back(ys, dest_tiles):
    n_tok_tiles = dest_tiles.shape[0]
    w = ys.shape[1]
    n_lists = n_tok_tiles * TOP_K
    iters = -(-n_lists // SC_WORKERS)

    def body(ys_hbm, dest_hbm, out_hbm, idx_v, rows_v, sem):
        wid = _sc_worker()

        @pl.loop(0, iters)
        def _(j):
            lst = wid + SC_WORKERS * j

            @pl.when(lst < n_lists)
            def _():
                pltpu.sync_copy(dest_hbm.at[pl.ds(lst * MOE_TILE, MOE_TILE)], idx_v)
                pltpu.async_copy(ys_hbm.at[idx_v], rows_v, sem).wait()
                pltpu.sync_copy(rows_v, out_hbm.at[pl.ds(lst * MOE_TILE, MOE_TILE)])

    out = pl.kernel(
        body,
        out_type=jax.ShapeDtypeStruct((n_lists * MOE_TILE, w), U32),
        mesh=_sc_mesh(),
        scratch_types=[pltpu.VMEM((MOE_TILE,), I32), pltpu.VMEM((MOE_TILE, w), U32), pltpu.SemaphoreType.DMA],
        name="gather_back",
    )(ys, dest_tiles.reshape(-1))
    return out.reshape(n_tok_tiles, TOP_K, MOE_TILE, w)


def _experts_kernel(first_ref, count_ref, rows_ref, xs_hbm, wg_ref, wu_ref, wd_ref, after_hbm, ys_hbm,
                    xbuf, ybuf, wgb_ref, wub_ref, wdb_ref, sem):
    del after_hbm
    e = pl.program_id(0)
    tm = EXPERT_TILE
    ring = EXPERT_RING
    n_e = count_ref[e]
    total = first_ref[N_EXPERTS - 1] + count_ref[N_EXPERTS - 1]

    def x_copy(t, slot):
        return pltpu.make_async_copy(xs_hbm.at[pl.ds(t * tm, tm), :], xbuf.at[slot], sem.at[0, slot])

    def y_copy(t, slot):
        return pltpu.make_async_copy(ybuf.at[slot], ys_hbm.at[pl.ds(t * tm, tm), :], sem.at[1, slot])

    @pl.when(e == 0)
    def _():
        for t in range(ring - 1):
            @pl.when(t < total)
            def _():
                x_copy(t, t).start()

    @pl.when(n_e > 0)
    def _():
        wgb_ref[...] = wg_ref[...].astype(BF16)
        wub_ref[...] = wu_ref[...].astype(BF16)
        wdb_ref[...] = wd_ref[...].astype(BF16)
        half = D_MODEL // 2

        def body(j, carry):
            t = first_ref[e] + j
            slot = t % ring
            x_copy(t, slot).wait()

            @pl.when(t + ring - 1 < total)
            def _():
                x_copy(t + ring - 1, (t + ring - 1) % ring).start()

            @pl.when(t >= ring)
            def _():
                y_copy(t - ring, slot).wait()

            xs = xbuf[slot]
            filled = lax.broadcasted_iota(I32, xs.shape, 0) < rows_ref[t]
            lo, hi = _unpack_bf16_pair(jnp.where(filled, xs, jnp.zeros_like(xs)))
            gate = _dot(lo, wgb_ref[:half, :]) + _dot(hi, wgb_ref[half:, :])
            up = _dot(lo, wub_ref[:half, :]) + _dot(hi, wub_ref[half:, :])
            act = (_silu(gate) * up).astype(BF16)
            ybuf[slot] = _pack_bf16_pair(_dot(act, wdb_ref[...]))
            y_copy(t, slot).start()
            return carry

        lax.fori_loop(0, n_e, body, 0)

    @pl.when(e == N_EXPERTS - 1)
    def _():
        for back in range(ring, 0, -1):
            @pl.when(total >= back)
            def _():
                y_copy(total - back, (total - back) % ring).wait()


def _experts(tile_first, tile_count, tile_rows, xs, wg, wu, wd, after):
    n_rows, w = xs.shape
    tm = EXPERT_TILE
    expert = lambda e, tf, tc, tr: (e, 0, 0)
    return pl.pallas_call(
        _experts_kernel,
        grid_spec=pltpu.PrefetchScalarGridSpec(
            num_scalar_prefetch=3,
            grid=(N_EXPERTS,),
            in_specs=[pl.BlockSpec(memory_space=pl.ANY),
                      pl.BlockSpec((None, D_MODEL, D_EXPERT), expert),
                      pl.BlockSpec((None, D_MODEL, D_EXPERT), expert),
                      pl.BlockSpec((None, D_EXPERT, D_MODEL), expert),
                      pl.BlockSpec(memory_space=pl.ANY)],
            out_specs=pl.BlockSpec(memory_space=pl.ANY),
            scratch_shapes=[pltpu.VMEM((EXPERT_RING, tm, w), U32), pltpu.VMEM((EXPERT_RING, tm, w), U32),
                            pltpu.VMEM((D_MODEL, D_EXPERT), BF16), pltpu.VMEM((D_MODEL, D_EXPERT), BF16),
                            pltpu.VMEM((D_EXPERT, D_MODEL), BF16),
                            pltpu.SemaphoreType.DMA((2, EXPERT_RING))]),
        out_shape=jax.ShapeDtypeStruct((n_rows, w), U32),
        compiler_params=_cparams(("arbitrary",)),
        name="experts",
    )(tile_first, tile_count, tile_rows, xs, wg, wu, wd, after)


def _shared_kernel(hp_ref, sg_ref, su_ref, sd_ref, o_ref):
    half = D_MODEL // 2
    lo, hi = _unpack_bf16_pair(hp_ref[...])
    sg = sg_ref[...]
    su = su_ref[...]
    gate = _dot(lo, sg[:half]) + _dot(hi, sg[half:])
    up = _dot(lo, su[:half]) + _dot(hi, su[half:])
    o_ref[...] = _dot((_silu(gate) * up).astype(BF16), sd_ref[...]).astype(o_ref.dtype)


def _shared(hp, tm, sg, su, sd):
    n = hp.shape[0]
    return pl.pallas_call(
        _shared_kernel,
        grid=(n // tm,),
        in_specs=[pl.BlockSpec((tm, D_MODEL // 2), lambda i: (i, 0)),
                  _const_spec(sg.shape), _const_spec(su.shape), _const_spec(sd.shape)],
        out_specs=pl.BlockSpec((tm, D_MODEL), lambda i: (i, 0)),
        out_shape=jax.ShapeDtypeStruct((n, D_MODEL), BF16),
        compiler_params=_cparams(("arbitrary",)),
        name="shared",
    )(hp, sg, su, sd)


def _combine_kernel(yt_ref, x1_ref, ysh_ref, w_ref, mod_ref, fg_ref, *rest):
    o_ref = rest[-1]
    y = ysh_ref[...].astype(F32)
    w = w_ref[...]
    for kk in range(TOP_K):
        ylo, yhi = _unpack_bf16_pair(yt_ref[kk])
        y = y + w[:, kk:kk + 1] * jnp.concatenate([ylo.astype(F32), yhi.astype(F32)], axis=1)
    x2 = x1_ref[...] + mod_ref[5] * y
    o_ref[...] = x2 * lax.rsqrt(jnp.mean(x2 * x2, axis=-1, keepdims=True) + EPS) * fg_ref[...]


def _combine(y_tok, y_tok_tile0, x1, y_shared, w, w_tile0, mod, rows_per_mod, fg, tile0, n_tiles, prev=None):
    n = x1.shape[0]
    tm = MOE_TILE
    mod_rows = mod.shape[2]
    tiles_per_mod = rows_per_mod // tm
    own = lambda i: (i + tile0, 0)
    return pl.pallas_call(
        _combine_kernel,
        grid=(n_tiles,),
        in_specs=[pl.BlockSpec((None, TOP_K, tm, D_MODEL // 2), lambda i: (i + y_tok_tile0, 0, 0, 0)),
                  pl.BlockSpec((tm, D_MODEL), own),
                  pl.BlockSpec((tm, D_MODEL), own),
                  pl.BlockSpec((tm, TOP_K), lambda i: (i + w_tile0, 0)),
                  pl.BlockSpec((6, None, mod_rows, D_MODEL), lambda i: (0, (i + tile0) // tiles_per_mod, 0, 0)),
                  _const_spec(fg.shape)] + ([] if prev is None else [pl.BlockSpec(memory_space=pl.ANY)]),
        out_specs=pl.BlockSpec((tm, D_MODEL), own),
        out_shape=jax.ShapeDtypeStruct((n, D_MODEL), F32),
        input_output_aliases={} if prev is None else {6: 0},
        compiler_params=_cparams(("arbitrary",)),
        name="combine",
    )(y_tok, x1, y_shared, w, mod, fg, *(() if prev is None else (prev,)))


def _split_w_in(w_in):
    sizes = (W_A, W_A, W_A, H_A, WK_B, WK_B, WV_B, WV_B, GK_RANK, D_MODEL, D_MODEL)
    segs, o = [], 0
    for s in sizes:
        segs.append(w_in[:, o:o + s])
        o += s
    return segs


def kernel(x_prompt, x_sample, cache_k, cache_v, cache_logf, state_gla, page_table, c_prompt, c_sample, w_ada, b_ada,
           norm1_g, w_in, b_f, w_gk2, b_gk, gla_norm_g, w_pa, w_pb, w_o, norm2_g, router_w, router_bias, w_gate, w_up,
           w_down, ws_gate, ws_up, ws_down, final_g):
    assert w_ada.shape[0] == 1, "single layer"
    bp, tp, d = x_prompt.shape
    bs, ts, _ = x_sample.shape
    n_p, n_s = bp * tp, bs * ts
    n_tot = n_p + n_s
    assert n_s == MOE_TILE and n_p % MOE_TILE == 0 and tp % FOX_TILE == 0

    q_a, k_a, v_a, f_a, q_b, k_b, v_b, r_b, gk1, gate_a, gate_b = _split_w_in(w_in[0])
    wts = (jnp.concatenate([q_a, k_a], axis=1).astype(BF16),
           v_a.astype(BF16),
           jnp.concatenate([k_a, v_a], axis=1).T.astype(BF16),
           jnp.pad(f_a, ((0, 0), (0, LANES - H_A))).astype(BF16),
           jnp.pad(b_f[0].reshape(1, H_A), ((0, 0), (0, LANES - H_A))),
           jnp.concatenate([q_b, k_b, v_b, r_b], axis=1).astype(BF16),
           jnp.pad(gk1, ((0, 0), (0, LANES - GK_RANK))).astype(BF16),
           jnp.pad(w_gk2[0], ((0, LANES - GK_RANK), (0, 0))).astype(BF16),
           b_gk[0].reshape(1, WK_B),
           jnp.concatenate([gate_a, gate_b], axis=1).astype(BF16))
    g1 = norm1_g[0].reshape(1, d)
    g2 = norm2_g[0].reshape(1, d)
    gn = gla_norm_g[0].reshape(1, DV_B)
    fg = final_g.reshape(1, d)
    wpa, wpb, wo = w_pa[0].astype(BF16), w_pb[0].astype(BF16), w_o[0].astype(BF16)
    rw_hi = router_w[0].T.astype(BF16)
    rw_t = jnp.stack([rw_hi, (router_w[0].T - rw_hi.astype(F32)).astype(BF16)])
    rbias = jnp.broadcast_to(router_bias[0].reshape(N_EXPERTS, 1), (N_EXPERTS, LANES))
    sg, su, sd = ws_gate[0].astype(BF16), ws_up[0].astype(BF16), ws_down[0].astype(BF16)

    ada = _ada(jnp.concatenate([c_prompt, c_sample], axis=0), w_ada[0], b_ada[0].reshape(1, -1))
    ada = ada.reshape(bp + bs, 6, d)
    mod_p = ada[:bp].transpose(1, 0, 2)[:, :, None, :]
    mod_s = jnp.repeat(ada[bp:], ts, axis=0).transpose(1, 0, 2)[:, None, :, :]

    xp = x_prompt.reshape(n_p, d)
    tm_p = 512
    (q, kh, vth, kt, vt, lf, fc, qb, kb, gk, vb, rb, ga_p, gb_p) = _inproj(
        xp, mod_p, tp, tp // tm_p, tm_p, g1, wts, True)
    ya_p = _fox(q, kh, vth, fc, bp, tp)
    yb_p, s_p = _gla(qb, kb, gk, vb, rb, jnp.zeros((bp, H_B, DK_B, DV_B), F32), gn, bp, tp, 256)
    k_prompt = kt.reshape(bp, H_A, DH_A, tp).transpose(0, 3, 1, 2)[None]
    v_prompt = vt.reshape(bp, H_A, DH_A, tp).transpose(0, 3, 1, 2)[None]
    logf_prompt = lf.reshape(1, bp, tp, H_A)

    xs_in = x_sample.reshape(n_s, d)
    (q, k, v, lf, qb, kb, gk, vb, rb, ga, gb) = _inproj(xs_in, mod_s, n_s, 1, n_s, g1, wts, False)
    page_rows = cache_k.shape[2]
    tq = 8
    cache_kt = jnp.transpose(cache_k, (0, 1, 3, 4, 2))
    cache_vt = jnp.transpose(cache_v, (0, 1, 3, 4, 2))
    suf = _lfsuf(page_table, jnp.transpose(cache_logf, (0, 1, 3, 2)))

    def new_kv(a):
        a = a.reshape(bs, ts, H_A, DH_A).transpose(0, 2, 3, 1)
        return jnp.pad(a, ((0, 0), (0, 0), (0, 0), (0, page_rows - ts)))

    q_dec = jnp.pad(q.astype(F32).reshape(bs, ts, H_A, DH_A).transpose(0, 2, 1, 3),
                    ((0, 0), (0, 0), (0, tq - ts), (0, 0)))
    lfn = jnp.pad(lf.reshape(bs, ts, H_A).transpose(0, 2, 1), ((0, 0), (0, 0), (0, page_rows - ts)))
    ya = _foxdec(page_table, q_dec, new_kv(k), new_kv(v), lfn, suf, cache_kt, cache_vt)
    ya = ya[:, :, :ts].transpose(0, 2, 1, 3).reshape(n_s, W_A).astype(BF16)
    pad = lambda a: jnp.pad(a.reshape(bs, ts, -1), ((0, 0), (0, GLA_CHUNK - ts), (0, 0))).reshape(bs * GLA_CHUNK, -1)
    yb, s_s = _gla(pad(qb), pad(kb), pad(gk), pad(vb), pad(rb), state_gla[0], gn, bs, GLA_CHUNK, GLA_CHUNK)
    yb = yb.reshape(bs, GLA_CHUNK, WV_B)[:, :ts].reshape(n_s, WV_B)
    k_sample = k.reshape(1, bs, ts, H_A, DH_A)
    v_sample = v.reshape(1, bs, ts, H_A, DH_A)
    logf_sample = lf.reshape(1, bs, ts, H_A)
    x1_s, hp_s, logits_s = _mixout(xs_in, ya, yb, ga, gb, mod_s, n_s, n_s, g2, wpa, wpb, wo, rw_t)

    x1_p, hp_p, logits_p = _mixout(xp, ya_p, yb_p, ga_p, gb_p, mod_p, tp, 4 * MOE_TILE, g2, wpa, wpb, wo, rw_t)
    idx_t, w_t, rank_t, counts = _route(logits_p, logits_s, rbias)
    dest_tiles = _dest(idx_t, rank_t, counts)
    padded = (counts[:, 0].astype(I32) + EXPERT_TILE - 1) // EXPERT_TILE * EXPERT_TILE
    pad_end = jnp.cumsum(padded)
    n_tiles = -(-n_tot * TOP_K // EXPERT_TILE) + N_EXPERTS
    tile_first_row = jnp.arange(n_tiles, dtype=I32) * EXPERT_TILE
    tile_expert = jnp.minimum(jnp.sum((pad_end[None, :] <= tile_first_row[:, None]).astype(I32), axis=1),
                              N_EXPERTS - 1)
    used_end = pad_end - padded + counts[:, 0].astype(I32)
    tile_rows = jnp.clip(used_end[tile_expert] - tile_first_row, 0, EXPERT_TILE).astype(I32)
    tile_first = ((pad_end - padded) // EXPERT_TILE).astype(I32)
    tile_count = (padded // EXPERT_TILE).astype(I32)

    xs_sorted = _dispatch(dest_tiles, hp_p, hp_s, n_tiles * EXPERT_TILE)
    ysh_p = _shared(hp_p, 2 * MOE_TILE, sg, su, sd)
    ysh_s = _shared(hp_s, MOE_TILE, sg, su, sd)
    ys_sorted = _experts(tile_first, tile_count, tile_rows, xs_sorted, w_gate[0], w_up[0], w_down[0], ysh_p)
    n_p_tiles = n_p // MOE_TILE
    part = n_p_tiles // GATHER_PARTS
    w_tok = w_t.T
    y_prompt = None
    for j in range(GATHER_PARTS):
        last = j == GATHER_PARTS - 1
        y_tok = _gather_back(ys_sorted, dest_tiles[j * part:] if last else dest_tiles[j * part:(j + 1) * part])
        y_prompt = _combine(y_tok, 0, x1_p, ysh_p, w_tok, j * part, mod_p, tp, fg, j * part, part, prev=y_prompt)
    y_sample = _combine(y_tok, part, x1_s, ysh_s, w_tok, n_p_tiles, mod_s, n_s, fg, 0, n_s // MOE_TILE)

    return (y_prompt.reshape(bp, tp, d), y_sample.reshape(bs, ts, d),
            k_prompt, v_prompt, logf_prompt, s_p[None],
            k_sample, v_sample, logf_sample, s_s[None])
```
